```python
import math
import jax
import jax.numpy as jnp
from jax import lax
import numpy as np

D_MODEL = 1024
BATCH = 8
SEQ = 4096
DEPTH = 1

HEAD_DIM = 64
A_HEADS = 8
DIL_PATTERNS = ((128, 1), (512, 4), (2048, 16))
DIL_BLOCK = 128
B_HEADS = 8
B_KV_HEADS = 2
B_GROUP = B_HEADS // B_KV_HEADS
CMP_LEN = 32
CMP_STRIDE = 16
CMP_HIDDEN = 128
SEL_BLOCK = 64
SEL_TOPN = 16
SEL_Q_CHUNK = 64
CMP_OVERLAP = (1.0, 2.0, 2.0, 2.0, 1.0)
WIN_LEN = 512
WIN_BLOCK = 128
N_BRANCH = 3
REL_BUCKETS = 32
REL_MAX_DIST = 2048
N_HEADS_TOTAL = A_HEADS + B_HEADS
FFN_HIDDEN = ((-(-8 * D_MODEL // 3)) + 255) // 256 * 256
RMS_EPS = 1e-6
A_WIDTH = A_HEADS * HEAD_DIM
B_WIDTH = B_HEADS * HEAD_DIM
KV_WIDTH = B_KV_HEADS * HEAD_DIM
GATE_WIDTH = B_HEADS * N_BRANCH
IN_SPLITS = (A_WIDTH, A_WIDTH, A_WIDTH, B_WIDTH, KV_WIDTH, KV_WIDTH, KV_WIDTH, KV_WIDTH, KV_WIDTH, KV_WIDTH, GATE_WIDTH)
IN_WIDTH = sum(IN_SPLITS)
SCALE = HEAD_DIM ** -0.5

kernel_name = "hybrid_dilated_nsa_layer"


def rms_norm(x, g):
    xf = x.astype(jnp.float32)
    y = xf * lax.rsqrt(jnp.mean(xf * xf, axis=-1, keepdims=True) + RMS_EPS)
    return (y * g.astype(jnp.float32)).astype(x.dtype)


def t5_bucket(dist):
    max_exact = REL_BUCKETS // 2
    df = jnp.maximum(dist, 1).astype(jnp.float32)
    large = max_exact + (jnp.log(df / max_exact) / math.log(REL_MAX_DIST / max_exact)
                         * (REL_BUCKETS - max_exact)).astype(jnp.int32)
    large = jnp.minimum(large, REL_BUCKETS - 1)
    return jnp.where(dist < max_exact, dist, large)


def masked_softmax(logits, mask):
    logits = jnp.where(mask, logits, -jnp.inf)
    m = jnp.max(logits, axis=-1, keepdims=True)
    m = jnp.where(jnp.isfinite(m), m, 0.0)
    p = jnp.exp(logits - m)
    den = jnp.sum(p, axis=-1, keepdims=True)
    return p / jnp.maximum(den, 1e-30)


def split_columns(t):
    out, start = [], 0
    for w in IN_SPLITS:
        out.append(t[..., start:start + w])
        start += w
    return out


def to_heads(t, n):
    b, s, _ = t.shape
    return t.reshape(b, s, n, HEAD_DIM).transpose(0, 2, 1, 3)


def dilated_attention(q, k, v, table, window, dil):
    b, h, s, dh = q.shape
    steps = window // dil
    unit = dil * DIL_BLOCK
    sp = -(-s // unit) * unit
    length = sp // dil
    nb = length // DIL_BLOCK

    def to_blocks(t):
        t = jnp.pad(t, ((0, 0), (0, 0), (0, sp - s), (0, 0)))
        t = t.reshape(b, h, length, dil, dh).transpose(0, 1, 3, 2, 4)
        return t.reshape(b, h, dil, nb, DIL_BLOCK, dh)

    def with_prev(t):
        prev = jnp.pad(t[:, :, :, :-1], ((0, 0), (0, 0), (0, 0), (1, 0), (0, 0), (0, 0)))
        return jnp.concatenate([prev, t], axis=4)

    qb = to_blocks(q)
    kk = with_prev(to_blocks(k))
    vv = with_prev(to_blocks(v))
    i = jnp.arange(DIL_BLOCK)[:, None]
    j = jnp.arange(2 * DIL_BLOCK)[None, :]
    dist = i + DIL_BLOCK - j
    mask = (dist >= 0) & (dist <= steps)
    first = (jnp.arange(nb)[:, None, None] > 0) | (j[None] >= DIL_BLOCK)
    mask = mask[None] & first
    bias = table[:, t5_bucket(jnp.maximum(dist, 0) * dil)]
    sc = jnp.einsum('bhrnqd,bhrnkd->bhrnqk', qb, kk, preferred_element_type=jnp.float32) * SCALE
    sc = jnp.where(mask, sc + bias[None, :, None, None], -jnp.inf)
    m = jnp.max(sc, axis=-1, keepdims=True)
    p = jnp.exp(sc - m)
    den = jnp.sum(p, axis=-1, keepdims=True)
    o = jnp.einsum('bhrnqk,bhrnkd->bhrnqd', p, vv.astype(jnp.float32)) / den
    lse = (m + jnp.log(den))[..., 0]
    o = o.reshape(b, h, dil, length, dh).transpose(0, 1, 3, 2, 4).reshape(b, h, sp, dh)[:, :, :s]
    lse = lse.reshape(b, h, dil, length).transpose(0, 1, 3, 2).reshape(b, h, sp)[:, :, :s]
    return o, lse


def mixer_dilated(q, k, v, table):
    outs, lses = [], []
    for window, dil in DIL_PATTERNS:
        o, lse = dilated_attention(q, k, v, table, window, dil)
        outs.append(o)
        lses.append(lse)
    w = jax.nn.softmax(jnp.stack(lses, axis=0), axis=0)
    return jnp.sum(w[..., None] * jnp.stack(outs, axis=0), axis=0)


def nsa_compress(kraw, pos, w1, b1, w2):
    b, g, s, dh = kraw.shape
    ch = kraw.reshape(b, g, s // CMP_STRIDE, CMP_STRIDE, dh)
    blk = jnp.concatenate([ch[:, :, :-1], ch[:, :, 1:]], axis=3) + pos
    hid = jax.nn.gelu(jnp.einsum('bgnld,ldh->bgnh', blk, w1) + b1)
    return hid @ w2


def nsa_compressed_branch(q, kc, vc):
    s = q.shape[3]
    nc = kc.shape[2]
    t = jnp.arange(s)
    block_end = jnp.arange(nc) * CMP_STRIDE + CMP_LEN - 1
    mask = block_end[None, :] <= t[:, None]
    sc = jnp.einsum('bgrqd,bgnd->bgrqn', q, kc, preferred_element_type=jnp.float32) * SCALE
    p = masked_softmax(sc, mask)
    o = jnp.einsum('bgrqn,bgnd->bgrqd', p, vc.astype(jnp.float32))
    return o, jnp.sum(p, axis=2)


def block_importance(pc, nb):
    ratio = SEL_BLOCK // CMP_STRIDE
    pp = jnp.pad(pc, ((0, 0), (0, 0), (0, 0), (1, 1)))
    imp = 0.0
    for off, w in zip(range(-1, ratio), CMP_OVERLAP):
        imp = imp + w * pp[..., off + 1: off + 2 + ratio * (nb - 1): ratio]
    return imp


def nsa_selected_branch(q, k, v, imp, table):
    b, g, r, s, dh = q.shape
    nb = s // SEL_BLOCK
    topn = min(SEL_TOPN, nb)
    t = jnp.arange(s)
    cur = (t // SEL_BLOCK)[:, None]
    j = jnp.arange(nb)[None, :]
    forced = (j == 0) | (j == cur) | (j == cur - 1)
    imp = jnp.where(j > cur, -jnp.inf, jnp.where(forced, jnp.inf, imp))
    _, idx = lax.top_k(imp, topn)
    kblk = k.reshape(b, g, nb, SEL_BLOCK * dh)
    vblk = v.reshape(b, g, nb, SEL_BLOCK * dh)
    nch = s // SEL_Q_CHUNK
    qc = q.reshape(b, g, r, nch, SEL_Q_CHUNK, dh).transpose(3, 0, 1, 2, 4, 5)
    ic = idx.reshape(b, g, nch, SEL_Q_CHUNK, topn).transpose(2, 0, 1, 3, 4)
    tab = table.transpose(0, 2, 1)
    b_idx = jnp.arange(b)[:, None, None]
    g_idx3 = jnp.arange(g)[None, :, None]
    g_idx4 = jnp.arange(g)[None, :, None, None]
    nkey = topn * SEL_BLOCK

    def chunk(args):
        qq, ii, c = args
        flat = ii.reshape(b, g, SEL_Q_CHUNK * topn)
        kg = kblk[b_idx, g_idx3, flat].reshape(b, g, SEL_Q_CHUNK, nkey, dh)
        vg = vblk[b_idx, g_idx3, flat].reshape(b, g, SEL_Q_CHUNK, nkey, dh)
        kpos = (ii[..., None] * SEL_BLOCK + jnp.arange(SEL_BLOCK)).reshape(b, g, SEL_Q_CHUNK, nkey)
        tq = c * SEL_Q_CHUNK + jnp.arange(SEL_Q_CHUNK)
        dist = tq[None, None, :, None] - kpos
        bias = tab[g_idx4, t5_bucket(jnp.maximum(dist, 0))].transpose(0, 1, 4, 2, 3)
        sc = jnp.einsum('bgrqd,bgqkd->bgrqk', qq, kg, preferred_element_type=jnp.float32) * SCALE + bias
        p = masked_softmax(sc, (dist >= 0)[:, :, None])
        return jnp.einsum('bgrqk,bgqkd->bgrqd', p, vg.astype(jnp.float32))

    out = lax.map(chunk, (qc, ic, jnp.arange(nch)))
    return out.transpose(1, 2, 3, 0, 4, 5).reshape(b, g, r, s, dh)


def nsa_window_branch(q, k, v, table):
    b, g, r, s, dh = q.shape
    nb = s // WIN_BLOCK
    nw = WIN_LEN // WIN_BLOCK
    qb = q.reshape(b, g, r, nb, WIN_BLOCK, dh)

    def band(t):
        tb = jnp.pad(t.reshape(b, g, nb, WIN_BLOCK, dh), ((0, 0), (0, 0), (nw, 0), (0, 0), (0, 0)))
        return jnp.concatenate([tb[:, :, o:o + nb] for o in range(nw + 1)], axis=3)

    kk, vv = band(k), band(v)
    i = jnp.arange(WIN_BLOCK)[:, None]
    j = jnp.arange((nw + 1) * WIN_BLOCK)[None, :]
    dist = i + nw * WIN_BLOCK - j
    kpos = jnp.arange(nb)[:, None, None] * WIN_BLOCK + j[None] - nw * WIN_BLOCK
    mask = ((dist >= 0) & (dist < WIN_LEN))[None] & (kpos >= 0)
    bias = table[:, :, t5_bucket(jnp.maximum(dist, 0))]
    sc = jnp.einsum('bgrnqd,bgnkd->bgrnqk', qb, kk, preferred_element_type=jnp.float32) * SCALE
    p = masked_softmax(sc + bias[None, :, :, None], mask)
    o = jnp.einsum('bgrnqk,bgnkd->bgrnqd', p, vv.astype(jnp.float32))
    return o.reshape(b, g, r, s, dh)


def setup_inputs(seed: int = 0) -> dict:
    key = jax.random.key(seed)
    ks = jax.random.split(key, 18)
    f32 = jnp.float32
    nrm = lambda k, shape, scale: jax.random.normal(k, shape, f32) * scale
    return {
        "x": nrm(ks[0], (BATCH, SEQ, D_MODEL), 1.0),
        "norm1_g": 1.0 + nrm(ks[1], (DEPTH, D_MODEL), 0.02),
        "w_in": nrm(ks[2], (DEPTH, D_MODEL, IN_WIDTH), D_MODEL ** -0.5),
        "rel_bias": nrm(ks[3], (REL_BUCKETS, N_HEADS_TOTAL), 0.1),
        "cmp_pos": nrm(ks[4], (DEPTH, CMP_LEN, HEAD_DIM), 0.2),
        "cmp_k_w1": nrm(ks[5], (DEPTH, CMP_LEN, HEAD_DIM, CMP_HIDDEN), (CMP_LEN * HEAD_DIM) ** -0.5),
        "cmp_k_b1": nrm(ks[6], (DEPTH, CMP_HIDDEN), 0.02),
        "cmp_k_w2": nrm(ks[7], (DEPTH, CMP_HIDDEN, HEAD_DIM), CMP_HIDDEN ** -0.5),
        "cmp_v_w1": nrm(ks[8], (DEPTH, CMP_LEN, HEAD_DIM, CMP_HIDDEN), (CMP_LEN * HEAD_DIM) ** -0.5),
        "cmp_v_b1": nrm(ks[9], (DEPTH, CMP_HIDDEN), 0.02),
        "cmp_v_w2": nrm(ks[10], (DEPTH, CMP_HIDDEN, HEAD_DIM), CMP_HIDDEN ** -0.5),
        "w_out": nrm(ks[11], (DEPTH, A_WIDTH + B_WIDTH, D_MODEL), (A_WIDTH + B_WIDTH) ** -0.5),
        "norm2_g": 1.0 + nrm(ks[12], (DEPTH, D_MODEL), 0.02),
        "w_gate": nrm(ks[13], (DEPTH, D_MODEL, FFN_HIDDEN), D_MODEL ** -0.5),
        "w_up": nrm(ks[14], (DEPTH, D_MODEL, FFN_HIDDEN), D_MODEL ** -0.5),
        "w_down": nrm(ks[15], (DEPTH, FFN_HIDDEN, D_MODEL), FFN_HIDDEN ** -0.5),
        "norm_f_g": 1.0 + nrm(ks[16], (D_MODEL,), 0.02),
    }


def reference(x, norm1_g, w_in, rel_bias, cmp_pos, cmp_k_w1, cmp_k_b1, cmp_k_w2,
              cmp_v_w1, cmp_v_b1, cmp_v_w2, w_out, norm2_g, w_gate, w_up, w_down, norm_f_g):
    b, s, _ = x.shape
    tab_a = rel_bias[:, :A_HEADS].T
    tab_b = rel_bias[:, A_HEADS:].T.reshape(B_KV_HEADS, B_GROUP, REL_BUCKETS)
    h = x
    for layer in range(DEPTH):
        xn = rms_norm(h, norm1_g[layer])
        aq, ak, av, bq, kc, vc, ksl, vsl, kw, vw, gl = split_columns(xn @ w_in[layer])
        o_a = mixer_dilated(to_heads(aq, A_HEADS), to_heads(ak, A_HEADS), to_heads(av, A_HEADS), tab_a)
        q_b = to_heads(bq, B_HEADS).reshape(b, B_KV_HEADS, B_GROUP, s, HEAD_DIM)
        k_cmp = nsa_compress(to_heads(kc, B_KV_HEADS), cmp_pos[layer], cmp_k_w1[layer], cmp_k_b1[layer], cmp_k_w2[layer])
        v_cmp = nsa_compress(to_heads(vc, B_KV_HEADS), cmp_pos[layer], cmp_v_w1[layer], cmp_v_b1[layer], cmp_v_w2[layer])
        o_cmp, p_cmp = nsa_compressed_branch(q_b, k_cmp, v_cmp)
        imp = block_importance(p_cmp, s // SEL_BLOCK)
        o_slc = nsa_selected_branch(q_b, to_heads(ksl, B_KV_HEADS), to_heads(vsl, B_KV_HEADS), imp, tab_b)
        o_win = nsa_window_branch(q_b, to_heads(kw, B_KV_HEADS), to_heads(vw, B_KV_HEADS), tab_b)
        gate = jax.nn.sigmoid(gl.astype(jnp.float32)).reshape(b, s, B_KV_HEADS, B_GROUP, N_BRANCH).transpose(0, 2, 3, 1, 4)
        o_b = gate[..., 0:1] * o_cmp + gate[..., 1:2] * o_slc + gate[..., 2:3] * o_win
        mix = jnp.concatenate([
            o_a.transpose(0, 2, 1, 3).reshape(b, s, A_WIDTH),
            o_b.reshape(b, B_HEADS, s, HEAD_DIM).transpose(0, 2, 1, 3).reshape(b, s, B_WIDTH),
        ], axis=-1).astype(h.dtype)
        h = h + mix @ w_out[layer]
        hn = rms_norm(h, norm2_g[layer])
        h = h + (jax.nn.silu(hn @ w_gate[layer]) * (hn @ w_up[layer])) @ w_down[layer]
    return rms_norm(h, norm_f_g)
```

```python
import functools
import math

import numpy as np
import jax
import jax.numpy as jnp
from jax import lax
from jax.experimental import pallas as pl
from jax.experimental.pallas import tpu as pltpu

F32 = jnp.float32
BF16 = jnp.bfloat16

HEAD_DIM = 64
LANES = 128
A_HEADS = 8
DIL_PATTERNS = ((128, 1), (512, 4), (2048, 16))
DIL_BLOCK = 128
B_HEADS = 8
B_KV_HEADS = 2
B_GROUP = B_HEADS // B_KV_HEADS
CMP_LEN = 32
CMP_STRIDE = 16
CMP_HIDDEN = 128
SEL_BLOCK = 64
SEL_TOPN = 16
CMP_OVERLAP = (1.0, 2.0, 2.0, 2.0, 1.0)
WIN_LEN = 512
WIN_BLOCK = 128
N_BRANCH = 3
REL_BUCKETS = 32
REL_MAX_DIST = 2048
RMS_EPS = 1e-6
A_WIDTH = A_HEADS * HEAD_DIM
B_WIDTH = B_HEADS * HEAD_DIM
KV_WIDTH = B_KV_HEADS * HEAD_DIM
GATE_WIDTH = B_HEADS * N_BRANCH
SCALE = HEAD_DIM ** -0.5

Q_TILE = 128
K_TILE = 128
NOT_SELECTED = -(2.0 ** 100)
M_INIT = -1e30
VMEM_LIMIT = 56 * 1024 * 1024

NT_DIMS = (((1,), (1,)), ((), ()))


def _cparams(*sem):
    return pltpu.CompilerParams(dimension_semantics=sem, vmem_limit_bytes=VMEM_LIMIT)


def _t5_bucket_np(dist):
    max_exact = REL_BUCKETS // 2
    d = np.asarray(dist)
    df = np.maximum(d, 1).astype(np.float32)
    large = max_exact + (np.log(df / np.float32(max_exact)) / np.float32(math.log(REL_MAX_DIST / max_exact))
                         * np.float32(REL_BUCKETS - max_exact)).astype(np.int32)
    large = np.minimum(large, REL_BUCKETS - 1)
    return np.where(d < max_exact, d, large).astype(np.int32)


def _bias_from_table(tab, dist, valid):
    idx = _t5_bucket_np(np.maximum(dist, 0))
    b = jnp.take(tab, jnp.asarray(idx.reshape(-1)), axis=1).reshape((tab.shape[0],) + dist.shape)
    return jnp.where(jnp.asarray(valid)[None], b, -jnp.inf)


def _half_mask(g):
    lane = lax.broadcasted_iota(jnp.int32, (1, LANES), 1)
    return (lane >= HEAD_DIM) if g else (lane < HEAD_DIM)


def _inproj_kernel(x_ref, g_ref, w_ref, *out_refs, widths):
    x = x_ref[...]
    y = x * lax.rsqrt(jnp.mean(x * x, axis=-1, keepdims=True) + RMS_EPS)
    xn = (y * g_ref[...]).astype(BF16)
    start = 0
    for o_ref, w in zip(out_refs, widths):
        r = jnp.dot(xn, w_ref[:, start:start + w], preferred_element_type=F32)
        o_ref[...] = r.astype(o_ref.dtype)
        start += w


def _inproj(x2, g, w, widths, dtypes, tm):
    n, d = x2.shape
    out_shape = [jax.ShapeDtypeStruct((n, w_), dt) for w_, dt in zip(widths, dtypes)]
    return pl.pallas_call(
        functools.partial(_inproj_kernel, widths=widths),
        grid=(n // tm,),
        in_specs=[pl.BlockSpec((tm, d), lambda i: (i, 0)),
                  pl.BlockSpec((1, d), lambda i: (0, 0)),
                  pl.BlockSpec(w.shape, lambda i: (0, 0))],
        out_specs=[pl.BlockSpec((tm, w_), lambda i: (i, 0)) for w_ in widths],
        out_shape=out_shape,
        compiler_params=_cparams("parallel"),
        name="inproj",
    )(x2, g, w)


def _dilated_kernel(q_ref, kp_ref, kc_ref, vp_ref, vc_ref, bias_ref, o_ref, lse_ref):
    n = pl.program_id(2)
    col = lax.broadcasted_iota(jnp.int32, (1, 2 * DIL_BLOCK), 1)
    keep = (col >= DIL_BLOCK) | (n > 0)
    for pair in range(A_HEADS // 2):
        sl = slice(pair * LANES, (pair + 1) * LANES)
        q2 = q_ref[0, :, sl]
        kcat = jnp.concatenate([kp_ref[0, :, sl], kc_ref[0, :, sl]], axis=0)
        vcat = jnp.concatenate([vp_ref[0, :, sl], vc_ref[0, :, sl]], axis=0)
        o_pair = None
        lse_pair = None
        for hh in range(2):
            half = _half_mask(hh)
            qm = jnp.where(half, q2, jnp.zeros_like(q2))
            s = lax.dot_general(qm, kcat, NT_DIMS, preferred_element_type=F32)
            s = jnp.where(keep, s + bias_ref[2 * pair + hh], -jnp.inf)
            m = jnp.max(s, axis=-1, keepdims=True)
            p = jnp.exp(s - m)
            den = jnp.sum(p, axis=-1, keepdims=True)
            o = jnp.dot(p.astype(BF16), vcat, preferred_element_type=F32) / den
            lse = jnp.broadcast_to(m + jnp.log(den), o.shape)
            o_pair = o if hh == 0 else jnp.where(half, o, o_pair)
            lse_pair = lse if hh == 0 else jnp.where(half, lse, lse_pair)
        o_ref[0, :, sl] = o_pair
        lse_ref[0, :, sl] = lse_pair


def _dilated(q, k, v, bias, dil):
    b, s, w = q.shape
    length = s // dil
    nb = length // DIL_BLOCK
    qv, kv, vv = (t.reshape(b, length, dil * w) for t in (q, k, v))
    cur = lambda bi, r, n: (bi, n, r)
    prev = lambda bi, r, n: (bi, jnp.maximum(n - 1, 0), r)
    blk = (1, DIL_BLOCK, w)
    o, lse = pl.pallas_call(
        _dilated_kernel,
        grid=(b, dil, nb),
        in_specs=[pl.BlockSpec(blk, cur), pl.BlockSpec(blk, prev), pl.BlockSpec(blk, cur),
                  pl.BlockSpec(blk, prev), pl.BlockSpec(blk, cur),
                  pl.BlockSpec(bias.shape, lambda bi, r, n: (0, 0, 0))],
        out_specs=[pl.BlockSpec(blk, cur), pl.BlockSpec(blk, cur)],
        out_shape=[jax.ShapeDtypeStruct(qv.shape, F32)] * 2,
        compiler_params=_cparams("parallel", "parallel", "parallel"),
        name=f"dilated_d{dil}",
    )(qv, kv, kv, vv, vv, bias)
    return o.reshape(b, s, w), lse.reshape(b, s, w)


def _compress_kernel(c_ref, pa_ref, pb_ref, w1a_ref, w1b_ref, b1_ref, w2_ref, o_ref):
    c = c_ref[0].astype(F32)
    xa = (c + pa_ref[...]).astype(BF16)
    xb = (c + pb_ref[...]).astype(BF16)
    ha = jnp.dot(xa, w1a_ref[...], preferred_element_type=F32)
    hb = jnp.dot(xb, w1b_ref[...], preferred_element_type=F32)
    hb_next = jnp.concatenate([hb[1:], jnp.zeros_like(hb[:1])], axis=0)
    hid = jax.nn.gelu(ha + hb_next + b1_ref[...])
    o_ref[0] = jnp.dot(hid.astype(BF16), w2_ref[...], preferred_element_type=F32).astype(o_ref.dtype)


def _compress(c, pos, w1, b1, w2):
    b, nch, _ = c.shape
    half = CMP_LEN // 2
    zero = jnp.zeros((half, HEAD_DIM, CMP_HIDDEN), F32)

    def grouped(wpart):
        g0 = jnp.concatenate([wpart, zero], axis=1).reshape(half * LANES, CMP_HIDDEN)
        g1 = jnp.concatenate([zero, wpart], axis=1).reshape(half * LANES, CMP_HIDDEN)
        return jnp.concatenate([g0, g1], axis=1).astype(BF16)

    w1a, w1b = grouped(w1[:half]), grouped(w1[half:])
    pa = jnp.tile(pos[:half], (1, 2)).reshape(1, half * LANES)
    pb = jnp.tile(pos[half:], (1, 2)).reshape(1, half * LANES)
    b1g = jnp.tile(b1, 2).reshape(1, 2 * CMP_HIDDEN)
    zw = jnp.zeros_like(w2)
    w2g = jnp.concatenate([jnp.concatenate([w2, zw], axis=1),
                           jnp.concatenate([zw, w2], axis=1)], axis=0).astype(BF16)
    full = lambda a: pl.BlockSpec(a.shape, lambda i: (0,) * a.ndim)
    return pl.pallas_call(
        _compress_kernel,
        grid=(b,),
        in_specs=[pl.BlockSpec((1,) + c.shape[1:], lambda i: (i, 0, 0)),
                  full(pa), full(pb), full(w1a), full(w1b), full(b1g), full(w2g)],
        out_specs=pl.BlockSpec((1, nch, LANES), lambda i: (i, 0, 0)),
        out_shape=jax.ShapeDtypeStruct((b, nch, LANES), BF16),
        compiler_params=_cparams("parallel"),
        name="compress",
    )(c, pa, pb, w1a, w1b, b1g, w2g)


def _cmp_attn_kernel(q_ref, gl_ref, kc_ref, vc_ref, wov_ref, o_ref, sb_ref, *, nsel):
    qt = pl.program_id(1)
    tq = q_ref.shape[1]
    nck = kc_ref.shape[1]
    t_row = qt * tq + lax.broadcasted_iota(jnp.int32, (tq, 1), 0)
    blk_end = lax.broadcasted_iota(jnp.int32, (1, nck), 1) * CMP_STRIDE + (CMP_LEN - 1)
    valid = blk_end <= t_row
    gate = jax.nn.sigmoid(gl_ref[0])
    kc = kc_ref[0]
    vc = vc_ref[0]
    pc = [jnp.zeros((tq, nck), F32) for _ in range(B_KV_HEADS)]
    for r in range(B_GROUP):
        sl = slice(r * LANES, (r + 1) * LANES)
        q2 = q_ref[0, :, sl]
        o_slot = None
        for g in range(B_KV_HEADS):
            half = _half_mask(g)
            qm = jnp.where(half, q2, jnp.zeros_like(q2))
            s = lax.dot_general(qm, kc, NT_DIMS, preferred_element_type=F32)
            s = jnp.where(valid, s, -jnp.inf)
            m = jnp.max(s, axis=-1, keepdims=True)
            m = jnp.where(m == -jnp.inf, 0.0, m)
            p = jnp.exp(s - m)
            den = jnp.sum(p, axis=-1, keepdims=True)
            p = p / jnp.maximum(den, 1e-30)
            pc[g] = pc[g] + p
            c = (g * B_GROUP + r) * N_BRANCH
            o = jnp.dot(p.astype(BF16), vc, preferred_element_type=F32) * gate[:, c:c + 1]
            o_slot = o if g == 0 else jnp.where(half, o, o_slot)
        o_ref[0, :, sl] = o_slot

    t_lane = qt * tq + lax.broadcasted_iota(jnp.int32, (1, tq), 1)
    cur = t_lane // SEL_BLOCK
    j = lax.broadcasted_iota(jnp.int32, (nsel, 1), 0)
    forced = (j == 0) | (j == cur) | (j == cur - 1)
    rows = []
    for g in range(B_KV_HEADS):
        imp = lax.dot_general(wov_ref[...], pc[g], NT_DIMS, preferred_element_type=F32,
                              precision=lax.Precision.HIGHEST)
        imp = jnp.where(j > cur, -jnp.inf, jnp.where(forced, jnp.inf, imp))
        rank = jnp.zeros((nsel, tq), jnp.int32)
        for jp in range(nsel):
            row = imp[jp:jp + 1, :]
            rank = rank + jnp.where(j > jp, (row >= imp).astype(jnp.int32), (row > imp).astype(jnp.int32))
        rows.append(jnp.where(rank < SEL_TOPN, 0.0, NOT_SELECTED))
        if nsel < HEAD_DIM:
            rows.append(jnp.zeros((HEAD_DIM - nsel, tq), F32))
    sb_ref[0] = jnp.concatenate(rows, axis=0).T.astype(sb_ref.dtype)


def _cmp_attn(qb, gl, kcmp, vcmp, nsel):
    b, s, w = qb.shape
    nck = kcmp.shape[1]
    ratio = SEL_BLOCK // CMP_STRIDE
    wov = np.zeros((nsel, nck), np.float32)
    for jj in range(nsel):
        for off, wt in zip(range(-1, ratio), CMP_OVERLAP):
            n = ratio * jj + off
            if 0 <= n < nck - 1:
                wov[jj, n] = wt
    wov = jnp.asarray(wov)
    tile = lambda width: pl.BlockSpec((1, Q_TILE, width), lambda bi, qi: (bi, qi, 0))
    whole = lambda a: pl.BlockSpec((1,) + a.shape[1:], lambda bi, qi: (bi, 0, 0))
    return pl.pallas_call(
        functools.partial(_cmp_attn_kernel, nsel=nsel),
        grid=(b, s // Q_TILE),
        in_specs=[tile(w), tile(LANES), whole(kcmp), whole(vcmp),
                  pl.BlockSpec(wov.shape, lambda bi, qi: (0, 0))],
        out_specs=[tile(w), tile(LANES)],
        out_shape=[jax.ShapeDtypeStruct((b, s, w), F32), jax.ShapeDtypeStruct((b, s, LANES), BF16)],
        compiler_params=_cparams("parallel", "parallel"),
        name="cmp_attn",
    )(qb, gl, kcmp, vcmp, wov)


def _selected_kernel(q_ref, sb_ref, gl_ref, kaug_ref, v_ref, bias_ref, o_ref, *, n_bias):
    qt = pl.program_id(1)
    tq = q_ref.shape[1]
    rows = B_GROUP * tq
    gate = jax.nn.sigmoid(gl_ref[0])
    sb = sb_ref[0]
    slots = [None] * B_GROUP
    for g in range(B_KV_HEADS):
        half = _half_mask(g)
        zero = jnp.zeros((tq, LANES), BF16)
        qs = [jnp.where(half, q_ref[0, :, r * LANES:(r + 1) * LANES], zero) for r in range(B_GROUP)]
        sbm = jnp.where(half, sb, zero)
        qaug = jnp.concatenate([jnp.concatenate(qs, axis=0),
                                jnp.concatenate([sbm] * B_GROUP, axis=0)], axis=1)

        def body(kt, carry, g=g, qaug=qaug):
            m, l, acc = carry
            k0 = pl.multiple_of(kt * K_TILE, K_TILE)
            k = kaug_ref[0, pl.ds(k0, K_TILE), :]
            v = v_ref[0, pl.ds(k0, K_TILE), :]
            s = lax.dot_general(qaug, k, NT_DIMS, preferred_element_type=F32)
            s = s + bias_ref[g, jnp.minimum(qt - kt, n_bias - 1)]
            m_new = jnp.maximum(m, jnp.max(s, axis=-1, keepdims=True))
            alpha = jnp.exp(m - m_new)
            p = jnp.exp(s - m_new)
            l = alpha * l + jnp.sum(p, axis=-1, keepdims=True)
            acc = alpha * acc + jnp.dot(p.astype(BF16), v, preferred_element_type=F32)
            return m_new, l, acc

        init = (jnp.full((rows, 1), M_INIT, F32), jnp.zeros((rows, 1), F32), jnp.zeros((rows, LANES), F32))
        _, l, acc = lax.fori_loop(0, qt + 1, body, init)
        o = acc / jnp.maximum(l, 1e-30)
        for r in range(B_GROUP):
            c = (g * B_GROUP + r) * N_BRANCH + 1
            o_r = o[r * tq:(r + 1) * tq] * gate[:, c:c + 1]
            slots[r] = o_r if g == 0 else jnp.where(half, o_r, slots[r])
    for r in range(B_GROUP):
        o_ref[0, :, r * LANES:(r + 1) * LANES] = slots[r]


def _selected(qb, sb, gl, kaug, v, bias):
    b, s, w = qb.shape
    tile = lambda width: pl.BlockSpec((1, Q_TILE, width), lambda bi, qi: (bi, qi, 0))
    whole = lambda a: pl.BlockSpec((1,) + a.shape[1:], lambda bi, qi: (bi, 0, 0))
    return pl.pallas_call(
        functools.partial(_selected_kernel, n_bias=bias.shape[1]),
        grid=(b, s // Q_TILE),
        in_specs=[tile(w), tile(LANES), tile(LANES), whole(kaug), whole(v),
                  pl.BlockSpec(bias.shape, lambda bi, qi: (0, 0, 0, 0))],
        out_specs=tile(w),
        out_shape=jax.ShapeDtypeStruct((b, s, w), F32),
        compiler_params=_cparams("parallel", "parallel"),
        name="selected",
    )(qb, sb, gl, kaug, v, bias)


def _window_kernel(q_ref, gl_ref, k_ref, v_ref, bias_ref, o_ref, *, kw):
    qt = pl.program_id(1)
    tq = q_ref.shape[1]
    gate = jax.nn.sigmoid(gl_ref[0])
    k0 = pl.multiple_of(qt * tq, tq)
    kwin = k_ref[0, pl.ds(k0, kw), :]
    vwin = v_ref[0, pl.ds(k0, kw), :]
    col = lax.broadcasted_iota(jnp.int32, (1, kw), 1)
    real = col >= WIN_LEN - qt * tq
    for r in range(B_GROUP):
        sl = slice(r * LANES, (r + 1) * LANES)
        q2 = q_ref[0, :, sl]
        o_slot = None
        for g in range(B_KV_HEADS):
            half = _half_mask(g)
            qm = jnp.where(half, q2, jnp.zeros_like(q2))
            s = lax.dot_general(qm, kwin, NT_DIMS, preferred_element_type=F32)
            s = jnp.where(real, s + bias_ref[g * B_GROUP + r], -jnp.inf)
            m = jnp.max(s, axis=-1, keepdims=True)
            p = jnp.exp(s - m)
            den = jnp.sum(p, axis=-1, keepdims=True)
            c = (g * B_GROUP + r) * N_BRANCH + 2
            o = jnp.dot(p.astype(BF16), vwin, preferred_element_type=F32) * (gate[:, c:c + 1] / den)
            o_slot = o if g == 0 else jnp.where(half, o, o_slot)
        o_ref[0, :, sl] = o_slot


def _window(qb, gl, kpad, vpad, bias):
    b, s, w = qb.shape
    kw = bias.shape[-1]
    tile = lambda width: pl.BlockSpec((1, Q_TILE, width), lambda bi, qi: (bi, qi, 0))
    whole = lambda a: pl.BlockSpec((1,) + a.shape[1:], lambda bi, qi: (bi, 0, 0))
    return pl.pallas_call(
        functools.partial(_window_kernel, kw=kw),
        grid=(b, s // Q_TILE),
        in_specs=[tile(w), tile(LANES), whole(kpad), whole(vpad),
                  pl.BlockSpec(bias.shape, lambda bi, qi: (0, 0, 0))],
        out_specs=tile(w),
        out_shape=jax.ShapeDtypeStruct((b, s, w), F32),
        compiler_params=_cparams("parallel", "parallel"),
        name="window",
    )(qb, gl, kpad, vpad, bias)


def _outproj_kernel(x_ref, o1_ref, o2_ref, o3_ref, l1_ref, l2_ref, l3_ref,
                    oc_ref, os_ref, ow_ref, wa_ref, wb_ref, h_ref):
    l1, l2, l3 = l1_ref[...], l2_ref[...], l3_ref[...]
    mx = jnp.maximum(jnp.maximum(l1, l2), l3)
    e1, e2, e3 = jnp.exp(l1 - mx), jnp.exp(l2 - mx), jnp.exp(l3 - mx)
    o_a = (e1 * o1_ref[...] + e2 * o2_ref[...] + e3 * o3_ref[...]) / (e1 + e2 + e3)
    o_b = oc_ref[...] + os_ref[...] + ow_ref[...]
    y = jnp.dot(o_a.astype(BF16), wa_ref[...], preferred_element_type=F32)
    y = y + jnp.dot(o_b.astype(BF16), wb_ref[...], preferred_element_type=F32)
    h_ref[...] = x_ref[...] + y


def _outproj(x2, oas, lses, obs, wa, wb, tm):
    n, d = x2.shape
    row = lambda width: pl.BlockSpec((tm, width), lambda i: (i, 0))
    full = lambda a: pl.BlockSpec(a.shape, lambda i: (0, 0))
    ops = list(oas) + list(lses) + list(obs)
    return pl.pallas_call(
        _outproj_kernel,
        grid=(n // tm,),
        in_specs=[row(d)] + [row(o.shape[1]) for o in ops] + [full(wa), full(wb)],
        out_specs=row(d),
        out_shape=jax.ShapeDtypeStruct((n, d), F32),
        compiler_params=_cparams("parallel"),
        name="outproj",
    )(x2, *ops, wa, wb)


def _ffn_kernel(h_ref, g2_ref, wg_ref, wu_ref, wd_ref, gf_ref, o_ref):
    h = h_ref[...]
    y = h * lax.rsqrt(jnp.mean(h * h, axis=-1, keepdims=True) + RMS_EPS)
    hn = (y * g2_ref[...]).astype(BF16)
    a = jnp.dot(hn, wg_ref[...], preferred_element_type=F32)
    u = jnp.dot(hn, wu_ref[...], preferred_element_type=F32)
    act = (jax.nn.silu(a) * u).astype(BF16)
    h2 = h + jnp.dot(act, wd_ref[...], preferred_element_type=F32)
    y2 = h2 * lax.rsqrt(jnp.mean(h2 * h2, axis=-1, keepdims=True) + RMS_EPS)
    o_ref[...] = y2 * gf_ref[...]


def _ffn(h, g2, wg, wu, wd, gf, tm):
    n, d = h.shape
    row = pl.BlockSpec((tm, d), lambda i: (i, 0))
    full = lambda a: pl.BlockSpec(a.shape, lambda i: (0, 0))
    return pl.pallas_call(
        _ffn_kernel,
        grid=(n // tm,),
        in_specs=[row, full(g2), full(wg), full(wu), full(wd), full(gf)],
        out_specs=row,
        out_shape=jax.ShapeDtypeStruct((n, d), F32),
        compiler_params=_cparams("parallel"),
        name="ffn",
    )(h, g2, wg, wu, wd, gf)


def _slot_perm():
    perm = np.zeros(B_WIDTH, np.int64)
    for r in range(B_GROUP):
        for g in range(B_KV_HEADS):
            src = (g * B_GROUP + r) * HEAD_DIM
            dst = r * LANES + g * HEAD_DIM
            perm[dst:dst + HEAD_DIM] = np.arange(src, src + HEAD_DIM)
    return perm


def _layer(h, tab_a, tab_b, norm1_g, w_in, cmp_pos, k_w1, k_b1, k_w2, v_w1, v_b1, v_w2,
           w_out, norm2_g, w_gate, w_up, w_down, norm_f_g):
    b, s, d = h.shape
    n = b * s
    assert s % (DIL_PATTERNS[-1][1] * DIL_BLOCK) == 0 and s % Q_TILE == 0
    perm = _slot_perm()

    cols = np.cumsum([0, A_WIDTH, A_WIDTH, A_WIDTH, B_WIDTH] + [KV_WIDTH] * 6 + [GATE_WIDTH])
    w_aq = w_in[:, cols[0]:cols[1]] * SCALE
    w_bq = (w_in[:, cols[3]:cols[4]] * SCALE)[:, perm]
    w_gl = jnp.pad(w_in[:, cols[10]:cols[11]], ((0, 0), (0, LANES - GATE_WIDTH)))
    w1 = jnp.concatenate([w_aq, w_in[:, cols[1]:cols[3]], w_bq, w_in[:, cols[4]:cols[10]], w_gl], axis=1).astype(BF16)
    widths = (A_WIDTH,) * 3 + (B_WIDTH,) + (KV_WIDTH,) * 6 + (LANES,)
    dtypes = (BF16,) * 10 + (F32,)
    x2 = h.reshape(n, d)
    qa, ka, va, qb, kc, vc, ksl, vsl, kw, vw, gl = _inproj(x2, norm1_g.reshape(1, d), w1, widths, dtypes, tm=512)
    r3 = lambda t: t.reshape(b, s, t.shape[-1])
    qa, ka, va, qb, ksl, vsl, kw, vw, gl = map(r3, (qa, ka, va, qb, ksl, vsl, kw, vw, gl))

    i = np.arange(DIL_BLOCK)[:, None]
    jj = np.arange(2 * DIL_BLOCK)[None, :]
    dist = i + DIL_BLOCK - jj
    o_as, lses = [], []
    for window, dil in DIL_PATTERNS:
        steps = window // dil
        bias = _bias_from_table(tab_a, np.maximum(dist, 0) * dil, (dist >= 0) & (dist <= steps))
        o, lse = _dilated(qa, ka, va, bias, dil)
        o_as.append(o.reshape(n, A_WIDTH))
        lses.append(lse.reshape(n, A_WIDTH))

    nch = s // CMP_STRIDE
    kcmp = _compress(kc.reshape(b, nch, CMP_STRIDE * KV_WIDTH), cmp_pos, k_w1, k_b1, k_w2)
    vcmp = _compress(vc.reshape(b, nch, CMP_STRIDE * KV_WIDTH), cmp_pos, v_w1, v_b1, v_w2)
    nsel = s // SEL_BLOCK
    o_cmp, sb = _cmp_attn(qb, gl, kcmp, vcmp, nsel)

    onehot = (np.arange(s)[:, None] // SEL_BLOCK) == (np.arange(LANES)[None, :] % HEAD_DIM)
    kaug = jnp.concatenate([ksl, jnp.broadcast_to(jnp.asarray(onehot, BF16), (b, s, LANES))], axis=-1)
    d_sat = int(np.nonzero(_t5_bucket_np(np.arange(s)) < REL_BUCKETS - 1)[0].max()) + 1
    first_far = -(-(d_sat + K_TILE - 1) // K_TILE)
    n_bias = min(s // K_TILE, first_far + 1)
    dd = (np.arange(n_bias)[:, None, None] * K_TILE + np.arange(Q_TILE)[None, :, None]
          - np.arange(K_TILE)[None, None, :])
    bias_sl = _bias_from_table(tab_b.reshape(B_HEADS, REL_BUCKETS), dd, dd >= 0)
    bias_sl = bias_sl.reshape(B_KV_HEADS, B_GROUP, n_bias, Q_TILE, K_TILE).transpose(0, 2, 1, 3, 4)
    bias_sl = bias_sl.reshape(B_KV_HEADS, n_bias, B_GROUP * Q_TILE, K_TILE)
    o_slc = _selected(qb, sb, gl, kaug, vsl, bias_sl)

    kwid = WIN_LEN + Q_TILE
    dw = np.arange(Q_TILE)[:, None] + WIN_LEN - np.arange(kwid)[None, :]
    bias_w = _bias_from_table(tab_b.reshape(B_HEADS, REL_BUCKETS), dw, (dw >= 0) & (dw < WIN_LEN))
    pad = lambda t: jnp.pad(t, ((0, 0), (WIN_LEN, 0), (0, 0)))
    o_win = _window(qb, gl, pad(kw), pad(vw), bias_w)

    wa = w_out[:A_WIDTH].astype(BF16)
    wb = w_out[A_WIDTH:][perm].astype(BF16)
    hmid = _outproj(x2, o_as, lses, [t.reshape(n, B_WIDTH) for t in (o_cmp, o_slc, o_win)], wa, wb, tm=256)
    out = _ffn(hmid, norm2_g.reshape(1, d), w_gate.astype(BF16), w_up.astype(BF16), w_down.astype(BF16),
               norm_f_g.reshape(1, d), tm=256)
    return out.reshape(b, s, d)


def kernel(x, norm1_g, w_in, rel_bias, cmp_pos, cmp_k_w1, cmp_k_b1, cmp_k_w2, cmp_v_w1, cmp_v_b1, cmp_v_w2,
           w_out, norm2_g, w_gate, w_up, w_down, norm_f_g):
    assert w_in.shape[0] == 1, "single-layer model"
    tab_a = rel_bias[:, :A_HEADS].T
    tab_b = rel_bias[:, A_HEADS:].T
    return _layer(x, tab_a, tab_b, norm1_g[0], w_in[0], cmp_pos[0], cmp_k_w1[0], cmp_k_b1[0], cmp_k_w2[0],
                  cmp_v_w1[0], cmp_v_b1[0], cmp_v_w2[0], w_out[0], norm2_g[0], w_gate[0], w_up[0], w_down[0],
                  norm_f_g)
```

```python
import functools
import math

import numpy as np
import jax
import jax.numpy as jnp
from jax import lax
from jax.experimental import pallas as pl
from jax.experimental.pallas import tpu as pltpu

F32 = jnp.float32
BF16 = jnp.bfloat16

HEAD_DIM = 64
LANES = 128
A_HEADS = 8
DIL_PATTERNS = ((128, 1), (512, 4), (2048, 16))
DIL_BLOCK = 128
B_HEADS = 8
B_KV_HEADS = 2
B_GROUP = B_HEADS // B_KV_HEADS
CMP_LEN = 32
CMP_STRIDE = 16
CMP_HIDDEN = 128
SEL_BLOCK = 64
SEL_TOPN = 16
CMP_OVERLAP = (1.0, 2.0, 2.0, 2.0, 1.0)
WIN_LEN = 512
WIN_BLOCK = 128
N_BRANCH = 3
REL_BUCKETS = 32
REL_MAX_DIST = 2048
RMS_EPS = 1e-6
A_WIDTH = A_HEADS * HEAD_DIM
B_WIDTH = B_HEADS * HEAD_DIM
KV_WIDTH = B_KV_HEADS * HEAD_DIM
GATE_WIDTH = B_HEADS * N_BRANCH
SCALE = HEAD_DIM ** -0.5

Q_TILE = 128
K_TILE = 256
NOT_SELECTED = -(2.0 ** 100)
M_INIT = -1e30
VMEM_LIMIT = 56 * 1024 * 1024

NT_DIMS = (((1,), (1,)), ((), ()))


def _cparams(*sem):
    return pltpu.CompilerParams(dimension_semantics=sem, vmem_limit_bytes=VMEM_LIMIT)


def _t5_bucket_np(dist):
    max_exact = REL_BUCKETS // 2
    d = np.asarray(dist)
    df = np.maximum(d, 1).astype(np.float32)
    large = max_exact + (np.log(df / np.float32(max_exact)) / np.float32(math.log(REL_MAX_DIST / max_exact))
                         * np.float32(REL_BUCKETS - max_exact)).astype(np.int32)
    large = np.minimum(large, REL_BUCKETS - 1)
    return np.where(d < max_exact, d, large).astype(np.int32)


def _bias_kernel(tab_ref, idx_ref, o_ref):
    head = pl.program_id(0) * pl.num_programs(2) + pl.program_id(2)
    idx = idx_ref[0]
    acc = jnp.full(idx.shape, -jnp.inf, F32)
    for bucket in range(REL_BUCKETS):
        acc = jnp.where(idx == bucket, tab_ref[head * REL_BUCKETS + bucket], acc)
    o_ref[0, 0, 0] = acc


def _bias_tiles(tab, dist, valid, inner):
    h = tab.shape[0]
    t, r, c = dist.shape
    idx = jnp.asarray(np.where(valid, _t5_bucket_np(np.maximum(dist, 0)), -1).astype(np.int32))
    return pl.pallas_call(
        _bias_kernel,
        grid=(h // inner, t, inner),
        in_specs=[pl.BlockSpec(memory_space=pltpu.SMEM),
                  pl.BlockSpec((1, r, c), lambda a, ti, bi: (ti, 0, 0))],
        out_specs=pl.BlockSpec((1, 1, 1, r, c), lambda a, ti, bi: (a, ti, bi, 0, 0)),
        out_shape=jax.ShapeDtypeStruct((h // inner, t, inner, r, c), F32),
        compiler_params=_cparams("parallel", "parallel", "parallel"),
        name="bias_tiles",
    )(tab.reshape(-1), idx)


def _half_mask(g):
    lane = lax.broadcasted_iota(jnp.int32, (1, LANES), 1)
    return (lane >= HEAD_DIM) if g else (lane < HEAD_DIM)


def _inproj_kernel(x_ref, g_ref, w_ref, *out_refs, widths):
    x = x_ref[...]
    y = x * lax.rsqrt(jnp.mean(x * x, axis=-1, keepdims=True) + RMS_EPS)
    xn = (y * g_ref[...]).astype(BF16)
    start = 0
    for o_ref, w in zip(out_refs, widths):
        r = jnp.dot(xn, w_ref[:, start:start + w], preferred_element_type=F32)
        o_ref[...] = r.astype(o_ref.dtype)
        start += w


def _inproj(x2, g, w, widths, dtypes, tm):
    n, d = x2.shape
    out_shape = [jax.ShapeDtypeStruct((n, w_), dt) for w_, dt in zip(widths, dtypes)]
    return pl.pallas_call(
        functools.partial(_inproj_kernel, widths=widths),
        grid=(n // tm,),
        in_specs=[pl.BlockSpec((tm, d), lambda i: (i, 0)),
                  pl.BlockSpec((1, d), lambda i: (0, 0)),
                  pl.BlockSpec(w.shape, lambda i: (0, 0))],
        out_specs=[pl.BlockSpec((tm, w_), lambda i: (i, 0)) for w_ in widths],
        out_shape=out_shape,
        compiler_params=_cparams("parallel"),
        name="inproj",
    )(x2, g, w)


def _dilated_kernel(q_ref, kp_ref, kc_ref, vp_ref, vc_ref, bias_ref, o_ref, lse_ref):
    n = pl.program_id(2)
    col = lax.broadcasted_iota(jnp.int32, (1, 2 * DIL_BLOCK), 1)
    keep = (col >= DIL_BLOCK) | (n > 0)
    for pair in range(A_HEADS // 2):
        sl = slice(pair * LANES, (pair + 1) * LANES)
        q2 = q_ref[0, :, sl]
        kcat = jnp.concatenate([kp_ref[0, :, sl], kc_ref[0, :, sl]], axis=0)
        vcat = jnp.concatenate([vp_ref[0, :, sl], vc_ref[0, :, sl]], axis=0)
        o_pair = None
        lse_pair = None
        for hh in range(2):
            half = _half_mask(hh)
            qm = jnp.where(half, q2, jnp.zeros_like(q2))
            s = lax.dot_general(qm, kcat, NT_DIMS, preferred_element_type=F32)
            s = jnp.where(keep, s + bias_ref[2 * pair + hh], -jnp.inf)
            m = jnp.max(s, axis=-1, keepdims=True)
            p = jnp.exp(s - m)
            den = jnp.sum(p, axis=-1, keepdims=True)
            o = jnp.dot(p.astype(BF16), vcat, preferred_element_type=F32) / den
            lse = jnp.broadcast_to(m + jnp.log(den), o.shape)
            o_pair = o if hh == 0 else jnp.where(half, o, o_pair)
            lse_pair = lse if hh == 0 else jnp.where(half, lse, lse_pair)
        o_ref[0, :, sl] = o_pair
        lse_ref[0, :, sl] = lse_pair


def _dilated(q, k, v, bias, dil):
    b, s, w = q.shape
    length = s // dil
    nb = length // DIL_BLOCK
    qv, kv, vv = (t.reshape(b, length, dil * w) for t in (q, k, v))
    cur = lambda bi, r, n: (bi, n, r)
    prev = lambda bi, r, n: (bi, jnp.maximum(n - 1, 0), r)
    blk = (1, DIL_BLOCK, w)
    o, lse = pl.pallas_call(
        _dilated_kernel,
        grid=(b, dil, nb),
        in_specs=[pl.BlockSpec(blk, cur), pl.BlockSpec(blk, prev), pl.BlockSpec(blk, cur),
                  pl.BlockSpec(blk, prev), pl.BlockSpec(blk, cur),
                  pl.BlockSpec(bias.shape, lambda bi, r, n: (0, 0, 0))],
        out_specs=[pl.BlockSpec(blk, cur), pl.BlockSpec(blk, cur)],
        out_shape=[jax.ShapeDtypeStruct(qv.shape, F32)] * 2,
        compiler_params=_cparams("parallel", "parallel", "parallel"),
        name=f"dilated_d{dil}",
    )(qv, kv, kv, vv, vv, bias)
    return o.reshape(b, s, w), lse.reshape(b, s, w)


def _compress_kernel(c_ref, pa_ref, pb_ref, w1a_ref, w1b_ref, b1_ref, w2_ref, o_ref):
    c = c_ref[0].astype(F32)
    xa = (c + pa_ref[...]).astype(BF16)
    xb = (c + pb_ref[...]).astype(BF16)
    ha = jnp.dot(xa, w1a_ref[...], preferred_element_type=F32)
    hb = jnp.dot(xb, w1b_ref[...], preferred_element_type=F32)
    hb_next = jnp.concatenate([hb[1:], jnp.zeros_like(hb[:1])], axis=0)
    hid = jax.nn.gelu(ha + hb_next + b1_ref[...])
    o_ref[0] = jnp.dot(hid.astype(BF16), w2_ref[...], preferred_element_type=F32).astype(o_ref.dtype)


def _compress(c, pos, w1, b1, w2):
    b, nch, _ = c.shape
    half = CMP_LEN // 2
    zero = jnp.zeros((half, HEAD_DIM, CMP_HIDDEN), F32)

    def grouped(wpart):
        g0 = jnp.concatenate([wpart, zero], axis=1).reshape(half * LANES, CMP_HIDDEN)
        g1 = jnp.concatenate([zero, wpart], axis=1).reshape(half * LANES, CMP_HIDDEN)
        return jnp.concatenate([g0, g1], axis=1).astype(BF16)

    w1a, w1b = grouped(w1[:half]), grouped(w1[half:])
    pa = jnp.tile(pos[:half], (1, 2)).reshape(1, half * LANES)
    pb = jnp.tile(pos[half:], (1, 2)).reshape(1, half * LANES)
    b1g = jnp.tile(b1, 2).reshape(1, 2 * CMP_HIDDEN)
    zw = jnp.zeros_like(w2)
    w2g = jnp.concatenate([jnp.concatenate([w2, zw], axis=1),
                           jnp.concatenate([zw, w2], axis=1)], axis=0).astype(BF16)
    full = lambda a: pl.BlockSpec(a.shape, lambda i: (0,) * a.ndim)
    return pl.pallas_call(
        _compress_kernel,
        grid=(b,),
        in_specs=[pl.BlockSpec((1,) + c.shape[1:], lambda i: (i, 0, 0)),
                  full(pa), full(pb), full(w1a), full(w1b), full(b1g), full(w2g)],
        out_specs=pl.BlockSpec((1, nch, LANES), lambda i: (i, 0, 0)),
        out_shape=jax.ShapeDtypeStruct((b, nch, LANES), BF16),
        compiler_params=_cparams("parallel"),
        name="compress",
    )(c, pa, pb, w1a, w1b, b1g, w2g)


def _cmp_attn_kernel(q_ref, gl_ref, kc_ref, vc_ref, wov_ref, o_ref, sb_ref, *, nsel):
    qt = pl.program_id(1)
    tq = q_ref.shape[1]
    nck = kc_ref.shape[1]
    t_row = qt * tq + lax.broadcasted_iota(jnp.int32, (tq, 1), 0)
    blk_end = lax.broadcasted_iota(jnp.int32, (1, nck), 1) * CMP_STRIDE + (CMP_LEN - 1)
    valid = blk_end <= t_row
    gate = jax.nn.sigmoid(gl_ref[0])
    kc = kc_ref[0]
    vc = vc_ref[0]
    pc = [jnp.zeros((tq, nck), F32) for _ in range(B_KV_HEADS)]
    for r in range(B_GROUP):
        sl = slice(r * LANES, (r + 1) * LANES)
        q2 = q_ref[0, :, sl]
        o_slot = None
        for g in range(B_KV_HEADS):
            half = _half_mask(g)
            qm = jnp.where(half, q2, jnp.zeros_like(q2))
            s = lax.dot_general(qm, kc, NT_DIMS, preferred_element_type=F32)
            s = jnp.where(valid, s, -jnp.inf)
            m = jnp.max(s, axis=-1, keepdims=True)
            m = jnp.where(m == -jnp.inf, 0.0, m)
            p = jnp.exp(s - m)
            den = jnp.sum(p, axis=-1, keepdims=True)
            p = p / jnp.maximum(den, 1e-30)
            pc[g] = pc[g] + p
            c = (g * B_GROUP + r) * N_BRANCH
            o = jnp.dot(p.astype(BF16), vc, preferred_element_type=F32) * gate[:, c:c + 1]
            o_slot = o if g == 0 else jnp.where(half, o, o_slot)
        o_ref[0, :, sl] = o_slot

    t_lane = qt * tq + lax.broadcasted_iota(jnp.int32, (1, tq), 1)
    cur = t_lane // SEL_BLOCK
    j = lax.broadcasted_iota(jnp.int32, (nsel, 1), 0)
    forced = (j == 0) | (j == cur) | (j == cur - 1)
    rows = []
    for g in reversed(range(B_KV_HEADS)):
        imp = lax.dot_general(wov_ref[...], pc[g], NT_DIMS, preferred_element_type=F32,
                              precision=lax.Precision.HIGHEST)
        imp = jnp.where(j > cur, -jnp.inf, jnp.where(forced, jnp.inf, imp))
        rank = jnp.zeros((nsel, tq), jnp.int32)
        for jp in range(nsel):
            row = imp[jp:jp + 1, :]
            rank = rank + jnp.where(j > jp, (row >= imp).astype(jnp.int32), (row > imp).astype(jnp.int32))
        rows.append(jnp.where(rank < SEL_TOPN, 0.0, NOT_SELECTED))
        if nsel < HEAD_DIM:
            rows.append(jnp.zeros((HEAD_DIM - nsel, tq), F32))
    sb_ref[0] = jnp.concatenate(rows, axis=0).T.astype(sb_ref.dtype)


def _cmp_attn(qb, gl, kcmp, vcmp, nsel):
    b, s, w = qb.shape
    nck = kcmp.shape[1]
    ratio = SEL_BLOCK // CMP_STRIDE
    wov = np.zeros((nsel, nck), np.float32)
    for jj in range(nsel):
        for off, wt in zip(range(-1, ratio), CMP_OVERLAP):
            n = ratio * jj + off
            if 0 <= n < nck - 1:
                wov[jj, n] = wt
    wov = jnp.asarray(wov)
    tile = lambda width: pl.BlockSpec((1, Q_TILE, width), lambda bi, qi: (bi, qi, 0))
    whole = lambda a: pl.BlockSpec((1,) + a.shape[1:], lambda bi, qi: (bi, 0, 0))
    return pl.pallas_call(
        functools.partial(_cmp_attn_kernel, nsel=nsel),
        grid=(b, s // Q_TILE),
        in_specs=[tile(w), tile(LANES), whole(kcmp), whole(vcmp),
                  pl.BlockSpec(wov.shape, lambda bi, qi: (0, 0))],
        out_specs=[tile(w), tile(LANES)],
        out_shape=[jax.ShapeDtypeStruct((b, s, w), F32), jax.ShapeDtypeStruct((b, s, LANES), BF16)],
        compiler_params=_cparams("parallel", "parallel"),
        name="cmp_attn",
    )(qb, gl, kcmp, vcmp, wov)


def _selected_kernel(q_ref, sb_ref, gl_ref, kaug_ref, v_ref, bias_ref, o_ref, acc_ref, m_ref, *, n_bias):
    qt = pl.program_id(1)
    tq = q_ref.shape[1]
    gate = jax.nn.sigmoid(gl_ref[0])
    sb = sb_ref[0]
    n_steps = (qt * tq + tq + K_TILE - 1) // K_TILE
    tiles_per_k = K_TILE // tq
    halves = [_half_mask(g) for g in range(B_KV_HEADS)]
    qaug = [jnp.concatenate([jnp.where(halves[g], q_ref[0, :, r * LANES:(r + 1) * LANES], sb)
                             for r in range(B_GROUP)], axis=0) for g in range(B_KV_HEADS)]
    acc_ref[...] = jnp.zeros_like(acc_ref)
    m_ref[...] = jnp.full_like(m_ref, M_INIT)

    def body(kt, carry):
        k0 = pl.multiple_of(kt * K_TILE, K_TILE)
        v2 = v_ref[0, pl.ds(k0, K_TILE), :]
        bi = jnp.minimum(qt - kt * tiles_per_k, n_bias - 1)
        for g in range(B_KV_HEADS):
            k = kaug_ref[0, pl.ds(k0, K_TILE), g * LANES:(g + 1) * LANES]
            v = jnp.where(halves[g], v2, jnp.ones((K_TILE, LANES), BF16))
            s = lax.dot_general(qaug[g], k, NT_DIMS, preferred_element_type=F32)
            s = s + bias_ref[g, bi]
            m_old = m_ref[g]
            m_new = jnp.maximum(m_old, jnp.max(s, axis=-1, keepdims=True))
            alpha = jnp.exp(m_old - m_new)
            p = jnp.exp(s - jnp.tile(m_new, (1, K_TILE // LANES)))
            acc_ref[g] = alpha * acc_ref[g] + jnp.dot(p.astype(BF16), v, preferred_element_type=F32)
            m_ref[g] = m_new
        return carry

    lax.fori_loop(0, n_steps, body, 0)
    slots = [None] * B_GROUP
    for g in range(B_KV_HEADS):
        acc = acc_ref[g]
        o = acc / jnp.maximum(pltpu.roll(acc, HEAD_DIM, axis=1), 1e-30)
        for r in range(B_GROUP):
            c = (g * B_GROUP + r) * N_BRANCH + 1
            o_r = o[r * tq:(r + 1) * tq] * gate[:, c:c + 1]
            slots[r] = o_r if g == 0 else jnp.where(halves[g], o_r, slots[r])
    for r in range(B_GROUP):
        o_ref[0, :, r * LANES:(r + 1) * LANES] = slots[r]


def _selected(qb, sb, gl, kaug, v, bias):
    b, s, w = qb.shape
    tile = lambda width: pl.BlockSpec((1, Q_TILE, width), lambda bi, qi: (bi, qi, 0))
    whole = lambda a: pl.BlockSpec((1,) + a.shape[1:], lambda bi, qi: (bi, 0, 0))
    rows = B_GROUP * Q_TILE
    return pl.pallas_call(
        functools.partial(_selected_kernel, n_bias=bias.shape[1]),
        grid=(b, s // Q_TILE),
        in_specs=[tile(w), tile(LANES), tile(LANES), whole(kaug), whole(v),
                  pl.BlockSpec(bias.shape, lambda bi, qi: (0, 0, 0, 0), pipeline_mode=pl.Buffered(1))],
        out_specs=tile(w),
        out_shape=jax.ShapeDtypeStruct((b, s, w), F32),
        scratch_shapes=[pltpu.VMEM((B_KV_HEADS, rows, LANES), F32), pltpu.VMEM((B_KV_HEADS, rows, LANES), F32)],
        compiler_params=_cparams("parallel", "parallel"),
        name="selected",
    )(qb, sb, gl, kaug, v, bias)


def _window_kernel(q_ref, gl_ref, k_ref, v_ref, bias_ref, o_ref, *, kw):
    qt = pl.program_id(1)
    tq = q_ref.shape[1]
    gate = jax.nn.sigmoid(gl_ref[0])
    k0 = pl.multiple_of(qt * tq, tq)
    kwin = k_ref[0, pl.ds(k0, kw), :]
    vwin = v_ref[0, pl.ds(k0, kw), :]
    col = lax.broadcasted_iota(jnp.int32, (1, kw), 1)
    real = col >= WIN_LEN - qt * tq
    for r in range(B_GROUP):
        sl = slice(r * LANES, (r + 1) * LANES)
        q2 = q_ref[0, :, sl]
        o_slot = None
        for g in range(B_KV_HEADS):
            half = _half_mask(g)
            qm = jnp.where(half, q2, jnp.zeros_like(q2))
            s = lax.dot_general(qm, kwin, NT_DIMS, preferred_element_type=F32)
            s = jnp.where(real, s + bias_ref[g * B_GROUP + r], -jnp.inf)
            m = jnp.max(s, axis=-1, keepdims=True)
            p = jnp.exp(s - m)
            den = jnp.sum(p, axis=-1, keepdims=True)
            c = (g * B_GROUP + r) * N_BRANCH + 2
            o = jnp.dot(p.astype(BF16), vwin, preferred_element_type=F32) * (gate[:, c:c + 1] / den)
            o_slot = o if g == 0 else jnp.where(half, o, o_slot)
        o_ref[0, :, sl] = o_slot


def _window(qb, gl, kpad, vpad, bias):
    b, s, w = qb.shape
    kw = bias.shape[-1]
    tile = lambda width: pl.BlockSpec((1, Q_TILE, width), lambda bi, qi: (bi, qi, 0))
    whole = lambda a: pl.BlockSpec((1,) + a.shape[1:], lambda bi, qi: (bi, 0, 0))
    return pl.pallas_call(
        functools.partial(_window_kernel, kw=kw),
        grid=(b, s // Q_TILE),
        in_specs=[tile(w), tile(LANES), whole(kpad), whole(vpad),
                  pl.BlockSpec(bias.shape, lambda bi, qi: (0, 0, 0))],
        out_specs=tile(w),
        out_shape=jax.ShapeDtypeStruct((b, s, w), F32),
        compiler_params=_cparams("parallel", "parallel"),
        name="window",
    )(qb, gl, kpad, vpad, bias)


def _outproj_kernel(x_ref, o1_ref, o2_ref, o3_ref, l1_ref, l2_ref, l3_ref,
                    oc_ref, os_ref, ow_ref, wa_ref, wb_ref, h_ref):
    l1, l2, l3 = l1_ref[...], l2_ref[...], l3_ref[...]
    mx = jnp.maximum(jnp.maximum(l1, l2), l3)
    e1, e2, e3 = jnp.exp(l1 - mx), jnp.exp(l2 - mx), jnp.exp(l3 - mx)
    o_a = (e1 * o1_ref[...] + e2 * o2_ref[...] + e3 * o3_ref[...]) / (e1 + e2 + e3)
    o_b = oc_ref[...] + os_ref[...] + ow_ref[...]
    y = jnp.dot(o_a.astype(BF16), wa_ref[...], preferred_element_type=F32)
    y = y + jnp.dot(o_b.astype(BF16), wb_ref[...], preferred_element_type=F32)
    h_ref[...] = x_ref[...] + y


def _outproj(x2, oas, lses, obs, wa, wb, tm):
    n, d = x2.shape
    row = lambda width: pl.BlockSpec((tm, width), lambda i: (i, 0))
    full = lambda a: pl.BlockSpec(a.shape, lambda i: (0, 0))
    ops = list(oas) + list(lses) + list(obs)
    return pl.pallas_call(
        _outproj_kernel,
        grid=(n // tm,),
        in_specs=[row(d)] + [row(o.shape[1]) for o in ops] + [full(wa), full(wb)],
        out_specs=row(d),
        out_shape=jax.ShapeDtypeStruct((n, d), F32),
        compiler_params=_cparams("parallel"),
        name="outproj",
    )(x2, *ops, wa, wb)


def _ffn_kernel(h_ref, g2_ref, wg_ref, wu_ref, wd_ref, gf_ref, o_ref):
    h = h_ref[...]
    y = h * lax.rsqrt(jnp.mean(h * h, axis=-1, keepdims=True) + RMS_EPS)
    hn = (y * g2_ref[...]).astype(BF16)
    a = jnp.dot(hn, wg_ref[...], preferred_element_type=F32)
    u = jnp.dot(hn, wu_ref[...], preferred_element_type=F32)
    act = (jax.nn.silu(a) * u).astype(BF16)
    h2 = h + jnp.dot(act, wd_ref[...], preferred_element_type=F32)
    y2 = h2 * lax.rsqrt(jnp.mean(h2 * h2, axis=-1, keepdims=True) + RMS_EPS)
    o_ref[...] = y2 * gf_ref[...]


def _ffn(h, g2, wg, wu, wd, gf, tm):
    n, d = h.shape
    row = pl.BlockSpec((tm, d), lambda i: (i, 0))
    full = lambda a: pl.BlockSpec(a.shape, lambda i: (0, 0))
    return pl.pallas_call(
        _ffn_kernel,
        grid=(n // tm,),
        in_specs=[row, full(g2), full(wg), full(wu), full(wd), full(gf)],
        out_specs=row,
        out_shape=jax.ShapeDtypeStruct((n, d), F32),
        compiler_params=_cparams("parallel"),
        name="ffn",
    )(h, g2, wg, wu, wd, gf)


def _slot_perm():
    perm = np.zeros(B_WIDTH, np.int64)
    for r in range(B_GROUP):
        for g in range(B_KV_HEADS):
            src = (g * B_GROUP + r) * HEAD_DIM
            dst = r * LANES + g * HEAD_DIM
            perm[dst:dst + HEAD_DIM] = np.arange(src, src + HEAD_DIM)
    return perm


def _layer(h, tab_a, tab_b, norm1_g, w_in, cmp_pos, k_w1, k_b1, k_w2, v_w1, v_b1, v_w2,
           w_out, norm2_g, w_gate, w_up, w_down, norm_f_g):
    b, s, d = h.shape
    n = b * s
    assert s % (DIL_PATTERNS[-1][1] * DIL_BLOCK) == 0 and s % Q_TILE == 0
    perm = _slot_perm()

    cols = np.cumsum([0, A_WIDTH, A_WIDTH, A_WIDTH, B_WIDTH] + [KV_WIDTH] * 6 + [GATE_WIDTH])
    w_aq = w_in[:, cols[0]:cols[1]] * SCALE
    w_bq = (w_in[:, cols[3]:cols[4]] * SCALE)[:, perm]
    w_gl = jnp.pad(w_in[:, cols[10]:cols[11]], ((0, 0), (0, LANES - GATE_WIDTH)))
    w1 = jnp.concatenate([w_aq, w_in[:, cols[1]:cols[3]], w_bq, w_in[:, cols[4]:cols[10]], w_gl], axis=1).astype(BF16)
    widths = (A_WIDTH,) * 3 + (B_WIDTH,) + (KV_WIDTH,) * 6 + (LANES,)
    dtypes = (BF16,) * 10 + (F32,)
    x2 = h.reshape(n, d)
    qa, ka, va, qb, kc, vc, ksl, vsl, kw, vw, gl = _inproj(x2, norm1_g.reshape(1, d), w1, widths, dtypes, tm=512)
    r3 = lambda t: t.reshape(b, s, t.shape[-1])
    qa, ka, va, qb, ksl, vsl, kw, vw, gl = map(r3, (qa, ka, va, qb, ksl, vsl, kw, vw, gl))

    i = np.arange(DIL_BLOCK)[:, None]
    jj = np.arange(2 * DIL_BLOCK)[None, :]
    dist = i + DIL_BLOCK - jj
    o_as, lses = [], []
    for window, dil in DIL_PATTERNS:
        steps = window // dil
        bias = _bias_tiles(tab_a, (np.maximum(dist, 0) * dil)[None], ((dist >= 0) & (dist <= steps))[None], inner=1)
        o, lse = _dilated(qa, ka, va, bias.reshape((A_HEADS,) + dist.shape), dil)
        o_as.append(o.reshape(n, A_WIDTH))
        lses.append(lse.reshape(n, A_WIDTH))

    nch = s // CMP_STRIDE
    kcmp = _compress(kc.reshape(b, nch, CMP_STRIDE * KV_WIDTH), cmp_pos, k_w1, k_b1, k_w2)
    vcmp = _compress(vc.reshape(b, nch, CMP_STRIDE * KV_WIDTH), cmp_pos, v_w1, v_b1, v_w2)
    nsel = s // SEL_BLOCK
    o_cmp, sb = _cmp_attn(qb, gl, kcmp, vcmp, nsel)

    onehot = (np.arange(s)[:, None] // SEL_BLOCK) == np.arange(HEAD_DIM)[None, :]
    onehot = jnp.broadcast_to(jnp.asarray(onehot, BF16), (b, s, HEAD_DIM))
    kaug = jnp.concatenate([ksl[..., :HEAD_DIM], onehot, onehot, ksl[..., HEAD_DIM:]], axis=-1)
    d_sat = int(np.nonzero(_t5_bucket_np(np.arange(s)) < REL_BUCKETS - 1)[0].max()) + 1
    first_far = -(-(d_sat + K_TILE - 1) // Q_TILE)
    n_bias = min(s // Q_TILE, first_far + 1)
    dd = (np.arange(n_bias)[:, None, None] * Q_TILE + np.arange(Q_TILE)[None, :, None]
          - np.arange(K_TILE)[None, None, :])
    bias_sl = _bias_tiles(tab_b, dd, dd >= 0, inner=B_GROUP)
    bias_sl = bias_sl.reshape(B_KV_HEADS, n_bias, B_GROUP * Q_TILE, K_TILE)
    o_slc = _selected(qb, sb, gl, kaug, vsl, bias_sl)

    kwid = WIN_LEN + Q_TILE
    dw = np.arange(Q_TILE)[:, None] + WIN_LEN - np.arange(kwid)[None, :]
    bias_w = _bias_tiles(tab_b, dw[None], ((dw >= 0) & (dw < WIN_LEN))[None], inner=1).reshape((B_HEADS,) + dw.shape)
    pad = lambda t: jnp.pad(t, ((0, 0), (WIN_LEN, 0), (0, 0)))
    o_win = _window(qb, gl, pad(kw), pad(vw), bias_w)

    wa = w_out[:A_WIDTH].astype(BF16)
    wb = w_out[A_WIDTH:][perm].astype(BF16)
    hmid = _outproj(x2, o_as, lses, [t.reshape(n, B_WIDTH) for t in (o_cmp, o_slc, o_win)], wa, wb, tm=256)
    out = _ffn(hmid, norm2_g.reshape(1, d), w_gate.astype(BF16), w_up.astype(BF16), w_down.astype(BF16),
               norm_f_g.reshape(1, d), tm=256)
    return out.reshape(b, s, d)


def kernel(x, norm1_g, w_in, rel_bias, cmp_pos, cmp_k_w1, cmp_k_b1, cmp_k_w2, cmp_v_w1, cmp_v_b1, cmp_v_w2,
           w_out, norm2_g, w_gate, w_up, w_down, norm_f_g):
    assert w_in.shape[0] == 1, "single-layer model"
    tab_a = rel_bias[:, :A_HEADS].T
    tab_b = rel_bias[:, A_HEADS:].T
    return _layer(x, tab_a, tab_b, norm1_g[0], w_in[0], cmp_pos[0], cmp_k_w1[0], cmp_k_b1[0], cmp_k_w2[0],
                  cmp_v_w1[0], cmp_v_b1[0], cmp_v_w2[0], w_out[0], norm2_g[0], w_gate[0], w_up[0], w_down[0],
                  norm_f_g)
```

```python
import functools
import math

import numpy as np
import jax
import jax.numpy as jnp
from jax import lax
from jax.experimental import pallas as pl
from jax.experimental.pallas import tpu as pltpu

F32 = jnp.float32
BF16 = jnp.bfloat16

HEAD_DIM = 64
LANES = 128
A_HEADS = 8
DIL_PATTERNS = ((128, 1), (512, 4), (2048, 16))
DIL_BLOCK = 128
B_HEADS = 8
B_KV_HEADS = 2
B_GROUP = B_HEADS // B_KV_HEADS
CMP_LEN = 32
CMP_STRIDE = 16
CMP_HIDDEN = 128
SEL_BLOCK = 64
SEL_TOPN = 16
CMP_OVERLAP = (1.0, 2.0, 2.0, 2.0, 1.0)
WIN_LEN = 512
WIN_BLOCK = 128
N_BRANCH = 3
REL_BUCKETS = 32
REL_MAX_DIST = 2048
RMS_EPS = 1e-6
A_WIDTH = A_HEADS * HEAD_DIM
B_WIDTH = B_HEADS * HEAD_DIM
KV_WIDTH = B_KV_HEADS * HEAD_DIM
GATE_WIDTH = B_HEADS * N_BRANCH
SCALE = HEAD_DIM ** -0.5

SPAN = DIL_PATTERNS[-1][1] * DIL_BLOCK
MIX_UNROLL = 4
Q_TILE = 128
K_TILE = 256
NOT_SELECTED = -(2.0 ** 100)
M_INIT = -1e30
VMEM_LIMIT = 56 * 1024 * 1024

NT_DIMS = (((1,), (1,)), ((), ()))


def _cparams(*sem):
    return pltpu.CompilerParams(dimension_semantics=sem, vmem_limit_bytes=VMEM_LIMIT)


def _t5_bucket_np(dist):
    max_exact = REL_BUCKETS // 2
    d = np.asarray(dist)
    df = np.maximum(d, 1).astype(np.float32)
    large = max_exact + (np.log(df / np.float32(max_exact)) / np.float32(math.log(REL_MAX_DIST / max_exact))
                         * np.float32(REL_BUCKETS - max_exact)).astype(np.int32)
    large = np.minimum(large, REL_BUCKETS - 1)
    return np.where(d < max_exact, d, large).astype(np.int32)


def _bias_kernel(tab_ref, idx_ref, o_ref):
    head = pl.program_id(0) * pl.num_programs(2) + pl.program_id(2)
    idx = idx_ref[0]
    acc = jnp.full(idx.shape, -jnp.inf, F32)
    for bucket in range(REL_BUCKETS):
        acc = jnp.where(idx == bucket, tab_ref[head * REL_BUCKETS + bucket], acc)
    o_ref[0, 0, 0] = acc


def _bias_tiles(tab, dist, valid, inner):
    h = tab.shape[0]
    t, r, c = dist.shape
    idx = jnp.asarray(np.where(valid, _t5_bucket_np(np.maximum(dist, 0)), -1).astype(np.int32))
    return pl.pallas_call(
        _bias_kernel,
        grid=(h // inner, t, inner),
        in_specs=[pl.BlockSpec(memory_space=pltpu.SMEM),
                  pl.BlockSpec((1, r, c), lambda a, ti, bi: (ti, 0, 0))],
        out_specs=pl.BlockSpec((1, 1, 1, r, c), lambda a, ti, bi: (a, ti, bi, 0, 0)),
        out_shape=jax.ShapeDtypeStruct((h // inner, t, inner, r, c), F32),
        compiler_params=_cparams("parallel", "parallel", "parallel"),
        name="bias_tiles",
    )(tab.reshape(-1), idx)


def _half_mask(g):
    lane = lax.broadcasted_iota(jnp.int32, (1, LANES), 1)
    return (lane >= HEAD_DIM) if g else (lane < HEAD_DIM)


def _inproj_kernel(x_ref, g_ref, w_ref, *out_refs, widths):
    x = x_ref[...]
    y = x * lax.rsqrt(jnp.mean(x * x, axis=-1, keepdims=True) + RMS_EPS)
    xn = (y * g_ref[...]).astype(BF16)
    start = 0
    for o_ref, w in zip(out_refs, widths):
        r = jnp.dot(xn, w_ref[:, start:start + w], preferred_element_type=F32)
        if len(o_ref.shape) == 3:
            for p in range(o_ref.shape[0]):
                o_ref[p] = r[:, p * LANES:(p + 1) * LANES].astype(o_ref.dtype)
        else:
            o_ref[...] = r.astype(o_ref.dtype)
        start += w


def _inproj(x2, g, w, widths, dtypes, slot_major, tm):
    n, d = x2.shape
    shape = lambda w_, sm: (w_ // LANES, n, LANES) if sm else (n, w_)
    spec = lambda w_, sm: (pl.BlockSpec((w_ // LANES, tm, LANES), lambda i: (0, i, 0)) if sm
                           else pl.BlockSpec((tm, w_), lambda i: (i, 0)))
    out_shape = [jax.ShapeDtypeStruct(shape(w_, sm), dt) for w_, dt, sm in zip(widths, dtypes, slot_major)]
    return pl.pallas_call(
        functools.partial(_inproj_kernel, widths=widths),
        grid=(n // tm,),
        in_specs=[pl.BlockSpec((tm, d), lambda i: (i, 0)),
                  pl.BlockSpec((1, d), lambda i: (0, 0)),
                  pl.BlockSpec(w.shape, lambda i: (0, 0))],
        out_specs=[spec(w_, sm) for w_, sm in zip(widths, slot_major)],
        out_shape=out_shape,
        compiler_params=_cparams("parallel"),
        name="inproj",
    )(x2, g, w)


def _mixer_a_kernel(q_ref, kp_ref, kc_ref, vp_ref, vc_ref, bias_ref, o_ref, qf, kf, vf, acc_scr, m_scr):
    sb = pl.program_id(2)
    qf[...] = q_ref[0, 0].astype(F32)
    kf[:SPAN] = kp_ref[0, 0].astype(F32)
    kf[SPAN:] = kc_ref[0, 0].astype(F32)
    vf[:SPAN] = vp_ref[0, 0].astype(F32)
    vf[SPAN:] = vc_ref[0, 0].astype(F32)
    halves = [_half_mask(hh) for hh in range(2)]
    col = lax.broadcasted_iota(jnp.int32, (1, 2 * DIL_BLOCK), 1)
    ones = jnp.ones((2 * DIL_BLOCK, LANES), BF16)
    n_blocks = SPAN // DIL_BLOCK
    for p, (_, dil) in enumerate(DIL_PATTERNS):
        step = dil * DIL_BLOCK
        per_class = SPAN // step

        def body(grp, carry, p=p, dil=dil, step=step, per_class=per_class):
            for u in range(MIX_UNROLL):
                idx = grp * MIX_UNROLL + u
                r, n = idx // per_class, idx % per_class
                q0 = pl.multiple_of(n * step, step) + r
                rows_q = pl.ds(q0, DIL_BLOCK, stride=dil)
                rows_k = pl.ds(q0 + SPAN - step, 2 * DIL_BLOCK, stride=dil)
                q2 = qf[rows_q, :].astype(BF16)
                kcat = kf[rows_k, :].astype(BF16)
                vcat = vf[rows_k, :].astype(BF16)
                keep = (col >= DIL_BLOCK) | (sb * per_class + n > 0)
                for hh in range(2):
                    qm = jnp.where(halves[hh], q2, jnp.zeros_like(q2))
                    s = lax.dot_general(qm, kcat, NT_DIMS, preferred_element_type=F32)
                    s = jnp.where(keep, s + bias_ref[hh, p], -jnp.inf)
                    m_blk = jnp.max(s, axis=-1, keepdims=True)
                    pe = jnp.exp(s - m_blk)
                    pv = jnp.dot(pe.astype(BF16), jnp.where(halves[hh], vcat, ones), preferred_element_type=F32)
                    acc_scr[p, hh, rows_q, :] = pv
                    m_scr[p, hh, rows_q, :] = jnp.broadcast_to(m_blk, pv.shape)
            return carry

        lax.fori_loop(0, n_blocks // MIX_UNROLL, body, 0)

    def finish(c, carry):
        rows = pl.ds(pl.multiple_of(c * DIL_BLOCK, DIL_BLOCK), DIL_BLOCK)
        outs = []
        for hh in range(2):
            ms = [m_scr[p, hh, rows, :] for p in range(len(DIL_PATTERNS))]
            m_all = functools.reduce(jnp.maximum, ms)
            tot = sum(jnp.exp(m - m_all) * acc_scr[p, hh, rows, :] for p, m in enumerate(ms))
            outs.append(tot / pltpu.roll(tot, HEAD_DIM, axis=1))
        o_ref[0, 0, rows, :] = jnp.where(halves[0], outs[0], outs[1])
        return carry

    lax.fori_loop(0, n_blocks, finish, 0)


def _mixer_a(q, k, v, bias):
    npair, b, s, _ = q.shape
    blk = (1, 1, SPAN, LANES)
    cur = lambda bi, pi, si: (pi, bi, si, 0)
    prev = lambda bi, pi, si: (pi, bi, jnp.maximum(si - 1, 0), 0)
    return pl.pallas_call(
        _mixer_a_kernel,
        grid=(b, npair, s // SPAN),
        in_specs=[pl.BlockSpec(blk, cur), pl.BlockSpec(blk, prev), pl.BlockSpec(blk, cur),
                  pl.BlockSpec(blk, prev), pl.BlockSpec(blk, cur),
                  pl.BlockSpec((2,) + bias.shape[1:], lambda bi, pi, si: (pi, 0, 0, 0))],
        out_specs=pl.BlockSpec(blk, cur),
        out_shape=jax.ShapeDtypeStruct(q.shape, F32),
        scratch_shapes=[pltpu.VMEM((SPAN, LANES), F32), pltpu.VMEM((2 * SPAN, LANES), F32),
                        pltpu.VMEM((2 * SPAN, LANES), F32),
                        pltpu.VMEM((len(DIL_PATTERNS), 2, SPAN, LANES), F32),
                        pltpu.VMEM((len(DIL_PATTERNS), 2, SPAN, LANES), F32)],
        compiler_params=_cparams("parallel", "parallel", "parallel"),
        name="mixer_a",
    )(q, k, k, v, v, bias)


def _compress_kernel(c_ref, pa_ref, pb_ref, w1a_ref, w1b_ref, b1_ref, w2_ref, o_ref):
    c = c_ref[0].astype(F32)
    xa = (c + pa_ref[...]).astype(BF16)
    xb = (c + pb_ref[...]).astype(BF16)
    ha = jnp.dot(xa, w1a_ref[...], preferred_element_type=F32)
    hb = jnp.dot(xb, w1b_ref[...], preferred_element_type=F32)
    hb_next = jnp.concatenate([hb[1:], jnp.zeros_like(hb[:1])], axis=0)
    hid = jax.nn.gelu(ha + hb_next + b1_ref[...])
    o_ref[0] = jnp.dot(hid.astype(BF16), w2_ref[...], preferred_element_type=F32).astype(o_ref.dtype)


def _compress(c, pos, w1, b1, w2):
    b, nch, _ = c.shape
    half = CMP_LEN // 2
    zero = jnp.zeros((half, HEAD_DIM, CMP_HIDDEN), F32)

    def grouped(wpart):
        g0 = jnp.concatenate([wpart, zero], axis=1).reshape(half * LANES, CMP_HIDDEN)
        g1 = jnp.concatenate([zero, wpart], axis=1).reshape(half * LANES, CMP_HIDDEN)
        return jnp.concatenate([g0, g1], axis=1).astype(BF16)

    w1a, w1b = grouped(w1[:half]), grouped(w1[half:])
    pa = jnp.tile(pos[:half], (1, 2)).reshape(1, half * LANES)
    pb = jnp.tile(pos[half:], (1, 2)).reshape(1, half * LANES)
    b1g = jnp.tile(b1, 2).reshape(1, 2 * CMP_HIDDEN)
    zw = jnp.zeros_like(w2)
    w2g = jnp.concatenate([jnp.concatenate([w2, zw], axis=1),
                           jnp.concatenate([zw, w2], axis=1)], axis=0).astype(BF16)
    full = lambda a: pl.BlockSpec(a.shape, lambda i: (0,) * a.ndim)
    return pl.pallas_call(
        _compress_kernel,
        grid=(b,),
        in_specs=[pl.BlockSpec((1,) + c.shape[1:], lambda i: (i, 0, 0)),
                  full(pa), full(pb), full(w1a), full(w1b), full(b1g), full(w2g)],
        out_specs=pl.BlockSpec((1, nch, LANES), lambda i: (i, 0, 0)),
        out_shape=jax.ShapeDtypeStruct((b, nch, LANES), BF16),
        compiler_params=_cparams("parallel"),
        name="compress",
    )(c, pa, pb, w1a, w1b, b1g, w2g)


def _cmp_attn_kernel(q_ref, gl_ref, kc_ref, vc_ref, wov_ref, o_ref, sb_ref, *, nsel):
    qt = pl.program_id(1)
    tq = q_ref.shape[1]
    nck = kc_ref.shape[1]
    t_row = qt * tq + lax.broadcasted_iota(jnp.int32, (tq, 1), 0)
    blk_end = lax.broadcasted_iota(jnp.int32, (1, nck), 1) * CMP_STRIDE + (CMP_LEN - 1)
    valid = blk_end <= t_row
    gate = jax.nn.sigmoid(gl_ref[0])
    kc = kc_ref[0]
    vc = vc_ref[0]
    pc = [jnp.zeros((tq, nck), F32) for _ in range(B_KV_HEADS)]
    for r in range(B_GROUP):
        sl = slice(r * LANES, (r + 1) * LANES)
        q2 = q_ref[0, :, sl]
        o_slot = None
        for g in range(B_KV_HEADS):
            half = _half_mask(g)
            qm = jnp.where(half, q2, jnp.zeros_like(q2))
            s = lax.dot_general(qm, kc, NT_DIMS, preferred_element_type=F32)
            s = jnp.where(valid, s, -jnp.inf)
            m = jnp.max(s, axis=-1, keepdims=True)
            m = jnp.where(m == -jnp.inf, 0.0, m)
            p = jnp.exp(s - m)
            den = jnp.sum(p, axis=-1, keepdims=True)
            p = p / jnp.maximum(den, 1e-30)
            pc[g] = pc[g] + p
            c = (g * B_GROUP + r) * N_BRANCH
            o = jnp.dot(p.astype(BF16), vc, preferred_element_type=F32) * gate[:, c:c + 1]
            o_slot = o if g == 0 else jnp.where(half, o, o_slot)
        o_ref[0, :, sl] = o_slot

    t_lane = qt * tq + lax.broadcasted_iota(jnp.int32, (1, tq), 1)
    cur = t_lane // SEL_BLOCK
    j = lax.broadcasted_iota(jnp.int32, (nsel, 1), 0)
    forced = (j == 0) | (j == cur) | (j == cur - 1)
    rows = []
    for g in reversed(range(B_KV_HEADS)):
        imp = lax.dot_general(wov_ref[...], pc[g], NT_DIMS, preferred_element_type=F32,
                              precision=lax.Precision.HIGHEST)
        imp = jnp.where(j > cur, -jnp.inf, jnp.where(forced, jnp.inf, imp))
        rank = jnp.zeros((nsel, tq), jnp.int32)
        for jp in range(nsel):
            row = imp[jp:jp + 1, :]
            rank = rank + jnp.where(j > jp, (row >= imp).astype(jnp.int32), (row > imp).astype(jnp.int32))
        rows.append(jnp.where(rank < SEL_TOPN, 0.0, NOT_SELECTED))
        if nsel < HEAD_DIM:
            rows.append(jnp.zeros((HEAD_DIM - nsel, tq), F32))
    sb_ref[0] = jnp.concatenate(rows, axis=0).T.astype(sb_ref.dtype)


def _cmp_attn(qb, gl, kcmp, vcmp, nsel):
    b, s, w = qb.shape
    nck = kcmp.shape[1]
    ratio = SEL_BLOCK // CMP_STRIDE
    wov = np.zeros((nsel, nck), np.float32)
    for jj in range(nsel):
        for off, wt in zip(range(-1, ratio), CMP_OVERLAP):
            n = ratio * jj + off
            if 0 <= n < nck - 1:
                wov[jj, n] = wt
    wov = jnp.asarray(wov)
    tile = lambda width: pl.BlockSpec((1, Q_TILE, width), lambda bi, qi: (bi, qi, 0))
    whole = lambda a: pl.BlockSpec((1,) + a.shape[1:], lambda bi, qi: (bi, 0, 0))
    return pl.pallas_call(
        functools.partial(_cmp_attn_kernel, nsel=nsel),
        grid=(b, s // Q_TILE),
        in_specs=[tile(w), tile(LANES), whole(kcmp), whole(vcmp),
                  pl.BlockSpec(wov.shape, lambda bi, qi: (0, 0))],
        out_specs=[tile(w), tile(LANES)],
        out_shape=[jax.ShapeDtypeStruct((b, s, w), F32), jax.ShapeDtypeStruct((b, s, LANES), BF16)],
        compiler_params=_cparams("parallel", "parallel"),
        name="cmp_attn",
    )(qb, gl, kcmp, vcmp, wov)


def _selected_kernel(q_ref, sb_ref, gl_ref, kaug_ref, v_ref, bias_ref, o_ref, acc_ref, m_ref, *, n_bias):
    qt = pl.program_id(1)
    tq = q_ref.shape[1]
    gate = jax.nn.sigmoid(gl_ref[0])
    sb = sb_ref[0]
    n_steps = (qt * tq + tq + K_TILE - 1) // K_TILE
    tiles_per_k = K_TILE // tq
    halves = [_half_mask(g) for g in range(B_KV_HEADS)]
    qaug = [jnp.concatenate([jnp.where(halves[g], q_ref[0, :, r * LANES:(r + 1) * LANES], sb)
                             for r in range(B_GROUP)], axis=0) for g in range(B_KV_HEADS)]
    acc_ref[...] = jnp.zeros_like(acc_ref)
    m_ref[...] = jnp.full_like(m_ref, M_INIT)

    def body(kt, carry):
        k0 = pl.multiple_of(kt * K_TILE, K_TILE)
        v2 = v_ref[0, pl.ds(k0, K_TILE), :]
        bi = jnp.minimum(qt - kt * tiles_per_k, n_bias - 1)
        for g in range(B_KV_HEADS):
            k = kaug_ref[0, pl.ds(k0, K_TILE), g * LANES:(g + 1) * LANES]
            v = jnp.where(halves[g], v2, jnp.ones((K_TILE, LANES), BF16))
            s = lax.dot_general(qaug[g], k, NT_DIMS, preferred_element_type=F32)
            s = s + bias_ref[g, bi]
            m_old = m_ref[g]
            m_new = jnp.maximum(m_old, jnp.max(s, axis=-1, keepdims=True))
            alpha = jnp.exp(m_old - m_new)
            p = jnp.exp(s - jnp.tile(m_new, (1, K_TILE // LANES)))
            acc_ref[g] = alpha * acc_ref[g] + jnp.dot(p.astype(BF16), v, preferred_element_type=F32)
            m_ref[g] = m_new
        return carry

    lax.fori_loop(0, n_steps, body, 0)
    slots = [None] * B_GROUP
    for g in range(B_KV_HEADS):
        acc = acc_ref[g]
        o = acc / jnp.maximum(pltpu.roll(acc, HEAD_DIM, axis=1), 1e-30)
        for r in range(B_GROUP):
            c = (g * B_GROUP + r) * N_BRANCH + 1
            o_r = o[r * tq:(r + 1) * tq] * gate[:, c:c + 1]
            slots[r] = o_r if g == 0 else jnp.where(halves[g], o_r, slots[r])
    for r in range(B_GROUP):
        o_ref[0, :, r * LANES:(r + 1) * LANES] = slots[r]


def _selected(qb, sb, gl, kaug, v, bias):
    b, s, w = qb.shape
    tile = lambda width: pl.BlockSpec((1, Q_TILE, width), lambda bi, qi: (bi, qi, 0))
    whole = lambda a: pl.BlockSpec((1,) + a.shape[1:], lambda bi, qi: (bi, 0, 0))
    rows = B_GROUP * Q_TILE
    return pl.pallas_call(
        functools.partial(_selected_kernel, n_bias=bias.shape[1]),
        grid=(b, s // Q_TILE),
        in_specs=[tile(w), tile(LANES), tile(LANES), whole(kaug), whole(v),
                  pl.BlockSpec(bias.shape, lambda bi, qi: (0, 0, 0, 0), pipeline_mode=pl.Buffered(1))],
        out_specs=tile(w),
        out_shape=jax.ShapeDtypeStruct((b, s, w), F32),
        scratch_shapes=[pltpu.VMEM((B_KV_HEADS, rows, LANES), F32), pltpu.VMEM((B_KV_HEADS, rows, LANES), F32)],
        compiler_params=_cparams("parallel", "parallel"),
        name="selected",
    )(qb, sb, gl, kaug, v, bias)


def _window_kernel(q_ref, gl_ref, k_ref, v_ref, bias_ref, o_ref, *, kw):
    qt = pl.program_id(1)
    tq = q_ref.shape[1]
    gate = jax.nn.sigmoid(gl_ref[0])
    k0 = pl.multiple_of(qt * tq, tq)
    kwin = k_ref[0, pl.ds(k0, kw), :]
    vwin = v_ref[0, pl.ds(k0, kw), :]
    col = lax.broadcasted_iota(jnp.int32, (1, kw), 1)
    real = col >= WIN_LEN - qt * tq
    zero = jnp.zeros((tq, LANES), BF16)
    slots = [None] * B_GROUP
    for g in range(B_KV_HEADS):
        half = _half_mask(g)
        qg = jnp.concatenate([jnp.where(half, q_ref[0, :, r * LANES:(r + 1) * LANES], zero)
                              for r in range(B_GROUP)], axis=0)
        s = lax.dot_general(qg, kwin, NT_DIMS, preferred_element_type=F32)
        s = jnp.where(real, s + bias_ref[g], -jnp.inf)
        p = jnp.exp(s - jnp.max(s, axis=-1, keepdims=True))
        acc = jnp.dot(p.astype(BF16), jnp.where(half, vwin, jnp.ones_like(vwin)), preferred_element_type=F32)
        o = acc / jnp.maximum(pltpu.roll(acc, HEAD_DIM, axis=1), 1e-30)
        for r in range(B_GROUP):
            c = (g * B_GROUP + r) * N_BRANCH + 2
            o_r = o[r * tq:(r + 1) * tq] * gate[:, c:c + 1]
            slots[r] = o_r if g == 0 else jnp.where(half, o_r, slots[r])
    for r in range(B_GROUP):
        o_ref[0, :, r * LANES:(r + 1) * LANES] = slots[r]


def _window(qb, gl, kpad, vpad, bias):
    b, s, w = qb.shape
    kw = bias.shape[-1]
    tile = lambda width: pl.BlockSpec((1, Q_TILE, width), lambda bi, qi: (bi, qi, 0))
    whole = lambda a: pl.BlockSpec((1,) + a.shape[1:], lambda bi, qi: (bi, 0, 0))
    return pl.pallas_call(
        functools.partial(_window_kernel, kw=kw),
        grid=(b, s // Q_TILE),
        in_specs=[tile(w), tile(LANES), whole(kpad), whole(vpad),
                  pl.BlockSpec(bias.shape, lambda bi, qi: (0, 0, 0))],
        out_specs=tile(w),
        out_shape=jax.ShapeDtypeStruct((b, s, w), F32),
        compiler_params=_cparams("parallel", "parallel"),
        name="window",
    )(qb, gl, kpad, vpad, bias)


def _outproj_kernel(x_ref, oa_ref, oc_ref, os_ref, ow_ref, wa_ref, wb_ref, h_ref):
    o_a = jnp.concatenate([oa_ref[p] for p in range(oa_ref.shape[0])], axis=1)
    o_b = oc_ref[...] + os_ref[...] + ow_ref[...]
    y = jnp.dot(o_a.astype(BF16), wa_ref[...], preferred_element_type=F32)
    y = y + jnp.dot(o_b.astype(BF16), wb_ref[...], preferred_element_type=F32)
    h_ref[...] = x_ref[...] + y


def _outproj(x2, oa, obs, wa, wb, tm):
    n, d = x2.shape
    row = lambda width: pl.BlockSpec((tm, width), lambda i: (i, 0))
    full = lambda a: pl.BlockSpec(a.shape, lambda i: (0, 0))
    return pl.pallas_call(
        _outproj_kernel,
        grid=(n // tm,),
        in_specs=[row(d), pl.BlockSpec((oa.shape[0], tm, LANES), lambda i: (0, i, 0))]
                 + [row(o.shape[1]) for o in obs] + [full(wa), full(wb)],
        out_specs=row(d),
        out_shape=jax.ShapeDtypeStruct((n, d), F32),
        compiler_params=_cparams("parallel"),
        name="outproj",
    )(x2, oa, *obs, wa, wb)


def _ffn_kernel(h_ref, g2_ref, wg_ref, wu_ref, wd_ref, gf_ref, o_ref):
    h = h_ref[...]
    y = h * lax.rsqrt(jnp.mean(h * h, axis=-1, keepdims=True) + RMS_EPS)
    hn = (y * g2_ref[...]).astype(BF16)
    a = jnp.dot(hn, wg_ref[...], preferred_element_type=F32)
    u = jnp.dot(hn, wu_ref[...], preferred_element_type=F32)
    act = (jax.nn.silu(a) * u).astype(BF16)
    h2 = h + jnp.dot(act, wd_ref[...], preferred_element_type=F32)
    y2 = h2 * lax.rsqrt(jnp.mean(h2 * h2, axis=-1, keepdims=True) + RMS_EPS)
    o_ref[...] = y2 * gf_ref[...]


def _ffn(h, g2, wg, wu, wd, gf, tm):
    n, d = h.shape
    row = pl.BlockSpec((tm, d), lambda i: (i, 0))
    full = lambda a: pl.BlockSpec(a.shape, lambda i: (0, 0))
    return pl.pallas_call(
        _ffn_kernel,
        grid=(n // tm,),
        in_specs=[row, full(g2), full(wg), full(wu), full(wd), full(gf)],
        out_specs=row,
        out_shape=jax.ShapeDtypeStruct((n, d), F32),
        compiler_params=_cparams("parallel"),
        name="ffn",
    )(h, g2, wg, wu, wd, gf)


def _slot_perm():
    perm = np.zeros(B_WIDTH, np.int64)
    for r in range(B_GROUP):
        for g in range(B_KV_HEADS):
            src = (g * B_GROUP + r) * HEAD_DIM
            dst = r * LANES + g * HEAD_DIM
            perm[dst:dst + HEAD_DIM] = np.arange(src, src + HEAD_DIM)
    return perm


def _layer(h, tab_a, tab_b, norm1_g, w_in, cmp_pos, k_w1, k_b1, k_w2, v_w1, v_b1, v_w2,
           w_out, norm2_g, w_gate, w_up, w_down, norm_f_g):
    b, s, d = h.shape
    n = b * s
    assert s % SPAN == 0 and s % K_TILE == 0
    perm = _slot_perm()

    cols = np.cumsum([0, A_WIDTH, A_WIDTH, A_WIDTH, B_WIDTH] + [KV_WIDTH] * 6 + [GATE_WIDTH])
    w_aq = w_in[:, cols[0]:cols[1]] * SCALE
    w_bq = (w_in[:, cols[3]:cols[4]] * SCALE)[:, perm]
    w_gl = jnp.pad(w_in[:, cols[10]:cols[11]], ((0, 0), (0, LANES - GATE_WIDTH)))
    w1 = jnp.concatenate([w_aq, w_in[:, cols[1]:cols[3]], w_bq, w_in[:, cols[4]:cols[10]], w_gl], axis=1).astype(BF16)
    widths = (A_WIDTH,) * 3 + (B_WIDTH,) + (KV_WIDTH,) * 6 + (LANES,)
    dtypes = (BF16,) * 10 + (F32,)
    slot_major = (True,) * 3 + (False,) * 8
    x2 = h.reshape(n, d)
    qa, ka, va, qb, kc, vc, ksl, vsl, kw, vw, gl = _inproj(x2, norm1_g.reshape(1, d), w1, widths, dtypes,
                                                          slot_major, tm=512)
    r3 = lambda t: t.reshape(b, s, t.shape[-1])
    qb, ksl, vsl, kw, vw, gl = map(r3, (qb, ksl, vsl, kw, vw, gl))

    i = np.arange(DIL_BLOCK)[:, None]
    jj = np.arange(2 * DIL_BLOCK)[None, :]
    dist = i + DIL_BLOCK - jj
    tok_dist = np.stack([np.maximum(dist, 0) * dil for _, dil in DIL_PATTERNS])
    in_window = np.stack([(dist >= 0) & (dist <= window // dil) for window, dil in DIL_PATTERNS])
    bias_a = _bias_tiles(tab_a, tok_dist, in_window, inner=1).reshape((A_HEADS,) + tok_dist.shape)
    r4 = lambda t: t.reshape(A_HEADS // 2, b, s, LANES)
    o_a = _mixer_a(r4(qa), r4(ka), r4(va), bias_a).reshape(A_HEADS // 2, n, LANES)

    nch = s // CMP_STRIDE
    kcmp = _compress(kc.reshape(b, nch, CMP_STRIDE * KV_WIDTH), cmp_pos, k_w1, k_b1, k_w2)
    vcmp = _compress(vc.reshape(b, nch, CMP_STRIDE * KV_WIDTH), cmp_pos, v_w1, v_b1, v_w2)
    nsel = s // SEL_BLOCK
    o_cmp, sb = _cmp_attn(qb, gl, kcmp, vcmp, nsel)

    onehot = (np.arange(s)[:, None] // SEL_BLOCK) == np.arange(HEAD_DIM)[None, :]
    onehot = jnp.broadcast_to(jnp.asarray(onehot, BF16), (b, s, HEAD_DIM))
    kaug = jnp.concatenate([ksl[..., :HEAD_DIM], onehot, onehot, ksl[..., HEAD_DIM:]], axis=-1)
    d_sat = int(np.nonzero(_t5_bucket_np(np.arange(s)) < REL_BUCKETS - 1)[0].max()) + 1
    first_far = -(-(d_sat + K_TILE - 1) // Q_TILE)
    n_bias = min(s // Q_TILE, first_far + 1)
    dd = (np.arange(n_bias)[:, None, None] * Q_TILE + np.arange(Q_TILE)[None, :, None]
          - np.arange(K_TILE)[None, None, :])
    bias_sl = _bias_tiles(tab_b, dd, dd >= 0, inner=B_GROUP)
    bias_sl = bias_sl.reshape(B_KV_HEADS, n_bias, B_GROUP * Q_TILE, K_TILE)
    o_slc = _selected(qb, sb, gl, kaug, vsl, bias_sl)

    kwid = WIN_LEN + Q_TILE
    dw = np.arange(Q_TILE)[:, None] + WIN_LEN - np.arange(kwid)[None, :]
    bias_w = _bias_tiles(tab_b, dw[None], ((dw >= 0) & (dw < WIN_LEN))[None], inner=1)
    bias_w = bias_w.reshape(B_KV_HEADS, B_GROUP * Q_TILE, kwid)
    pad = lambda t: jnp.pad(t, ((0, 0), (WIN_LEN, 0), (0, 0)))
    o_win = _window(qb, gl, pad(kw), pad(vw), bias_w)

    wa = w_out[:A_WIDTH].astype(BF16)
    wb = w_out[A_WIDTH:][perm].astype(BF16)
    hmid = _outproj(x2, o_a, [t.reshape(n, B_WIDTH) for t in (o_cmp, o_slc, o_win)], wa, wb, tm=256)
    out = _ffn(hmid, norm2_g.reshape(1, d), w_gate.astype(BF16), w_up.astype(BF16), w_down.astype(BF16),
               norm_f_g.reshape(1, d), tm=256)
    return out.reshape(b, s, d)


def kernel(x, norm1_g, w_in, rel_bias, cmp_pos, cmp_k_w1, cmp_k_b1, cmp_k_w2, cmp_v_w1, cmp_v_b1, cmp_v_w2,
           w_out, norm2_g, w_gate, w_up, w_down, norm_f_g):
    assert w_in.shape[0] == 1, "single-layer model"
    tab_a = rel_bias[:, :A_HEADS].T
    tab_b = rel_bias[:, A_HEADS:].T
    return _layer(x, tab_a, tab_b, norm1_g[0], w_in[0], cmp_pos[0], cmp_k_w1[0], cmp_k_b1[0], cmp_k_w2[0],
                  cmp_v_w1[0], cmp_v_b1[0], cmp_v_w2[0], w_out[0], norm2_g[0], w_gate[0], w_up[0], w_down[0],
                  norm_f_g)
```

```python
import functools
import math

import numpy as np
import jax
import jax.numpy as jnp
from jax import lax
from jax.experimental import pallas as pl
from jax.experimental.pallas import tpu as pltpu

F32 = jnp.float32
BF16 = jnp.bfloat16

HEAD_DIM = 64
LANES = 128
A_HEADS = 8
DIL_PATTERNS = ((128, 1), (512, 4), (2048, 16))
DIL_BLOCK = 128
B_HEADS = 8
B_KV_HEADS = 2
B_GROUP = B_HEADS // B_KV_HEADS
CMP_LEN = 32
CMP_STRIDE = 16
CMP_HIDDEN = 128
SEL_BLOCK = 64
SEL_TOPN = 16
CMP_OVERLAP = (1.0, 2.0, 2.0, 2.0, 1.0)
WIN_LEN = 512
WIN_BLOCK = 128
N_BRANCH = 3
REL_BUCKETS = 32
REL_MAX_DIST = 2048
RMS_EPS = 1e-6
A_WIDTH = A_HEADS * HEAD_DIM
B_WIDTH = B_HEADS * HEAD_DIM
KV_WIDTH = B_KV_HEADS * HEAD_DIM
GATE_WIDTH = B_HEADS * N_BRANCH
SCALE = HEAD_DIM ** -0.5

SPAN = DIL_PATTERNS[-1][1] * DIL_BLOCK
MIX_UNROLL = 4
RANK_CHUNK = 16
Q_TILE = 128
K_TILE = 512
BIAS_TILE = 256
NOT_SELECTED = -(2.0 ** 100)
M_INIT = -1e30
VMEM_LIMIT = 56 * 1024 * 1024

NT_DIMS = (((1,), (1,)), ((), ()))


def _cparams(*sem):
    return pltpu.CompilerParams(dimension_semantics=sem, vmem_limit_bytes=VMEM_LIMIT)


def _t5_bucket_np(dist):
    max_exact = REL_BUCKETS // 2
    d = np.asarray(dist)
    df = np.maximum(d, 1).astype(np.float32)
    large = max_exact + (np.log(df / np.float32(max_exact)) / np.float32(math.log(REL_MAX_DIST / max_exact))
                         * np.float32(REL_BUCKETS - max_exact)).astype(np.int32)
    large = np.minimum(large, REL_BUCKETS - 1)
    return np.where(d < max_exact, d, large).astype(np.int32)


def _bias_kernel(tab_ref, idx_ref, o_ref):
    head = pl.program_id(0) * pl.num_programs(2) + pl.program_id(2)
    idx = idx_ref[0]
    acc = jnp.full(idx.shape, -jnp.inf, F32)
    for bucket in range(REL_BUCKETS):
        acc = jnp.where(idx == bucket, tab_ref[head * REL_BUCKETS + bucket], acc)
    o_ref[0, 0, 0] = acc


def _bias_tiles(tab, dist, valid, inner):
    h = tab.shape[0]
    t, r, c = dist.shape
    idx = jnp.asarray(np.where(valid, _t5_bucket_np(np.maximum(dist, 0)), -1).astype(np.int32))
    return pl.pallas_call(
        _bias_kernel,
        grid=(h // inner, t, inner),
        in_specs=[pl.BlockSpec(memory_space=pltpu.SMEM),
                  pl.BlockSpec((1, r, c), lambda a, ti, bi: (ti, 0, 0))],
        out_specs=pl.BlockSpec((1, 1, 1, r, c), lambda a, ti, bi: (a, ti, bi, 0, 0)),
        out_shape=jax.ShapeDtypeStruct((h // inner, t, inner, r, c), F32),
        compiler_params=_cparams("parallel", "parallel", "parallel"),
        name="bias_tiles",
    )(tab.reshape(-1), idx)


def _half_mask(g):
    lane = lax.broadcasted_iota(jnp.int32, (1, LANES), 1)
    return (lane >= HEAD_DIM) if g else (lane < HEAD_DIM)


def _inproj_kernel(x_ref, g_ref, w_ref, *out_refs, widths):
    x = x_ref[...]
    y = x * lax.rsqrt(jnp.mean(x * x, axis=-1, keepdims=True) + RMS_EPS)
    xn = (y * g_ref[...]).astype(BF16)
    start = 0
    for o_ref, w in zip(out_refs, widths):
        r = jnp.dot(xn, w_ref[:, start:start + w], preferred_element_type=F32)
        if len(o_ref.shape) == 3:
            for p in range(o_ref.shape[0]):
                o_ref[p] = r[:, p * LANES:(p + 1) * LANES].astype(o_ref.dtype)
        else:
            o_ref[...] = r.astype(o_ref.dtype)
        start += w


def _inproj(x2, g, w, widths, dtypes, slot_major, tm):
    n, d = x2.shape
    shape = lambda w_, sm: (w_ // LANES, n, LANES) if sm else (n, w_)
    spec = lambda w_, sm: (pl.BlockSpec((w_ // LANES, tm, LANES), lambda i: (0, i, 0)) if sm
                           else pl.BlockSpec((tm, w_), lambda i: (i, 0)))
    out_shape = [jax.ShapeDtypeStruct(shape(w_, sm), dt) for w_, dt, sm in zip(widths, dtypes, slot_major)]
    return pl.pallas_call(
        functools.partial(_inproj_kernel, widths=widths),
        grid=(n // tm,),
        in_specs=[pl.BlockSpec((tm, d), lambda i: (i, 0)),
                  pl.BlockSpec((1, d), lambda i: (0, 0)),
                  pl.BlockSpec(w.shape, lambda i: (0, 0))],
        out_specs=[spec(w_, sm) for w_, sm in zip(widths, slot_major)],
        out_shape=out_shape,
        compiler_params=_cparams("parallel"),
        name="inproj",
    )(x2, g, w)


def _mixer_a_kernel(q_ref, kp_ref, kc_ref, vp_ref, vc_ref, bias_ref, o_ref, qf, kf, vf, acc_scr, m_scr):
    sb = pl.program_id(2)
    qf[...] = q_ref[0, 0].astype(F32)
    kf[:SPAN] = kp_ref[0, 0].astype(F32)
    kf[SPAN:] = kc_ref[0, 0].astype(F32)
    vf[:SPAN] = vp_ref[0, 0].astype(F32)
    vf[SPAN:] = vc_ref[0, 0].astype(F32)
    halves = [_half_mask(hh) for hh in range(2)]
    col = lax.broadcasted_iota(jnp.int32, (1, 2 * DIL_BLOCK), 1)
    ones = jnp.ones((2 * DIL_BLOCK, LANES), BF16)
    n_blocks = SPAN // DIL_BLOCK
    for p, (_, dil) in enumerate(DIL_PATTERNS):
        step = dil * DIL_BLOCK
        per_class = SPAN // step

        def body(grp, carry, p=p, dil=dil, step=step, per_class=per_class):
            for u in range(MIX_UNROLL):
                idx = grp * MIX_UNROLL + u
                r, n = idx // per_class, idx % per_class
                q0 = pl.multiple_of(n * step, step) + r
                rows_q = pl.ds(q0, DIL_BLOCK, stride=dil)
                rows_k = pl.ds(q0 + SPAN - step, 2 * DIL_BLOCK, stride=dil)
                q2 = qf[rows_q, :].astype(BF16)
                kcat = kf[rows_k, :].astype(BF16)
                vcat = vf[rows_k, :].astype(BF16)
                keep = (col >= DIL_BLOCK) | (sb * per_class + n > 0)
                for hh in range(2):
                    qm = jnp.where(halves[hh], q2, jnp.zeros_like(q2))
                    s = lax.dot_general(qm, kcat, NT_DIMS, preferred_element_type=F32)
                    s = jnp.where(keep, s + bias_ref[hh, p], -jnp.inf)
                    m_blk = jnp.max(s, axis=-1, keepdims=True)
                    pe = jnp.exp(s - m_blk)
                    pv = jnp.dot(pe.astype(BF16), jnp.where(halves[hh], vcat, ones), preferred_element_type=F32)
                    acc_scr[p, hh, rows_q, :] = pv
                    m_scr[p, hh, rows_q, :] = jnp.broadcast_to(m_blk, pv.shape)
            return carry

        lax.fori_loop(0, n_blocks // MIX_UNROLL, body, 0)

    def finish(c, carry):
        rows = pl.ds(pl.multiple_of(c * DIL_BLOCK, DIL_BLOCK), DIL_BLOCK)
        outs = []
        for hh in range(2):
            ms = [m_scr[p, hh, rows, :] for p in range(len(DIL_PATTERNS))]
            m_all = functools.reduce(jnp.maximum, ms)
            tot = sum(jnp.exp(m - m_all) * acc_scr[p, hh, rows, :] for p, m in enumerate(ms))
            outs.append(tot / pltpu.roll(tot, HEAD_DIM, axis=1))
        o_ref[0, 0, rows, :] = jnp.where(halves[0], outs[0], outs[1])
        return carry

    lax.fori_loop(0, n_blocks, finish, 0)


def _mixer_a(q, k, v, bias):
    npair, b, s, _ = q.shape
    blk = (1, 1, SPAN, LANES)
    cur = lambda bi, pi, si: (pi, bi, si, 0)
    prev = lambda bi, pi, si: (pi, bi, jnp.maximum(si - 1, 0), 0)
    return pl.pallas_call(
        _mixer_a_kernel,
        grid=(b, npair, s // SPAN),
        in_specs=[pl.BlockSpec(blk, cur), pl.BlockSpec(blk, prev), pl.BlockSpec(blk, cur),
                  pl.BlockSpec(blk, prev), pl.BlockSpec(blk, cur),
                  pl.BlockSpec((2,) + bias.shape[1:], lambda bi, pi, si: (pi, 0, 0, 0))],
        out_specs=pl.BlockSpec(blk, cur),
        out_shape=jax.ShapeDtypeStruct(q.shape, F32),
        scratch_shapes=[pltpu.VMEM((SPAN, LANES), F32), pltpu.VMEM((2 * SPAN, LANES), F32),
                        pltpu.VMEM((2 * SPAN, LANES), F32),
                        pltpu.VMEM((len(DIL_PATTERNS), 2, SPAN, LANES), F32),
                        pltpu.VMEM((len(DIL_PATTERNS), 2, SPAN, LANES), F32)],
        compiler_params=_cparams("parallel", "parallel", "parallel"),
        name="mixer_a",
    )(q, k, k, v, v, bias)


def _compress_kernel(c_ref, pa_ref, pb_ref, w1a_ref, w1b_ref, b1_ref, w2_ref, o_ref):
    c = c_ref[0].astype(F32)
    xa = (c + pa_ref[...]).astype(BF16)
    xb = (c + pb_ref[...]).astype(BF16)
    ha = jnp.dot(xa, w1a_ref[...], preferred_element_type=F32)
    hb = jnp.dot(xb, w1b_ref[...], preferred_element_type=F32)
    hb_next = jnp.concatenate([hb[1:], jnp.zeros_like(hb[:1])], axis=0)
    hid = jax.nn.gelu(ha + hb_next + b1_ref[...])
    o_ref[0] = jnp.dot(hid.astype(BF16), w2_ref[...], preferred_element_type=F32).astype(o_ref.dtype)


def _compress(c, pos, w1, b1, w2):
    b, nch, _ = c.shape
    half = CMP_LEN // 2
    zero = jnp.zeros((half, HEAD_DIM, CMP_HIDDEN), F32)

    def grouped(wpart):
        g0 = jnp.concatenate([wpart, zero], axis=1).reshape(half * LANES, CMP_HIDDEN)
        g1 = jnp.concatenate([zero, wpart], axis=1).reshape(half * LANES, CMP_HIDDEN)
        return jnp.concatenate([g0, g1], axis=1).astype(BF16)

    w1a, w1b = grouped(w1[:half]), grouped(w1[half:])
    pa = jnp.tile(pos[:half], (1, 2)).reshape(1, half * LANES)
    pb = jnp.tile(pos[half:], (1, 2)).reshape(1, half * LANES)
    b1g = jnp.tile(b1, 2).reshape(1, 2 * CMP_HIDDEN)
    zw = jnp.zeros_like(w2)
    w2g = jnp.concatenate([jnp.concatenate([w2, zw], axis=1),
                           jnp.concatenate([zw, w2], axis=1)], axis=0).astype(BF16)
    full = lambda a: pl.BlockSpec(a.shape, lambda i: (0,) * a.ndim)
    return pl.pallas_call(
        _compress_kernel,
        grid=(b,),
        in_specs=[pl.BlockSpec((1,) + c.shape[1:], lambda i: (i, 0, 0)),
                  full(pa), full(pb), full(w1a), full(w1b), full(b1g), full(w2g)],
        out_specs=pl.BlockSpec((1, nch, LANES), lambda i: (i, 0, 0)),
        out_shape=jax.ShapeDtypeStruct((b, nch, LANES), BF16),
        compiler_params=_cparams("parallel"),
        name="compress",
    )(c, pa, pb, w1a, w1b, b1g, w2g)


def _cmp_attn_kernel(q_ref, gl_ref, kc_ref, vc_ref, wov_ref, o_ref, sb_ref, *, nsel):
    qt = pl.program_id(1)
    tq = q_ref.shape[1]
    nck = kc_ref.shape[1]
    t_row = qt * tq + lax.broadcasted_iota(jnp.int32, (tq, 1), 0)
    blk_end = lax.broadcasted_iota(jnp.int32, (1, nck), 1) * CMP_STRIDE + (CMP_LEN - 1)
    valid = blk_end <= t_row
    gate = jax.nn.sigmoid(gl_ref[0])
    kc = kc_ref[0]
    vc = vc_ref[0]
    pc = [jnp.zeros((tq, nck), F32) for _ in range(B_KV_HEADS)]
    for r in range(B_GROUP):
        sl = slice(r * LANES, (r + 1) * LANES)
        q2 = q_ref[0, :, sl]
        o_slot = None
        for g in range(B_KV_HEADS):
            half = _half_mask(g)
            qm = jnp.where(half, q2, jnp.zeros_like(q2))
            s = lax.dot_general(qm, kc, NT_DIMS, preferred_element_type=F32)
            s = jnp.where(valid, s, -jnp.inf)
            m = jnp.max(s, axis=-1, keepdims=True)
            m = jnp.where(m == -jnp.inf, 0.0, m)
            p = jnp.exp(s - m)
            den = jnp.sum(p, axis=-1, keepdims=True)
            p = p / jnp.maximum(den, 1e-30)
            pc[g] = pc[g] + p
            c = (g * B_GROUP + r) * N_BRANCH
            o = jnp.dot(p.astype(BF16), vc, preferred_element_type=F32) * gate[:, c:c + 1]
            o_slot = o if g == 0 else jnp.where(half, o, o_slot)
        o_ref[0, :, sl] = o_slot

    t_lane = qt * tq + lax.broadcasted_iota(jnp.int32, (1, tq), 1)
    cur = t_lane // SEL_BLOCK
    j = lax.broadcasted_iota(jnp.int32, (nsel, 1), 0)
    forced = (j == 0) | (j == cur) | (j == cur - 1)
    imps = []
    for g in reversed(range(B_KV_HEADS)):
        imp = lax.dot_general(wov_ref[...], pc[g], NT_DIMS, preferred_element_type=F32,
                              precision=lax.Precision.HIGHEST)
        imps.append(jnp.where(j > cur, -jnp.inf, jnp.where(forced, jnp.inf, imp)))

    last_block = (qt * tq + tq - 1) // SEL_BLOCK
    for level in range(-(-nsel // RANK_CHUNK)):
        n_live = min((level + 1) * RANK_CHUNK, nsel)

        @pl.when(last_block // RANK_CHUNK == level)
        def _(n_live=n_live):
            rows = []
            for imp in imps:
                if n_live <= SEL_TOPN:
                    rows.append(jnp.zeros((HEAD_DIM, tq), F32))
                    continue
                live = imp[:n_live]
                jl = j[:n_live]
                rank = jnp.zeros((n_live, tq), jnp.int32)
                for jp in range(n_live):
                    row = live[jp:jp + 1, :]
                    rank = rank + jnp.where(jl > jp, (row >= live).astype(jnp.int32), (row > live).astype(jnp.int32))
                rows.append(jnp.where(rank < SEL_TOPN, 0.0, NOT_SELECTED))
                if n_live < HEAD_DIM:
                    rows.append(jnp.zeros((HEAD_DIM - n_live, tq), F32))
            sb_ref[0] = jnp.concatenate(rows, axis=0).T.astype(sb_ref.dtype)


def _cmp_attn(qb, gl, kcmp, vcmp, nsel):
    b, s, w = qb.shape
    nck = kcmp.shape[1]
    ratio = SEL_BLOCK // CMP_STRIDE
    wov = np.zeros((nsel, nck), np.float32)
    for jj in range(nsel):
        for off, wt in zip(range(-1, ratio), CMP_OVERLAP):
            n = ratio * jj + off
            if 0 <= n < nck - 1:
                wov[jj, n] = wt
    wov = jnp.asarray(wov)
    tile = lambda width: pl.BlockSpec((1, Q_TILE, width), lambda bi, qi: (bi, qi, 0))
    whole = lambda a: pl.BlockSpec((1,) + a.shape[1:], lambda bi, qi: (bi, 0, 0))
    return pl.pallas_call(
        functools.partial(_cmp_attn_kernel, nsel=nsel),
        grid=(b, s // Q_TILE),
        in_specs=[tile(w), tile(LANES), whole(kcmp), whole(vcmp),
                  pl.BlockSpec(wov.shape, lambda bi, qi: (0, 0))],
        out_specs=[tile(w), tile(LANES)],
        out_shape=[jax.ShapeDtypeStruct((b, s, w), F32), jax.ShapeDtypeStruct((b, s, LANES), BF16)],
        compiler_params=_cparams("parallel", "parallel"),
        name="cmp_attn",
    )(qb, gl, kcmp, vcmp, wov)


def _selected_kernel(q_ref, sb_ref, gl_ref, kaug_ref, v_ref, bias_ref, o_ref, acc_ref, m_ref, s_ref, *, n_bias):
    qt = pl.program_id(1)
    tq = q_ref.shape[1]
    gate = jax.nn.sigmoid(gl_ref[0])
    sb = sb_ref[0]
    n_steps = (qt * tq + tq + K_TILE - 1) // K_TILE
    halves = [_half_mask(g) for g in range(B_KV_HEADS)]
    qaug = [jnp.concatenate([jnp.where(halves[g], q_ref[0, :, r * LANES:(r + 1) * LANES], sb)
                             for r in range(B_GROUP)], axis=0) for g in range(B_KV_HEADS)]
    acc_ref[...] = jnp.zeros_like(acc_ref)
    m_ref[...] = jnp.full_like(m_ref, M_INIT)

    def scores(kt):
        for j in range(K_TILE // BIAS_TILE):
            k0 = pl.multiple_of(kt * K_TILE + j * BIAS_TILE, BIAS_TILE)
            bi = jnp.clip(qt - k0 // tq + 1, 0, n_bias - 1)
            for g in range(B_KV_HEADS):
                k = kaug_ref[0, pl.ds(k0, BIAS_TILE), g * LANES:(g + 1) * LANES]
                s = lax.dot_general(qaug[g], k, NT_DIMS, preferred_element_type=F32)
                s_ref[kt % 2, g, :, j * BIAS_TILE:(j + 1) * BIAS_TILE] = s + bias_ref[g, bi]

    def accumulate(kt):
        k0 = pl.multiple_of(kt * K_TILE, K_TILE)
        v2 = v_ref[0, pl.ds(k0, K_TILE), :]
        for g in range(B_KV_HEADS):
            v = jnp.where(halves[g], v2, jnp.ones((K_TILE, LANES), BF16))
            s = s_ref[kt % 2, g]
            m_old = m_ref[g]
            m_new = jnp.maximum(m_old, jnp.max(s, axis=-1, keepdims=True))
            alpha = jnp.exp(m_old - m_new)
            p = jnp.exp(s - jnp.tile(m_new, (1, K_TILE // LANES)))
            acc_ref[g] = alpha * acc_ref[g] + jnp.dot(p.astype(BF16), v, preferred_element_type=F32)
            m_ref[g] = m_new

    def body(kt, carry):
        accumulate(kt)
        scores(kt + 1)
        return carry

    scores(0)
    lax.fori_loop(0, n_steps - 1, body, 0)
    accumulate(n_steps - 1)
    slots = [None] * B_GROUP
    for g in range(B_KV_HEADS):
        acc = acc_ref[g]
        o = acc / jnp.maximum(pltpu.roll(acc, HEAD_DIM, axis=1), 1e-30)
        for r in range(B_GROUP):
            c = (g * B_GROUP + r) * N_BRANCH + 1
            o_r = o[r * tq:(r + 1) * tq] * gate[:, c:c + 1]
            slots[r] = o_r if g == 0 else jnp.where(halves[g], o_r, slots[r])
    for r in range(B_GROUP):
        o_ref[0, :, r * LANES:(r + 1) * LANES] = slots[r]


def _selected(qb, sb, gl, kaug, v, bias):
    b, s, w = qb.shape
    tile = lambda width: pl.BlockSpec((1, Q_TILE, width), lambda bi, qi: (bi, qi, 0))
    whole = lambda a: pl.BlockSpec((1,) + a.shape[1:], lambda bi, qi: (bi, 0, 0))
    rows = B_GROUP * Q_TILE
    return pl.pallas_call(
        functools.partial(_selected_kernel, n_bias=bias.shape[1]),
        grid=(b, s // Q_TILE),
        in_specs=[tile(w), tile(LANES), tile(LANES), whole(kaug), whole(v),
                  pl.BlockSpec(bias.shape, lambda bi, qi: (0, 0, 0, 0), pipeline_mode=pl.Buffered(1))],
        out_specs=tile(w),
        out_shape=jax.ShapeDtypeStruct((b, s, w), F32),
        scratch_shapes=[pltpu.VMEM((B_KV_HEADS, rows, LANES), F32), pltpu.VMEM((B_KV_HEADS, rows, LANES), F32),
                        pltpu.VMEM((2, B_KV_HEADS, rows, K_TILE), F32)],
        compiler_params=_cparams("parallel", "parallel"),
        name="selected",
    )(qb, sb, gl, kaug, v, bias)


def _window_kernel(q_ref, gl_ref, k_ref, v_ref, bias_ref, o_ref, *, kw):
    qt = pl.program_id(1)
    tq = q_ref.shape[1]
    gate = jax.nn.sigmoid(gl_ref[0])
    k0 = pl.multiple_of(qt * tq, tq)
    kwin = k_ref[0, pl.ds(k0, kw), :]
    vwin = v_ref[0, pl.ds(k0, kw), :]
    col = lax.broadcasted_iota(jnp.int32, (1, kw), 1)
    real = col >= WIN_LEN - qt * tq
    zero = jnp.zeros((tq, LANES), BF16)
    slots = [None] * B_GROUP
    for g in range(B_KV_HEADS):
        half = _half_mask(g)
        qg = jnp.concatenate([jnp.where(half, q_ref[0, :, r * LANES:(r + 1) * LANES], zero)
                              for r in range(B_GROUP)], axis=0)
        s = lax.dot_general(qg, kwin, NT_DIMS, preferred_element_type=F32)
        s = jnp.where(real, s + bias_ref[g], -jnp.inf)
        p = jnp.exp(s - jnp.max(s, axis=-1, keepdims=True))
        acc = jnp.dot(p.astype(BF16), jnp.where(half, vwin, jnp.ones_like(vwin)), preferred_element_type=F32)
        o = acc / jnp.maximum(pltpu.roll(acc, HEAD_DIM, axis=1), 1e-30)
        for r in range(B_GROUP):
            c = (g * B_GROUP + r) * N_BRANCH + 2
            o_r = o[r * tq:(r + 1) * tq] * gate[:, c:c + 1]
            slots[r] = o_r if g == 0 else jnp.where(half, o_r, slots[r])
    for r in range(B_GROUP):
        o_ref[0, :, r * LANES:(r + 1) * LANES] = slots[r]


def _window(qb, gl, kpad, vpad, bias):
    b, s, w = qb.shape
    kw = bias.shape[-1]
    tile = lambda width: pl.BlockSpec((1, Q_TILE, width), lambda bi, qi: (bi, qi, 0))
    whole = lambda a: pl.BlockSpec((1,) + a.shape[1:], lambda bi, qi: (bi, 0, 0))
    return pl.pallas_call(
        functools.partial(_window_kernel, kw=kw),
        grid=(b, s // Q_TILE),
        in_specs=[tile(w), tile(LANES), whole(kpad), whole(vpad),
                  pl.BlockSpec(bias.shape, lambda bi, qi: (0, 0, 0))],
        out_specs=tile(w),
        out_shape=jax.ShapeDtypeStruct((b, s, w), F32),
        compiler_params=_cparams("parallel", "parallel"),
        name="window",
    )(qb, gl, kpad, vpad, bias)


def _outproj_kernel(x_ref, oa_ref, oc_ref, os_ref, ow_ref, wa_ref, wb_ref, h_ref):
    o_a = jnp.concatenate([oa_ref[p] for p in range(oa_ref.shape[0])], axis=1)
    o_b = oc_ref[...] + os_ref[...] + ow_ref[...]
    y = jnp.dot(o_a.astype(BF16), wa_ref[...], preferred_element_type=F32)
    y = y + jnp.dot(o_b.astype(BF16), wb_ref[...], preferred_element_type=F32)
    h_ref[...] = x_ref[...] + y


def _outproj(x2, oa, obs, wa, wb, tm):
    n, d = x2.shape
    row = lambda width: pl.BlockSpec((tm, width), lambda i: (i, 0))
    full = lambda a: pl.BlockSpec(a.shape, lambda i: (0, 0))
    return pl.pallas_call(
        _outproj_kernel,
        grid=(n // tm,),
        in_specs=[row(d), pl.BlockSpec((oa.shape[0], tm, LANES), lambda i: (0, i, 0))]
                 + [row(o.shape[1]) for o in obs] + [full(wa), full(wb)],
        out_specs=row(d),
        out_shape=jax.ShapeDtypeStruct((n, d), F32),
        compiler_params=_cparams("parallel"),
        name="outproj",
    )(x2, oa, *obs, wa, wb)


def _ffn_kernel(h_ref, g2_ref, wg_ref, wu_ref, wd_ref, gf_ref, o_ref):
    h = h_ref[...]
    y = h * lax.rsqrt(jnp.mean(h * h, axis=-1, keepdims=True) + RMS_EPS)
    hn = (y * g2_ref[...]).astype(BF16)
    a = jnp.dot(hn, wg_ref[...], preferred_element_type=F32)
    u = jnp.dot(hn, wu_ref[...], preferred_element_type=F32)
    act = (jax.nn.silu(a) * u).astype(BF16)
    h2 = h + jnp.dot(act, wd_ref[...], preferred_element_type=F32)
    y2 = h2 * lax.rsqrt(jnp.mean(h2 * h2, axis=-1, keepdims=True) + RMS_EPS)
    o_ref[...] = y2 * gf_ref[...]


def _ffn(h, g2, wg, wu, wd, gf, tm):
    n, d = h.shape
    row = pl.BlockSpec((tm, d), lambda i: (i, 0))
    full = lambda a: pl.BlockSpec(a.shape, lambda i: (0, 0))
    return pl.pallas_call(
        _ffn_kernel,
        grid=(n // tm,),
        in_specs=[row, full(g2), full(wg), full(wu), full(wd), full(gf)],
        out_specs=row,
        out_shape=jax.ShapeDtypeStruct((n, d), F32),
        compiler_params=_cparams("parallel"),
        name="ffn",
    )(h, g2, wg, wu, wd, gf)


def _slot_perm():
    perm = np.zeros(B_WIDTH, np.int64)
    for r in range(B_GROUP):
        for g in range(B_KV_HEADS):
            src = (g * B_GROUP + r) * HEAD_DIM
            dst = r * LANES + g * HEAD_DIM
            perm[dst:dst + HEAD_DIM] = np.arange(src, src + HEAD_DIM)
    return perm


def _layer(h, tab_a, tab_b, norm1_g, w_in, cmp_pos, k_w1, k_b1, k_w2, v_w1, v_b1, v_w2,
           w_out, norm2_g, w_gate, w_up, w_down, norm_f_g):
    b, s, d = h.shape
    n = b * s
    assert s % SPAN == 0 and s % K_TILE == 0
    perm = _slot_perm()

    cols = np.cumsum([0, A_WIDTH, A_WIDTH, A_WIDTH, B_WIDTH] + [KV_WIDTH] * 6 + [GATE_WIDTH])
    w_aq = w_in[:, cols[0]:cols[1]] * SCALE
    w_bq = (w_in[:, cols[3]:cols[4]] * SCALE)[:, perm]
    w_gl = jnp.pad(w_in[:, cols[10]:cols[11]], ((0, 0), (0, LANES - GATE_WIDTH)))
    w1 = jnp.concatenate([w_aq, w_in[:, cols[1]:cols[3]], w_bq, w_in[:, cols[4]:cols[10]], w_gl], axis=1).astype(BF16)
    widths = (A_WIDTH,) * 3 + (B_WIDTH,) + (KV_WIDTH,) * 6 + (LANES,)
    dtypes = (BF16,) * 10 + (F32,)
    slot_major = (True,) * 3 + (False,) * 8
    x2 = h.reshape(n, d)
    qa, ka, va, qb, kc, vc, ksl, vsl, kw, vw, gl = _inproj(x2, norm1_g.reshape(1, d), w1, widths, dtypes,
                                                          slot_major, tm=512)
    r3 = lambda t: t.reshape(b, s, t.shape[-1])
    qb, ksl, vsl, kw, vw, gl = map(r3, (qb, ksl, vsl, kw, vw, gl))

    i = np.arange(DIL_BLOCK)[:, None]
    jj = np.arange(2 * DIL_BLOCK)[None, :]
    dist = i + DIL_BLOCK - jj
    tok_dist = np.stack([np.maximum(dist, 0) * dil for _, dil in DIL_PATTERNS])
    in_window = np.stack([(dist >= 0) & (dist <= window // dil) for window, dil in DIL_PATTERNS])
    bias_a = _bias_tiles(tab_a, tok_dist, in_window, inner=1).reshape((A_HEADS,) + tok_dist.shape)
    r4 = lambda t: t.reshape(A_HEADS // 2, b, s, LANES)
    o_a = _mixer_a(r4(qa), r4(ka), r4(va), bias_a).reshape(A_HEADS // 2, n, LANES)

    nch = s // CMP_STRIDE
    kcmp = _compress(kc.reshape(b, nch, CMP_STRIDE * KV_WIDTH), cmp_pos, k_w1, k_b1, k_w2)
    vcmp = _compress(vc.reshape(b, nch, CMP_STRIDE * KV_WIDTH), cmp_pos, v_w1, v_b1, v_w2)
    nsel = s // SEL_BLOCK
    o_cmp, sb = _cmp_attn(qb, gl, kcmp, vcmp, nsel)

    onehot = (np.arange(s)[:, None] // SEL_BLOCK) == np.arange(HEAD_DIM)[None, :]
    onehot = jnp.broadcast_to(jnp.asarray(onehot, BF16), (b, s, HEAD_DIM))
    kaug = jnp.concatenate([ksl[..., :HEAD_DIM], onehot, onehot, ksl[..., HEAD_DIM:]], axis=-1)
    d_sat = int(np.nonzero(_t5_bucket_np(np.arange(s)) < REL_BUCKETS - 1)[0].max()) + 1
    first_far = -(-(d_sat + BIAS_TILE - 1) // Q_TILE)
    n_bias = min(s // Q_TILE, first_far + 1) + 1
    dd = ((np.arange(n_bias)[:, None, None] - 1) * Q_TILE + np.arange(Q_TILE)[None, :, None]
          - np.arange(BIAS_TILE)[None, None, :])
    bias_sl = _bias_tiles(tab_b, dd, dd >= 0, inner=B_GROUP)
    bias_sl = bias_sl.reshape(B_KV_HEADS, n_bias, B_GROUP * Q_TILE, BIAS_TILE)
    o_slc = _selected(qb, sb, gl, kaug, vsl, bias_sl)

    kwid = WIN_LEN + Q_TILE
    dw = np.arange(Q_TILE)[:, None] + WIN_LEN - np.arange(kwid)[None, :]
    bias_w = _bias_tiles(tab_b, dw[None], ((dw >= 0) & (dw < WIN_LEN))[None], inner=1)
    bias_w = bias_w.reshape(B_KV_HEADS, B_GROUP * Q_TILE, kwid)
    pad = lambda t: jnp.pad(t, ((0, 0), (WIN_LEN, 0), (0, 0)))
    o_win = _window(qb, gl, pad(kw), pad(vw), bias_w)

    wa = w_out[:A_WIDTH].astype(BF16)
    wb = w_out[A_WIDTH:][perm].astype(BF16)
    hmid = _outproj(x2, o_a, [t.reshape(n, B_WIDTH) for t in (o_cmp, o_slc, o_win)], wa, wb, tm=256)
    out = _ffn(hmid, norm2_g.reshape(1, d), w_gate.astype(BF16), w_up.astype(BF16), w_down.astype(BF16),
               norm_f_g.reshape(1, d), tm=256)
    return out.reshape(b, s, d)


def kernel(x, norm1_g, w_in, rel_bias, cmp_pos, cmp_k_w1, cmp_k_b1, cmp_k_w2, cmp_v_w1, cmp_v_b1, cmp_v_w2,
           w_out, norm2_g, w_gate, w_up, w_down, norm_f_g):
    assert w_in.shape[0] == 1, "single-layer model"
    tab_a = rel_bias[:, :A_HEADS].T
    tab_b = rel_bias[:, A_HEADS:].T
    return _layer(x, tab_a, tab_b, norm1_g[0], w_in[0], cmp_pos[0], cmp_k_w1[0], cmp_k_b1[0], cmp_k_w2[0],
                  cmp_v_w1[0], cmp_v_b1[0], cmp_v_w2[0], w_out[0], norm2_g[0], w_gate[0], w_up[0], w_down[0],
                  norm_f_g)
```

```python
import functools
import math

import numpy as np
import jax
import jax.numpy as jnp
from jax import lax
from jax.experimental import pallas as pl
from jax.experimental.pallas import tpu as pltpu

F32 = jnp.float32
BF16 = jnp.bfloat16

HEAD_DIM = 64
LANES = 128
A_HEADS = 8
DIL_PATTERNS = ((128, 1), (512, 4), (2048, 16))
DIL_BLOCK = 128
B_HEADS = 8
B_KV_HEADS = 2
B_GROUP = B_HEADS // B_KV_HEADS
CMP_LEN = 32
CMP_STRIDE = 16
CMP_HIDDEN = 128
SEL_BLOCK = 64
SEL_TOPN = 16
CMP_OVERLAP = (1.0, 2.0, 2.0, 2.0, 1.0)
WIN_LEN = 512
WIN_BLOCK = 128
N_BRANCH = 3
REL_BUCKETS = 32
REL_MAX_DIST = 2048
RMS_EPS = 1e-6
A_WIDTH = A_HEADS * HEAD_DIM
B_WIDTH = B_HEADS * HEAD_DIM
KV_WIDTH = B_KV_HEADS * HEAD_DIM
GATE_WIDTH = B_HEADS * N_BRANCH
SCALE = HEAD_DIM ** -0.5

SPAN = DIL_PATTERNS[-1][1] * DIL_BLOCK
MIX_UNROLL = 16
RANK_CHUNK = 16
Q_TILE = 128
K_TILE = 512
BIAS_TILE = 256
NOT_SELECTED = -(2.0 ** 100)
M_INIT = -1e30
VMEM_LIMIT = 56 * 1024 * 1024

NT_DIMS = (((1,), (1,)), ((), ()))


def _cparams(*sem):
    return pltpu.CompilerParams(dimension_semantics=sem, vmem_limit_bytes=VMEM_LIMIT)


def _t5_bucket_np(dist):
    max_exact = REL_BUCKETS // 2
    d = np.asarray(dist)
    df = np.maximum(d, 1).astype(np.float32)
    large = max_exact + (np.log(df / np.float32(max_exact)) / np.float32(math.log(REL_MAX_DIST / max_exact))
                         * np.float32(REL_BUCKETS - max_exact)).astype(np.int32)
    large = np.minimum(large, REL_BUCKETS - 1)
    return np.where(d < max_exact, d, large).astype(np.int32)


def _bias_kernel(tab_ref, idx_ref, o_ref):
    head = pl.program_id(0) * pl.num_programs(2) + pl.program_id(2)
    idx = idx_ref[0]
    acc = jnp.full(idx.shape, -jnp.inf, F32)
    for bucket in range(REL_BUCKETS):
        acc = jnp.where(idx == bucket, tab_ref[head * REL_BUCKETS + bucket], acc)
    o_ref[0, 0, 0] = acc


def _bias_tiles(tab, dist, valid, inner):
    h = tab.shape[0]
    t, r, c = dist.shape
    idx = jnp.asarray(np.where(valid, _t5_bucket_np(np.maximum(dist, 0)), -1).astype(np.int32))
    return pl.pallas_call(
        _bias_kernel,
        grid=(h // inner, t, inner),
        in_specs=[pl.BlockSpec(memory_space=pltpu.SMEM),
                  pl.BlockSpec((1, r, c), lambda a, ti, bi: (ti, 0, 0))],
        out_specs=pl.BlockSpec((1, 1, 1, r, c), lambda a, ti, bi: (a, ti, bi, 0, 0)),
        out_shape=jax.ShapeDtypeStruct((h // inner, t, inner, r, c), F32),
        compiler_params=_cparams("parallel", "parallel", "parallel"),
        name="bias_tiles",
    )(tab.reshape(-1), idx)


def _half_mask(g):
    lane = lax.broadcasted_iota(jnp.int32, (1, LANES), 1)
    return (lane >= HEAD_DIM) if g else (lane < HEAD_DIM)


def _inproj_kernel(x_ref, g_ref, w_ref, *out_refs, widths):
    x = x_ref[...]
    y = x * lax.rsqrt(jnp.mean(x * x, axis=-1, keepdims=True) + RMS_EPS)
    xn = (y * g_ref[...]).astype(BF16)
    start = 0
    for o_ref, w in zip(out_refs, widths):
        r = jnp.dot(xn, w_ref[:, start:start + w], preferred_element_type=F32)
        if len(o_ref.shape) == 3:
            for p in range(o_ref.shape[0]):
                o_ref[p] = r[:, p * LANES:(p + 1) * LANES].astype(o_ref.dtype)
        else:
            o_ref[...] = r.astype(o_ref.dtype)
        start += w


def _inproj(x2, g, w, widths, dtypes, slot_major, tm):
    n, d = x2.shape
    shape = lambda w_, sm: (w_ // LANES, n, LANES) if sm else (n, w_)
    spec = lambda w_, sm: (pl.BlockSpec((w_ // LANES, tm, LANES), lambda i: (0, i, 0)) if sm
                           else pl.BlockSpec((tm, w_), lambda i: (i, 0)))
    out_shape = [jax.ShapeDtypeStruct(shape(w_, sm), dt) for w_, dt, sm in zip(widths, dtypes, slot_major)]
    return pl.pallas_call(
        functools.partial(_inproj_kernel, widths=widths),
        grid=(n // tm,),
        in_specs=[pl.BlockSpec((tm, d), lambda i: (i, 0)),
                  pl.BlockSpec((1, d), lambda i: (0, 0)),
                  pl.BlockSpec(w.shape, lambda i: (0, 0))],
        out_specs=[spec(w_, sm) for w_, sm in zip(widths, slot_major)],
        out_shape=out_shape,
        compiler_params=_cparams("parallel"),
        name="inproj",
    )(x2, g, w)


def _mixer_a_kernel(q_ref, kp_ref, kc_ref, vp_ref, vc_ref, bias_ref, o_ref, qf, kf, vf, acc_scr, m_scr):
    sb = pl.program_id(2)
    qf[...] = q_ref[0, 0].astype(F32)
    kf[:SPAN] = kp_ref[0, 0].astype(F32)
    kf[SPAN:] = kc_ref[0, 0].astype(F32)
    vf[:SPAN] = vp_ref[0, 0].astype(F32)
    vf[SPAN:] = vc_ref[0, 0].astype(F32)
    halves = [_half_mask(hh) for hh in range(2)]
    col = lax.broadcasted_iota(jnp.int32, (1, 2 * DIL_BLOCK), 1)
    ones = jnp.ones((2 * DIL_BLOCK, LANES), BF16)
    n_blocks = SPAN // DIL_BLOCK
    for p, (_, dil) in enumerate(DIL_PATTERNS):
        step = dil * DIL_BLOCK
        per_class = SPAN // step

        def body(grp, carry, p=p, dil=dil, step=step, per_class=per_class):
            for u in range(MIX_UNROLL):
                idx = grp * MIX_UNROLL + u
                r, n = idx // per_class, idx % per_class
                q0 = pl.multiple_of(n * step, step) + r
                rows_q = pl.ds(q0, DIL_BLOCK, stride=dil)
                rows_k = pl.ds(q0 + SPAN - step, 2 * DIL_BLOCK, stride=dil)
                q2 = qf[rows_q, :].astype(BF16)
                kcat = kf[rows_k, :].astype(BF16)
                vcat = vf[rows_k, :].astype(BF16)
                keep = (col >= DIL_BLOCK) | (sb * per_class + n > 0)
                for hh in range(2):
                    qm = jnp.where(halves[hh], q2, jnp.zeros_like(q2))
                    s = lax.dot_general(qm, kcat, NT_DIMS, preferred_element_type=F32)
                    s = jnp.where(keep, s + bias_ref[hh, p], -jnp.inf)
                    m_blk = jnp.max(s, axis=-1, keepdims=True)
                    pe = jnp.exp(s - m_blk)
                    pv = jnp.dot(pe.astype(BF16), jnp.where(halves[hh], vcat, ones), preferred_element_type=F32)
                    acc_scr[p, hh, rows_q, :] = pv
                    m_scr[p, hh, rows_q, :] = jnp.broadcast_to(m_blk, pv.shape)
            return carry

        lax.fori_loop(0, n_blocks // MIX_UNROLL, body, 0)

    def finish(c, carry):
        rows = pl.ds(pl.multiple_of(c * DIL_BLOCK, DIL_BLOCK), DIL_BLOCK)
        outs = []
        for hh in range(2):
            ms = [m_scr[p, hh, rows, :] for p in range(len(DIL_PATTERNS))]
            m_all = functools.reduce(jnp.maximum, ms)
            tot = sum(jnp.exp(m - m_all) * acc_scr[p, hh, rows, :] for p, m in enumerate(ms))
            outs.append(tot / pltpu.roll(tot, HEAD_DIM, axis=1))
        o_ref[0, 0, rows, :] = jnp.where(halves[0], outs[0], outs[1])
        return carry

    lax.fori_loop(0, n_blocks, finish, 0, unroll=4)


def _mixer_a(q, k, v, bias):
    npair, b, s, _ = q.shape
    blk = (1, 1, SPAN, LANES)
    cur = lambda bi, pi, si: (pi, bi, si, 0)
    prev = lambda bi, pi, si: (pi, bi, jnp.maximum(si - 1, 0), 0)
    return pl.pallas_call(
        _mixer_a_kernel,
        grid=(b, npair, s // SPAN),
        in_specs=[pl.BlockSpec(blk, cur), pl.BlockSpec(blk, prev), pl.BlockSpec(blk, cur),
                  pl.BlockSpec(blk, prev), pl.BlockSpec(blk, cur),
                  pl.BlockSpec((2,) + bias.shape[1:], lambda bi, pi, si: (pi, 0, 0, 0))],
        out_specs=pl.BlockSpec(blk, cur),
        out_shape=jax.ShapeDtypeStruct(q.shape, F32),
        scratch_shapes=[pltpu.VMEM((SPAN, LANES), F32), pltpu.VMEM((2 * SPAN, LANES), F32),
                        pltpu.VMEM((2 * SPAN, LANES), F32),
                        pltpu.VMEM((len(DIL_PATTERNS), 2, SPAN, LANES), F32),
                        pltpu.VMEM((len(DIL_PATTERNS), 2, SPAN, LANES), F32)],
        compiler_params=_cparams("parallel", "parallel", "parallel"),
        name="mixer_a",
    )(q, k, k, v, v, bias)


def _compress_kernel(c_ref, pa_ref, pb_ref, w1a_ref, w1b_ref, b1_ref, w2_ref, o_ref):
    c = c_ref[0].astype(F32)
    xa = (c + pa_ref[...]).astype(BF16)
    xb = (c + pb_ref[...]).astype(BF16)
    ha = jnp.dot(xa, w1a_ref[...], preferred_element_type=F32)
    hb = jnp.dot(xb, w1b_ref[...], preferred_element_type=F32)
    hb_next = jnp.concatenate([hb[1:], jnp.zeros_like(hb[:1])], axis=0)
    hid = jax.nn.gelu(ha + hb_next + b1_ref[...])
    o_ref[0] = jnp.dot(hid.astype(BF16), w2_ref[...], preferred_element_type=F32).astype(o_ref.dtype)


def _compress(c, pos, w1, b1, w2):
    b, nch, _ = c.shape
    half = CMP_LEN // 2
    zero = jnp.zeros((half, HEAD_DIM, CMP_HIDDEN), F32)

    def grouped(wpart):
        g0 = jnp.concatenate([wpart, zero], axis=1).reshape(half * LANES, CMP_HIDDEN)
        g1 = jnp.concatenate([zero, wpart], axis=1).reshape(half * LANES, CMP_HIDDEN)
        return jnp.concatenate([g0, g1], axis=1).astype(BF16)

    w1a, w1b = grouped(w1[:half]), grouped(w1[half:])
    pa = jnp.tile(pos[:half], (1, 2)).reshape(1, half * LANES)
    pb = jnp.tile(pos[half:], (1, 2)).reshape(1, half * LANES)
    b1g = jnp.tile(b1, 2).reshape(1, 2 * CMP_HIDDEN)
    zw = jnp.zeros_like(w2)
    w2g = jnp.concatenate([jnp.concatenate([w2, zw], axis=1),
                           jnp.concatenate([zw, w2], axis=1)], axis=0).astype(BF16)
    full = lambda a: pl.BlockSpec(a.shape, lambda i: (0,) * a.ndim)
    return pl.pallas_call(
        _compress_kernel,
        grid=(b,),
        in_specs=[pl.BlockSpec((1,) + c.shape[1:], lambda i: (i, 0, 0)),
                  full(pa), full(pb), full(w1a), full(w1b), full(b1g), full(w2g)],
        out_specs=pl.BlockSpec((1, nch, LANES), lambda i: (i, 0, 0)),
        out_shape=jax.ShapeDtypeStruct((b, nch, LANES), BF16),
        compiler_params=_cparams("parallel"),
        name="compress",
    )(c, pa, pb, w1a, w1b, b1g, w2g)


def _cmp_attn_kernel(q_ref, gl_ref, kc_ref, vc_ref, wov_ref, o_ref, sb_ref, *, nsel):
    qt = pl.program_id(1)
    tq = q_ref.shape[1]
    nck = kc_ref.shape[1]
    t_row = qt * tq + lax.broadcasted_iota(jnp.int32, (tq, 1), 0)
    blk_end = lax.broadcasted_iota(jnp.int32, (1, nck), 1) * CMP_STRIDE + (CMP_LEN - 1)
    valid = blk_end <= t_row
    gate = jax.nn.sigmoid(gl_ref[0])
    kc = kc_ref[0]
    vc = vc_ref[0]
    pc = [jnp.zeros((tq, nck), F32) for _ in range(B_KV_HEADS)]
    for r in range(B_GROUP):
        sl = slice(r * LANES, (r + 1) * LANES)
        q2 = q_ref[0, :, sl]
        o_slot = None
        for g in range(B_KV_HEADS):
            half = _half_mask(g)
            qm = jnp.where(half, q2, jnp.zeros_like(q2))
            s = lax.dot_general(qm, kc, NT_DIMS, preferred_element_type=F32)
            s = jnp.where(valid, s, -jnp.inf)
            m = jnp.max(s, axis=-1, keepdims=True)
            m = jnp.where(m == -jnp.inf, 0.0, m)
            p = jnp.exp(s - m)
            den = jnp.sum(p, axis=-1, keepdims=True)
            p = p / jnp.maximum(den, 1e-30)
            pc[g] = pc[g] + p
            c = (g * B_GROUP + r) * N_BRANCH
            o = jnp.dot(p.astype(BF16), vc, preferred_element_type=F32) * gate[:, c:c + 1]
            o_slot = o if g == 0 else jnp.where(half, o, o_slot)
        o_ref[0, :, sl] = o_slot

    t_lane = qt * tq + lax.broadcasted_iota(jnp.int32, (1, tq), 1)
    cur = t_lane // SEL_BLOCK
    j = lax.broadcasted_iota(jnp.int32, (nsel, 1), 0)
    forced = (j == 0) | (j == cur) | (j == cur - 1)
    imps = []
    for g in reversed(range(B_KV_HEADS)):
        imp = lax.dot_general(wov_ref[...], pc[g], NT_DIMS, preferred_element_type=F32,
                              precision=lax.Precision.HIGHEST)
        imps.append(jnp.where(j > cur, -jnp.inf, jnp.where(forced, jnp.inf, imp)))

    last_block = (qt * tq + tq - 1) // SEL_BLOCK
    for level in range(-(-nsel // RANK_CHUNK)):
        n_live = min((level + 1) * RANK_CHUNK, nsel)

        @pl.when(last_block // RANK_CHUNK == level)
        def _(n_live=n_live):
            rows = []
            for imp in imps:
                if n_live <= SEL_TOPN:
                    rows.append(jnp.zeros((HEAD_DIM, tq), F32))
                    continue
                live = imp[:n_live]
                jl = j[:n_live]
                rank = jnp.zeros((n_live, tq), jnp.int32)
                for jp in range(n_live):
                    row = live[jp:jp + 1, :]
                    rank = rank + jnp.where(jl > jp, (row >= live).astype(jnp.int32), (row > live).astype(jnp.int32))
                rows.append(jnp.where(rank < SEL_TOPN, 0.0, NOT_SELECTED))
                if n_live < HEAD_DIM:
                    rows.append(jnp.zeros((HEAD_DIM - n_live, tq), F32))
            sb_ref[0] = jnp.concatenate(rows, axis=0).T.astype(sb_ref.dtype)


def _cmp_attn(qb, gl, kcmp, vcmp, nsel):
    b, s, w = qb.shape
    nck = kcmp.shape[1]
    ratio = SEL_BLOCK // CMP_STRIDE
    wov = np.zeros((nsel, nck), np.float32)
    for jj in range(nsel):
        for off, wt in zip(range(-1, ratio), CMP_OVERLAP):
            n = ratio * jj + off
            if 0 <= n < nck - 1:
                wov[jj, n] = wt
    wov = jnp.asarray(wov)
    tile = lambda width: pl.BlockSpec((1, Q_TILE, width), lambda bi, qi: (bi, qi, 0))
    whole = lambda a: pl.BlockSpec((1,) + a.shape[1:], lambda bi, qi: (bi, 0, 0))
    return pl.pallas_call(
        functools.partial(_cmp_attn_kernel, nsel=nsel),
        grid=(b, s // Q_TILE),
        in_specs=[tile(w), tile(LANES), whole(kcmp), whole(vcmp),
                  pl.BlockSpec(wov.shape, lambda bi, qi: (0, 0))],
        out_specs=[tile(w), tile(LANES)],
        out_shape=[jax.ShapeDtypeStruct((b, s, w), F32), jax.ShapeDtypeStruct((b, s, LANES), BF16)],
        compiler_params=_cparams("parallel", "parallel"),
        name="cmp_attn",
    )(qb, gl, kcmp, vcmp, wov)


def _selected_kernel(q_ref, sb_ref, gl_ref, kaug_ref, v_ref, bias_ref, o_ref, acc_ref, m_ref, s_ref, *, n_bias):
    qt = pl.program_id(1)
    tq = q_ref.shape[1]
    gate = jax.nn.sigmoid(gl_ref[0])
    sb = sb_ref[0]
    n_steps = (qt * tq + tq + K_TILE - 1) // K_TILE
    halves = [_half_mask(g) for g in range(B_KV_HEADS)]
    qaug = [jnp.concatenate([jnp.where(halves[g], q_ref[0, :, r * LANES:(r + 1) * LANES], sb)
                             for r in range(B_GROUP)], axis=0) for g in range(B_KV_HEADS)]
    acc_ref[...] = jnp.zeros_like(acc_ref)
    m_ref[...] = jnp.full_like(m_ref, M_INIT)

    def scores(kt):
        for j in range(K_TILE // BIAS_TILE):
            k0 = pl.multiple_of(kt * K_TILE + j * BIAS_TILE, BIAS_TILE)
            bi = jnp.clip(qt - k0 // tq + 1, 0, n_bias - 1)
            for g in range(B_KV_HEADS):
                k = kaug_ref[0, pl.ds(k0, BIAS_TILE), g * LANES:(g + 1) * LANES]
                s = lax.dot_general(qaug[g], k, NT_DIMS, preferred_element_type=F32)
                s_ref[kt % 2, g, :, j * BIAS_TILE:(j + 1) * BIAS_TILE] = s + bias_ref[g, bi]

    def accumulate(kt):
        k0 = pl.multiple_of(kt * K_TILE, K_TILE)
        v2 = v_ref[0, pl.ds(k0, K_TILE), :]
        for g in range(B_KV_HEADS):
            v = jnp.where(halves[g], v2, jnp.ones((K_TILE, LANES), BF16))
            s = s_ref[kt % 2, g]
            m_old = m_ref[g]
            m_new = jnp.maximum(m_old, jnp.max(s, axis=-1, keepdims=True))
            alpha = jnp.exp(m_old - m_new)
            p = jnp.exp(s - jnp.tile(m_new, (1, K_TILE // LANES)))
            acc_ref[g] = alpha * acc_ref[g] + jnp.dot(p.astype(BF16), v, preferred_element_type=F32)
            m_ref[g] = m_new

    def body(kt, carry):
        accumulate(kt)
        scores(kt + 1)
        return carry

    scores(0)
    lax.fori_loop(0, n_steps - 1, body, 0)
    accumulate(n_steps - 1)
    slots = [None] * B_GROUP
    for g in range(B_KV_HEADS):
        acc = acc_ref[g]
        o = acc / jnp.maximum(pltpu.roll(acc, HEAD_DIM, axis=1), 1e-30)
        for r in range(B_GROUP):
            c = (g * B_GROUP + r) * N_BRANCH + 1
            o_r = o[r * tq:(r + 1) * tq] * gate[:, c:c + 1]
            slots[r] = o_r if g == 0 else jnp.where(halves[g], o_r, slots[r])
    for r in range(B_GROUP):
        o_ref[0, :, r * LANES:(r + 1) * LANES] = slots[r]


def _selected(qb, sb, gl, kaug, v, bias):
    b, s, w = qb.shape
    tile = lambda width: pl.BlockSpec((1, Q_TILE, width), lambda bi, qi: (bi, qi, 0))
    whole = lambda a: pl.BlockSpec((1,) + a.shape[1:], lambda bi, qi: (bi, 0, 0))
    rows = B_GROUP * Q_TILE
    return pl.pallas_call(
        functools.partial(_selected_kernel, n_bias=bias.shape[1]),
        grid=(b, s // Q_TILE),
        in_specs=[tile(w), tile(LANES), tile(LANES), whole(kaug), whole(v),
                  pl.BlockSpec(bias.shape, lambda bi, qi: (0, 0, 0, 0), pipeline_mode=pl.Buffered(1))],
        out_specs=tile(w),
        out_shape=jax.ShapeDtypeStruct((b, s, w), F32),
        scratch_shapes=[pltpu.VMEM((B_KV_HEADS, rows, LANES), F32), pltpu.VMEM((B_KV_HEADS, rows, LANES), F32),
                        pltpu.VMEM((2, B_KV_HEADS, rows, K_TILE), F32)],
        compiler_params=_cparams("parallel", "parallel"),
        name="selected",
    )(qb, sb, gl, kaug, v, bias)


def _window_kernel(q_ref, gl_ref, k_ref, v_ref, bias_ref, o_ref, *, kw):
    qt = pl.program_id(1)
    tq = q_ref.shape[1]
    gate = jax.nn.sigmoid(gl_ref[0])
    k0 = pl.multiple_of(qt * tq, tq)
    kwin = k_ref[0, pl.ds(k0, kw), :]
    vwin = v_ref[0, pl.ds(k0, kw), :]
    col = lax.broadcasted_iota(jnp.int32, (1, kw), 1)
    real = col >= WIN_LEN - qt * tq
    zero = jnp.zeros((tq, LANES), BF16)
    slots = [None] * B_GROUP
    for g in range(B_KV_HEADS):
        half = _half_mask(g)
        qg = jnp.concatenate([jnp.where(half, q_ref[0, :, r * LANES:(r + 1) * LANES], zero)
                              for r in range(B_GROUP)], axis=0)
        s = lax.dot_general(qg, kwin, NT_DIMS, preferred_element_type=F32)
        s = jnp.where(real, s + bias_ref[g], -jnp.inf)
        p = jnp.exp(s - jnp.max(s, axis=-1, keepdims=True))
        acc = jnp.dot(p.astype(BF16), jnp.where(half, vwin, jnp.ones_like(vwin)), preferred_element_type=F32)
        o = acc / jnp.maximum(pltpu.roll(acc, HEAD_DIM, axis=1), 1e-30)
        for r in range(B_GROUP):
            c = (g * B_GROUP + r) * N_BRANCH + 2
            o_r = o[r * tq:(r + 1) * tq] * gate[:, c:c + 1]
            slots[r] = o_r if g == 0 else jnp.where(half, o_r, slots[r])
    for r in range(B_GROUP):
        o_ref[0, :, r * LANES:(r + 1) * LANES] = slots[r]


def _window(qb, gl, kpad, vpad, bias):
    b, s, w = qb.shape
    kw = bias.shape[-1]
    tile = lambda width: pl.BlockSpec((1, Q_TILE, width), lambda bi, qi: (bi, qi, 0))
    whole = lambda a: pl.BlockSpec((1,) + a.shape[1:], lambda bi, qi: (bi, 0, 0))
    return pl.pallas_call(
        functools.partial(_window_kernel, kw=kw),
        grid=(b, s // Q_TILE),
        in_specs=[tile(w), tile(LANES), whole(kpad), whole(vpad),
                  pl.BlockSpec(bias.shape, lambda bi, qi: (0, 0, 0))],
        out_specs=tile(w),
        out_shape=jax.ShapeDtypeStruct((b, s, w), F32),
        compiler_params=_cparams("parallel", "parallel"),
        name="window",
    )(qb, gl, kpad, vpad, bias)


def _outproj_kernel(x_ref, oa_ref, oc_ref, os_ref, ow_ref, wa_ref, wb_ref, h_ref):
    o_a = jnp.concatenate([oa_ref[p] for p in range(oa_ref.shape[0])], axis=1)
    o_b = oc_ref[...] + os_ref[...] + ow_ref[...]
    y = jnp.dot(o_a.astype(BF16), wa_ref[...], preferred_element_type=F32)
    y = y + jnp.dot(o_b.astype(BF16), wb_ref[...], preferred_element_type=F32)
    h_ref[...] = x_ref[...] + y


def _outproj(x2, oa, obs, wa, wb, tm):
    n, d = x2.shape
    row = lambda width: pl.BlockSpec((tm, width), lambda i: (i, 0))
    full = lambda a: pl.BlockSpec(a.shape, lambda i: (0, 0))
    return pl.pallas_call(
        _outproj_kernel,
        grid=(n // tm,),
        in_specs=[row(d), pl.BlockSpec((oa.shape[0], tm, LANES), lambda i: (0, i, 0))]
                 + [row(o.shape[1]) for o in obs] + [full(wa), full(wb)],
        out_specs=row(d),
        out_shape=jax.ShapeDtypeStruct((n, d), F32),
        compiler_params=_cparams("parallel"),
        name="outproj",
    )(x2, oa, *obs, wa, wb)


def _ffn_kernel(h_ref, g2_ref, wg_ref, wu_ref, wd_ref, gf_ref, o_ref):
    h = h_ref[...]
    y = h * lax.rsqrt(jnp.mean(h * h, axis=-1, keepdims=True) + RMS_EPS)
    hn = (y * g2_ref[...]).astype(BF16)
    a = jnp.dot(hn, wg_ref[...], preferred_element_type=F32)
    u = jnp.dot(hn, wu_ref[...], preferred_element_type=F32)
    act = (jax.nn.silu(a) * u).astype(BF16)
    h2 = h + jnp.dot(act, wd_ref[...], preferred_element_type=F32)
    y2 = h2 * lax.rsqrt(jnp.mean(h2 * h2, axis=-1, keepdims=True) + RMS_EPS)
    o_ref[...] = y2 * gf_ref[...]


def _ffn(h, g2, wg, wu, wd, gf, tm):
    n, d = h.shape
    row = pl.BlockSpec((tm, d), lambda i: (i, 0))
    full = lambda a: pl.BlockSpec(a.shape, lambda i: (0, 0))
    return pl.pallas_call(
        _ffn_kernel,
        grid=(n // tm,),
        in_specs=[row, full(g2), full(wg), full(wu), full(wd), full(gf)],
        out_specs=row,
        out_shape=jax.ShapeDtypeStruct((n, d), F32),
        compiler_params=_cparams("parallel"),
        name="ffn",
    )(h, g2, wg, wu, wd, gf)


def _slot_perm():
    perm = np.zeros(B_WIDTH, np.int64)
    for r in range(B_GROUP):
        for g in range(B_KV_HEADS):
            src = (g * B_GROUP + r) * HEAD_DIM
            dst = r * LANES + g * HEAD_DIM
            perm[dst:dst + HEAD_DIM] = np.arange(src, src + HEAD_DIM)
    return perm


def _layer(h, tab_a, tab_b, norm1_g, w_in, cmp_pos, k_w1, k_b1, k_w2, v_w1, v_b1, v_w2,
           w_out, norm2_g, w_gate, w_up, w_down, norm_f_g):
    b, s, d = h.shape
    n = b * s
    assert s % SPAN == 0 and s % K_TILE == 0
    perm = _slot_perm()

    cols = np.cumsum([0, A_WIDTH, A_WIDTH, A_WIDTH, B_WIDTH] + [KV_WIDTH] * 6 + [GATE_WIDTH])
    w_aq = w_in[:, cols[0]:cols[1]] * SCALE
    w_bq = (w_in[:, cols[3]:cols[4]] * SCALE)[:, perm]
    w_gl = jnp.pad(w_in[:, cols[10]:cols[11]], ((0, 0), (0, LANES - GATE_WIDTH)))
    w1 = jnp.concatenate([w_aq, w_in[:, cols[1]:cols[3]], w_bq, w_in[:, cols[4]:cols[10]], w_gl], axis=1).astype(BF16)
    widths = (A_WIDTH,) * 3 + (B_WIDTH,) + (KV_WIDTH,) * 6 + (LANES,)
    dtypes = (BF16,) * 10 + (F32,)
    slot_major = (True,) * 3 + (False,) * 8
    x2 = h.reshape(n, d)
    qa, ka, va, qb, kc, vc, ksl, vsl, kw, vw, gl = _inproj(x2, norm1_g.reshape(1, d), w1, widths, dtypes,
                                                          slot_major, tm=512)
    r3 = lambda t: t.reshape(b, s, t.shape[-1])
    qb, ksl, vsl, kw, vw, gl = map(r3, (qb, ksl, vsl, kw, vw, gl))

    i = np.arange(DIL_BLOCK)[:, None]
    jj = np.arange(2 * DIL_BLOCK)[None, :]
    dist = i + DIL_BLOCK - jj
    tok_dist = np.stack([np.maximum(dist, 0) * dil for _, dil in DIL_PATTERNS])
    in_window = np.stack([(dist >= 0) & (dist <= window // dil) for window, dil in DIL_PATTERNS])
    bias_a = _bias_tiles(tab_a, tok_dist, in_window, inner=1).reshape((A_HEADS,) + tok_dist.shape)
    r4 = lambda t: t.reshape(A_HEADS // 2, b, s, LANES)
    o_a = _mixer_a(r4(qa), r4(ka), r4(va), bias_a).reshape(A_HEADS // 2, n, LANES)

    nch = s // CMP_STRIDE
    kcmp = _compress(kc.reshape(b, nch, CMP_STRIDE * KV_WIDTH), cmp_pos, k_w1, k_b1, k_w2)
    vcmp = _compress(vc.reshape(b, nch, CMP_STRIDE * KV_WIDTH), cmp_pos, v_w1, v_b1, v_w2)
    nsel = s // SEL_BLOCK
    o_cmp, sb = _cmp_attn(qb, gl, kcmp, vcmp, nsel)

    onehot = (np.arange(s)[:, None] // SEL_BLOCK) == np.arange(HEAD_DIM)[None, :]
    onehot = jnp.broadcast_to(jnp.asarray(onehot, BF16), (b, s, HEAD_DIM))
    kaug = jnp.concatenate([ksl[..., :HEAD_DIM], onehot, onehot, ksl[..., HEAD_DIM:]], axis=-1)
    d_sat = int(np.nonzero(_t5_bucket_np(np.arange(s)) < REL_BUCKETS - 1)[0].max()) + 1
    first_far = -(-(d_sat + BIAS_TILE - 1) // Q_TILE)
    n_bias = min(s // Q_TILE, first_far + 1) + 1
    dd = ((np.arange(n_bias)[:, None, None] - 1) * Q_TILE + np.arange(Q_TILE)[None, :, None]
          - np.arange(BIAS_TILE)[None, None, :])
    bias_sl = _bias_tiles(tab_b, dd, dd >= 0, inner=B_GROUP)
    bias_sl = bias_sl.reshape(B_KV_HEADS, n_bias, B_GROUP * Q_TILE, BIAS_TILE)
    o_slc = _selected(qb, sb, gl, kaug, vsl, bias_sl)

    kwid = WIN_LEN + Q_TILE
    dw = np.arange(Q_TILE)[:, None] + WIN_LEN - np.arange(kwid)[None, :]
    bias_w = _bias_tiles(tab_b, dw[None], ((dw >= 0) & (dw < WIN_LEN))[None], inner=1)
    bias_w = bias_w.reshape(B_KV_HEADS, B_GROUP * Q_TILE, kwid)
    pad = lambda t: jnp.pad(t, ((0, 0), (WIN_LEN, 0), (0, 0)))
    o_win = _window(qb, gl, pad(kw), pad(vw), bias_w)

    wa = w_out[:A_WIDTH].astype(BF16)
    wb = w_out[A_WIDTH:][perm].astype(BF16)
    hmid = _outproj(x2, o_a, [t.reshape(n, B_WIDTH) for t in (o_cmp, o_slc, o_win)], wa, wb, tm=256)
    out = _ffn(hmid, norm2_g.reshape(1, d), w_gate.astype(BF16), w_up.astype(BF16), w_down.astype(BF16),
               norm_f_g.reshape(1, d), tm=256)
    return out.reshape(b, s, d)


def kernel(x, norm1_g, w_in, rel_bias, cmp_pos, cmp_k_w1, cmp_k_b1, cmp_k_w2, cmp_v_w1, cmp_v_b1, cmp_v_w2,
           w_out, norm2_g, w_gate, w_up, w_down, norm_f_g):
    assert w_in.shape[0] == 1, "single-layer model"
    tab_a = rel_bias[:, :A_HEADS].T
    tab_b = rel_bias[:, A_HEADS:].T
    return _layer(x, tab_a, tab_b, norm1_g[0], w_in[0], cmp_pos[0], cmp_k_w1[0], cmp_k_b1[0], cmp_k_w2[0],
                  cmp_v_w1[0], cmp_v_b1[0], cmp_v_w2[0], w_out[0], norm2_g[0], w_gate[0], w_up[0], w_down[0],
                  norm_f_g)
```

```python
import functools
import math

import numpy as np
import jax
import jax.numpy as jnp
from jax import lax
from jax.experimental import pallas as pl
from jax.experimental.pallas import tpu as pltpu

F32 = jnp.float32
BF16 = jnp.bfloat16

HEAD_DIM = 64
LANES = 128
A_HEADS = 8
DIL_PATTERNS = ((128, 1), (512, 4), (2048, 16))
DIL_BLOCK = 128
B_HEADS = 8
B_KV_HEADS = 2
B_GROUP = B_HEADS // B_KV_HEADS
CMP_LEN = 32
CMP_STRIDE = 16
CMP_HIDDEN = 128
SEL_BLOCK = 64
SEL_TOPN = 16
CMP_OVERLAP = (1.0, 2.0, 2.0, 2.0, 1.0)
WIN_LEN = 512
WIN_BLOCK = 128
N_BRANCH = 3
REL_BUCKETS = 32
REL_MAX_DIST = 2048
RMS_EPS = 1e-6
A_WIDTH = A_HEADS * HEAD_DIM
B_WIDTH = B_HEADS * HEAD_DIM
KV_WIDTH = B_KV_HEADS * HEAD_DIM
GATE_WIDTH = B_HEADS * N_BRANCH
SCALE = HEAD_DIM ** -0.5

SPAN = DIL_PATTERNS[-1][1] * DIL_BLOCK
MIX_UNROLL = 16
RANK_CHUNK = 16
Q_TILE = 128
K_TILE = 512
BIAS_TILE = 256
NOT_SELECTED = -(2.0 ** 100)
M_INIT = -1e30
VMEM_LIMIT = 56 * 1024 * 1024

NT_DIMS = (((1,), (1,)), ((), ()))


def _cparams(*sem):
    return pltpu.CompilerParams(dimension_semantics=sem, vmem_limit_bytes=VMEM_LIMIT)


def _t5_bucket_np(dist):
    max_exact = REL_BUCKETS // 2
    d = np.asarray(dist)
    df = np.maximum(d, 1).astype(np.float32)
    large = max_exact + (np.log(df / np.float32(max_exact)) / np.float32(math.log(REL_MAX_DIST / max_exact))
                         * np.float32(REL_BUCKETS - max_exact)).astype(np.int32)
    large = np.minimum(large, REL_BUCKETS - 1)
    return np.where(d < max_exact, d, large).astype(np.int32)


def _bias_kernel(tab_ref, idx_ref, o_ref):
    head = pl.program_id(0) * pl.num_programs(2) + pl.program_id(2)
    idx = idx_ref[0]
    acc = jnp.full(idx.shape, -jnp.inf, F32)
    for bucket in range(REL_BUCKETS):
        acc = jnp.where(idx == bucket, tab_ref[head * REL_BUCKETS + bucket], acc)
    o_ref[0, 0, 0] = acc


def _bias_tiles(tab, dist, valid, inner):
    h = tab.shape[0]
    t, r, c = dist.shape
    idx = jnp.asarray(np.where(valid, _t5_bucket_np(np.maximum(dist, 0)), -1).astype(np.int32))
    return pl.pallas_call(
        _bias_kernel,
        grid=(h // inner, t, inner),
        in_specs=[pl.BlockSpec(memory_space=pltpu.SMEM),
                  pl.BlockSpec((1, r, c), lambda a, ti, bi: (ti, 0, 0))],
        out_specs=pl.BlockSpec((1, 1, 1, r, c), lambda a, ti, bi: (a, ti, bi, 0, 0)),
        out_shape=jax.ShapeDtypeStruct((h // inner, t, inner, r, c), F32),
        compiler_params=_cparams("parallel", "parallel", "parallel"),
        name="bias_tiles",
    )(tab.reshape(-1), idx)


def _half_mask(g):
    lane = lax.broadcasted_iota(jnp.int32, (1, LANES), 1)
    return (lane >= HEAD_DIM) if g else (lane < HEAD_DIM)


def _inproj_kernel(x_ref, g_ref, w_ref, *out_refs, widths):
    x = x_ref[...]
    y = x * lax.rsqrt(jnp.mean(x * x, axis=-1, keepdims=True) + RMS_EPS)
    xn = (y * g_ref[...]).astype(BF16)
    start = 0
    for o_ref, w in zip(out_refs, widths):
        r = jnp.dot(xn, w_ref[:, start:start + w], preferred_element_type=F32)
        if len(o_ref.shape) == 3:
            for p in range(o_ref.shape[0]):
                o_ref[p] = r[:, p * LANES:(p + 1) * LANES].astype(o_ref.dtype)
        else:
            o_ref[...] = r.astype(o_ref.dtype)
        start += w


def _inproj(x2, g, w, widths, dtypes, slot_major, tm):
    n, d = x2.shape
    shape = lambda w_, sm: (w_ // LANES, n, LANES) if sm else (n, w_)
    spec = lambda w_, sm: (pl.BlockSpec((w_ // LANES, tm, LANES), lambda i: (0, i, 0)) if sm
                           else pl.BlockSpec((tm, w_), lambda i: (i, 0)))
    out_shape = [jax.ShapeDtypeStruct(shape(w_, sm), dt) for w_, dt, sm in zip(widths, dtypes, slot_major)]
    return pl.pallas_call(
        functools.partial(_inproj_kernel, widths=widths),
        grid=(n // tm,),
        in_specs=[pl.BlockSpec((tm, d), lambda i: (i, 0)),
                  pl.BlockSpec((1, d), lambda i: (0, 0)),
                  pl.BlockSpec(w.shape, lambda i: (0, 0))],
        out_specs=[spec(w_, sm) for w_, sm in zip(widths, slot_major)],
        out_shape=out_shape,
        compiler_params=_cparams("parallel"),
        name="inproj",
    )(x2, g, w)


def _mixer_a_kernel(q_ref, kp_ref, kc_ref, vp_ref, vc_ref, bias_ref, o_ref, qf, kf, vf, acc_scr, m_scr):
    sb = pl.program_id(2)
    qf[...] = q_ref[0, 0].astype(F32)
    kf[:SPAN] = kp_ref[0, 0].astype(F32)
    kf[SPAN:] = kc_ref[0, 0].astype(F32)
    vf[:SPAN] = vp_ref[0, 0].astype(F32)
    vf[SPAN:] = vc_ref[0, 0].astype(F32)
    halves = [_half_mask(hh) for hh in range(2)]
    col = lax.broadcasted_iota(jnp.int32, (1, 2 * DIL_BLOCK), 1)
    ones = jnp.ones((2 * DIL_BLOCK, LANES), BF16)
    n_blocks = SPAN // DIL_BLOCK
    for p, (_, dil) in enumerate(DIL_PATTERNS):
        step = dil * DIL_BLOCK
        per_class = SPAN // step

        def body(grp, carry, p=p, dil=dil, step=step, per_class=per_class):
            for u in range(MIX_UNROLL):
                idx = grp * MIX_UNROLL + u
                r, n = idx // per_class, idx % per_class
                q0 = pl.multiple_of(n * step, step) + r
                rows_q = pl.ds(q0, DIL_BLOCK, stride=dil)
                rows_k = pl.ds(q0 + SPAN - step, 2 * DIL_BLOCK, stride=dil)
                q2 = qf[rows_q, :].astype(BF16)
                kcat = kf[rows_k, :].astype(BF16)
                vcat = vf[rows_k, :].astype(BF16)
                keep = (col >= DIL_BLOCK) | (sb * per_class + n > 0)
                for hh in range(2):
                    qm = jnp.where(halves[hh], q2, jnp.zeros_like(q2))
                    s = lax.dot_general(qm, kcat, NT_DIMS, preferred_element_type=F32)
                    s = jnp.where(keep, s + bias_ref[hh, p], -jnp.inf)
                    m_blk = jnp.max(s, axis=-1, keepdims=True)
                    pe = jnp.exp(s - m_blk)
                    pv = jnp.dot(pe.astype(BF16), jnp.where(halves[hh], vcat, ones), preferred_element_type=F32)
                    acc_scr[p, hh, rows_q, :] = pv
                    m_scr[p, hh, rows_q, :] = jnp.broadcast_to(m_blk, pv.shape)
            return carry

        lax.fori_loop(0, n_blocks // MIX_UNROLL, body, 0)

    def finish(c, carry):
        rows = pl.ds(pl.multiple_of(c * DIL_BLOCK, DIL_BLOCK), DIL_BLOCK)
        outs = []
        for hh in range(2):
            ms = [m_scr[p, hh, rows, :] for p in range(len(DIL_PATTERNS))]
            m_all = functools.reduce(jnp.maximum, ms)
            tot = sum(jnp.exp(m - m_all) * acc_scr[p, hh, rows, :] for p, m in enumerate(ms))
            outs.append(tot / pltpu.roll(tot, HEAD_DIM, axis=1))
        o_ref[0, 0, rows, :] = jnp.where(halves[0], outs[0], outs[1]).astype(o_ref.dtype)
        return carry

    lax.fori_loop(0, n_blocks, finish, 0, unroll=4)


def _mixer_a(q, k, v, bias):
    npair, b, s, _ = q.shape
    blk = (1, 1, SPAN, LANES)
    cur = lambda bi, pi, si: (pi, bi, si, 0)
    prev = lambda bi, pi, si: (pi, bi, jnp.maximum(si - 1, 0), 0)
    return pl.pallas_call(
        _mixer_a_kernel,
        grid=(b, npair, s // SPAN),
        in_specs=[pl.BlockSpec(blk, cur), pl.BlockSpec(blk, prev), pl.BlockSpec(blk, cur),
                  pl.BlockSpec(blk, prev), pl.BlockSpec(blk, cur),
                  pl.BlockSpec((2,) + bias.shape[1:], lambda bi, pi, si: (pi, 0, 0, 0))],
        out_specs=pl.BlockSpec(blk, cur),
        out_shape=jax.ShapeDtypeStruct(q.shape, BF16),
        scratch_shapes=[pltpu.VMEM((SPAN, LANES), F32), pltpu.VMEM((2 * SPAN, LANES), F32),
                        pltpu.VMEM((2 * SPAN, LANES), F32),
                        pltpu.VMEM((len(DIL_PATTERNS), 2, SPAN, LANES), F32),
                        pltpu.VMEM((len(DIL_PATTERNS), 2, SPAN, LANES), F32)],
        compiler_params=_cparams("parallel", "parallel", "parallel"),
        name="mixer_a",
    )(q, k, k, v, v, bias)


def _compress_kernel(c_ref, pa_ref, pb_ref, w1a_ref, w1b_ref, b1_ref, w2_ref, o_ref):
    c = c_ref[0].astype(F32)
    xa = (c + pa_ref[...]).astype(BF16)
    xb = (c + pb_ref[...]).astype(BF16)
    ha = jnp.dot(xa, w1a_ref[...], preferred_element_type=F32)
    hb = jnp.dot(xb, w1b_ref[...], preferred_element_type=F32)
    hb_next = jnp.concatenate([hb[1:], jnp.zeros_like(hb[:1])], axis=0)
    hid = jax.nn.gelu(ha + hb_next + b1_ref[...])
    o_ref[0] = jnp.dot(hid.astype(BF16), w2_ref[...], preferred_element_type=F32).astype(o_ref.dtype)


def _compress(c, pos, w1, b1, w2):
    b, nch, _ = c.shape
    half = CMP_LEN // 2
    zero = jnp.zeros((half, HEAD_DIM, CMP_HIDDEN), F32)

    def grouped(wpart):
        g0 = jnp.concatenate([wpart, zero], axis=1).reshape(half * LANES, CMP_HIDDEN)
        g1 = jnp.concatenate([zero, wpart], axis=1).reshape(half * LANES, CMP_HIDDEN)
        return jnp.concatenate([g0, g1], axis=1).astype(BF16)

    w1a, w1b = grouped(w1[:half]), grouped(w1[half:])
    pa = jnp.tile(pos[:half], (1, 2)).reshape(1, half * LANES)
    pb = jnp.tile(pos[half:], (1, 2)).reshape(1, half * LANES)
    b1g = jnp.tile(b1, 2).reshape(1, 2 * CMP_HIDDEN)
    zw = jnp.zeros_like(w2)
    w2g = jnp.concatenate([jnp.concatenate([w2, zw], axis=1),
                           jnp.concatenate([zw, w2], axis=1)], axis=0).astype(BF16)
    full = lambda a: pl.BlockSpec(a.shape, lambda i: (0,) * a.ndim)
    return pl.pallas_call(
        _compress_kernel,
        grid=(b,),
        in_specs=[pl.BlockSpec((1,) + c.shape[1:], lambda i: (i, 0, 0)),
                  full(pa), full(pb), full(w1a), full(w1b), full(b1g), full(w2g)],
        out_specs=pl.BlockSpec((1, nch, LANES), lambda i: (i, 0, 0)),
        out_shape=jax.ShapeDtypeStruct((b, nch, LANES), BF16),
        compiler_params=_cparams("parallel"),
        name="compress",
    )(c, pa, pb, w1a, w1b, b1g, w2g)


def _cmp_attn_kernel(q_ref, gl_ref, kc_ref, vc_ref, wov_ref, o_ref, sb_ref, *, nsel):
    qt = pl.program_id(1)
    tq = q_ref.shape[1]
    nck = kc_ref.shape[1]
    t_row = qt * tq + lax.broadcasted_iota(jnp.int32, (tq, 1), 0)
    blk_end = lax.broadcasted_iota(jnp.int32, (1, nck), 1) * CMP_STRIDE + (CMP_LEN - 1)
    valid = blk_end <= t_row
    gate = jax.nn.sigmoid(gl_ref[0])
    kc = kc_ref[0]
    vc = vc_ref[0]
    pc = [jnp.zeros((tq, nck), F32) for _ in range(B_KV_HEADS)]
    for r in range(B_GROUP):
        sl = slice(r * LANES, (r + 1) * LANES)
        q2 = q_ref[0, :, sl]
        o_slot = None
        for g in range(B_KV_HEADS):
            half = _half_mask(g)
            qm = jnp.where(half, q2, jnp.zeros_like(q2))
            s = lax.dot_general(qm, kc, NT_DIMS, preferred_element_type=F32)
            s = jnp.where(valid, s, -jnp.inf)
            m = jnp.max(s, axis=-1, keepdims=True)
            m = jnp.where(m == -jnp.inf, 0.0, m)
            p = jnp.exp(s - m)
            den = jnp.sum(p, axis=-1, keepdims=True)
            p = p / jnp.maximum(den, 1e-30)
            pc[g] = pc[g] + p
            c = (g * B_GROUP + r) * N_BRANCH
            o = jnp.dot(p.astype(BF16), vc, preferred_element_type=F32) * gate[:, c:c + 1]
            o_slot = o if g == 0 else jnp.where(half, o, o_slot)
        o_ref[0, :, sl] = o_slot

    t_lane = qt * tq + lax.broadcasted_iota(jnp.int32, (1, tq), 1)
    cur = t_lane // SEL_BLOCK
    j = lax.broadcasted_iota(jnp.int32, (nsel, 1), 0)
    forced = (j == 0) | (j == cur) | (j == cur - 1)
    imps = []
    for g in reversed(range(B_KV_HEADS)):
        imp = lax.dot_general(wov_ref[...], pc[g], NT_DIMS, preferred_element_type=F32,
                              precision=lax.Precision.HIGHEST)
        imps.append(jnp.where(j > cur, -jnp.inf, jnp.where(forced, jnp.inf, imp)))

    last_block = (qt * tq + tq - 1) // SEL_BLOCK
    for level in range(-(-nsel // RANK_CHUNK)):
        n_live = min((level + 1) * RANK_CHUNK, nsel)

        @pl.when(last_block // RANK_CHUNK == level)
        def _(n_live=n_live):
            rows = []
            for imp in imps:
                if n_live <= SEL_TOPN:
                    rows.append(jnp.zeros((HEAD_DIM, tq), F32))
                    continue
                live = imp[:n_live]
                jl = j[:n_live]
                rank = jnp.zeros((n_live, tq), jnp.int32)
                for jp in range(n_live):
                    row = live[jp:jp + 1, :]
                    rank = rank + jnp.where(jl > jp, (row >= live).astype(jnp.int32), (row > live).astype(jnp.int32))
                rows.append(jnp.where(rank < SEL_TOPN, 0.0, NOT_SELECTED))
                if n_live < HEAD_DIM:
                    rows.append(jnp.zeros((HEAD_DIM - n_live, tq), F32))
            sb_ref[0] = jnp.concatenate(rows, axis=0).T.astype(sb_ref.dtype)


def _cmp_attn(qb, gl, kcmp, vcmp, nsel):
    b, s, w = qb.shape
    nck = kcmp.shape[1]
    ratio = SEL_BLOCK // CMP_STRIDE
    wov = np.zeros((nsel, nck), np.float32)
    for jj in range(nsel):
        for off, wt in zip(range(-1, ratio), CMP_OVERLAP):
            n = ratio * jj + off
            if 0 <= n < nck - 1:
                wov[jj, n] = wt
    wov = jnp.asarray(wov)
    tile = lambda width: pl.BlockSpec((1, Q_TILE, width), lambda bi, qi: (bi, qi, 0))
    whole = lambda a: pl.BlockSpec((1,) + a.shape[1:], lambda bi, qi: (bi, 0, 0))
    return pl.pallas_call(
        functools.partial(_cmp_attn_kernel, nsel=nsel),
        grid=(b, s // Q_TILE),
        in_specs=[tile(w), tile(LANES), whole(kcmp), whole(vcmp),
                  pl.BlockSpec(wov.shape, lambda bi, qi: (0, 0))],
        out_specs=[tile(w), tile(LANES)],
        out_shape=[jax.ShapeDtypeStruct((b, s, w), F32), jax.ShapeDtypeStruct((b, s, LANES), BF16)],
        compiler_params=_cparams("parallel", "parallel"),
        name="cmp_attn",
    )(qb, gl, kcmp, vcmp, wov)


def _selected_kernel(q_ref, sb_ref, gl_ref, kaug_ref, v_ref, bias_ref, prev_ref, o_ref, acc_ref, m_ref, s_ref, *,
                     n_bias):
    qt = pl.program_id(1)
    tq = q_ref.shape[1]
    gate = jax.nn.sigmoid(gl_ref[0])
    sb = sb_ref[0]
    n_steps = (qt * tq + tq + K_TILE - 1) // K_TILE
    halves = [_half_mask(g) for g in range(B_KV_HEADS)]
    qaug = [jnp.concatenate([jnp.where(halves[g], q_ref[0, :, r * LANES:(r + 1) * LANES], sb)
                             for r in range(B_GROUP)], axis=0) for g in range(B_KV_HEADS)]
    acc_ref[...] = jnp.zeros_like(acc_ref)
    m_ref[...] = jnp.full_like(m_ref, M_INIT)

    def scores(kt):
        for j in range(K_TILE // BIAS_TILE):
            k0 = pl.multiple_of(kt * K_TILE + j * BIAS_TILE, BIAS_TILE)
            bi = jnp.clip(qt - k0 // tq + 1, 0, n_bias - 1)
            for g in range(B_KV_HEADS):
                k = kaug_ref[0, pl.ds(k0, BIAS_TILE), g * LANES:(g + 1) * LANES]
                s = lax.dot_general(qaug[g], k, NT_DIMS, preferred_element_type=F32)
                s_ref[kt % 2, g, :, j * BIAS_TILE:(j + 1) * BIAS_TILE] = s + bias_ref[g, bi]

    def accumulate(kt):
        k0 = pl.multiple_of(kt * K_TILE, K_TILE)
        v2 = v_ref[0, pl.ds(k0, K_TILE), :]
        for g in range(B_KV_HEADS):
            v = jnp.where(halves[g], v2, jnp.ones((K_TILE, LANES), BF16))
            s = s_ref[kt % 2, g]
            m_old = m_ref[g]
            m_new = jnp.maximum(m_old, jnp.max(s, axis=-1, keepdims=True))
            alpha = jnp.exp(m_old - m_new)
            p = jnp.exp(s - jnp.tile(m_new, (1, K_TILE // LANES)))
            acc_ref[g] = alpha * acc_ref[g] + jnp.dot(p.astype(BF16), v, preferred_element_type=F32)
            m_ref[g] = m_new

    def body(kt, carry):
        accumulate(kt)
        scores(kt + 1)
        return carry

    scores(0)
    lax.fori_loop(0, n_steps - 1, body, 0)
    accumulate(n_steps - 1)
    slots = [None] * B_GROUP
    for g in range(B_KV_HEADS):
        acc = acc_ref[g]
        o = acc / jnp.maximum(pltpu.roll(acc, HEAD_DIM, axis=1), 1e-30)
        for r in range(B_GROUP):
            c = (g * B_GROUP + r) * N_BRANCH + 1
            o_r = o[r * tq:(r + 1) * tq] * gate[:, c:c + 1]
            slots[r] = o_r if g == 0 else jnp.where(halves[g], o_r, slots[r])
    for r in range(B_GROUP):
        sl = slice(r * LANES, (r + 1) * LANES)
        o_ref[0, :, sl] = prev_ref[0, :, sl] + slots[r]


def _selected(qb, sb, gl, kaug, v, bias, prev):
    b, s, w = qb.shape
    tile = lambda width: pl.BlockSpec((1, Q_TILE, width), lambda bi, qi: (bi, qi, 0))
    whole = lambda a: pl.BlockSpec((1,) + a.shape[1:], lambda bi, qi: (bi, 0, 0))
    rows = B_GROUP * Q_TILE
    return pl.pallas_call(
        functools.partial(_selected_kernel, n_bias=bias.shape[1]),
        grid=(b, s // Q_TILE),
        in_specs=[tile(w), tile(LANES), tile(LANES), whole(kaug), whole(v),
                  pl.BlockSpec(bias.shape, lambda bi, qi: (0, 0, 0, 0), pipeline_mode=pl.Buffered(1)), tile(w)],
        out_specs=tile(w),
        out_shape=jax.ShapeDtypeStruct((b, s, w), F32),
        scratch_shapes=[pltpu.VMEM((B_KV_HEADS, rows, LANES), F32), pltpu.VMEM((B_KV_HEADS, rows, LANES), F32),
                        pltpu.VMEM((2, B_KV_HEADS, rows, K_TILE), F32)],
        compiler_params=_cparams("parallel", "parallel"),
        name="selected",
    )(qb, sb, gl, kaug, v, bias, prev)


def _window_kernel(q_ref, gl_ref, k_ref, v_ref, bias_ref, prev_ref, o_ref, *, kw):
    qt = pl.program_id(1)
    tq = q_ref.shape[1]
    gate = jax.nn.sigmoid(gl_ref[0])
    k0 = pl.multiple_of(qt * tq, tq)
    kwin = k_ref[0, pl.ds(k0, kw), :]
    vwin = v_ref[0, pl.ds(k0, kw), :]
    col = lax.broadcasted_iota(jnp.int32, (1, kw), 1)
    real = col >= WIN_LEN - qt * tq
    zero = jnp.zeros((tq, LANES), BF16)
    slots = [None] * B_GROUP
    for g in range(B_KV_HEADS):
        half = _half_mask(g)
        qg = jnp.concatenate([jnp.where(half, q_ref[0, :, r * LANES:(r + 1) * LANES], zero)
                              for r in range(B_GROUP)], axis=0)
        s = lax.dot_general(qg, kwin, NT_DIMS, preferred_element_type=F32)
        s = jnp.where(real, s + bias_ref[g], -jnp.inf)
        p = jnp.exp(s - jnp.max(s, axis=-1, keepdims=True))
        acc = jnp.dot(p.astype(BF16), jnp.where(half, vwin, jnp.ones_like(vwin)), preferred_element_type=F32)
        o = acc / jnp.maximum(pltpu.roll(acc, HEAD_DIM, axis=1), 1e-30)
        for r in range(B_GROUP):
            c = (g * B_GROUP + r) * N_BRANCH + 2
            o_r = o[r * tq:(r + 1) * tq] * gate[:, c:c + 1]
            slots[r] = o_r if g == 0 else jnp.where(half, o_r, slots[r])
    for r in range(B_GROUP):
        sl = slice(r * LANES, (r + 1) * LANES)
        o_ref[0, :, sl] = (prev_ref[0, :, sl] + slots[r]).astype(o_ref.dtype)


def _window(qb, gl, kpad, vpad, bias, prev):
    b, s, w = qb.shape
    kw = bias.shape[-1]
    tile = lambda width: pl.BlockSpec((1, Q_TILE, width), lambda bi, qi: (bi, qi, 0))
    whole = lambda a: pl.BlockSpec((1,) + a.shape[1:], lambda bi, qi: (bi, 0, 0))
    return pl.pallas_call(
        functools.partial(_window_kernel, kw=kw),
        grid=(b, s // Q_TILE),
        in_specs=[tile(w), tile(LANES), whole(kpad), whole(vpad),
                  pl.BlockSpec(bias.shape, lambda bi, qi: (0, 0, 0)), tile(w)],
        out_specs=tile(w),
        out_shape=jax.ShapeDtypeStruct((b, s, w), BF16),
        compiler_params=_cparams("parallel", "parallel"),
        name="window",
    )(qb, gl, kpad, vpad, bias, prev)


def _tail_kernel(x_ref, oa_ref, ob_ref, wa_ref, wb_ref, g2_ref, wg_ref, wu_ref, wd_ref, gf_ref, o_ref):
    o_a = jnp.concatenate([oa_ref[p] for p in range(oa_ref.shape[0])], axis=1)
    mix = jnp.dot(o_a, wa_ref[...], preferred_element_type=F32)
    mix = mix + jnp.dot(ob_ref[...], wb_ref[...], preferred_element_type=F32)
    h = x_ref[...] + mix
    y = h * lax.rsqrt(jnp.mean(h * h, axis=-1, keepdims=True) + RMS_EPS)
    hn = (y * g2_ref[...]).astype(BF16)
    a = jnp.dot(hn, wg_ref[...], preferred_element_type=F32)
    u = jnp.dot(hn, wu_ref[...], preferred_element_type=F32)
    act = (jax.nn.silu(a) * u).astype(BF16)
    h2 = h + jnp.dot(act, wd_ref[...], preferred_element_type=F32)
    y2 = h2 * lax.rsqrt(jnp.mean(h2 * h2, axis=-1, keepdims=True) + RMS_EPS)
    o_ref[...] = y2 * gf_ref[...]


def _tail(x2, oa, ob, wa, wb, g2, wg, wu, wd, gf, tm):
    n, d = x2.shape
    row = lambda width: pl.BlockSpec((tm, width), lambda i: (i, 0))
    const = lambda a: pl.BlockSpec(a.shape, lambda i: (0, 0), pipeline_mode=pl.Buffered(1))
    return pl.pallas_call(
        _tail_kernel,
        grid=(n // tm,),
        in_specs=[row(d), pl.BlockSpec((oa.shape[0], tm, LANES), lambda i: (0, i, 0)), row(ob.shape[1]),
                  const(wa), const(wb), const(g2), const(wg), const(wu), const(wd), const(gf)],
        out_specs=row(d),
        out_shape=jax.ShapeDtypeStruct((n, d), F32),
        compiler_params=_cparams("parallel"),
        name="tail",
    )(x2, oa, ob, wa, wb, g2, wg, wu, wd, gf)


def _slot_perm():
    perm = np.zeros(B_WIDTH, np.int64)
    for r in range(B_GROUP):
        for g in range(B_KV_HEADS):
            src = (g * B_GROUP + r) * HEAD_DIM
            dst = r * LANES + g * HEAD_DIM
            perm[dst:dst + HEAD_DIM] = np.arange(src, src + HEAD_DIM)
    return perm


def _layer(h, tab_a, tab_b, norm1_g, w_in, cmp_pos, k_w1, k_b1, k_w2, v_w1, v_b1, v_w2,
           w_out, norm2_g, w_gate, w_up, w_down, norm_f_g):
    b, s, d = h.shape
    n = b * s
    assert s % SPAN == 0 and s % K_TILE == 0
    perm = _slot_perm()

    cols = np.cumsum([0, A_WIDTH, A_WIDTH, A_WIDTH, B_WIDTH] + [KV_WIDTH] * 6 + [GATE_WIDTH])
    w_aq = w_in[:, cols[0]:cols[1]] * SCALE
    w_bq = (w_in[:, cols[3]:cols[4]] * SCALE)[:, perm]
    w_gl = jnp.pad(w_in[:, cols[10]:cols[11]], ((0, 0), (0, LANES - GATE_WIDTH)))
    w1 = jnp.concatenate([w_aq, w_in[:, cols[1]:cols[3]], w_bq, w_in[:, cols[4]:cols[10]], w_gl], axis=1).astype(BF16)
    widths = (A_WIDTH,) * 3 + (B_WIDTH,) + (KV_WIDTH,) * 6 + (LANES,)
    dtypes = (BF16,) * 10 + (F32,)
    slot_major = (True,) * 3 + (False,) * 8
    x2 = h.reshape(n, d)
    qa, ka, va, qb, kc, vc, ksl, vsl, kw, vw, gl = _inproj(x2, norm1_g.reshape(1, d), w1, widths, dtypes,
                                                          slot_major, tm=512)
    r3 = lambda t: t.reshape(b, s, t.shape[-1])
    qb, ksl, vsl, kw, vw, gl = map(r3, (qb, ksl, vsl, kw, vw, gl))

    i = np.arange(DIL_BLOCK)[:, None]
    jj = np.arange(2 * DIL_BLOCK)[None, :]
    dist = i + DIL_BLOCK - jj
    tok_dist = np.stack([np.maximum(dist, 0) * dil for _, dil in DIL_PATTERNS])
    in_window = np.stack([(dist >= 0) & (dist <= window // dil) for window, dil in DIL_PATTERNS])
    bias_a = _bias_tiles(tab_a, tok_dist, in_window, inner=1).reshape((A_HEADS,) + tok_dist.shape)
    r4 = lambda t: t.reshape(A_HEADS // 2, b, s, LANES)
    o_a = _mixer_a(r4(qa), r4(ka), r4(va), bias_a).reshape(A_HEADS // 2, n, LANES)

    nch = s // CMP_STRIDE
    kcmp = _compress(kc.reshape(b, nch, CMP_STRIDE * KV_WIDTH), cmp_pos, k_w1, k_b1, k_w2)
    vcmp = _compress(vc.reshape(b, nch, CMP_STRIDE * KV_WIDTH), cmp_pos, v_w1, v_b1, v_w2)
    nsel = s // SEL_BLOCK
    o_cmp, sb = _cmp_attn(qb, gl, kcmp, vcmp, nsel)

    onehot = (np.arange(s)[:, None] // SEL_BLOCK) == np.arange(HEAD_DIM)[None, :]
    onehot = jnp.broadcast_to(jnp.asarray(onehot, BF16), (b, s, HEAD_DIM))
    kaug = jnp.concatenate([ksl[..., :HEAD_DIM], onehot, onehot, ksl[..., HEAD_DIM:]], axis=-1)
    d_sat = int(np.nonzero(_t5_bucket_np(np.arange(s)) < REL_BUCKETS - 1)[0].max()) + 1
    first_far = -(-(d_sat + BIAS_TILE - 1) // Q_TILE)
    n_bias = min(s // Q_TILE, first_far + 1) + 1
    dd = ((np.arange(n_bias)[:, None, None] - 1) * Q_TILE + np.arange(Q_TILE)[None, :, None]
          - np.arange(BIAS_TILE)[None, None, :])
    bias_sl = _bias_tiles(tab_b, dd, dd >= 0, inner=B_GROUP)
    bias_sl = bias_sl.reshape(B_KV_HEADS, n_bias, B_GROUP * Q_TILE, BIAS_TILE)
    o_cs = _selected(qb, sb, gl, kaug, vsl, bias_sl, o_cmp)

    kwid = WIN_LEN + Q_TILE
    dw = np.arange(Q_TILE)[:, None] + WIN_LEN - np.arange(kwid)[None, :]
    bias_w = _bias_tiles(tab_b, dw[None], ((dw >= 0) & (dw < WIN_LEN))[None], inner=1)
    bias_w = bias_w.reshape(B_KV_HEADS, B_GROUP * Q_TILE, kwid)
    pad = lambda t: jnp.pad(t, ((0, 0), (WIN_LEN, 0), (0, 0)))
    o_b = _window(qb, gl, pad(kw), pad(vw), bias_w, o_cs)

    wa = w_out[:A_WIDTH].astype(BF16)
    wb = w_out[A_WIDTH:][perm].astype(BF16)
    out = _tail(x2, o_a, o_b.reshape(n, B_WIDTH), wa, wb, norm2_g.reshape(1, d), w_gate.astype(BF16),
                w_up.astype(BF16), w_down.astype(BF16), norm_f_g.reshape(1, d), tm=256)
    return out.reshape(b, s, d)


def kernel(x, norm1_g, w_in, rel_bias, cmp_pos, cmp_k_w1, cmp_k_b1, cmp_k_w2, cmp_v_w1, cmp_v_b1, cmp_v_w2,
           w_out, norm2_g, w_gate, w_up, w_down, norm_f_g):
    assert w_in.shape[0] == 1, "single-layer model"
    tab_a = rel_bias[:, :A_HEADS].T
    tab_b = rel_bias[:, A_HEADS:].T
    return _layer(x, tab_a, tab_b, norm1_g[0], w_in[0], cmp_pos[0], cmp_k_w1[0], cmp_k_b1[0], cmp_k_w2[0],
                  cmp_v_w1[0], cmp_v_b1[0], cmp_v_w2[0], w_out[0], norm2_g[0], w_gate[0], w_up[0], w_down[0],
                  norm_f_g)
```

```python
import functools
import math

import numpy as np
import jax
import jax.numpy as jnp
from jax import lax
from jax.experimental import pallas as pl
from jax.experimental.pallas import tpu as pltpu

F32 = jnp.float32
BF16 = jnp.bfloat16

HEAD_DIM = 64
LANES = 128
A_HEADS = 8
DIL_PATTERNS = ((128, 1), (512, 4), (2048, 16))
DIL_BLOCK = 128
B_HEADS = 8
B_KV_HEADS = 2
B_GROUP = B_HEADS // B_KV_HEADS
CMP_LEN = 32
CMP_STRIDE = 16
CMP_HIDDEN = 128
SEL_BLOCK = 64
SEL_TOPN = 16
CMP_OVERLAP = (1.0, 2.0, 2.0, 2.0, 1.0)
WIN_LEN = 512
N_BRANCH = 3
REL_BUCKETS = 32
REL_MAX_DIST = 2048
RMS_EPS = 1e-6
A_WIDTH = A_HEADS * HEAD_DIM
B_WIDTH = B_HEADS * HEAD_DIM
KV_WIDTH = B_KV_HEADS * HEAD_DIM
GATE_WIDTH = B_HEADS * N_BRANCH
SCALE = HEAD_DIM ** -0.5

SPAN = DIL_PATTERNS[-1][1] * DIL_BLOCK
NARROW_FROM = 4
MIX_UNROLL = 16
RANK_CHUNK = 16
Q_TILE = 128
K_TILE = 512
BIAS_TILE = 256
NOT_SELECTED = -(2.0 ** 100)
M_INIT = -1e30
VMEM_LIMIT = 56 * 1024 * 1024

NT_DIMS = (((1,), (1,)), ((), ()))


def _cparams(*sem):
    return pltpu.CompilerParams(dimension_semantics=sem, vmem_limit_bytes=VMEM_LIMIT)


def _t5_bucket_np(dist):
    max_exact = REL_BUCKETS // 2
    d = np.asarray(dist)
    df = np.maximum(d, 1).astype(np.float32)
    large = max_exact + (np.log(df / np.float32(max_exact)) / np.float32(math.log(REL_MAX_DIST / max_exact))
                         * np.float32(REL_BUCKETS - max_exact)).astype(np.int32)
    large = np.minimum(large, REL_BUCKETS - 1)
    return np.where(d < max_exact, d, large).astype(np.int32)


def _bias_kernel(tab_ref, idx_ref, o_ref):
    head = pl.program_id(0) * pl.num_programs(2) + pl.program_id(2)
    idx = idx_ref[0]
    acc = jnp.full(idx.shape, -jnp.inf, F32)
    for bucket in range(REL_BUCKETS):
        acc = jnp.where(idx == bucket, tab_ref[head * REL_BUCKETS + bucket], acc)
    o_ref[0, 0, 0] = acc


def _bias_tiles(tab, dist, valid, inner):
    h = tab.shape[0]
    t, r, c = dist.shape
    idx = jnp.asarray(np.where(valid, _t5_bucket_np(np.maximum(dist, 0)), -1).astype(np.int32))
    return pl.pallas_call(
        _bias_kernel,
        grid=(h // inner, t, inner),
        in_specs=[pl.BlockSpec(memory_space=pltpu.SMEM),
                  pl.BlockSpec((1, r, c), lambda a, ti, bi: (ti, 0, 0))],
        out_specs=pl.BlockSpec((1, 1, 1, r, c), lambda a, ti, bi: (a, ti, bi, 0, 0)),
        out_shape=jax.ShapeDtypeStruct((h // inner, t, inner, r, c), F32),
        compiler_params=_cparams("parallel", "parallel", "parallel"),
        name="bias_tiles",
    )(tab.reshape(-1), idx)


def _half_mask(g):
    lane = lax.broadcasted_iota(jnp.int32, (1, LANES), 1)
    return (lane >= HEAD_DIM) if g else (lane < HEAD_DIM)


def _inproj_kernel(x_ref, g_ref, w_ref, *out_refs, widths):
    x = x_ref[...]
    y = x * lax.rsqrt(jnp.mean(x * x, axis=-1, keepdims=True) + RMS_EPS)
    xn = (y * g_ref[...]).astype(BF16)
    merged = jnp.dot(xn, w_ref[:, sum(widths[:NARROW_FROM]):], preferred_element_type=F32)
    start = 0
    for idx, (o_ref, w) in enumerate(zip(out_refs, widths)):
        if idx < NARROW_FROM:
            r = jnp.dot(xn, w_ref[:, start:start + w], preferred_element_type=F32)
        else:
            off = start - sum(widths[:NARROW_FROM])
            r = merged[:, off:off + w]
        if len(o_ref.shape) == 3:
            for p in range(o_ref.shape[0]):
                o_ref[p] = r[:, p * LANES:(p + 1) * LANES].astype(o_ref.dtype)
        else:
            o_ref[...] = r.astype(o_ref.dtype)
        start += w


def _inproj(x2, g, w, widths, dtypes, slot_major, tm):
    n, d = x2.shape
    shape = lambda w_, sm: (w_ // LANES, n, LANES) if sm else (n, w_)
    spec = lambda w_, sm: (pl.BlockSpec((w_ // LANES, tm, LANES), lambda i: (0, i, 0)) if sm
                           else pl.BlockSpec((tm, w_), lambda i: (i, 0)))
    out_shape = [jax.ShapeDtypeStruct(shape(w_, sm), dt) for w_, dt, sm in zip(widths, dtypes, slot_major)]
    return pl.pallas_call(
        functools.partial(_inproj_kernel, widths=widths),
        grid=(n // tm,),
        in_specs=[pl.BlockSpec((tm, d), lambda i: (i, 0)),
                  pl.BlockSpec((1, d), lambda i: (0, 0)),
                  pl.BlockSpec(w.shape, lambda i: (0, 0))],
        out_specs=[spec(w_, sm) for w_, sm in zip(widths, slot_major)],
        out_shape=out_shape,
        compiler_params=_cparams("parallel"),
        name="inproj",
    )(x2, g, w)


def _mixer_a_kernel(q_ref, kp_ref, kc_ref, vp_ref, vc_ref, bias_ref, o_ref, qf, kf, vf, acc_scr, m_scr):
    sb = pl.program_id(2)
    qf[...] = q_ref[0, 0].astype(F32)
    kf[:SPAN] = kp_ref[0, 0].astype(F32)
    kf[SPAN:] = kc_ref[0, 0].astype(F32)
    vf[:SPAN] = vp_ref[0, 0].astype(F32)
    vf[SPAN:] = vc_ref[0, 0].astype(F32)
    halves = [_half_mask(hh) for hh in range(2)]
    col = lax.broadcasted_iota(jnp.int32, (1, 2 * DIL_BLOCK), 1)
    ones = jnp.ones((2 * DIL_BLOCK, LANES), BF16)
    n_blocks = SPAN // DIL_BLOCK
    for p, (_, dil) in enumerate(DIL_PATTERNS):
        step = dil * DIL_BLOCK
        per_class = SPAN // step

        def body(grp, carry, p=p, dil=dil, step=step, per_class=per_class):
            for u in range(MIX_UNROLL):
                idx = grp * MIX_UNROLL + u
                r, n = idx // per_class, idx % per_class
                q0 = pl.multiple_of(n * step, step) + r
                rows_q = pl.ds(q0, DIL_BLOCK, stride=dil)
                rows_k = pl.ds(q0 + SPAN - step, 2 * DIL_BLOCK, stride=dil)
                q2 = qf[rows_q, :].astype(BF16)
                kcat = kf[rows_k, :].astype(BF16)
                vcat = vf[rows_k, :].astype(BF16)
                keep = (col >= DIL_BLOCK) | (sb * per_class + n > 0)
                for hh in range(2):
                    qm = jnp.where(halves[hh], q2, jnp.zeros_like(q2))
                    s = lax.dot_general(qm, kcat, NT_DIMS, preferred_element_type=F32)
                    s = jnp.where(keep, s + bias_ref[hh, p], -jnp.inf)
                    m_blk = jnp.max(s, axis=-1, keepdims=True)
                    pe = jnp.exp(s - m_blk)
                    pv = jnp.dot(pe.astype(BF16), jnp.where(halves[hh], vcat, ones), preferred_element_type=F32)
                    acc_scr[p, hh, rows_q, :] = pv
                    m_scr[p, hh, rows_q, :] = jnp.broadcast_to(m_blk, pv.shape)
            return carry

        lax.fori_loop(0, n_blocks // MIX_UNROLL, body, 0)

    def finish(c, carry):
        rows = pl.ds(pl.multiple_of(c * DIL_BLOCK, DIL_BLOCK), DIL_BLOCK)
        outs = []
        for hh in range(2):
            ms = [m_scr[p, hh, rows, :] for p in range(len(DIL_PATTERNS))]
            m_all = functools.reduce(jnp.maximum, ms)
            tot = sum(jnp.exp(m - m_all) * acc_scr[p, hh, rows, :] for p, m in enumerate(ms))
            outs.append(tot / pltpu.roll(tot, HEAD_DIM, axis=1))
        o_ref[0, 0, rows, :] = jnp.where(halves[0], outs[0], outs[1]).astype(o_ref.dtype)
        return carry

    lax.fori_loop(0, n_blocks, finish, 0, unroll=4)


def _mixer_a(q, k, v, bias):
    npair, b, s, _ = q.shape
    blk = (1, 1, SPAN, LANES)
    cur = lambda bi, pi, si: (pi, bi, si, 0)
    prev = lambda bi, pi, si: (pi, bi, jnp.maximum(si - 1, 0), 0)
    return pl.pallas_call(
        _mixer_a_kernel,
        grid=(b, npair, s // SPAN),
        in_specs=[pl.BlockSpec(blk, cur), pl.BlockSpec(blk, prev), pl.BlockSpec(blk, cur),
                  pl.BlockSpec(blk, prev), pl.BlockSpec(blk, cur),
                  pl.BlockSpec((2,) + bias.shape[1:], lambda bi, pi, si: (pi, 0, 0, 0))],
        out_specs=pl.BlockSpec(blk, cur),
        out_shape=jax.ShapeDtypeStruct(q.shape, BF16),
        scratch_shapes=[pltpu.VMEM((SPAN, LANES), F32), pltpu.VMEM((2 * SPAN, LANES), F32),
                        pltpu.VMEM((2 * SPAN, LANES), F32),
                        pltpu.VMEM((len(DIL_PATTERNS), 2, SPAN, LANES), F32),
                        pltpu.VMEM((len(DIL_PATTERNS), 2, SPAN, LANES), F32)],
        compiler_params=_cparams("parallel", "parallel", "parallel"),
        name="mixer_a",
    )(q, k, k, v, v, bias)


def _compress_kernel(c_ref, pa_ref, pb_ref, w1a_ref, w1b_ref, b1_ref, w2_ref, o_ref):
    c = c_ref[0].astype(F32)
    xa = (c + pa_ref[...]).astype(BF16)
    xb = (c + pb_ref[...]).astype(BF16)
    ha = jnp.dot(xa, w1a_ref[...], preferred_element_type=F32)
    hb = jnp.dot(xb, w1b_ref[...], preferred_element_type=F32)
    hb_next = jnp.concatenate([hb[1:], jnp.zeros_like(hb[:1])], axis=0)
    hid = jax.nn.gelu(ha + hb_next + b1_ref[...])
    o_ref[0] = jnp.dot(hid.astype(BF16), w2_ref[...], preferred_element_type=F32).astype(o_ref.dtype)


def _compress(c, pos, w1, b1, w2):
    b, nch, _ = c.shape
    half = CMP_LEN // 2
    zero = jnp.zeros((half, HEAD_DIM, CMP_HIDDEN), F32)

    def grouped(wpart):
        g0 = jnp.concatenate([wpart, zero], axis=1).reshape(half * LANES, CMP_HIDDEN)
        g1 = jnp.concatenate([zero, wpart], axis=1).reshape(half * LANES, CMP_HIDDEN)
        return jnp.concatenate([g0, g1], axis=1).astype(BF16)

    w1a, w1b = grouped(w1[:half]), grouped(w1[half:])
    pa = jnp.tile(pos[:half], (1, 2)).reshape(1, half * LANES)
    pb = jnp.tile(pos[half:], (1, 2)).reshape(1, half * LANES)
    b1g = jnp.tile(b1, 2).reshape(1, 2 * CMP_HIDDEN)
    zw = jnp.zeros_like(w2)
    w2g = jnp.concatenate([jnp.concatenate([w2, zw], axis=1),
                           jnp.concatenate([zw, w2], axis=1)], axis=0).astype(BF16)
    full = lambda a: pl.BlockSpec(a.shape, lambda i: (0,) * a.ndim)
    return pl.pallas_call(
        _compress_kernel,
        grid=(b,),
        in_specs=[pl.BlockSpec((1,) + c.shape[1:], lambda i: (i, 0, 0)),
                  full(pa), full(pb), full(w1a), full(w1b), full(b1g), full(w2g)],
        out_specs=pl.BlockSpec((1, nch, LANES), lambda i: (i, 0, 0)),
        out_shape=jax.ShapeDtypeStruct((b, nch, LANES), BF16),
        compiler_params=_cparams("parallel"),
        name="compress",
    )(c, pa, pb, w1a, w1b, b1g, w2g)


def _nsa_kernel(q_ref, gl_ref, kc_ref, vc_ref, wov_ref, kaug_ref, vs_ref, bias_s_ref, kw_ref, vw_ref, bias_w_ref,
                o_ref, acc_ref, m_ref, s_ref, part_ref, sb_ref, *, nsel, n_bias, kw):
    qt = pl.program_id(1)
    tq = q_ref.shape[1]
    gate = jax.nn.sigmoid(gl_ref[0])
    halves = [_half_mask(g) for g in range(B_KV_HEADS)]
    q_slots = [q_ref[0, :, r * LANES:(r + 1) * LANES] for r in range(B_GROUP)]
    zero = jnp.zeros((tq, LANES), BF16)

    def gated_slots(o, g, branch, slots):
        for r in range(B_GROUP):
            c = (g * B_GROUP + r) * N_BRANCH + branch
            o_r = o[r * tq:(r + 1) * tq] * gate[:, c:c + 1]
            slots[r] = o_r if slots[r] is None else jnp.where(halves[g], o_r, slots[r])

    nck = kc_ref.shape[1]
    t_row = qt * tq + lax.broadcasted_iota(jnp.int32, (tq, 1), 0)
    blk_end = lax.broadcasted_iota(jnp.int32, (1, nck), 1) * CMP_STRIDE + (CMP_LEN - 1)
    valid = blk_end <= t_row
    kc = kc_ref[0]
    vc = vc_ref[0]
    pc = [jnp.zeros((tq, nck), F32) for _ in range(B_KV_HEADS)]
    for r in range(B_GROUP):
        o_slot = None
        for g in range(B_KV_HEADS):
            qm = jnp.where(halves[g], q_slots[r], zero)
            s = lax.dot_general(qm, kc, NT_DIMS, preferred_element_type=F32)
            s = jnp.where(valid, s, -jnp.inf)
            m = jnp.max(s, axis=-1, keepdims=True)
            m = jnp.where(m == -jnp.inf, 0.0, m)
            p = jnp.exp(s - m)
            den = jnp.sum(p, axis=-1, keepdims=True)
            p = p / jnp.maximum(den, 1e-30)
            pc[g] = pc[g] + p
            c = (g * B_GROUP + r) * N_BRANCH
            o = jnp.dot(p.astype(BF16), vc, preferred_element_type=F32) * gate[:, c:c + 1]
            o_slot = o if g == 0 else jnp.where(halves[g], o, o_slot)
        part_ref[0, :, r * LANES:(r + 1) * LANES] = o_slot

    w0 = pl.multiple_of(qt * tq, tq)
    kwin = kw_ref[0, pl.ds(w0, kw), :]
    vwin = vw_ref[0, pl.ds(w0, kw), :]
    col = lax.broadcasted_iota(jnp.int32, (1, kw), 1)
    real = col >= WIN_LEN - qt * tq
    win_slots = [None] * B_GROUP
    for g in range(B_KV_HEADS):
        qg = jnp.concatenate([jnp.where(halves[g], q_slots[r], zero) for r in range(B_GROUP)], axis=0)
        s = lax.dot_general(qg, kwin, NT_DIMS, preferred_element_type=F32)
        s = jnp.where(real, s + bias_w_ref[g], -jnp.inf)
        p = jnp.exp(s - jnp.max(s, axis=-1, keepdims=True))
        acc = jnp.dot(p.astype(BF16), jnp.where(halves[g], vwin, jnp.ones_like(vwin)), preferred_element_type=F32)
        gated_slots(acc / jnp.maximum(pltpu.roll(acc, HEAD_DIM, axis=1), 1e-30), g, 2, win_slots)
    for r in range(B_GROUP):
        part_ref[1, :, r * LANES:(r + 1) * LANES] = win_slots[r]

    t_lane = qt * tq + lax.broadcasted_iota(jnp.int32, (1, tq), 1)
    cur = t_lane // SEL_BLOCK
    j = lax.broadcasted_iota(jnp.int32, (nsel, 1), 0)
    forced = (j == 0) | (j == cur) | (j == cur - 1)
    imps = []
    for g in reversed(range(B_KV_HEADS)):
        imp = lax.dot_general(wov_ref[...], pc[g], NT_DIMS, preferred_element_type=F32,
                              precision=lax.Precision.HIGHEST)
        imps.append(jnp.where(j > cur, -jnp.inf, jnp.where(forced, jnp.inf, imp)))

    last_block = (qt * tq + tq - 1) // SEL_BLOCK
    for level in range(-(-nsel // RANK_CHUNK)):
        n_live = min((level + 1) * RANK_CHUNK, nsel)

        @pl.when(last_block // RANK_CHUNK == level)
        def _(n_live=n_live):
            blocks = []
            for imp in imps:
                if n_live <= SEL_TOPN:
                    blocks.append(jnp.zeros((HEAD_DIM, tq), F32))
                    continue
                live = imp[:n_live]
                jl = j[:n_live]
                rank = jnp.zeros((n_live, tq), jnp.int32)
                for jp in range(n_live):
                    row = live[jp:jp + 1, :]
                    rank = rank + jnp.where(jl > jp, (row >= live).astype(jnp.int32), (row > live).astype(jnp.int32))
                blocks.append(jnp.where(rank < SEL_TOPN, 0.0, NOT_SELECTED))
                if n_live < HEAD_DIM:
                    blocks.append(jnp.zeros((HEAD_DIM - n_live, tq), F32))
            sb_ref[...] = jnp.concatenate(blocks, axis=0).T.astype(sb_ref.dtype)

    sb = sb_ref[...]
    qaug = [jnp.concatenate([jnp.where(halves[g], q_slots[r], sb) for r in range(B_GROUP)], axis=0)
            for g in range(B_KV_HEADS)]
    n_steps = (qt * tq + tq + K_TILE - 1) // K_TILE
    acc_ref[...] = jnp.zeros_like(acc_ref)
    m_ref[...] = jnp.full_like(m_ref, M_INIT)

    def scores(kt):
        for jt in range(K_TILE // BIAS_TILE):
            k0 = pl.multiple_of(kt * K_TILE + jt * BIAS_TILE, BIAS_TILE)
            bi = jnp.clip(qt - k0 // tq + 1, 0, n_bias - 1)
            for g in range(B_KV_HEADS):
                k = kaug_ref[0, pl.ds(k0, BIAS_TILE), g * LANES:(g + 1) * LANES]
                s = lax.dot_general(qaug[g], k, NT_DIMS, preferred_element_type=F32)
                s_ref[kt % 2, g, :, jt * BIAS_TILE:(jt + 1) * BIAS_TILE] = s + bias_s_ref[g, bi]

    def accumulate(kt):
        k0 = pl.multiple_of(kt * K_TILE, K_TILE)
        v2 = vs_ref[0, pl.ds(k0, K_TILE), :]
        for g in range(B_KV_HEADS):
            v = jnp.where(halves[g], v2, jnp.ones((K_TILE, LANES), BF16))
            s = s_ref[kt % 2, g]
            m_old = m_ref[g]
            m_new = jnp.maximum(m_old, jnp.max(s, axis=-1, keepdims=True))
            alpha = jnp.exp(m_old - m_new)
            p = jnp.exp(s - jnp.tile(m_new, (1, K_TILE // LANES)))
            acc_ref[g] = alpha * acc_ref[g] + jnp.dot(p.astype(BF16), v, preferred_element_type=F32)
            m_ref[g] = m_new

    def body(kt, carry):
        accumulate(kt)
        scores(kt + 1)
        return carry

    scores(0)
    lax.fori_loop(0, n_steps - 1, body, 0)
    accumulate(n_steps - 1)
    sel_slots = [None] * B_GROUP
    for g in range(B_KV_HEADS):
        acc = acc_ref[g]
        gated_slots(acc / jnp.maximum(pltpu.roll(acc, HEAD_DIM, axis=1), 1e-30), g, 1, sel_slots)
    for r in range(B_GROUP):
        sl = slice(r * LANES, (r + 1) * LANES)
        o_ref[0, :, sl] = ((part_ref[0, :, sl] + sel_slots[r]) + part_ref[1, :, sl]).astype(o_ref.dtype)


def _nsa(qb, gl, kcmp, vcmp, kaug, vsl, bias_s, kwpad, vwpad, bias_w):
    b, s, w = qb.shape
    nsel = s // SEL_BLOCK
    nck = kcmp.shape[1]
    ratio = SEL_BLOCK // CMP_STRIDE
    wov = np.zeros((nsel, nck), np.float32)
    for jj in range(nsel):
        for off, wt in zip(range(-1, ratio), CMP_OVERLAP):
            n = ratio * jj + off
            if 0 <= n < nck - 1:
                wov[jj, n] = wt
    wov = jnp.asarray(wov)
    rows = B_GROUP * Q_TILE
    tile = lambda width: pl.BlockSpec((1, Q_TILE, width), lambda bi, qi: (bi, qi, 0))
    whole = lambda a: pl.BlockSpec((1,) + a.shape[1:], lambda bi, qi: (bi, 0, 0))
    const = lambda a: pl.BlockSpec(a.shape, lambda bi, qi: (0,) * a.ndim, pipeline_mode=pl.Buffered(1))
    return pl.pallas_call(
        functools.partial(_nsa_kernel, nsel=nsel, n_bias=bias_s.shape[1], kw=bias_w.shape[-1]),
        grid=(b, s // Q_TILE),
        in_specs=[tile(w), tile(LANES), whole(kcmp), whole(vcmp), const(wov), whole(kaug), whole(vsl), const(bias_s),
                  whole(kwpad), whole(vwpad), const(bias_w)],
        out_specs=tile(w),
        out_shape=jax.ShapeDtypeStruct((b, s, w), BF16),
        scratch_shapes=[pltpu.VMEM((B_KV_HEADS, rows, LANES), F32), pltpu.VMEM((B_KV_HEADS, rows, LANES), F32),
                        pltpu.VMEM((2, B_KV_HEADS, rows, K_TILE), F32), pltpu.VMEM((2, Q_TILE, w), F32),
                        pltpu.VMEM((Q_TILE, LANES), BF16)],
        compiler_params=_cparams("parallel", "parallel"),
        name="nsa",
    )(qb, gl, kcmp, vcmp, wov, kaug, vsl, bias_s, kwpad, vwpad, bias_w)


def _tail_kernel(x_ref, oa_ref, ob_ref, wa_ref, wb_ref, g2_ref, wg_ref, wu_ref, wd_ref, gf_ref, o_ref):
    o_a = jnp.concatenate([oa_ref[p] for p in range(oa_ref.shape[0])], axis=1)
    mix = jnp.dot(o_a, wa_ref[...], preferred_element_type=F32)
    mix = mix + jnp.dot(ob_ref[...], wb_ref[...], preferred_element_type=F32)
    h = x_ref[...] + mix
    y = h * lax.rsqrt(jnp.mean(h * h, axis=-1, keepdims=True) + RMS_EPS)
    hn = (y * g2_ref[...]).astype(BF16)
    a = jnp.dot(hn, wg_ref[...], preferred_element_type=F32)
    u = jnp.dot(hn, wu_ref[...], preferred_element_type=F32)
    act = (jax.nn.silu(a) * u).astype(BF16)
    h2 = h + jnp.dot(act, wd_ref[...], preferred_element_type=F32)
    y2 = h2 * lax.rsqrt(jnp.mean(h2 * h2, axis=-1, keepdims=True) + RMS_EPS)
    o_ref[...] = y2 * gf_ref[...]


def _tail(x2, oa, ob, wa, wb, g2, wg, wu, wd, gf, tm):
    n, d = x2.shape
    row = lambda width: pl.BlockSpec((tm, width), lambda i: (i, 0))
    const = lambda a: pl.BlockSpec(a.shape, lambda i: (0, 0), pipeline_mode=pl.Buffered(1))
    return pl.pallas_call(
        _tail_kernel,
        grid=(n // tm,),
        in_specs=[row(d), pl.BlockSpec((oa.shape[0], tm, LANES), lambda i: (0, i, 0)), row(ob.shape[1]),
                  const(wa), const(wb), const(g2), const(wg), const(wu), const(wd), const(gf)],
        out_specs=row(d),
        out_shape=jax.ShapeDtypeStruct((n, d), F32),
        compiler_params=_cparams("parallel"),
        name="tail",
    )(x2, oa, ob, wa, wb, g2, wg, wu, wd, gf)


def _slot_perm():
    perm = np.zeros(B_WIDTH, np.int64)
    for r in range(B_GROUP):
        for g in range(B_KV_HEADS):
            src = (g * B_GROUP + r) * HEAD_DIM
            dst = r * LANES + g * HEAD_DIM
            perm[dst:dst + HEAD_DIM] = np.arange(src, src + HEAD_DIM)
    return perm


def _layer(h, tab_a, tab_b, norm1_g, w_in, cmp_pos, k_w1, k_b1, k_w2, v_w1, v_b1, v_w2,
           w_out, norm2_g, w_gate, w_up, w_down, norm_f_g):
    b, s, d = h.shape
    n = b * s
    assert s % SPAN == 0 and s % K_TILE == 0
    perm = _slot_perm()

    cols = np.cumsum([0, A_WIDTH, A_WIDTH, A_WIDTH, B_WIDTH] + [KV_WIDTH] * 6 + [GATE_WIDTH])
    w_aq = w_in[:, cols[0]:cols[1]] * SCALE
    w_bq = (w_in[:, cols[3]:cols[4]] * SCALE)[:, perm]
    w_gl = jnp.pad(w_in[:, cols[10]:cols[11]], ((0, 0), (0, LANES - GATE_WIDTH)))
    w1 = jnp.concatenate([w_aq, w_in[:, cols[1]:cols[3]], w_bq, w_in[:, cols[4]:cols[10]], w_gl], axis=1).astype(BF16)
    widths = (A_WIDTH,) * 3 + (B_WIDTH,) + (KV_WIDTH,) * 6 + (LANES,)
    dtypes = (BF16,) * 10 + (F32,)
    slot_major = (True,) * 3 + (False,) * 8
    x2 = h.reshape(n, d)
    qa, ka, va, qb, kc, vc, ksl, vsl, kw, vw, gl = _inproj(x2, norm1_g.reshape(1, d), w1, widths, dtypes,
                                                          slot_major, tm=512)
    r3 = lambda t: t.reshape(b, s, t.shape[-1])
    qb, ksl, vsl, kw, vw, gl = map(r3, (qb, ksl, vsl, kw, vw, gl))

    i = np.arange(DIL_BLOCK)[:, None]
    jj = np.arange(2 * DIL_BLOCK)[None, :]
    dist = i + DIL_BLOCK - jj
    tok_dist = np.stack([np.maximum(dist, 0) * dil for _, dil in DIL_PATTERNS])
    in_window = np.stack([(dist >= 0) & (dist <= window // dil) for window, dil in DIL_PATTERNS])
    bias_a = _bias_tiles(tab_a, tok_dist, in_window, inner=1).reshape((A_HEADS,) + tok_dist.shape)
    r4 = lambda t: t.reshape(A_HEADS // 2, b, s, LANES)
    o_a = _mixer_a(r4(qa), r4(ka), r4(va), bias_a).reshape(A_HEADS // 2, n, LANES)

    nch = s // CMP_STRIDE
    kcmp = _compress(kc.reshape(b, nch, CMP_STRIDE * KV_WIDTH), cmp_pos, k_w1, k_b1, k_w2)
    vcmp = _compress(vc.reshape(b, nch, CMP_STRIDE * KV_WIDTH), cmp_pos, v_w1, v_b1, v_w2)
    onehot = (np.arange(s)[:, None] // SEL_BLOCK) == np.arange(HEAD_DIM)[None, :]
    onehot = jnp.broadcast_to(jnp.asarray(onehot, BF16), (b, s, HEAD_DIM))
    kaug = jnp.concatenate([ksl[..., :HEAD_DIM], onehot, onehot, ksl[..., HEAD_DIM:]], axis=-1)
    d_sat = int(np.nonzero(_t5_bucket_np(np.arange(s)) < REL_BUCKETS - 1)[0].max()) + 1
    first_far = -(-(d_sat + BIAS_TILE - 1) // Q_TILE)
    n_bias = min(s // Q_TILE, first_far + 1) + 1
    dd = ((np.arange(n_bias)[:, None, None] - 1) * Q_TILE + np.arange(Q_TILE)[None, :, None]
          - np.arange(BIAS_TILE)[None, None, :])
    bias_sl = _bias_tiles(tab_b, dd, dd >= 0, inner=B_GROUP)
    bias_sl = bias_sl.reshape(B_KV_HEADS, n_bias, B_GROUP * Q_TILE, BIAS_TILE)
    kwid = WIN_LEN + Q_TILE
    dw = np.arange(Q_TILE)[:, None] + WIN_LEN - np.arange(kwid)[None, :]
    bias_w = _bias_tiles(tab_b, dw[None], ((dw >= 0) & (dw < WIN_LEN))[None], inner=1)
    bias_w = bias_w.reshape(B_KV_HEADS, B_GROUP * Q_TILE, kwid)
    pad = lambda t: jnp.pad(t, ((0, 0), (WIN_LEN, 0), (0, 0)))
    o_b = _nsa(qb, gl, kcmp, vcmp, kaug, vsl, bias_sl, pad(kw), pad(vw), bias_w)

    wa = w_out[:A_WIDTH].astype(BF16)
    wb = w_out[A_WIDTH:][perm].astype(BF16)
    out = _tail(x2, o_a, o_b.reshape(n, B_WIDTH), wa, wb, norm2_g.reshape(1, d), w_gate.astype(BF16),
                w_up.astype(BF16), w_down.astype(BF16), norm_f_g.reshape(1, d), tm=256)
    return out.reshape(b, s, d)


def kernel(x, norm1_g, w_in, rel_bias, cmp_pos, cmp_k_w1, cmp_k_b1, cmp_k_w2, cmp_v_w1, cmp_v_b1, cmp_v_w2,
           w_out, norm2_g, w_gate, w_up, w_down, norm_f_g):
    assert w_in.shape[0] == 1, "single-layer model"
    tab_a = rel_bias[:, :A_HEADS].T
    tab_b = rel_bias[:, A_HEADS:].T
    return _layer(x, tab_a, tab_b, norm1_g[0], w_in[0], cmp_pos[0], cmp_k_w1[0], cmp_k_b1[0], cmp_k_w2[0],
                  cmp_v_w1[0], cmp_v_b1[0], cmp_v_w2[0], w_out[0], norm2_g[0], w_gate[0], w_up[0], w_down[0],
                  norm_f_g)
```

```python
import functools
import math

import numpy as np
import jax
import jax.numpy as jnp
from jax import lax
from jax.experimental import pallas as pl
from jax.experimental.pallas import tpu as pltpu

F32 = jnp.float32
BF16 = jnp.bfloat16

HEAD_DIM = 64
LANES = 128
A_HEADS = 8
DIL_PATTERNS = ((128, 1), (512, 4), (2048, 16))
DIL_BLOCK = 128
B_HEADS = 8
B_KV_HEADS = 2
B_GROUP = B_HEADS // B_KV_HEADS
CMP_LEN = 32
CMP_STRIDE = 16
CMP_HIDDEN = 128
SEL_BLOCK = 64
SEL_TOPN = 16
CMP_OVERLAP = (1.0, 2.0, 2.0, 2.0, 1.0)
WIN_LEN = 512
N_BRANCH = 3
REL_BUCKETS = 32
REL_MAX_DIST = 2048
RMS_EPS = 1e-6
A_WIDTH = A_HEADS * HEAD_DIM
B_WIDTH = B_HEADS * HEAD_DIM
KV_WIDTH = B_KV_HEADS * HEAD_DIM
GATE_WIDTH = B_HEADS * N_BRANCH
SCALE = HEAD_DIM ** -0.5

SPAN = DIL_PATTERNS[-1][1] * DIL_BLOCK
NARROW_FROM = 4
MIX_UNROLL = 16
RANK_CHUNK = 16
Q_TILE = 128
K_TILE = 512
BIAS_TILE = 256
NOT_SELECTED = -(2.0 ** 100)
M_INIT = -1e30
VMEM_LIMIT = 56 * 1024 * 1024

NT_DIMS = (((1,), (1,)), ((), ()))


def _cparams(*sem):
    return pltpu.CompilerParams(dimension_semantics=sem, vmem_limit_bytes=VMEM_LIMIT)


def _t5_bucket_np(dist):
    max_exact = REL_BUCKETS // 2
    d = np.asarray(dist)
    df = np.maximum(d, 1).astype(np.float32)
    large = max_exact + (np.log(df / np.float32(max_exact)) / np.float32(math.log(REL_MAX_DIST / max_exact))
                         * np.float32(REL_BUCKETS - max_exact)).astype(np.int32)
    large = np.minimum(large, REL_BUCKETS - 1)
    return np.where(d < max_exact, d, large).astype(np.int32)


def _bias_kernel(tab_ref, idx_ref, o_ref):
    inner = o_ref.shape[2]
    idx = idx_ref[0]
    accs = [jnp.full(idx.shape, -jnp.inf, F32) for _ in range(inner)]
    for bucket in range(REL_BUCKETS):
        hit = idx == bucket
        for hi in range(inner):
            head = pl.program_id(0) * inner + hi
            accs[hi] = jnp.where(hit, tab_ref[head * REL_BUCKETS + bucket], accs[hi])
    for hi in range(inner):
        o_ref[0, 0, hi] = accs[hi]


def _bias_tiles(tab, dist, valid, inner):
    h = tab.shape[0]
    t, r, c = dist.shape
    idx = jnp.asarray(np.where(valid, _t5_bucket_np(np.maximum(dist, 0)), -1).astype(np.int32))
    return pl.pallas_call(
        _bias_kernel,
        grid=(h // inner, t),
        in_specs=[pl.BlockSpec(memory_space=pltpu.SMEM),
                  pl.BlockSpec((1, r, c), lambda a, ti: (ti, 0, 0))],
        out_specs=pl.BlockSpec((1, 1, inner, r, c), lambda a, ti: (a, ti, 0, 0, 0)),
        out_shape=jax.ShapeDtypeStruct((h // inner, t, inner, r, c), F32),
        compiler_params=_cparams("parallel", "parallel"),
        name="bias_tiles",
    )(tab.reshape(-1), idx)


def _half_mask(g):
    lane = lax.broadcasted_iota(jnp.int32, (1, LANES), 1)
    return (lane >= HEAD_DIM) if g else (lane < HEAD_DIM)


def _inproj_kernel(x_ref, g_ref, w_ref, *out_refs, widths):
    x = x_ref[...]
    y = x * lax.rsqrt(jnp.mean(x * x, axis=-1, keepdims=True) + RMS_EPS)
    xn = (y * g_ref[...]).astype(BF16)
    merged = jnp.dot(xn, w_ref[:, sum(widths[:NARROW_FROM]):], preferred_element_type=F32)
    start = 0
    for idx, (o_ref, w) in enumerate(zip(out_refs, widths)):
        if idx < NARROW_FROM:
            r = jnp.dot(xn, w_ref[:, start:start + w], preferred_element_type=F32)
        else:
            off = start - sum(widths[:NARROW_FROM])
            r = merged[:, off:off + w]
        if len(o_ref.shape) == 3:
            for p in range(o_ref.shape[0]):
                o_ref[p] = r[:, p * LANES:(p + 1) * LANES].astype(o_ref.dtype)
        else:
            o_ref[...] = r.astype(o_ref.dtype)
        start += w


def _inproj(x2, g, w, widths, dtypes, slot_major, tm):
    n, d = x2.shape
    shape = lambda w_, sm: (w_ // LANES, n, LANES) if sm else (n, w_)
    spec = lambda w_, sm: (pl.BlockSpec((w_ // LANES, tm, LANES), lambda i: (0, i, 0)) if sm
                           else pl.BlockSpec((tm, w_), lambda i: (i, 0)))
    out_shape = [jax.ShapeDtypeStruct(shape(w_, sm), dt) for w_, dt, sm in zip(widths, dtypes, slot_major)]
    return pl.pallas_call(
        functools.partial(_inproj_kernel, widths=widths),
        grid=(n // tm,),
        in_specs=[pl.BlockSpec((tm, d), lambda i: (i, 0)),
                  pl.BlockSpec((1, d), lambda i: (0, 0)),
                  pl.BlockSpec(w.shape, lambda i: (0, 0))],
        out_specs=[spec(w_, sm) for w_, sm in zip(widths, slot_major)],
        out_shape=out_shape,
        compiler_params=_cparams("parallel"),
        name="inproj",
    )(x2, g, w)


def _mixer_a_kernel(q_ref, kp_ref, kc_ref, vp_ref, vc_ref, bias_ref, o_ref, qf, kf, vf, acc_scr, m_scr):
    sb = pl.program_id(2)
    qf[...] = q_ref[0, 0].astype(F32)
    kf[:SPAN] = kp_ref[0, 0].astype(F32)
    kf[SPAN:] = kc_ref[0, 0].astype(F32)
    vf[:SPAN] = vp_ref[0, 0].astype(F32)
    vf[SPAN:] = vc_ref[0, 0].astype(F32)
    halves = [_half_mask(hh) for hh in range(2)]
    col = lax.broadcasted_iota(jnp.int32, (1, 2 * DIL_BLOCK), 1)
    ones = jnp.ones((2 * DIL_BLOCK, LANES), BF16)
    n_blocks = SPAN // DIL_BLOCK
    for p, (_, dil) in enumerate(DIL_PATTERNS):
        step = dil * DIL_BLOCK
        per_class = SPAN // step

        def body(grp, carry, p=p, dil=dil, step=step, per_class=per_class):
            for u in range(MIX_UNROLL):
                idx = grp * MIX_UNROLL + u
                r, n = idx // per_class, idx % per_class
                q0 = pl.multiple_of(n * step, step) + r
                rows_q = pl.ds(q0, DIL_BLOCK, stride=dil)
                rows_k = pl.ds(q0 + SPAN - step, 2 * DIL_BLOCK, stride=dil)
                q2 = qf[rows_q, :].astype(BF16)
                kcat = kf[rows_k, :].astype(BF16)
                vcat = vf[rows_k, :].astype(BF16)
                keep = (col >= DIL_BLOCK) | (sb * per_class + n > 0)
                for hh in range(2):
                    qm = jnp.where(halves[hh], q2, jnp.zeros_like(q2))
                    s = lax.dot_general(qm, kcat, NT_DIMS, preferred_element_type=F32)
                    s = jnp.where(keep, s + bias_ref[p, hh], -jnp.inf)
                    m_blk = jnp.max(s, axis=-1, keepdims=True)
                    pe = jnp.exp(s - m_blk)
                    pv = jnp.dot(pe.astype(BF16), jnp.where(halves[hh], vcat, ones), preferred_element_type=F32)
                    acc_scr[p, hh, rows_q, :] = pv
                    m_scr[p, hh, rows_q, :] = jnp.broadcast_to(m_blk, pv.shape)
            return carry

        lax.fori_loop(0, n_blocks // MIX_UNROLL, body, 0)

    def finish(c, carry):
        rows = pl.ds(pl.multiple_of(c * DIL_BLOCK, DIL_BLOCK), DIL_BLOCK)
        outs = []
        for hh in range(2):
            ms = [m_scr[p, hh, rows, :] for p in range(len(DIL_PATTERNS))]
            m_all = functools.reduce(jnp.maximum, ms)
            tot = sum(jnp.exp(m - m_all) * acc_scr[p, hh, rows, :] for p, m in enumerate(ms))
            outs.append(tot / pltpu.roll(tot, HEAD_DIM, axis=1))
        o_ref[0, 0, rows, :] = jnp.where(halves[0], outs[0], outs[1]).astype(o_ref.dtype)
        return carry

    lax.fori_loop(0, n_blocks, finish, 0, unroll=4)


def _mixer_a(q, k, v, bias):
    npair, b, s, _ = q.shape
    blk = (1, 1, SPAN, LANES)
    cur = lambda bi, pi, si: (pi, bi, si, 0)
    prev = lambda bi, pi, si: (pi, bi, jnp.maximum(si - 1, 0), 0)
    return pl.pallas_call(
        _mixer_a_kernel,
        grid=(b, npair, s // SPAN),
        in_specs=[pl.BlockSpec(blk, cur), pl.BlockSpec(blk, prev), pl.BlockSpec(blk, cur),
                  pl.BlockSpec(blk, prev), pl.BlockSpec(blk, cur),
                  pl.BlockSpec((bias.shape[0], 2) + bias.shape[2:], lambda bi, pi, si: (0, pi, 0, 0))],
        out_specs=pl.BlockSpec(blk, cur),
        out_shape=jax.ShapeDtypeStruct(q.shape, BF16),
        scratch_shapes=[pltpu.VMEM((SPAN, LANES), F32), pltpu.VMEM((2 * SPAN, LANES), F32),
                        pltpu.VMEM((2 * SPAN, LANES), F32),
                        pltpu.VMEM((len(DIL_PATTERNS), 2, SPAN, LANES), F32),
                        pltpu.VMEM((len(DIL_PATTERNS), 2, SPAN, LANES), F32)],
        compiler_params=_cparams("parallel", "parallel", "parallel"),
        name="mixer_a",
    )(q, k, k, v, v, bias)


def _compress_kernel(c_ref, pa_ref, pb_ref, w1a_ref, w1b_ref, b1_ref, w2_ref, o_ref):
    nch = o_ref.shape[1]
    c = jnp.concatenate([c_ref[0, pl.ds(l, nch, stride=CMP_STRIDE), :] for l in range(CMP_STRIDE)], axis=1)
    xa = (c + pa_ref[...]).astype(BF16)
    xb = (c + pb_ref[...]).astype(BF16)
    ha = jnp.dot(xa, w1a_ref[...], preferred_element_type=F32)
    hb = jnp.dot(xb, w1b_ref[...], preferred_element_type=F32)
    hb_next = jnp.concatenate([hb[1:], jnp.zeros_like(hb[:1])], axis=0)
    hid = jax.nn.gelu(ha + hb_next + b1_ref[...])
    o_ref[0] = jnp.dot(hid.astype(BF16), w2_ref[...], preferred_element_type=F32).astype(o_ref.dtype)


def _compress(c, pos, w1, b1, w2):
    b, s, _ = c.shape
    nch = s // CMP_STRIDE
    half = CMP_LEN // 2
    zero = jnp.zeros((half, HEAD_DIM, CMP_HIDDEN), F32)

    def grouped(wpart):
        g0 = jnp.concatenate([wpart, zero], axis=1).reshape(half * LANES, CMP_HIDDEN)
        g1 = jnp.concatenate([zero, wpart], axis=1).reshape(half * LANES, CMP_HIDDEN)
        return jnp.concatenate([g0, g1], axis=1).astype(BF16)

    w1a, w1b = grouped(w1[:half]), grouped(w1[half:])
    pa = jnp.tile(pos[:half], (1, 2)).reshape(1, half * LANES)
    pb = jnp.tile(pos[half:], (1, 2)).reshape(1, half * LANES)
    b1g = jnp.tile(b1, 2).reshape(1, 2 * CMP_HIDDEN)
    zw = jnp.zeros_like(w2)
    w2g = jnp.concatenate([jnp.concatenate([w2, zw], axis=1),
                           jnp.concatenate([zw, w2], axis=1)], axis=0).astype(BF16)
    full = lambda a: pl.BlockSpec(a.shape, lambda i: (0,) * a.ndim)
    return pl.pallas_call(
        _compress_kernel,
        grid=(b,),
        in_specs=[pl.BlockSpec((1,) + c.shape[1:], lambda i: (i, 0, 0)),
                  full(pa), full(pb), full(w1a), full(w1b), full(b1g), full(w2g)],
        out_specs=pl.BlockSpec((1, nch, LANES), lambda i: (i, 0, 0)),
        out_shape=jax.ShapeDtypeStruct((b, nch, LANES), BF16),
        compiler_params=_cparams("parallel"),
        name="compress",
    )(c, pa, pb, w1a, w1b, b1g, w2g)


def _nsa_kernel(q_ref, gl_ref, kc_ref, vc_ref, wov_ref, kaug_ref, vs_ref, bias_s_ref, kw_ref, vw_ref, bias_w_ref,
                o_ref, acc_ref, m_ref, s_ref, part_ref, sb_ref, *, nsel, n_bias, kw):
    qt = pl.program_id(1)
    tq = q_ref.shape[1]
    gate = jax.nn.sigmoid(gl_ref[0])
    halves = [_half_mask(g) for g in range(B_KV_HEADS)]
    q_slots = [q_ref[0, :, r * LANES:(r + 1) * LANES] for r in range(B_GROUP)]
    zero = jnp.zeros((tq, LANES), BF16)

    def gated_slots(o, g, branch, slots):
        for r in range(B_GROUP):
            c = (g * B_GROUP + r) * N_BRANCH + branch
            o_r = o[r * tq:(r + 1) * tq] * gate[:, c:c + 1]
            slots[r] = o_r if slots[r] is None else jnp.where(halves[g], o_r, slots[r])

    nck = kc_ref.shape[1]
    t_row = qt * tq + lax.broadcasted_iota(jnp.int32, (tq, 1), 0)
    blk_end = lax.broadcasted_iota(jnp.int32, (1, nck), 1) * CMP_STRIDE + (CMP_LEN - 1)
    valid = blk_end <= t_row
    kc = kc_ref[0]
    vc = vc_ref[0]
    pc = [jnp.zeros((tq, nck), F32) for _ in range(B_KV_HEADS)]
    for r in range(B_GROUP):
        o_slot = None
        for g in range(B_KV_HEADS):
            qm = jnp.where(halves[g], q_slots[r], zero)
            s = lax.dot_general(qm, kc, NT_DIMS, preferred_element_type=F32)
            s = jnp.where(valid, s, -jnp.inf)
            m = jnp.max(s, axis=-1, keepdims=True)
            m = jnp.where(m == -jnp.inf, 0.0, m)
            p = jnp.exp(s - m)
            den = jnp.sum(p, axis=-1, keepdims=True)
            p = p / jnp.maximum(den, 1e-30)
            pc[g] = pc[g] + p
            c = (g * B_GROUP + r) * N_BRANCH
            o = jnp.dot(p.astype(BF16), vc, preferred_element_type=F32) * gate[:, c:c + 1]
            o_slot = o if g == 0 else jnp.where(halves[g], o, o_slot)
        part_ref[0, :, r * LANES:(r + 1) * LANES] = o_slot

    w0 = pl.multiple_of(qt * tq, tq)
    kwin = kw_ref[0, pl.ds(w0, kw), :]
    vwin = vw_ref[0, pl.ds(w0, kw), :]
    col = lax.broadcasted_iota(jnp.int32, (1, kw), 1)
    real = col >= WIN_LEN - qt * tq
    win_slots = [None] * B_GROUP
    for g in range(B_KV_HEADS):
        qg = jnp.concatenate([jnp.where(halves[g], q_slots[r], zero) for r in range(B_GROUP)], axis=0)
        s = lax.dot_general(qg, kwin, NT_DIMS, preferred_element_type=F32)
        s = jnp.where(real, s + bias_w_ref[g], -jnp.inf)
        p = jnp.exp(s - jnp.max(s, axis=-1, keepdims=True))
        acc = jnp.dot(p.astype(BF16), jnp.where(halves[g], vwin, jnp.ones_like(vwin)), preferred_element_type=F32)
        gated_slots(acc / jnp.maximum(pltpu.roll(acc, HEAD_DIM, axis=1), 1e-30), g, 2, win_slots)
    for r in range(B_GROUP):
        part_ref[1, :, r * LANES:(r + 1) * LANES] = win_slots[r]

    t_lane = qt * tq + lax.broadcasted_iota(jnp.int32, (1, tq), 1)
    cur = t_lane // SEL_BLOCK
    j = lax.broadcasted_iota(jnp.int32, (nsel, 1), 0)
    forced = (j == 0) | (j == cur) | (j == cur - 1)
    imps = []
    for g in reversed(range(B_KV_HEADS)):
        imp = lax.dot_general(wov_ref[...], pc[g], NT_DIMS, preferred_element_type=F32,
                              precision=lax.Precision.HIGHEST)
        imps.append(jnp.where(j > cur, -jnp.inf, jnp.where(forced, jnp.inf, imp)))

    last_block = (qt * tq + tq - 1) // SEL_BLOCK
    for level in range(-(-nsel // RANK_CHUNK)):
        n_live = min((level + 1) * RANK_CHUNK, nsel)

        @pl.when(last_block // RANK_CHUNK == level)
        def _(n_live=n_live):
            blocks = []
            for imp in imps:
                if n_live <= SEL_TOPN:
                    blocks.append(jnp.zeros((HEAD_DIM, tq), F32))
                    continue
                live = imp[:n_live]
                jl = j[:n_live]
                rank = jnp.zeros((n_live, tq), jnp.int32)
                for jp in range(n_live):
                    row = live[jp:jp + 1, :]
                    rank = rank + jnp.where(jl > jp, (row >= live).astype(jnp.int32), (row > live).astype(jnp.int32))
                blocks.append(jnp.where(rank < SEL_TOPN, 0.0, NOT_SELECTED))
                if n_live < HEAD_DIM:
                    blocks.append(jnp.zeros((HEAD_DIM - n_live, tq), F32))
            sb_ref[...] = jnp.concatenate(blocks, axis=0).T.astype(sb_ref.dtype)

    sb = sb_ref[...]
    qaug = [jnp.concatenate([jnp.where(halves[g], q_slots[r], sb) for r in range(B_GROUP)], axis=0)
            for g in range(B_KV_HEADS)]
    n_steps = (qt * tq + tq + K_TILE - 1) // K_TILE
    acc_ref[...] = jnp.zeros_like(acc_ref)
    m_ref[...] = jnp.full_like(m_ref, M_INIT)

    def scores(kt):
        for jt in range(K_TILE // BIAS_TILE):
            k0 = pl.multiple_of(kt * K_TILE + jt * BIAS_TILE, BIAS_TILE)
            bi = jnp.clip(qt - k0 // tq + 1, 0, n_bias - 1)
            for g in range(B_KV_HEADS):
                k = kaug_ref[0, pl.ds(k0, BIAS_TILE), g * LANES:(g + 1) * LANES]
                s = lax.dot_general(qaug[g], k, NT_DIMS, preferred_element_type=F32)
                s_ref[kt % 2, g, :, jt * BIAS_TILE:(jt + 1) * BIAS_TILE] = s + bias_s_ref[g, bi]

    def accumulate(kt):
        k0 = pl.multiple_of(kt * K_TILE, K_TILE)
        v2 = vs_ref[0, pl.ds(k0, K_TILE), :]
        for g in range(B_KV_HEADS):
            v = jnp.where(halves[g], v2, jnp.ones((K_TILE, LANES), BF16))
            s = s_ref[kt % 2, g]
            m_old = m_ref[g]
            m_new = jnp.maximum(m_old, jnp.max(s, axis=-1, keepdims=True))
            alpha = jnp.exp(m_old - m_new)
            p = jnp.exp(s - jnp.tile(m_new, (1, K_TILE // LANES)))
            acc_ref[g] = alpha * acc_ref[g] + jnp.dot(p.astype(BF16), v, preferred_element_type=F32)
            m_ref[g] = m_new

    def body(kt, carry):
        accumulate(kt)
        scores(kt + 1)
        return carry

    scores(0)
    lax.fori_loop(0, n_steps - 1, body, 0)
    accumulate(n_steps - 1)
    sel_slots = [None] * B_GROUP
    for g in range(B_KV_HEADS):
        acc = acc_ref[g]
        gated_slots(acc / jnp.maximum(pltpu.roll(acc, HEAD_DIM, axis=1), 1e-30), g, 1, sel_slots)
    for r in range(B_GROUP):
        sl = slice(r * LANES, (r + 1) * LANES)
        o_ref[0, :, sl] = ((part_ref[0, :, sl] + sel_slots[r]) + part_ref[1, :, sl]).astype(o_ref.dtype)


def _nsa(qb, gl, kcmp, vcmp, kaug, vsl, bias_s, kwpad, vwpad, bias_w):
    b, s, w = qb.shape
    nsel = s // SEL_BLOCK
    nck = kcmp.shape[1]
    ratio = SEL_BLOCK // CMP_STRIDE
    wov = np.zeros((nsel, nck), np.float32)
    for jj in range(nsel):
        for off, wt in zip(range(-1, ratio), CMP_OVERLAP):
            n = ratio * jj + off
            if 0 <= n < nck - 1:
                wov[jj, n] = wt
    wov = jnp.asarray(wov)
    rows = B_GROUP * Q_TILE
    tile = lambda width: pl.BlockSpec((1, Q_TILE, width), lambda bi, qi: (bi, qi, 0))
    whole = lambda a: pl.BlockSpec((1,) + a.shape[1:], lambda bi, qi: (bi, 0, 0))
    const = lambda a: pl.BlockSpec(a.shape, lambda bi, qi: (0,) * a.ndim, pipeline_mode=pl.Buffered(1))
    return pl.pallas_call(
        functools.partial(_nsa_kernel, nsel=nsel, n_bias=bias_s.shape[1], kw=bias_w.shape[-1]),
        grid=(b, s // Q_TILE),
        in_specs=[tile(w), tile(LANES), whole(kcmp), whole(vcmp), const(wov), whole(kaug), whole(vsl), const(bias_s),
                  whole(kwpad), whole(vwpad), const(bias_w)],
        out_specs=tile(w),
        out_shape=jax.ShapeDtypeStruct((b, s, w), BF16),
        scratch_shapes=[pltpu.VMEM((B_KV_HEADS, rows, LANES), F32), pltpu.VMEM((B_KV_HEADS, rows, LANES), F32),
                        pltpu.VMEM((2, B_KV_HEADS, rows, K_TILE), F32), pltpu.VMEM((2, Q_TILE, w), F32),
                        pltpu.VMEM((Q_TILE, LANES), BF16)],
        compiler_params=_cparams("parallel", "parallel"),
        name="nsa",
    )(qb, gl, kcmp, vcmp, wov, kaug, vsl, bias_s, kwpad, vwpad, bias_w)


def _tail_kernel(x_ref, oa_ref, ob_ref, wa_ref, wb_ref, g2_ref, wg_ref, wu_ref, wd_ref, gf_ref, o_ref):
    o_a = jnp.concatenate([oa_ref[p] for p in range(oa_ref.shape[0])], axis=1)
    mix = jnp.dot(o_a, wa_ref[...], preferred_element_type=F32)
    mix = mix + jnp.dot(ob_ref[...], wb_ref[...], preferred_element_type=F32)
    h = x_ref[...] + mix
    y = h * lax.rsqrt(jnp.mean(h * h, axis=-1, keepdims=True) + RMS_EPS)
    hn = (y * g2_ref[...]).astype(BF16)
    a = jnp.dot(hn, wg_ref[...], preferred_element_type=F32)
    u = jnp.dot(hn, wu_ref[...], preferred_element_type=F32)
    act = (jax.nn.silu(a) * u).astype(BF16)
    h2 = h + jnp.dot(act, wd_ref[...], preferred_element_type=F32)
    y2 = h2 * lax.rsqrt(jnp.mean(h2 * h2, axis=-1, keepdims=True) + RMS_EPS)
    o_ref[...] = y2 * gf_ref[...]


def _tail(x2, oa, ob, wa, wb, g2, wg, wu, wd, gf, tm):
    n, d = x2.shape
    row = lambda width: pl.BlockSpec((tm, width), lambda i: (i, 0))
    const = lambda a: pl.BlockSpec(a.shape, lambda i: (0, 0), pipeline_mode=pl.Buffered(1))
    return pl.pallas_call(
        _tail_kernel,
        grid=(n // tm,),
        in_specs=[row(d), pl.BlockSpec((oa.shape[0], tm, LANES), lambda i: (0, i, 0)), row(ob.shape[1]),
                  const(wa), const(wb), const(g2), const(wg), const(wu), const(wd), const(gf)],
        out_specs=row(d),
        out_shape=jax.ShapeDtypeStruct((n, d), F32),
        compiler_params=_cparams("parallel"),
        name="tail",
    )(x2, oa, ob, wa, wb, g2, wg, wu, wd, gf)


def _slot_perm():
    perm = np.zeros(B_WIDTH, np.int64)
    for r in range(B_GROUP):
        for g in range(B_KV_HEADS):
            src = (g * B_GROUP + r) * HEAD_DIM
            dst = r * LANES + g * HEAD_DIM
            perm[dst:dst + HEAD_DIM] = np.arange(src, src + HEAD_DIM)
    return perm


def _layer(h, tab_a, tab_b, norm1_g, w_in, cmp_pos, k_w1, k_b1, k_w2, v_w1, v_b1, v_w2,
           w_out, norm2_g, w_gate, w_up, w_down, norm_f_g):
    b, s, d = h.shape
    n = b * s
    assert s % SPAN == 0 and s % K_TILE == 0
    perm = _slot_perm()

    cols = np.cumsum([0, A_WIDTH, A_WIDTH, A_WIDTH, B_WIDTH] + [KV_WIDTH] * 6 + [GATE_WIDTH])
    w_aq = w_in[:, cols[0]:cols[1]] * SCALE
    w_bq = (w_in[:, cols[3]:cols[4]] * SCALE)[:, perm]
    w_gl = jnp.pad(w_in[:, cols[10]:cols[11]], ((0, 0), (0, LANES - GATE_WIDTH)))
    w1 = jnp.concatenate([w_aq, w_in[:, cols[1]:cols[3]], w_bq, w_in[:, cols[4]:cols[10]], w_gl], axis=1).astype(BF16)
    widths = (A_WIDTH,) * 3 + (B_WIDTH,) + (KV_WIDTH,) * 6 + (LANES,)
    dtypes = (BF16,) * 4 + (F32,) * 2 + (BF16,) * 4 + (F32,)
    slot_major = (True,) * 3 + (False,) * 8
    x2 = h.reshape(n, d)
    qa, ka, va, qb, kc, vc, ksl, vsl, kw, vw, gl = _inproj(x2, norm1_g.reshape(1, d), w1, widths, dtypes,
                                                          slot_major, tm=512)
    r3 = lambda t: t.reshape(b, s, t.shape[-1])
    qb, ksl, vsl, kw, vw, gl = map(r3, (qb, ksl, vsl, kw, vw, gl))

    i = np.arange(DIL_BLOCK)[:, None]
    jj = np.arange(2 * DIL_BLOCK)[None, :]
    dist = i + DIL_BLOCK - jj
    tok_dist = np.stack([np.maximum(dist, 0) * dil for _, dil in DIL_PATTERNS])
    in_window = np.stack([(dist >= 0) & (dist <= window // dil) for window, dil in DIL_PATTERNS])
    bias_a = _bias_tiles(tab_a, tok_dist, in_window, inner=A_HEADS)[0]
    r4 = lambda t: t.reshape(A_HEADS // 2, b, s, LANES)
    o_a = _mixer_a(r4(qa), r4(ka), r4(va), bias_a).reshape(A_HEADS // 2, n, LANES)

    nch = s // CMP_STRIDE
    kcmp = _compress(kc.reshape(b, s, KV_WIDTH), cmp_pos, k_w1, k_b1, k_w2)
    vcmp = _compress(vc.reshape(b, s, KV_WIDTH), cmp_pos, v_w1, v_b1, v_w2)
    onehot = (np.arange(s)[:, None] // SEL_BLOCK) == np.arange(HEAD_DIM)[None, :]
    onehot = jnp.broadcast_to(jnp.asarray(onehot, BF16), (b, s, HEAD_DIM))
    kaug = jnp.concatenate([ksl[..., :HEAD_DIM], onehot, onehot, ksl[..., HEAD_DIM:]], axis=-1)
    d_sat = int(np.nonzero(_t5_bucket_np(np.arange(s)) < REL_BUCKETS - 1)[0].max()) + 1
    first_far = -(-(d_sat + BIAS_TILE - 1) // Q_TILE)
    n_bias = min(s // Q_TILE, first_far + 1) + 1
    dd = ((np.arange(n_bias)[:, None, None] - 1) * Q_TILE + np.arange(Q_TILE)[None, :, None]
          - np.arange(BIAS_TILE)[None, None, :])
    bias_sl = _bias_tiles(tab_b, dd, dd >= 0, inner=B_GROUP)
    bias_sl = bias_sl.reshape(B_KV_HEADS, n_bias, B_GROUP * Q_TILE, BIAS_TILE)
    kwid = WIN_LEN + Q_TILE
    dw = np.arange(Q_TILE)[:, None] + WIN_LEN - np.arange(kwid)[None, :]
    bias_w = _bias_tiles(tab_b, dw[None], ((dw >= 0) & (dw < WIN_LEN))[None], inner=B_HEADS)
    bias_w = bias_w.reshape(B_KV_HEADS, B_GROUP * Q_TILE, kwid)
    pad = lambda t: jnp.pad(t, ((0, 0), (WIN_LEN, 0), (0, 0)))
    o_b = _nsa(qb, gl, kcmp, vcmp, kaug, vsl, bias_sl, pad(kw), pad(vw), bias_w)

    wa = w_out[:A_WIDTH].astype(BF16)
    wb = w_out[A_WIDTH:][perm].astype(BF16)
    out = _tail(x2, o_a, o_b.reshape(n, B_WIDTH), wa, wb, norm2_g.reshape(1, d), w_gate.astype(BF16),
                w_up.astype(BF16), w_down.astype(BF16), norm_f_g.reshape(1, d), tm=512)
    return out.reshape(b, s, d)


def kernel(x, norm1_g, w_in, rel_bias, cmp_pos, cmp_k_w1, cmp_k_b1, cmp_k_w2, cmp_v_w1, cmp_v_b1, cmp_v_w2,
           w_out, norm2_g, w_gate, w_up, w_down, norm_f_g):
    assert w_in.shape[0] == 1, "single-layer model"
    tab_a = rel_bias[:, :A_HEADS].T
    tab_b = rel_bias[:, A_HEADS:].T
    return _layer(x, tab_a, tab_b, norm1_g[0], w_in[0], cmp_pos[0], cmp_k_w1[0], cmp_k_b1[0], cmp_k_w2[0],
                  cmp_v_w1[0], cmp_v_b1[0], cmp_v_w2[0], w_out[0], norm2_g[0], w_gate[0], w_up[0], w_down[0],
                  norm_f_g)
```

```python
import functools
import math

import numpy as np
import jax
import jax.numpy as jnp
from jax import lax
from jax.experimental import pallas as pl
from jax.experimental.pallas import tpu as pltpu

F32 = jnp.float32
BF16 = jnp.bfloat16

HEAD_DIM = 64
LANES = 128
A_HEADS = 8
DIL_PATTERNS = ((128, 1), (512, 4), (2048, 16))
DIL_BLOCK = 128
B_HEADS = 8
B_KV_HEADS = 2
B_GROUP = B_HEADS // B_KV_HEADS
CMP_LEN = 32
CMP_STRIDE = 16
CMP_HIDDEN = 128
SEL_BLOCK = 64
SEL_TOPN = 16
CMP_OVERLAP = (1.0, 2.0, 2.0, 2.0, 1.0)
WIN_LEN = 512
N_BRANCH = 3
REL_BUCKETS = 32
REL_MAX_DIST = 2048
RMS_EPS = 1e-6
A_WIDTH = A_HEADS * HEAD_DIM
B_WIDTH = B_HEADS * HEAD_DIM
KV_WIDTH = B_KV_HEADS * HEAD_DIM
GATE_WIDTH = B_HEADS * N_BRANCH
SCALE = HEAD_DIM ** -0.5
LOG2E = math.log2(math.e)

SPAN = DIL_PATTERNS[-1][1] * DIL_BLOCK
NARROW_FROM = 4
MIX_UNROLL = 16
RANK_CHUNK = 16
Q_TILE = 128
K_TILE = 512
BIAS_TILE = 256
NOT_SELECTED = -(2.0 ** 100)
M_INIT = -1e30
VMEM_LIMIT = 56 * 1024 * 1024

NT_DIMS = (((1,), (1,)), ((), ()))


def _cparams(*sem):
    return pltpu.CompilerParams(dimension_semantics=sem, vmem_limit_bytes=VMEM_LIMIT)


def _t5_bucket_np(dist):
    max_exact = REL_BUCKETS // 2
    d = np.asarray(dist)
    df = np.maximum(d, 1).astype(np.float32)
    large = max_exact + (np.log(df / np.float32(max_exact)) / np.float32(math.log(REL_MAX_DIST / max_exact))
                         * np.float32(REL_BUCKETS - max_exact)).astype(np.int32)
    large = np.minimum(large, REL_BUCKETS - 1)
    return np.where(d < max_exact, d, large).astype(np.int32)


def _bias_kernel(tab_ref, idx_ref, o_ref):
    inner = o_ref.shape[2]
    idx = idx_ref[0]
    accs = [jnp.full(idx.shape, -jnp.inf, F32) for _ in range(inner)]
    for bucket in range(REL_BUCKETS):
        hit = idx == bucket
        for hi in range(inner):
            head = pl.program_id(0) * inner + hi
            accs[hi] = jnp.where(hit, tab_ref[head * REL_BUCKETS + bucket], accs[hi])
    for hi in range(inner):
        o_ref[0, 0, hi] = accs[hi]


def _bias_tiles(tab, dist, valid, inner):
    h = tab.shape[0]
    t, r, c = dist.shape
    idx = jnp.asarray(np.where(valid, _t5_bucket_np(np.maximum(dist, 0)), -1).astype(np.int32))
    return pl.pallas_call(
        _bias_kernel,
        grid=(h // inner, t),
        in_specs=[pl.BlockSpec(memory_space=pltpu.SMEM),
                  pl.BlockSpec((1, r, c), lambda a, ti: (ti, 0, 0))],
        out_specs=pl.BlockSpec((1, 1, inner, r, c), lambda a, ti: (a, ti, 0, 0, 0)),
        out_shape=jax.ShapeDtypeStruct((h // inner, t, inner, r, c), F32),
        compiler_params=_cparams("parallel", "parallel"),
        name="bias_tiles",
    )(tab.reshape(-1), idx)


def _half_mask(g):
    lane = lax.broadcasted_iota(jnp.int32, (1, LANES), 1)
    return (lane >= HEAD_DIM) if g else (lane < HEAD_DIM)


def _inproj_kernel(x_ref, g_ref, w_ref, *out_refs, widths):
    x = x_ref[...]
    y = x * lax.rsqrt(jnp.mean(x * x, axis=-1, keepdims=True) + RMS_EPS)
    xn = (y * g_ref[...]).astype(BF16)
    merged = jnp.dot(xn, w_ref[:, sum(widths[:NARROW_FROM]):], preferred_element_type=F32)
    start = 0
    for idx, (o_ref, w) in enumerate(zip(out_refs, widths)):
        if idx < NARROW_FROM:
            r = jnp.dot(xn, w_ref[:, start:start + w], preferred_element_type=F32)
        else:
            off = start - sum(widths[:NARROW_FROM])
            r = merged[:, off:off + w]
        if len(o_ref.shape) == 3:
            for p in range(o_ref.shape[0]):
                o_ref[p] = r[:, p * LANES:(p + 1) * LANES].astype(o_ref.dtype)
        else:
            o_ref[...] = r.astype(o_ref.dtype)
        start += w


def _inproj(x2, g, w, widths, dtypes, slot_major, tm):
    n, d = x2.shape
    shape = lambda w_, sm: (w_ // LANES, n, LANES) if sm else (n, w_)
    spec = lambda w_, sm: (pl.BlockSpec((w_ // LANES, tm, LANES), lambda i: (0, i, 0)) if sm
                           else pl.BlockSpec((tm, w_), lambda i: (i, 0)))
    out_shape = [jax.ShapeDtypeStruct(shape(w_, sm), dt) for w_, dt, sm in zip(widths, dtypes, slot_major)]
    return pl.pallas_call(
        functools.partial(_inproj_kernel, widths=widths),
        grid=(n // tm,),
        in_specs=[pl.BlockSpec((tm, d), lambda i: (i, 0)),
                  pl.BlockSpec((1, d), lambda i: (0, 0)),
                  pl.BlockSpec(w.shape, lambda i: (0, 0))],
        out_specs=[spec(w_, sm) for w_, sm in zip(widths, slot_major)],
        out_shape=out_shape,
        compiler_params=_cparams("parallel"),
        name="inproj",
    )(x2, g, w)


def _mixer_a_kernel(q_ref, kp_ref, kc_ref, vp_ref, vc_ref, bias_ref, o_ref, qf, kf, vf, acc_scr, m_scr):
    sb = pl.program_id(2)
    qf[...] = q_ref[0, 0].astype(F32)
    kf[:SPAN] = kp_ref[0, 0].astype(F32)
    kf[SPAN:] = kc_ref[0, 0].astype(F32)
    vf[:SPAN] = vp_ref[0, 0].astype(F32)
    vf[SPAN:] = vc_ref[0, 0].astype(F32)
    halves = [_half_mask(hh) for hh in range(2)]
    col = lax.broadcasted_iota(jnp.int32, (1, 2 * DIL_BLOCK), 1)
    ones = jnp.ones((2 * DIL_BLOCK, LANES), BF16)
    n_blocks = SPAN // DIL_BLOCK
    for p, (_, dil) in enumerate(DIL_PATTERNS):
        step = dil * DIL_BLOCK
        per_class = SPAN // step

        def body(grp, carry, p=p, dil=dil, step=step, per_class=per_class):
            for u in range(MIX_UNROLL):
                idx = grp * MIX_UNROLL + u
                r, n = idx // per_class, idx % per_class
                q0 = pl.multiple_of(n * step, step) + r
                rows_q = pl.ds(q0, DIL_BLOCK, stride=dil)
                rows_k = pl.ds(q0 + SPAN - step, 2 * DIL_BLOCK, stride=dil)
                q2 = qf[rows_q, :].astype(BF16)
                kcat = kf[rows_k, :].astype(BF16)
                vcat = vf[rows_k, :].astype(BF16)
                keep = (col >= DIL_BLOCK) | (sb * per_class + n > 0)
                for hh in range(2):
                    qm = jnp.where(halves[hh], q2, jnp.zeros_like(q2))
                    s = lax.dot_general(qm, kcat, NT_DIMS, preferred_element_type=F32)
                    s = jnp.where(keep, s + bias_ref[p, hh], -jnp.inf)
                    m_blk = jnp.max(s, axis=-1, keepdims=True)
                    pe = jnp.exp2(s - m_blk)
                    pv = jnp.dot(pe.astype(BF16), jnp.where(halves[hh], vcat, ones), preferred_element_type=F32)
                    acc_scr[p, hh, rows_q, :] = pv
                    m_scr[p, hh, rows_q, :] = jnp.broadcast_to(m_blk, pv.shape)
            return carry

        lax.fori_loop(0, n_blocks // MIX_UNROLL, body, 0)

    def finish(c, carry):
        rows = pl.ds(pl.multiple_of(c * DIL_BLOCK, DIL_BLOCK), DIL_BLOCK)
        outs = []
        for hh in range(2):
            ms = [m_scr[p, hh, rows, :] for p in range(len(DIL_PATTERNS))]
            m_all = functools.reduce(jnp.maximum, ms)
            tot = sum(jnp.exp2(m - m_all) * acc_scr[p, hh, rows, :] for p, m in enumerate(ms))
            outs.append(tot / pltpu.roll(tot, HEAD_DIM, axis=1))
        o_ref[0, 0, rows, :] = jnp.where(halves[0], outs[0], outs[1]).astype(o_ref.dtype)
        return carry

    lax.fori_loop(0, n_blocks, finish, 0, unroll=4)


def _mixer_a(q, k, v, bias):
    npair, b, s, _ = q.shape
    blk = (1, 1, SPAN, LANES)
    cur = lambda bi, pi, si: (pi, bi, si, 0)
    prev = lambda bi, pi, si: (pi, bi, jnp.maximum(si - 1, 0), 0)
    return pl.pallas_call(
        _mixer_a_kernel,
        grid=(b, npair, s // SPAN),
        in_specs=[pl.BlockSpec(blk, cur), pl.BlockSpec(blk, prev), pl.BlockSpec(blk, cur),
                  pl.BlockSpec(blk, prev), pl.BlockSpec(blk, cur),
                  pl.BlockSpec((bias.shape[0], 2) + bias.shape[2:], lambda bi, pi, si: (0, pi, 0, 0))],
        out_specs=pl.BlockSpec(blk, cur),
        out_shape=jax.ShapeDtypeStruct(q.shape, BF16),
        scratch_shapes=[pltpu.VMEM((SPAN, LANES), F32), pltpu.VMEM((2 * SPAN, LANES), F32),
                        pltpu.VMEM((2 * SPAN, LANES), F32),
                        pltpu.VMEM((len(DIL_PATTERNS), 2, SPAN, LANES), F32),
                        pltpu.VMEM((len(DIL_PATTERNS), 2, SPAN, LANES), F32)],
        compiler_params=_cparams("parallel", "parallel", "parallel"),
        name="mixer_a",
    )(q, k, k, v, v, bias)


def _compress_kernel(c_ref, pa_ref, pb_ref, w1a_ref, w1b_ref, b1_ref, w2_ref, o_ref):
    nch = o_ref.shape[1]
    c = jnp.concatenate([c_ref[0, pl.ds(l, nch, stride=CMP_STRIDE), :] for l in range(CMP_STRIDE)], axis=1)
    xa = (c + pa_ref[...]).astype(BF16)
    xb = (c + pb_ref[...]).astype(BF16)
    ha = jnp.dot(xa, w1a_ref[...], preferred_element_type=F32)
    hb = jnp.dot(xb, w1b_ref[...], preferred_element_type=F32)
    hb_next = jnp.concatenate([hb[1:], jnp.zeros_like(hb[:1])], axis=0)
    hid = jax.nn.gelu(ha + hb_next + b1_ref[...])
    o_ref[0] = jnp.dot(hid.astype(BF16), w2_ref[...], preferred_element_type=F32).astype(o_ref.dtype)


def _compress(c, pos, w1, b1, w2):
    b, s, _ = c.shape
    nch = s // CMP_STRIDE
    half = CMP_LEN // 2
    zero = jnp.zeros((half, HEAD_DIM, CMP_HIDDEN), F32)

    def grouped(wpart):
        g0 = jnp.concatenate([wpart, zero], axis=1).reshape(half * LANES, CMP_HIDDEN)
        g1 = jnp.concatenate([zero, wpart], axis=1).reshape(half * LANES, CMP_HIDDEN)
        return jnp.concatenate([g0, g1], axis=1).astype(BF16)

    w1a, w1b = grouped(w1[:half]), grouped(w1[half:])
    pa = jnp.tile(pos[:half], (1, 2)).reshape(1, half * LANES)
    pb = jnp.tile(pos[half:], (1, 2)).reshape(1, half * LANES)
    b1g = jnp.tile(b1, 2).reshape(1, 2 * CMP_HIDDEN)
    zw = jnp.zeros_like(w2)
    w2g = jnp.concatenate([jnp.concatenate([w2, zw], axis=1),
                           jnp.concatenate([zw, w2], axis=1)], axis=0).astype(BF16)
    full = lambda a: pl.BlockSpec(a.shape, lambda i: (0,) * a.ndim)
    return pl.pallas_call(
        _compress_kernel,
        grid=(b,),
        in_specs=[pl.BlockSpec((1,) + c.shape[1:], lambda i: (i, 0, 0)),
                  full(pa), full(pb), full(w1a), full(w1b), full(b1g), full(w2g)],
        out_specs=pl.BlockSpec((1, nch, LANES), lambda i: (i, 0, 0)),
        out_shape=jax.ShapeDtypeStruct((b, nch, LANES), BF16),
        compiler_params=_cparams("parallel"),
        name="compress",
    )(c, pa, pb, w1a, w1b, b1g, w2g)


def _nsa_kernel(q_ref, gl_ref, kc_ref, vc_ref, wov_ref, kaug_ref, vs_ref, bias_s_ref, kw_ref, vw_ref, bias_w_ref,
                o_ref, acc_ref, m_ref, s_ref, part_ref, sb_ref, *, nsel, n_bias, kw):
    qt = pl.program_id(1)
    tq = q_ref.shape[1]
    gate = jax.nn.sigmoid(gl_ref[0])
    halves = [_half_mask(g) for g in range(B_KV_HEADS)]
    q_slots = [q_ref[0, :, r * LANES:(r + 1) * LANES] for r in range(B_GROUP)]
    zero = jnp.zeros((tq, LANES), BF16)

    def gated_slots(o, g, branch, slots):
        for r in range(B_GROUP):
            c = (g * B_GROUP + r) * N_BRANCH + branch
            o_r = o[r * tq:(r + 1) * tq] * gate[:, c:c + 1]
            slots[r] = o_r if slots[r] is None else jnp.where(halves[g], o_r, slots[r])

    nck = kc_ref.shape[1]
    t_row = qt * tq + lax.broadcasted_iota(jnp.int32, (tq, 1), 0)
    blk_end = lax.broadcasted_iota(jnp.int32, (1, nck), 1) * CMP_STRIDE + (CMP_LEN - 1)
    valid = blk_end <= t_row
    kc = kc_ref[0]
    vc = vc_ref[0]
    pc = [jnp.zeros((tq, nck), F32) for _ in range(B_KV_HEADS)]
    for r in range(B_GROUP):
        o_slot = None
        for g in range(B_KV_HEADS):
            qm = jnp.where(halves[g], q_slots[r], zero)
            s = lax.dot_general(qm, kc, NT_DIMS, preferred_element_type=F32)
            s = jnp.where(valid, s, -jnp.inf)
            m = jnp.max(s, axis=-1, keepdims=True)
            m = jnp.where(m == -jnp.inf, 0.0, m)
            p = jnp.exp2(s - m)
            den = jnp.sum(p, axis=-1, keepdims=True)
            p = p / jnp.maximum(den, 1e-30)
            pc[g] = pc[g] + p
            c = (g * B_GROUP + r) * N_BRANCH
            o = jnp.dot(p.astype(BF16), vc, preferred_element_type=F32) * gate[:, c:c + 1]
            o_slot = o if g == 0 else jnp.where(halves[g], o, o_slot)
        part_ref[0, :, r * LANES:(r + 1) * LANES] = o_slot

    t_lane = qt * tq + lax.broadcasted_iota(jnp.int32, (1, tq), 1)
    cur = t_lane // SEL_BLOCK
    j = lax.broadcasted_iota(jnp.int32, (nsel, 1), 0)
    forced = (j == 0) | (j == cur) | (j == cur - 1)
    imps = []
    for g in reversed(range(B_KV_HEADS)):
        imp = lax.dot_general(wov_ref[...], pc[g], NT_DIMS, preferred_element_type=F32,
                              precision=lax.Precision.HIGHEST)
        imps.append(jnp.where(j > cur, -jnp.inf, jnp.where(forced, jnp.inf, imp)))

    last_block = (qt * tq + tq - 1) // SEL_BLOCK
    for level in range(-(-nsel // RANK_CHUNK)):
        n_live = min((level + 1) * RANK_CHUNK, nsel)

        @pl.when(last_block // RANK_CHUNK == level)
        def _(n_live=n_live):
            blocks = []
            for imp in imps:
                if n_live <= SEL_TOPN:
                    blocks.append(jnp.zeros((HEAD_DIM, tq), F32))
                    continue
                live = imp[:n_live]
                jl = j[:n_live]
                rank = jnp.zeros((n_live, tq), jnp.int32)
                for jp in range(n_live):
                    row = live[jp:jp + 1, :]
                    rank = rank + jnp.where(jl > jp, (row >= live).astype(jnp.int32), (row > live).astype(jnp.int32))
                blocks.append(jnp.where(rank < SEL_TOPN, 0.0, NOT_SELECTED))
                if n_live < HEAD_DIM:
                    blocks.append(jnp.zeros((HEAD_DIM - n_live, tq), F32))
            sb_ref[...] = jnp.concatenate(blocks, axis=0).T.astype(sb_ref.dtype)

    sb = sb_ref[...]
    qaug = [jnp.concatenate([jnp.where(halves[g], q_slots[r], sb) for r in range(B_GROUP)], axis=0)
            for g in range(B_KV_HEADS)]
    n_steps = (qt * tq + tq + K_TILE - 1) // K_TILE
    acc_ref[...] = jnp.zeros_like(acc_ref)
    m_ref[...] = jnp.full_like(m_ref, M_INIT)

    def scores(kt):
        for jt in range(K_TILE // BIAS_TILE):
            k0 = pl.multiple_of(kt * K_TILE + jt * BIAS_TILE, BIAS_TILE)
            bi = jnp.clip(qt - k0 // tq + 1, 0, n_bias - 1)
            for g in range(B_KV_HEADS):
                k = kaug_ref[0, pl.ds(k0, BIAS_TILE), g * LANES:(g + 1) * LANES]
                s = lax.dot_general(qaug[g], k, NT_DIMS, preferred_element_type=F32)
                s_ref[kt % 2, g, :, jt * BIAS_TILE:(jt + 1) * BIAS_TILE] = s + bias_s_ref[g, bi]

    def accumulate(kt):
        k0 = pl.multiple_of(kt * K_TILE, K_TILE)
        v2 = vs_ref[0, pl.ds(k0, K_TILE), :]
        for g in range(B_KV_HEADS):
            v = jnp.where(halves[g], v2, jnp.ones((K_TILE, LANES), BF16))
            s = s_ref[kt % 2, g]
            m_old = m_ref[g]
            m_new = jnp.maximum(m_old, jnp.max(s, axis=-1, keepdims=True))
            alpha = jnp.exp2(m_old - m_new)
            p = jnp.exp2(s - jnp.tile(m_new, (1, K_TILE // LANES)))
            acc_ref[g] = alpha * acc_ref[g] + jnp.dot(p.astype(BF16), v, preferred_element_type=F32)
            m_ref[g] = m_new

    def body(kt, carry):
        accumulate(kt)
        scores(kt + 1)
        return carry

    w0 = pl.multiple_of(qt * tq, tq)
    kwin = kw_ref[0, pl.ds(w0, kw), :]
    vwin = vw_ref[0, pl.ds(w0, kw), :]
    col = lax.broadcasted_iota(jnp.int32, (1, kw), 1)
    real = col >= WIN_LEN - qt * tq
    win_slots = [None] * B_GROUP
    for g in range(B_KV_HEADS):
        qg = jnp.concatenate([jnp.where(halves[g], q_slots[r], zero) for r in range(B_GROUP)], axis=0)
        s = lax.dot_general(qg, kwin, NT_DIMS, preferred_element_type=F32)
        s = jnp.where(real, s + bias_w_ref[g], -jnp.inf)
        p = jnp.exp2(s - jnp.max(s, axis=-1, keepdims=True))
        acc = jnp.dot(p.astype(BF16), jnp.where(halves[g], vwin, jnp.ones_like(vwin)), preferred_element_type=F32)
        gated_slots(acc / jnp.maximum(pltpu.roll(acc, HEAD_DIM, axis=1), 1e-30), g, 2, win_slots)
    for r in range(B_GROUP):
        part_ref[1, :, r * LANES:(r + 1) * LANES] = win_slots[r]

    scores(0)
    lax.fori_loop(0, n_steps - 1, body, 0)
    accumulate(n_steps - 1)
    sel_slots = [None] * B_GROUP
    for g in range(B_KV_HEADS):
        acc = acc_ref[g]
        gated_slots(acc / jnp.maximum(pltpu.roll(acc, HEAD_DIM, axis=1), 1e-30), g, 1, sel_slots)
    for r in range(B_GROUP):
        sl = slice(r * LANES, (r + 1) * LANES)
        o_ref[0, :, sl] = ((part_ref[0, :, sl] + sel_slots[r]) + part_ref[1, :, sl]).astype(o_ref.dtype)


def _nsa(qb, gl, kcmp, vcmp, kaug, vsl, bias_s, kwpad, vwpad, bias_w):
    b, s, w = qb.shape
    nsel = s // SEL_BLOCK
    nck = kcmp.shape[1]
    ratio = SEL_BLOCK // CMP_STRIDE
    wov = np.zeros((nsel, nck), np.float32)
    for jj in range(nsel):
        for off, wt in zip(range(-1, ratio), CMP_OVERLAP):
            n = ratio * jj + off
            if 0 <= n < nck - 1:
                wov[jj, n] = wt
    wov = jnp.asarray(wov)
    rows = B_GROUP * Q_TILE
    tile = lambda width: pl.BlockSpec((1, Q_TILE, width), lambda bi, qi: (bi, qi, 0))
    whole = lambda a: pl.BlockSpec((1,) + a.shape[1:], lambda bi, qi: (bi, 0, 0))
    const = lambda a: pl.BlockSpec(a.shape, lambda bi, qi: (0,) * a.ndim, pipeline_mode=pl.Buffered(1))
    return pl.pallas_call(
        functools.partial(_nsa_kernel, nsel=nsel, n_bias=bias_s.shape[1], kw=bias_w.shape[-1]),
        grid=(b, s // Q_TILE),
        in_specs=[tile(w), tile(LANES), whole(kcmp), whole(vcmp), const(wov), whole(kaug), whole(vsl), const(bias_s),
                  whole(kwpad), whole(vwpad), const(bias_w)],
        out_specs=tile(w),
        out_shape=jax.ShapeDtypeStruct((b, s, w), BF16),
        scratch_shapes=[pltpu.VMEM((B_KV_HEADS, rows, LANES), F32), pltpu.VMEM((B_KV_HEADS, rows, LANES), F32),
                        pltpu.VMEM((2, B_KV_HEADS, rows, K_TILE), F32), pltpu.VMEM((2, Q_TILE, w), F32),
                        pltpu.VMEM((Q_TILE, LANES), BF16)],
        compiler_params=_cparams("parallel", "parallel"),
        name="nsa",
    )(qb, gl, kcmp, vcmp, wov, kaug, vsl, bias_s, kwpad, vwpad, bias_w)


def _tail_kernel(x_ref, oa_ref, ob_ref, wa_ref, wb_ref, g2_ref, wg_ref, wu_ref, wd_ref, gf_ref, o_ref):
    o_a = jnp.concatenate([oa_ref[p] for p in range(oa_ref.shape[0])], axis=1)
    mix = jnp.dot(o_a, wa_ref[...], preferred_element_type=F32)
    mix = mix + jnp.dot(ob_ref[...], wb_ref[...], preferred_element_type=F32)
    h = x_ref[...] + mix
    y = h * lax.rsqrt(jnp.mean(h * h, axis=-1, keepdims=True) + RMS_EPS)
    hn = (y * g2_ref[...]).astype(BF16)
    a = jnp.dot(hn, wg_ref[...], preferred_element_type=F32)
    u = jnp.dot(hn, wu_ref[...], preferred_element_type=F32)
    act = (jax.nn.silu(a) * u).astype(BF16)
    h2 = h + jnp.dot(act, wd_ref[...], preferred_element_type=F32)
    y2 = h2 * lax.rsqrt(jnp.mean(h2 * h2, axis=-1, keepdims=True) + RMS_EPS)
    o_ref[...] = y2 * gf_ref[...]


def _tail(x2, oa, ob, wa, wb, g2, wg, wu, wd, gf, tm):
    n, d = x2.shape
    row = lambda width: pl.BlockSpec((tm, width), lambda i: (i, 0))
    const = lambda a: pl.BlockSpec(a.shape, lambda i: (0, 0), pipeline_mode=pl.Buffered(1))
    return pl.pallas_call(
        _tail_kernel,
        grid=(n // tm,),
        in_specs=[row(d), pl.BlockSpec((oa.shape[0], tm, LANES), lambda i: (0, i, 0)), row(ob.shape[1]),
                  const(wa), const(wb), const(g2), const(wg), const(wu), const(wd), const(gf)],
        out_specs=row(d),
        out_shape=jax.ShapeDtypeStruct((n, d), F32),
        compiler_params=_cparams("parallel"),
        name="tail",
    )(x2, oa, ob, wa, wb, g2, wg, wu, wd, gf)


def _slot_perm():
    perm = np.zeros(B_WIDTH, np.int64)
    for r in range(B_GROUP):
        for g in range(B_KV_HEADS):
            src = (g * B_GROUP + r) * HEAD_DIM
            dst = r * LANES + g * HEAD_DIM
            perm[dst:dst + HEAD_DIM] = np.arange(src, src + HEAD_DIM)
    return perm


def _layer(h, tab_a, tab_b, norm1_g, w_in, cmp_pos, k_w1, k_b1, k_w2, v_w1, v_b1, v_w2,
           w_out, norm2_g, w_gate, w_up, w_down, norm_f_g):
    b, s, d = h.shape
    n = b * s
    assert s % SPAN == 0 and s % K_TILE == 0
    perm = _slot_perm()

    cols = np.cumsum([0, A_WIDTH, A_WIDTH, A_WIDTH, B_WIDTH] + [KV_WIDTH] * 6 + [GATE_WIDTH])
    w_aq = w_in[:, cols[0]:cols[1]] * (SCALE * LOG2E)
    w_bq = (w_in[:, cols[3]:cols[4]] * (SCALE * LOG2E))[:, perm]
    w_gl = jnp.pad(w_in[:, cols[10]:cols[11]], ((0, 0), (0, LANES - GATE_WIDTH)))
    w1 = jnp.concatenate([w_aq, w_in[:, cols[1]:cols[3]], w_bq, w_in[:, cols[4]:cols[10]], w_gl], axis=1).astype(BF16)
    widths = (A_WIDTH,) * 3 + (B_WIDTH,) + (KV_WIDTH,) * 6 + (LANES,)
    dtypes = (BF16,) * 4 + (F32,) * 2 + (BF16,) * 4 + (F32,)
    slot_major = (True,) * 3 + (False,) * 8
    x2 = h.reshape(n, d)
    qa, ka, va, qb, kc, vc, ksl, vsl, kw, vw, gl = _inproj(x2, norm1_g.reshape(1, d), w1, widths, dtypes,
                                                          slot_major, tm=512)
    r3 = lambda t: t.reshape(b, s, t.shape[-1])
    qb, ksl, vsl, kw, vw, gl = map(r3, (qb, ksl, vsl, kw, vw, gl))

    i = np.arange(DIL_BLOCK)[:, None]
    jj = np.arange(2 * DIL_BLOCK)[None, :]
    dist = i + DIL_BLOCK - jj
    tok_dist = np.stack([np.maximum(dist, 0) * dil for _, dil in DIL_PATTERNS])
    in_window = np.stack([(dist >= 0) & (dist <= window // dil) for window, dil in DIL_PATTERNS])
    bias_a = _bias_tiles(tab_a, tok_dist, in_window, inner=A_HEADS)[0]
    r4 = lambda t: t.reshape(A_HEADS // 2, b, s, LANES)
    o_a = _mixer_a(r4(qa), r4(ka), r4(va), bias_a).reshape(A_HEADS // 2, n, LANES)

    nch = s // CMP_STRIDE
    kcmp = _compress(kc.reshape(b, s, KV_WIDTH), cmp_pos, k_w1, k_b1, k_w2)
    vcmp = _compress(vc.reshape(b, s, KV_WIDTH), cmp_pos, v_w1, v_b1, v_w2)
    onehot = (np.arange(s)[:, None] // SEL_BLOCK) == np.arange(HEAD_DIM)[None, :]
    onehot = jnp.broadcast_to(jnp.asarray(onehot, BF16), (b, s, HEAD_DIM))
    kaug = jnp.concatenate([ksl[..., :HEAD_DIM], onehot, onehot, ksl[..., HEAD_DIM:]], axis=-1)
    d_sat = int(np.nonzero(_t5_bucket_np(np.arange(s)) < REL_BUCKETS - 1)[0].max()) + 1
    first_far = -(-(d_sat + BIAS_TILE - 1) // Q_TILE)
    n_bias = min(s // Q_TILE, first_far + 1) + 1
    dd = ((np.arange(n_bias)[:, None, None] - 1) * Q_TILE + np.arange(Q_TILE)[None, :, None]
          - np.arange(BIAS_TILE)[None, None, :])
    bias_sl = _bias_tiles(tab_b, dd, dd >= 0, inner=B_GROUP)
    bias_sl = bias_sl.reshape(B_KV_HEADS, n_bias, B_GROUP * Q_TILE, BIAS_TILE)
    kwid = WIN_LEN + Q_TILE
    dw = np.arange(Q_TILE)[:, None] + WIN_LEN - np.arange(kwid)[None, :]
    bias_w = _bias_tiles(tab_b, dw[None], ((dw >= 0) & (dw < WIN_LEN))[None], inner=B_HEADS)
    bias_w = bias_w.reshape(B_KV_HEADS, B_GROUP * Q_TILE, kwid)
    pad = lambda t: jnp.pad(t, ((0, 0), (WIN_LEN, 0), (0, 0)))
    o_b = _nsa(qb, gl, kcmp, vcmp, kaug, vsl, bias_sl, pad(kw), pad(vw), bias_w)

    wa = w_out[:A_WIDTH].astype(BF16)
    wb = w_out[A_WIDTH:][perm].astype(BF16)
    out = _tail(x2, o_a, o_b.reshape(n, B_WIDTH), wa, wb, norm2_g.reshape(1, d), w_gate.astype(BF16),
                w_up.astype(BF16), w_down.astype(BF16), norm_f_g.reshape(1, d), tm=512)
    return out.reshape(b, s, d)


def kernel(x, norm1_g, w_in, rel_bias, cmp_pos, cmp_k_w1, cmp_k_b1, cmp_k_w2, cmp_v_w1, cmp_v_b1, cmp_v_w2,
           w_out, norm2_g, w_gate, w_up, w_down, norm_f_g):
    assert w_in.shape[0] == 1, "single-layer model"
    tab_a = rel_bias[:, :A_HEADS].T * LOG2E
    tab_b = rel_bias[:, A_HEADS:].T * LOG2E
    return _layer(x, tab_a, tab_b, norm1_g[0], w_in[0], cmp_pos[0], cmp_k_w1[0], cmp_k_b1[0], cmp_k_w2[0],
                  cmp_v_w1[0], cmp_v_b1[0], cmp_v_w2[0], w_out[0], norm2_g[0], w_gate[0], w_up[0], w_down[0],
                  norm_f_g)
```

```python
import functools
import math

import numpy as np
import jax
import jax.numpy as jnp
from jax import lax
from jax.experimental import pallas as pl
from jax.experimental.pallas import tpu as pltpu

F32 = jnp.float32
BF16 = jnp.bfloat16

HEAD_DIM = 64
LANES = 128
A_HEADS = 8
DIL_PATTERNS = ((128, 1), (512, 4), (2048, 16))
DIL_BLOCK = 128
B_HEADS = 8
B_KV_HEADS = 2
B_GROUP = B_HEADS // B_KV_HEADS
CMP_LEN = 32
CMP_STRIDE = 16
CMP_HIDDEN = 128
SEL_BLOCK = 64
SEL_TOPN = 16
CMP_OVERLAP = (1.0, 2.0, 2.0, 2.0, 1.0)
WIN_LEN = 512
N_BRANCH = 3
REL_BUCKETS = 32
REL_MAX_DIST = 2048
RMS_EPS = 1e-6
A_WIDTH = A_HEADS * HEAD_DIM
B_WIDTH = B_HEADS * HEAD_DIM
KV_WIDTH = B_KV_HEADS * HEAD_DIM
GATE_WIDTH = B_HEADS * N_BRANCH
SCALE = HEAD_DIM ** -0.5
LOG2E = math.log2(math.e)

SPAN = DIL_PATTERNS[-1][1] * DIL_BLOCK
NARROW_FROM = 4
MIX_UNROLL = 16
RANK_CHUNK = 16
Q_TILE = 128
K_TILE = 512
BIAS_TILE = 256
NOT_SELECTED = -(2.0 ** 100)
M_INIT = -1e30
VMEM_LIMIT = 56 * 1024 * 1024

NT_DIMS = (((1,), (1,)), ((), ()))


def _cparams(*sem):
    return pltpu.CompilerParams(dimension_semantics=sem, vmem_limit_bytes=VMEM_LIMIT)


def _t5_bucket_np(dist):
    max_exact = REL_BUCKETS // 2
    d = np.asarray(dist)
    df = np.maximum(d, 1).astype(np.float32)
    large = max_exact + (np.log(df / np.float32(max_exact)) / np.float32(math.log(REL_MAX_DIST / max_exact))
                         * np.float32(REL_BUCKETS - max_exact)).astype(np.int32)
    large = np.minimum(large, REL_BUCKETS - 1)
    return np.where(d < max_exact, d, large).astype(np.int32)


def _bias_kernel(tab_ref, idx_ref, o_ref):
    inner = o_ref.shape[2]
    idx = idx_ref[0]
    accs = [jnp.full(idx.shape, -jnp.inf, F32) for _ in range(inner)]
    for bucket in range(REL_BUCKETS):
        hit = idx == bucket
        for hi in range(inner):
            head = pl.program_id(0) * inner + hi
            accs[hi] = jnp.where(hit, tab_ref[head * REL_BUCKETS + bucket], accs[hi])
    for hi in range(inner):
        o_ref[0, 0, hi] = accs[hi]


def _bias_tiles(tab, dist, valid, inner):
    h = tab.shape[0]
    t, r, c = dist.shape
    idx = jnp.asarray(np.where(valid, _t5_bucket_np(np.maximum(dist, 0)), -1).astype(np.int32))
    return pl.pallas_call(
        _bias_kernel,
        grid=(h // inner, t),
        in_specs=[pl.BlockSpec(memory_space=pltpu.SMEM),
                  pl.BlockSpec((1, r, c), lambda a, ti: (ti, 0, 0))],
        out_specs=pl.BlockSpec((1, 1, inner, r, c), lambda a, ti: (a, ti, 0, 0, 0)),
        out_shape=jax.ShapeDtypeStruct((h // inner, t, inner, r, c), F32),
        compiler_params=_cparams("parallel", "parallel"),
        name="bias_tiles",
    )(tab.reshape(-1), idx)


def _half_mask(g):
    lane = lax.broadcasted_iota(jnp.int32, (1, LANES), 1)
    return (lane >= HEAD_DIM) if g else (lane < HEAD_DIM)


def _inproj_kernel(x_ref, g_ref, w_ref, *out_refs, widths):
    x = x_ref[...]
    y = x * lax.rsqrt(jnp.mean(x * x, axis=-1, keepdims=True) + RMS_EPS)
    xn = (y * g_ref[...]).astype(BF16)
    merged = jnp.dot(xn, w_ref[:, sum(widths[:NARROW_FROM]):], preferred_element_type=F32)
    start = 0
    for idx, (o_ref, w) in enumerate(zip(out_refs, widths)):
        if idx < NARROW_FROM:
            r = jnp.dot(xn, w_ref[:, start:start + w], preferred_element_type=F32)
        else:
            off = start - sum(widths[:NARROW_FROM])
            r = merged[:, off:off + w]
        if len(o_ref.shape) == 3:
            for p in range(o_ref.shape[0]):
                o_ref[p] = r[:, p * LANES:(p + 1) * LANES].astype(o_ref.dtype)
        else:
            o_ref[...] = r.astype(o_ref.dtype)
        start += w


def _inproj(x2, g, w, widths, dtypes, slot_major, tm):
    n, d = x2.shape
    shape = lambda w_, sm: (w_ // LANES, n, LANES) if sm else (n, w_)
    spec = lambda w_, sm: (pl.BlockSpec((w_ // LANES, tm, LANES), lambda i: (0, i, 0)) if sm
                           else pl.BlockSpec((tm, w_), lambda i: (i, 0)))
    out_shape = [jax.ShapeDtypeStruct(shape(w_, sm), dt) for w_, dt, sm in zip(widths, dtypes, slot_major)]
    return pl.pallas_call(
        functools.partial(_inproj_kernel, widths=widths),
        grid=(n // tm,),
        in_specs=[pl.BlockSpec((tm, d), lambda i: (i, 0)),
                  pl.BlockSpec((1, d), lambda i: (0, 0)),
                  pl.BlockSpec(w.shape, lambda i: (0, 0))],
        out_specs=[spec(w_, sm) for w_, sm in zip(widths, slot_major)],
        out_shape=out_shape,
        compiler_params=_cparams("parallel"),
        name="inproj",
    )(x2, g, w)


def _mixer_a_kernel(q_ref, kp_ref, kc_ref, vp_ref, vc_ref, bias_ref, o_ref, qf, kf, vf, acc_scr, m_scr):
    sb = pl.program_id(2)
    qf[...] = q_ref[0, 0].astype(F32)
    kf[:SPAN] = kp_ref[0, 0].astype(F32)
    kf[SPAN:] = kc_ref[0, 0].astype(F32)
    vf[:SPAN] = vp_ref[0, 0].astype(F32)
    vf[SPAN:] = vc_ref[0, 0].astype(F32)
    halves = [_half_mask(hh) for hh in range(2)]
    col = lax.broadcasted_iota(jnp.int32, (1, 2 * DIL_BLOCK), 1)
    ones = jnp.ones((2 * DIL_BLOCK, LANES), BF16)
    n_blocks = SPAN // DIL_BLOCK
    for p, (_, dil) in enumerate(DIL_PATTERNS):
        step = dil * DIL_BLOCK
        per_class = SPAN // step

        def body(grp, carry, p=p, dil=dil, step=step, per_class=per_class):
            for u in range(MIX_UNROLL):
                idx = grp * MIX_UNROLL + u
                r, n = idx // per_class, idx % per_class
                q0 = pl.multiple_of(n * step, step) + r
                rows_q = pl.ds(q0, DIL_BLOCK, stride=dil)
                rows_k = pl.ds(q0 + SPAN - step, 2 * DIL_BLOCK, stride=dil)
                q2 = qf[rows_q, :].astype(BF16)
                kcat = kf[rows_k, :].astype(BF16)
                vcat = vf[rows_k, :].astype(BF16)
                keep = (col >= DIL_BLOCK) | (sb * per_class + n > 0)
                qm = jnp.concatenate([jnp.where(halves[hh], q2, jnp.zeros_like(q2)) for hh in range(2)], axis=0)
                s2 = lax.dot_general(qm, kcat, NT_DIMS, preferred_element_type=F32)
                for hh in range(2):
                    s = s2[hh * DIL_BLOCK:(hh + 1) * DIL_BLOCK]
                    s = jnp.where(keep, s + bias_ref[p, hh], -jnp.inf)
                    m_blk = jnp.max(s, axis=-1, keepdims=True)
                    pe = jnp.exp2(s - m_blk)
                    pv = jnp.dot(pe.astype(BF16), jnp.where(halves[hh], vcat, ones), preferred_element_type=F32)
                    acc_scr[p, hh, rows_q, :] = pv
                    m_scr[p, hh, rows_q, :] = jnp.broadcast_to(m_blk, pv.shape)
            return carry

        lax.fori_loop(0, n_blocks // MIX_UNROLL, body, 0)

    def finish(c, carry):
        rows = pl.ds(pl.multiple_of(c * DIL_BLOCK, DIL_BLOCK), DIL_BLOCK)
        outs = []
        for hh in range(2):
            ms = [m_scr[p, hh, rows, :] for p in range(len(DIL_PATTERNS))]
            m_all = functools.reduce(jnp.maximum, ms)
            tot = sum(jnp.exp2(m - m_all) * acc_scr[p, hh, rows, :] for p, m in enumerate(ms))
            outs.append(tot / pltpu.roll(tot, HEAD_DIM, axis=1))
        o_ref[0, 0, rows, :] = jnp.where(halves[0], outs[0], outs[1]).astype(o_ref.dtype)
        return carry

    lax.fori_loop(0, n_blocks, finish, 0, unroll=4)


def _mixer_a(q, k, v, bias):
    npair, b, s, _ = q.shape
    blk = (1, 1, SPAN, LANES)
    cur = lambda bi, pi, si: (pi, bi, si, 0)
    prev = lambda bi, pi, si: (pi, bi, jnp.maximum(si - 1, 0), 0)
    return pl.pallas_call(
        _mixer_a_kernel,
        grid=(b, npair, s // SPAN),
        in_specs=[pl.BlockSpec(blk, cur), pl.BlockSpec(blk, prev), pl.BlockSpec(blk, cur),
                  pl.BlockSpec(blk, prev), pl.BlockSpec(blk, cur),
                  pl.BlockSpec((bias.shape[0], 2) + bias.shape[2:], lambda bi, pi, si: (0, pi, 0, 0))],
        out_specs=pl.BlockSpec(blk, cur),
        out_shape=jax.ShapeDtypeStruct(q.shape, BF16),
        scratch_shapes=[pltpu.VMEM((SPAN, LANES), F32), pltpu.VMEM((2 * SPAN, LANES), F32),
                        pltpu.VMEM((2 * SPAN, LANES), F32),
                        pltpu.VMEM((len(DIL_PATTERNS), 2, SPAN, LANES), F32),
                        pltpu.VMEM((len(DIL_PATTERNS), 2, SPAN, LANES), F32)],
        compiler_params=_cparams("parallel", "parallel", "parallel"),
        name="mixer_a",
    )(q, k, k, v, v, bias)


def _compress_kernel(c_ref, pa_ref, pb_ref, w1a_ref, w1b_ref, b1_ref, w2_ref, o_ref):
    nch = o_ref.shape[1]
    c = jnp.concatenate([c_ref[0, pl.ds(l, nch, stride=CMP_STRIDE), :] for l in range(CMP_STRIDE)], axis=1)
    xa = (c + pa_ref[...]).astype(BF16)
    xb = (c + pb_ref[...]).astype(BF16)
    ha = jnp.dot(xa, w1a_ref[...], preferred_element_type=F32)
    hb = jnp.dot(xb, w1b_ref[...], preferred_element_type=F32)
    hb_next = jnp.concatenate([hb[1:], jnp.zeros_like(hb[:1])], axis=0)
    hid = jax.nn.gelu(ha + hb_next + b1_ref[...])
    o_ref[0] = jnp.dot(hid.astype(BF16), w2_ref[...], preferred_element_type=F32).astype(o_ref.dtype)


def _compress(c, pos, w1, b1, w2):
    b, s, _ = c.shape
    nch = s // CMP_STRIDE
    half = CMP_LEN // 2
    zero = jnp.zeros((half, HEAD_DIM, CMP_HIDDEN), F32)

    def grouped(wpart):
        g0 = jnp.concatenate([wpart, zero], axis=1).reshape(half * LANES, CMP_HIDDEN)
        g1 = jnp.concatenate([zero, wpart], axis=1).reshape(half * LANES, CMP_HIDDEN)
        return jnp.concatenate([g0, g1], axis=1).astype(BF16)

    w1a, w1b = grouped(w1[:half]), grouped(w1[half:])
    pa = jnp.tile(pos[:half], (1, 2)).reshape(1, half * LANES)
    pb = jnp.tile(pos[half:], (1, 2)).reshape(1, half * LANES)
    b1g = jnp.tile(b1, 2).reshape(1, 2 * CMP_HIDDEN)
    zw = jnp.zeros_like(w2)
    w2g = jnp.concatenate([jnp.concatenate([w2, zw], axis=1),
                           jnp.concatenate([zw, w2], axis=1)], axis=0).astype(BF16)
    full = lambda a: pl.BlockSpec(a.shape, lambda i: (0,) * a.ndim)
    return pl.pallas_call(
        _compress_kernel,
        grid=(b,),
        in_specs=[pl.BlockSpec((1,) + c.shape[1:], lambda i: (i, 0, 0)),
                  full(pa), full(pb), full(w1a), full(w1b), full(b1g), full(w2g)],
        out_specs=pl.BlockSpec((1, nch, LANES), lambda i: (i, 0, 0)),
        out_shape=jax.ShapeDtypeStruct((b, nch, LANES), BF16),
        compiler_params=_cparams("parallel"),
        name="compress",
    )(c, pa, pb, w1a, w1b, b1g, w2g)


def _nsa_kernel(q_ref, gl_ref, kc_ref, vc_ref, wov_ref, ks_ref, blk_ref, vs_ref, bias_s_ref, kw_ref, vw_ref, bias_w_ref,
                o_ref, acc_ref, m_ref, s_ref, part_ref, sb_ref, *, nsel, n_bias, kw):
    qt = pl.program_id(1)
    tq = q_ref.shape[1]
    gate = jax.nn.sigmoid(gl_ref[0])
    halves = [_half_mask(g) for g in range(B_KV_HEADS)]
    q_slots = [q_ref[0, :, r * LANES:(r + 1) * LANES] for r in range(B_GROUP)]
    zero = jnp.zeros((tq, LANES), BF16)

    def gated_slots(o, g, branch, slots):
        for r in range(B_GROUP):
            c = (g * B_GROUP + r) * N_BRANCH + branch
            o_r = o[r * tq:(r + 1) * tq] * gate[:, c:c + 1]
            slots[r] = o_r if slots[r] is None else jnp.where(halves[g], o_r, slots[r])

    nck = kc_ref.shape[1]
    t_row = qt * tq + lax.broadcasted_iota(jnp.int32, (B_GROUP * tq, 1), 0) % tq
    blk_end = lax.broadcasted_iota(jnp.int32, (1, nck), 1) * CMP_STRIDE + (CMP_LEN - 1)
    valid = blk_end <= t_row
    kc = kc_ref[0]
    vc = vc_ref[0]
    pc = []
    cmp_slots = [None] * B_GROUP
    for g in range(B_KV_HEADS):
        qg = jnp.concatenate([jnp.where(halves[g], q_slots[r], zero) for r in range(B_GROUP)], axis=0)
        s = lax.dot_general(qg, kc, NT_DIMS, preferred_element_type=F32)
        s = jnp.where(valid, s, -jnp.inf)
        m = jnp.max(s, axis=-1, keepdims=True)
        m = jnp.where(m == -jnp.inf, 0.0, m)
        p = jnp.exp2(s - m)
        den = jnp.sum(p, axis=-1, keepdims=True)
        p = p / jnp.maximum(den, 1e-30)
        pc.append(sum(p[r * tq:(r + 1) * tq] for r in range(B_GROUP)))
        gated_slots(jnp.dot(p.astype(BF16), vc, preferred_element_type=F32), g, 0, cmp_slots)
    for r in range(B_GROUP):
        part_ref[0, :, r * LANES:(r + 1) * LANES] = cmp_slots[r]

    t_lane = qt * tq + lax.broadcasted_iota(jnp.int32, (1, tq), 1)
    cur = t_lane // SEL_BLOCK
    j = lax.broadcasted_iota(jnp.int32, (nsel, 1), 0)
    forced = (j == 0) | (j == cur) | (j == cur - 1)
    imps = []
    for g in reversed(range(B_KV_HEADS)):
        imp = lax.dot_general(wov_ref[...], pc[g], NT_DIMS, preferred_element_type=F32,
                              precision=lax.Precision.HIGHEST)
        imps.append(jnp.where(j > cur, -jnp.inf, jnp.where(forced, jnp.inf, imp)))

    last_block = (qt * tq + tq - 1) // SEL_BLOCK
    for level in range(-(-nsel // RANK_CHUNK)):
        n_live = min((level + 1) * RANK_CHUNK, nsel)

        @pl.when(last_block // RANK_CHUNK == level)
        def _(n_live=n_live):
            blocks = []
            for imp in imps:
                if n_live <= SEL_TOPN:
                    blocks.append(jnp.zeros((HEAD_DIM, tq), F32))
                    continue
                live = imp[:n_live]
                jl = j[:n_live]
                rank = jnp.zeros((n_live, tq), jnp.int32)
                for jp in range(n_live):
                    row = live[jp:jp + 1, :]
                    rank = rank + jnp.where(jl > jp, (row >= live).astype(jnp.int32), (row > live).astype(jnp.int32))
                blocks.append(jnp.where(rank < SEL_TOPN, 0.0, NOT_SELECTED))
                if n_live < HEAD_DIM:
                    blocks.append(jnp.zeros((HEAD_DIM - n_live, tq), F32))
            sb_ref[...] = jnp.concatenate(blocks, axis=0).T.astype(sb_ref.dtype)

    sb = sb_ref[...]
    qaug = [jnp.concatenate([jnp.where(halves[g], q_slots[r], sb) for r in range(B_GROUP)], axis=0)
            for g in range(B_KV_HEADS)]
    n_steps = (qt * tq + tq + K_TILE - 1) // K_TILE
    acc_ref[...] = jnp.zeros_like(acc_ref)
    m_ref[...] = jnp.full_like(m_ref, M_INIT)

    def scores(kt):
        for jt in range(K_TILE // BIAS_TILE):
            k0 = pl.multiple_of(kt * K_TILE + jt * BIAS_TILE, BIAS_TILE)
            bi = jnp.clip(qt - k0 // tq + 1, 0, n_bias - 1)
            k2 = ks_ref[0, pl.ds(k0, BIAS_TILE), :]
            e2 = blk_ref[pl.ds(k0, BIAS_TILE), :]
            for g in range(B_KV_HEADS):
                k = jnp.where(halves[g], k2, e2)
                s = lax.dot_general(qaug[g], k, NT_DIMS, preferred_element_type=F32)
                s_ref[kt % 2, g, :, jt * BIAS_TILE:(jt + 1) * BIAS_TILE] = s + bias_s_ref[g, bi]

    def accumulate(kt):
        k0 = pl.multiple_of(kt * K_TILE, K_TILE)
        v2 = vs_ref[0, pl.ds(k0, K_TILE), :]
        for g in range(B_KV_HEADS):
            v = jnp.where(halves[g], v2, jnp.ones((K_TILE, LANES), BF16))
            s = s_ref[kt % 2, g]
            m_old = m_ref[g]
            m_new = jnp.maximum(m_old, jnp.max(s, axis=-1, keepdims=True))
            alpha = jnp.exp2(m_old - m_new)
            p = jnp.exp2(s - jnp.tile(m_new, (1, K_TILE // LANES)))
            acc_ref[g] = alpha * acc_ref[g] + jnp.dot(p.astype(BF16), v, preferred_element_type=F32)
            m_ref[g] = m_new

    def body(kt, carry):
        accumulate(kt)
        scores(kt + 1)
        return carry

    n_wt = kw // tq
    w0 = pl.multiple_of(jnp.maximum(qt * tq - WIN_LEN, 0), tq)
    skip = jnp.maximum(WIN_LEN // tq - qt, 0)
    kwin = kw_ref[0, pl.ds(w0, kw), :]
    vwin = vw_ref[0, pl.ds(w0, kw), :]
    win_slots = [None] * B_GROUP
    for g in range(B_KV_HEADS):
        qg = jnp.concatenate([jnp.where(halves[g], q_slots[r], zero) for r in range(B_GROUP)], axis=0)
        s = lax.dot_general(qg, kwin, NT_DIMS, preferred_element_type=F32)
        s = jnp.concatenate([s[:, jw * tq:(jw + 1) * tq] + bias_w_ref[g, jnp.minimum(jw + skip, n_wt)]
                             for jw in range(n_wt)], axis=1)
        p = jnp.exp2(s - jnp.max(s, axis=-1, keepdims=True))
        acc = jnp.dot(p.astype(BF16), jnp.where(halves[g], vwin, jnp.ones_like(vwin)), preferred_element_type=F32)
        gated_slots(acc / jnp.maximum(pltpu.roll(acc, HEAD_DIM, axis=1), 1e-30), g, 2, win_slots)
    for r in range(B_GROUP):
        part_ref[1, :, r * LANES:(r + 1) * LANES] = win_slots[r]

    scores(0)
    lax.fori_loop(0, n_steps - 1, body, 0)
    accumulate(n_steps - 1)
    sel_slots = [None] * B_GROUP
    for g in range(B_KV_HEADS):
        acc = acc_ref[g]
        gated_slots(acc / jnp.maximum(pltpu.roll(acc, HEAD_DIM, axis=1), 1e-30), g, 1, sel_slots)
    for r in range(B_GROUP):
        sl = slice(r * LANES, (r + 1) * LANES)
        o_ref[0, :, sl] = ((part_ref[0, :, sl] + sel_slots[r]) + part_ref[1, :, sl]).astype(o_ref.dtype)


def _nsa(qb, gl, kcmp, vcmp, ksl, vsl, bias_s, kwin, vwin, bias_w):
    b, s, w = qb.shape
    nsel = s // SEL_BLOCK
    nck = kcmp.shape[1]
    ratio = SEL_BLOCK // CMP_STRIDE
    wov = np.zeros((nsel, nck), np.float32)
    for jj in range(nsel):
        for off, wt in zip(range(-1, ratio), CMP_OVERLAP):
            n = ratio * jj + off
            if 0 <= n < nck - 1:
                wov[jj, n] = wt
    wov = jnp.asarray(wov)
    blk = jnp.asarray((np.arange(s)[:, None] // SEL_BLOCK) == (np.arange(LANES)[None, :] % HEAD_DIM), BF16)
    rows = B_GROUP * Q_TILE
    tile = lambda width: pl.BlockSpec((1, Q_TILE, width), lambda bi, qi: (bi, qi, 0))
    whole = lambda a: pl.BlockSpec((1,) + a.shape[1:], lambda bi, qi: (bi, 0, 0))
    const = lambda a: pl.BlockSpec(a.shape, lambda bi, qi: (0,) * a.ndim, pipeline_mode=pl.Buffered(1))
    return pl.pallas_call(
        functools.partial(_nsa_kernel, nsel=nsel, n_bias=bias_s.shape[1], kw=WIN_LEN + Q_TILE),
        grid=(b, s // Q_TILE),
        in_specs=[tile(w), tile(LANES), whole(kcmp), whole(vcmp), const(wov), whole(ksl), const(blk), whole(vsl),
                  const(bias_s), whole(kwin), whole(vwin), const(bias_w)],
        out_specs=tile(w),
        out_shape=jax.ShapeDtypeStruct((b, s, w), BF16),
        scratch_shapes=[pltpu.VMEM((B_KV_HEADS, rows, LANES), F32), pltpu.VMEM((B_KV_HEADS, rows, LANES), F32),
                        pltpu.VMEM((2, B_KV_HEADS, rows, K_TILE), F32), pltpu.VMEM((2, Q_TILE, w), F32),
                        pltpu.VMEM((Q_TILE, LANES), BF16)],
        compiler_params=_cparams("parallel", "parallel"),
        name="nsa",
    )(qb, gl, kcmp, vcmp, wov, ksl, blk, vsl, bias_s, kwin, vwin, bias_w)


def _tail_kernel(x_ref, oa_ref, ob_ref, wa_ref, wb_ref, g2_ref, wg_ref, wu_ref, wd_ref, gf_ref, o_ref):
    o_a = jnp.concatenate([oa_ref[p] for p in range(oa_ref.shape[0])], axis=1)
    mix = jnp.dot(o_a, wa_ref[...], preferred_element_type=F32)
    mix = mix + jnp.dot(ob_ref[...], wb_ref[...], preferred_element_type=F32)
    h = x_ref[...] + mix
    y = h * lax.rsqrt(jnp.mean(h * h, axis=-1, keepdims=True) + RMS_EPS)
    hn = (y * g2_ref[...]).astype(BF16)
    a = jnp.dot(hn, wg_ref[...], preferred_element_type=F32)
    u = jnp.dot(hn, wu_ref[...], preferred_element_type=F32)
    act = (jax.nn.silu(a) * u).astype(BF16)
    h2 = h + jnp.dot(act, wd_ref[...], preferred_element_type=F32)
    y2 = h2 * lax.rsqrt(jnp.mean(h2 * h2, axis=-1, keepdims=True) + RMS_EPS)
    o_ref[...] = y2 * gf_ref[...]


def _tail(x2, oa, ob, wa, wb, g2, wg, wu, wd, gf, tm):
    n, d = x2.shape
    row = lambda width: pl.BlockSpec((tm, width), lambda i: (i, 0))
    const = lambda a: pl.BlockSpec(a.shape, lambda i: (0, 0), pipeline_mode=pl.Buffered(1))
    return pl.pallas_call(
        _tail_kernel,
        grid=(n // tm,),
        in_specs=[row(d), pl.BlockSpec((oa.shape[0], tm, LANES), lambda i: (0, i, 0)), row(ob.shape[1]),
                  const(wa), const(wb), const(g2), const(wg), const(wu), const(wd), const(gf)],
        out_specs=row(d),
        out_shape=jax.ShapeDtypeStruct((n, d), F32),
        compiler_params=_cparams("parallel"),
        name="tail",
    )(x2, oa, ob, wa, wb, g2, wg, wu, wd, gf)


def _slot_perm():
    perm = np.zeros(B_WIDTH, np.int64)
    for r in range(B_GROUP):
        for g in range(B_KV_HEADS):
            src = (g * B_GROUP + r) * HEAD_DIM
            dst = r * LANES + g * HEAD_DIM
            perm[dst:dst + HEAD_DIM] = np.arange(src, src + HEAD_DIM)
    return perm


def _layer(h, tab_a, tab_b, norm1_g, w_in, cmp_pos, k_w1, k_b1, k_w2, v_w1, v_b1, v_w2,
           w_out, norm2_g, w_gate, w_up, w_down, norm_f_g):
    b, s, d = h.shape
    n = b * s
    assert s % SPAN == 0 and s % K_TILE == 0
    perm = _slot_perm()

    cols = np.cumsum([0, A_WIDTH, A_WIDTH, A_WIDTH, B_WIDTH] + [KV_WIDTH] * 6 + [GATE_WIDTH])
    w_aq = w_in[:, cols[0]:cols[1]] * (SCALE * LOG2E)
    w_bq = (w_in[:, cols[3]:cols[4]] * (SCALE * LOG2E))[:, perm]
    w_gl = jnp.pad(w_in[:, cols[10]:cols[11]], ((0, 0), (0, LANES - GATE_WIDTH)))
    w1 = jnp.concatenate([w_aq, w_in[:, cols[1]:cols[3]], w_bq, w_in[:, cols[4]:cols[10]], w_gl], axis=1).astype(BF16)
    widths = (A_WIDTH,) * 3 + (B_WIDTH,) + (KV_WIDTH,) * 6 + (LANES,)
    dtypes = (BF16,) * 4 + (F32,) * 2 + (BF16,) * 4 + (F32,)
    slot_major = (True,) * 3 + (False,) * 8
    x2 = h.reshape(n, d)
    qa, ka, va, qb, kc, vc, ksl, vsl, kw, vw, gl = _inproj(x2, norm1_g.reshape(1, d), w1, widths, dtypes,
                                                          slot_major, tm=512)
    r3 = lambda t: t.reshape(b, s, t.shape[-1])
    qb, ksl, vsl, kw, vw, gl = map(r3, (qb, ksl, vsl, kw, vw, gl))

    i = np.arange(DIL_BLOCK)[:, None]
    jj = np.arange(2 * DIL_BLOCK)[None, :]
    dist = i + DIL_BLOCK - jj
    tok_dist = np.stack([np.maximum(dist, 0) * dil for _, dil in DIL_PATTERNS])
    in_window = np.stack([(dist >= 0) & (dist <= window // dil) for window, dil in DIL_PATTERNS])
    bias_a = _bias_tiles(tab_a, tok_dist, in_window, inner=A_HEADS)[0]
    r4 = lambda t: t.reshape(A_HEADS // 2, b, s, LANES)
    o_a = _mixer_a(r4(qa), r4(ka), r4(va), bias_a).reshape(A_HEADS // 2, n, LANES)

    nch = s // CMP_STRIDE
    kcmp = _compress(kc.reshape(b, s, KV_WIDTH), cmp_pos, k_w1, k_b1, k_w2)
    vcmp = _compress(vc.reshape(b, s, KV_WIDTH), cmp_pos, v_w1, v_b1, v_w2)
    d_sat = int(np.nonzero(_t5_bucket_np(np.arange(s)) < REL_BUCKETS - 1)[0].max()) + 1
    first_far = -(-(d_sat + BIAS_TILE - 1) // Q_TILE)
    n_bias = min(s // Q_TILE, first_far + 1) + 1
    dd = ((np.arange(n_bias)[:, None, None] - 1) * Q_TILE + np.arange(Q_TILE)[None, :, None]
          - np.arange(BIAS_TILE)[None, None, :])
    bias_sl = _bias_tiles(tab_b, dd, dd >= 0, inner=B_GROUP)
    bias_sl = bias_sl.reshape(B_KV_HEADS, n_bias, B_GROUP * Q_TILE, BIAS_TILE)
    n_wt = WIN_LEN // Q_TILE + 1
    dw = ((n_wt - 1 - np.arange(n_wt + 1))[:, None, None] * Q_TILE + np.arange(Q_TILE)[None, :, None]
          - np.arange(Q_TILE)[None, None, :])
    in_win = (dw >= 0) & (dw < WIN_LEN) & (np.arange(n_wt + 1) < n_wt)[:, None, None]
    bias_w = _bias_tiles(tab_b, dw, in_win, inner=B_GROUP)
    bias_w = bias_w.reshape(B_KV_HEADS, n_wt + 1, B_GROUP * Q_TILE, Q_TILE)
    o_b = _nsa(qb, gl, kcmp, vcmp, ksl, vsl, bias_sl, kw, vw, bias_w)

    wa = w_out[:A_WIDTH].astype(BF16)
    wb = w_out[A_WIDTH:][perm].astype(BF16)
    out = _tail(x2, o_a, o_b.reshape(n, B_WIDTH), wa, wb, norm2_g.reshape(1, d), w_gate.astype(BF16),
                w_up.astype(BF16), w_down.astype(BF16), norm_f_g.reshape(1, d), tm=512)
    return out.reshape(b, s, d)


def kernel(x, norm1_g, w_in, rel_bias, cmp_pos, cmp_k_w1, cmp_k_b1, cmp_k_w2, cmp_v_w1, cmp_v_b1, cmp_v_w2,
           w_out, norm2_g, w_gate, w_up, w_down, norm_f_g):
    assert w_in.shape[0] == 1, "single-layer model"
    tab_a = rel_bias[:, :A_HEADS].T * LOG2E
    tab_b = rel_bias[:, A_HEADS:].T * LOG2E
    return _layer(x, tab_a, tab_b, norm1_g[0], w_in[0], cmp_pos[0], cmp_k_w1[0], cmp_k_b1[0], cmp_k_w2[0],
                  cmp_v_w1[0], cmp_v_b1[0], cmp_v_w2[0], w_out[0], norm2_g[0], w_gate[0], w_up[0], w_down[0],
                  norm_f_g)
```

```python
import functools
import math

import numpy as np
import jax
import jax.numpy as jnp
from jax import lax
from jax.experimental import pallas as pl
from jax.experimental.pallas import tpu as pltpu

F32 = jnp.float32
BF16 = jnp.bfloat16

HEAD_DIM = 64
LANES = 128
A_HEADS = 8
DIL_PATTERNS = ((128, 1), (512, 4), (2048, 16))
DIL_BLOCK = 128
B_HEADS = 8
B_KV_HEADS = 2
B_GROUP = B_HEADS // B_KV_HEADS
CMP_LEN = 32
CMP_STRIDE = 16
CMP_HIDDEN = 128
SEL_BLOCK = 64
SEL_TOPN = 16
CMP_OVERLAP = (1.0, 2.0, 2.0, 2.0, 1.0)
WIN_LEN = 512
N_BRANCH = 3
REL_BUCKETS = 32
REL_MAX_DIST = 2048
RMS_EPS = 1e-6
A_WIDTH = A_HEADS * HEAD_DIM
B_WIDTH = B_HEADS * HEAD_DIM
KV_WIDTH = B_KV_HEADS * HEAD_DIM
GATE_WIDTH = B_HEADS * N_BRANCH
SCALE = HEAD_DIM ** -0.5
LOG2E = math.log2(math.e)

SPAN = DIL_PATTERNS[-1][1] * DIL_BLOCK
NARROW_FROM = 4
MIX_UNROLL = 16
RANK_CHUNK = 16
SUB_TILE = 128
Q_TILE = 256
K_TILE = 512
BIAS_TILE = 256
NOT_SELECTED = -(2.0 ** 100)
M_INIT = -1e30
VMEM_LIMIT = 56 * 1024 * 1024

NT_DIMS = (((1,), (1,)), ((), ()))


def _cparams(*sem):
    return pltpu.CompilerParams(dimension_semantics=sem, vmem_limit_bytes=VMEM_LIMIT)


def _t5_bucket_np(dist):
    max_exact = REL_BUCKETS // 2
    d = np.asarray(dist)
    df = np.maximum(d, 1).astype(np.float32)
    large = max_exact + (np.log(df / np.float32(max_exact)) / np.float32(math.log(REL_MAX_DIST / max_exact))
                         * np.float32(REL_BUCKETS - max_exact)).astype(np.int32)
    large = np.minimum(large, REL_BUCKETS - 1)
    return np.where(d < max_exact, d, large).astype(np.int32)


def _bias_kernel(tab_ref, idx_ref, o_ref):
    inner = o_ref.shape[2]
    idx = idx_ref[0]
    accs = [jnp.full(idx.shape, -jnp.inf, F32) for _ in range(inner)]
    for bucket in range(REL_BUCKETS):
        hit = idx == bucket
        for hi in range(inner):
            head = pl.program_id(0) * inner + hi
            accs[hi] = jnp.where(hit, tab_ref[head * REL_BUCKETS + bucket], accs[hi])
    for hi in range(inner):
        o_ref[0, 0, hi] = accs[hi]


def _bias_tiles(tab, dist, valid, inner):
    h = tab.shape[0]
    t, r, c = dist.shape
    idx = jnp.asarray(np.where(valid, _t5_bucket_np(np.maximum(dist, 0)), -1).astype(np.int32))
    return pl.pallas_call(
        _bias_kernel,
        grid=(h // inner, t),
        in_specs=[pl.BlockSpec(memory_space=pltpu.SMEM),
                  pl.BlockSpec((1, r, c), lambda a, ti: (ti, 0, 0))],
        out_specs=pl.BlockSpec((1, 1, inner, r, c), lambda a, ti: (a, ti, 0, 0, 0)),
        out_shape=jax.ShapeDtypeStruct((h // inner, t, inner, r, c), F32),
        compiler_params=_cparams("parallel", "parallel"),
        name="bias_tiles",
    )(tab.reshape(-1), idx)


def _half_mask(g):
    lane = lax.broadcasted_iota(jnp.int32, (1, LANES), 1)
    return (lane >= HEAD_DIM) if g else (lane < HEAD_DIM)


def _inproj_kernel(x_ref, g_ref, w_ref, *out_refs, widths):
    x = x_ref[...]
    y = x * lax.rsqrt(jnp.mean(x * x, axis=-1, keepdims=True) + RMS_EPS)
    xn = (y * g_ref[...]).astype(BF16)
    merged = jnp.dot(xn, w_ref[:, sum(widths[:NARROW_FROM]):], preferred_element_type=F32)
    start = 0
    for idx, (o_ref, w) in enumerate(zip(out_refs, widths)):
        if idx < NARROW_FROM:
            r = jnp.dot(xn, w_ref[:, start:start + w], preferred_element_type=F32)
        else:
            off = start - sum(widths[:NARROW_FROM])
            r = merged[:, off:off + w]
        if len(o_ref.shape) == 3:
            for p in range(o_ref.shape[0]):
                o_ref[p] = r[:, p * LANES:(p + 1) * LANES].astype(o_ref.dtype)
        else:
            o_ref[...] = r.astype(o_ref.dtype)
        start += w


def _inproj(x2, g, w, widths, dtypes, slot_major, tm):
    n, d = x2.shape
    shape = lambda w_, sm: (w_ // LANES, n, LANES) if sm else (n, w_)
    spec = lambda w_, sm: (pl.BlockSpec((w_ // LANES, tm, LANES), lambda i: (0, i, 0)) if sm
                           else pl.BlockSpec((tm, w_), lambda i: (i, 0)))
    out_shape = [jax.ShapeDtypeStruct(shape(w_, sm), dt) for w_, dt, sm in zip(widths, dtypes, slot_major)]
    return pl.pallas_call(
        functools.partial(_inproj_kernel, widths=widths),
        grid=(n // tm,),
        in_specs=[pl.BlockSpec((tm, d), lambda i: (i, 0)),
                  pl.BlockSpec((1, d), lambda i: (0, 0)),
                  pl.BlockSpec(w.shape, lambda i: (0, 0))],
        out_specs=[spec(w_, sm) for w_, sm in zip(widths, slot_major)],
        out_shape=out_shape,
        compiler_params=_cparams("parallel"),
        name="inproj",
    )(x2, g, w)


def _mixer_a_kernel(q_ref, kp_ref, kc_ref, vp_ref, vc_ref, bias_ref, o_ref, qf, kf, vf, acc_scr, m_scr):
    sb = pl.program_id(2)
    qf[...] = q_ref[0, 0].astype(F32)
    kf[:SPAN] = kp_ref[0, 0].astype(F32)
    kf[SPAN:] = kc_ref[0, 0].astype(F32)
    vf[:SPAN] = vp_ref[0, 0].astype(F32)
    vf[SPAN:] = vc_ref[0, 0].astype(F32)
    halves = [_half_mask(hh) for hh in range(2)]
    col = lax.broadcasted_iota(jnp.int32, (1, 2 * DIL_BLOCK), 1)
    ones = jnp.ones((2 * DIL_BLOCK, LANES), BF16)
    n_blocks = SPAN // DIL_BLOCK
    for p, (_, dil) in enumerate(DIL_PATTERNS):
        step = dil * DIL_BLOCK
        per_class = SPAN // step

        def body(grp, carry, p=p, dil=dil, step=step, per_class=per_class):
            for u in range(MIX_UNROLL):
                idx = grp * MIX_UNROLL + u
                r, n = idx // per_class, idx % per_class
                q0 = pl.multiple_of(n * step, step) + r
                rows_q = pl.ds(q0, DIL_BLOCK, stride=dil)
                rows_k = pl.ds(q0 + SPAN - step, 2 * DIL_BLOCK, stride=dil)
                q2 = qf[rows_q, :].astype(BF16)
                kcat = kf[rows_k, :].astype(BF16)
                vcat = vf[rows_k, :].astype(BF16)
                keep = (col >= DIL_BLOCK) | (sb * per_class + n > 0)
                qm = jnp.concatenate([jnp.where(halves[hh], q2, jnp.zeros_like(q2)) for hh in range(2)], axis=0)
                s2 = lax.dot_general(qm, kcat, NT_DIMS, preferred_element_type=F32)
                for hh in range(2):
                    s = s2[hh * DIL_BLOCK:(hh + 1) * DIL_BLOCK]
                    s = jnp.where(keep, s + bias_ref[p, hh], -jnp.inf)
                    m_blk = jnp.max(s, axis=-1, keepdims=True)
                    pe = jnp.exp2(s - m_blk)
                    pv = jnp.dot(pe.astype(BF16), jnp.where(halves[hh], vcat, ones), preferred_element_type=F32)
                    acc_scr[p, hh, rows_q, :] = pv
                    m_scr[p, hh, rows_q, :] = jnp.broadcast_to(m_blk, pv.shape)
            return carry

        lax.fori_loop(0, n_blocks // MIX_UNROLL, body, 0)

    def finish(c, carry):
        rows = pl.ds(pl.multiple_of(c * DIL_BLOCK, DIL_BLOCK), DIL_BLOCK)
        outs = []
        for hh in range(2):
            ms = [m_scr[p, hh, rows, :] for p in range(len(DIL_PATTERNS))]
            m_all = functools.reduce(jnp.maximum, ms)
            tot = sum(jnp.exp2(m - m_all) * acc_scr[p, hh, rows, :] for p, m in enumerate(ms))
            outs.append(tot / pltpu.roll(tot, HEAD_DIM, axis=1))
        o_ref[0, 0, rows, :] = jnp.where(halves[0], outs[0], outs[1]).astype(o_ref.dtype)
        return carry

    lax.fori_loop(0, n_blocks, finish, 0, unroll=4)


def _mixer_a(q, k, v, bias):
    npair, b, s, _ = q.shape
    blk = (1, 1, SPAN, LANES)
    cur = lambda bi, pi, si: (pi, bi, si, 0)
    prev = lambda bi, pi, si: (pi, bi, jnp.maximum(si - 1, 0), 0)
    return pl.pallas_call(
        _mixer_a_kernel,
        grid=(b, npair, s // SPAN),
        in_specs=[pl.BlockSpec(blk, cur), pl.BlockSpec(blk, prev), pl.BlockSpec(blk, cur),
                  pl.BlockSpec(blk, prev), pl.BlockSpec(blk, cur),
                  pl.BlockSpec((bias.shape[0], 2) + bias.shape[2:], lambda bi, pi, si: (0, pi, 0, 0))],
        out_specs=pl.BlockSpec(blk, cur),
        out_shape=jax.ShapeDtypeStruct(q.shape, BF16),
        scratch_shapes=[pltpu.VMEM((SPAN, LANES), F32), pltpu.VMEM((2 * SPAN, LANES), F32),
                        pltpu.VMEM((2 * SPAN, LANES), F32),
                        pltpu.VMEM((len(DIL_PATTERNS), 2, SPAN, LANES), F32),
                        pltpu.VMEM((len(DIL_PATTERNS), 2, SPAN, LANES), F32)],
        compiler_params=_cparams("parallel", "parallel", "parallel"),
        name="mixer_a",
    )(q, k, k, v, v, bias)


def _compress_kernel(c_ref, pa_ref, pb_ref, w1a_ref, w1b_ref, b1_ref, w2_ref, o_ref):
    nch = o_ref.shape[1]
    c = jnp.concatenate([c_ref[0, pl.ds(l, nch, stride=CMP_STRIDE), :] for l in range(CMP_STRIDE)], axis=1)
    xa = (c + pa_ref[...]).astype(BF16)
    xb = (c + pb_ref[...]).astype(BF16)
    ha = jnp.dot(xa, w1a_ref[...], preferred_element_type=F32)
    hb = jnp.dot(xb, w1b_ref[...], preferred_element_type=F32)
    hb_next = jnp.concatenate([hb[1:], jnp.zeros_like(hb[:1])], axis=0)
    hid = jax.nn.gelu(ha + hb_next + b1_ref[...])
    o_ref[0] = jnp.dot(hid.astype(BF16), w2_ref[...], preferred_element_type=F32).astype(o_ref.dtype)


def _compress(c, pos, w1, b1, w2):
    b, s, _ = c.shape
    nch = s // CMP_STRIDE
    half = CMP_LEN // 2
    zero = jnp.zeros((half, HEAD_DIM, CMP_HIDDEN), F32)

    def grouped(wpart):
        g0 = jnp.concatenate([wpart, zero], axis=1).reshape(half * LANES, CMP_HIDDEN)
        g1 = jnp.concatenate([zero, wpart], axis=1).reshape(half * LANES, CMP_HIDDEN)
        return jnp.concatenate([g0, g1], axis=1).astype(BF16)

    w1a, w1b = grouped(w1[:half]), grouped(w1[half:])
    pa = jnp.tile(pos[:half], (1, 2)).reshape(1, half * LANES)
    pb = jnp.tile(pos[half:], (1, 2)).reshape(1, half * LANES)
    b1g = jnp.tile(b1, 2).reshape(1, 2 * CMP_HIDDEN)
    zw = jnp.zeros_like(w2)
    w2g = jnp.concatenate([jnp.concatenate([w2, zw], axis=1),
                           jnp.concatenate([zw, w2], axis=1)], axis=0).astype(BF16)
    full = lambda a: pl.BlockSpec(a.shape, lambda i: (0,) * a.ndim)
    return pl.pallas_call(
        _compress_kernel,
        grid=(b,),
        in_specs=[pl.BlockSpec((1,) + c.shape[1:], lambda i: (i, 0, 0)),
                  full(pa), full(pb), full(w1a), full(w1b), full(b1g), full(w2g)],
        out_specs=pl.BlockSpec((1, nch, LANES), lambda i: (i, 0, 0)),
        out_shape=jax.ShapeDtypeStruct((b, nch, LANES), BF16),
        compiler_params=_cparams("parallel"),
        name="compress",
    )(c, pa, pb, w1a, w1b, b1g, w2g)


def _nsa_kernel(q_ref, gl_ref, kc_ref, vc_ref, wov_ref, ks_ref, blk_ref, vs_ref, bias_s_ref, kw_ref, vw_ref, bias_w_ref,
                o_ref, acc_ref, m_ref, s_ref, part_ref, sb_ref, *, nsel, n_bias, kw):
    qt = pl.program_id(1)
    tq = q_ref.shape[1]
    n_sub = tq // SUB_TILE
    sub_rows = B_GROUP * SUB_TILE
    gate = jax.nn.sigmoid(gl_ref[0])
    halves = [_half_mask(g) for g in range(B_KV_HEADS)]
    q_slots = [q_ref[0, :, r * LANES:(r + 1) * LANES] for r in range(B_GROUP)]
    zero = jnp.zeros((tq, LANES), BF16)

    def stacked(slots):
        return jnp.concatenate([slots[r][u * SUB_TILE:(u + 1) * SUB_TILE]
                                for u in range(n_sub) for r in range(B_GROUP)], axis=0)

    def gated_slots(o, g, branch, slots):
        for r in range(B_GROUP):
            c = (g * B_GROUP + r) * N_BRANCH + branch
            o_r = jnp.concatenate([o[(u * B_GROUP + r) * SUB_TILE:(u * B_GROUP + r + 1) * SUB_TILE]
                                   for u in range(n_sub)], axis=0) * gate[:, c:c + 1]
            slots[r] = o_r if slots[r] is None else jnp.where(halves[g], o_r, slots[r])

    nck = kc_ref.shape[1]
    row = lax.broadcasted_iota(jnp.int32, (n_sub * sub_rows, 1), 0)
    t_row = qt * tq + (row // sub_rows) * SUB_TILE + row % SUB_TILE
    blk_end = lax.broadcasted_iota(jnp.int32, (1, nck), 1) * CMP_STRIDE + (CMP_LEN - 1)
    valid = blk_end <= t_row
    kc = kc_ref[0]
    vc = vc_ref[0]
    pc = []
    cmp_slots = [None] * B_GROUP
    for g in range(B_KV_HEADS):
        qg = stacked([jnp.where(halves[g], q_slots[r], zero) for r in range(B_GROUP)])
        s = lax.dot_general(qg, kc, NT_DIMS, preferred_element_type=F32)
        s = jnp.where(valid, s, -jnp.inf)
        m = jnp.max(s, axis=-1, keepdims=True)
        m = jnp.where(m == -jnp.inf, 0.0, m)
        p = jnp.exp2(s - m)
        den = jnp.sum(p, axis=-1, keepdims=True)
        p = p / jnp.maximum(den, 1e-30)
        pc.append(jnp.concatenate(
            [sum(p[(u * B_GROUP + r) * SUB_TILE:(u * B_GROUP + r + 1) * SUB_TILE] for r in range(B_GROUP))
             for u in range(n_sub)], axis=0))
        gated_slots(jnp.dot(p.astype(BF16), vc, preferred_element_type=F32), g, 0, cmp_slots)
    for r in range(B_GROUP):
        part_ref[0, :, r * LANES:(r + 1) * LANES] = cmp_slots[r]

    t_lane = qt * tq + lax.broadcasted_iota(jnp.int32, (1, tq), 1)
    cur = t_lane // SEL_BLOCK
    j = lax.broadcasted_iota(jnp.int32, (nsel, 1), 0)
    forced = (j == 0) | (j == cur) | (j == cur - 1)
    imps = []
    for g in reversed(range(B_KV_HEADS)):
        imp = lax.dot_general(wov_ref[...], pc[g], NT_DIMS, preferred_element_type=F32,
                              precision=lax.Precision.HIGHEST)
        imps.append(jnp.where(j > cur, -jnp.inf, jnp.where(forced, jnp.inf, imp)))

    last_block = (qt * tq + tq - 1) // SEL_BLOCK
    for level in range(-(-nsel // RANK_CHUNK)):
        n_live = min((level + 1) * RANK_CHUNK, nsel)

        @pl.when(last_block // RANK_CHUNK == level)
        def _(n_live=n_live):
            blocks = []
            for imp in imps:
                if n_live <= SEL_TOPN:
                    blocks.append(jnp.zeros((HEAD_DIM, tq), F32))
                    continue
                live = imp[:n_live]
                jl = j[:n_live]
                rank = jnp.zeros((n_live, tq), jnp.int32)
                for jp in range(n_live):
                    row = live[jp:jp + 1, :]
                    rank = rank + jnp.where(jl > jp, (row >= live).astype(jnp.int32), (row > live).astype(jnp.int32))
                blocks.append(jnp.where(rank < SEL_TOPN, 0.0, NOT_SELECTED))
                if n_live < HEAD_DIM:
                    blocks.append(jnp.zeros((HEAD_DIM - n_live, tq), F32))
            sb_ref[...] = jnp.concatenate(blocks, axis=0).T.astype(sb_ref.dtype)

    sb = sb_ref[...]
    qaug = [stacked([jnp.where(halves[g], q_slots[r], sb) for r in range(B_GROUP)])
            for g in range(B_KV_HEADS)]
    n_steps = (qt * tq + tq + K_TILE - 1) // K_TILE
    acc_ref[...] = jnp.zeros_like(acc_ref)
    m_ref[...] = jnp.full_like(m_ref, M_INIT)

    def scores(kt):
        for jt in range(K_TILE // BIAS_TILE):
            k0 = pl.multiple_of(kt * K_TILE + jt * BIAS_TILE, BIAS_TILE)
            k2 = ks_ref[0, pl.ds(k0, BIAS_TILE), :]
            e2 = blk_ref[pl.ds(k0, BIAS_TILE), :]
            for g in range(B_KV_HEADS):
                k = jnp.where(halves[g], k2, e2)
                s = lax.dot_general(qaug[g], k, NT_DIMS, preferred_element_type=F32)
                for u in range(n_sub):
                    bi = jnp.clip(qt * n_sub + u - k0 // SUB_TILE + 1, 0, n_bias - 1)
                    rows_u = slice(u * sub_rows, (u + 1) * sub_rows)
                    s_ref[kt % 2, g, rows_u, jt * BIAS_TILE:(jt + 1) * BIAS_TILE] = s[rows_u] + bias_s_ref[g, bi]

    def accumulate(kt):
        k0 = pl.multiple_of(kt * K_TILE, K_TILE)
        v2 = vs_ref[0, pl.ds(k0, K_TILE), :]
        for g in range(B_KV_HEADS):
            v = jnp.where(halves[g], v2, jnp.ones((K_TILE, LANES), BF16))
            s = s_ref[kt % 2, g]
            m_old = m_ref[g]
            m_new = jnp.maximum(m_old, jnp.max(s, axis=-1, keepdims=True))
            alpha = jnp.exp2(m_old - m_new)
            p = jnp.exp2(s - jnp.tile(m_new, (1, K_TILE // LANES)))
            acc_ref[g] = alpha * acc_ref[g] + jnp.dot(p.astype(BF16), v, preferred_element_type=F32)
            m_ref[g] = m_new

    def body(kt, carry):
        accumulate(kt)
        scores(kt + 1)
        return carry

    n_wt = kw // SUB_TILE
    q_win = [stacked([jnp.where(halves[g], q_slots[r], zero) for r in range(B_GROUP)]) for g in range(B_KV_HEADS)]
    o_win = [[] for _ in range(B_KV_HEADS)]
    for u in range(n_sub):
        sub = qt * n_sub + u
        w0 = pl.multiple_of(jnp.maximum(sub * SUB_TILE - WIN_LEN, 0), SUB_TILE)
        skip = jnp.maximum(WIN_LEN // SUB_TILE - sub, 0)
        kwin = kw_ref[0, pl.ds(w0, kw), :]
        vwin = vw_ref[0, pl.ds(w0, kw), :]
        for g in range(B_KV_HEADS):
            s = lax.dot_general(q_win[g][u * sub_rows:(u + 1) * sub_rows], kwin, NT_DIMS, preferred_element_type=F32)
            s = jnp.concatenate([s[:, jw * SUB_TILE:(jw + 1) * SUB_TILE] + bias_w_ref[g, jnp.minimum(jw + skip, n_wt)]
                                 for jw in range(n_wt)], axis=1)
            p = jnp.exp2(s - jnp.max(s, axis=-1, keepdims=True))
            acc = jnp.dot(p.astype(BF16), jnp.where(halves[g], vwin, jnp.ones_like(vwin)),
                          preferred_element_type=F32)
            o_win[g].append(acc / jnp.maximum(pltpu.roll(acc, HEAD_DIM, axis=1), 1e-30))
    win_slots = [None] * B_GROUP
    for g in range(B_KV_HEADS):
        gated_slots(jnp.concatenate(o_win[g], axis=0), g, 2, win_slots)
    for r in range(B_GROUP):
        part_ref[1, :, r * LANES:(r + 1) * LANES] = win_slots[r]

    scores(0)
    lax.fori_loop(0, n_steps - 1, body, 0)
    accumulate(n_steps - 1)
    sel_slots = [None] * B_GROUP
    for g in range(B_KV_HEADS):
        acc = acc_ref[g]
        gated_slots(acc / jnp.maximum(pltpu.roll(acc, HEAD_DIM, axis=1), 1e-30), g, 1, sel_slots)
    for r in range(B_GROUP):
        sl = slice(r * LANES, (r + 1) * LANES)
        o_ref[0, :, sl] = ((part_ref[0, :, sl] + sel_slots[r]) + part_ref[1, :, sl]).astype(o_ref.dtype)


def _nsa(qb, gl, kcmp, vcmp, ksl, vsl, bias_s, kwin, vwin, bias_w):
    b, s, w = qb.shape
    nsel = s // SEL_BLOCK
    nck = kcmp.shape[1]
    ratio = SEL_BLOCK // CMP_STRIDE
    wov = np.zeros((nsel, nck), np.float32)
    for jj in range(nsel):
        for off, wt in zip(range(-1, ratio), CMP_OVERLAP):
            n = ratio * jj + off
            if 0 <= n < nck - 1:
                wov[jj, n] = wt
    wov = jnp.asarray(wov)
    blk = jnp.asarray((np.arange(s)[:, None] // SEL_BLOCK) == (np.arange(LANES)[None, :] % HEAD_DIM), BF16)
    rows = B_GROUP * Q_TILE
    tile = lambda width: pl.BlockSpec((1, Q_TILE, width), lambda bi, qi: (bi, qi, 0))
    whole = lambda a: pl.BlockSpec((1,) + a.shape[1:], lambda bi, qi: (bi, 0, 0))
    const = lambda a: pl.BlockSpec(a.shape, lambda bi, qi: (0,) * a.ndim, pipeline_mode=pl.Buffered(1))
    return pl.pallas_call(
        functools.partial(_nsa_kernel, nsel=nsel, n_bias=bias_s.shape[1], kw=WIN_LEN + SUB_TILE),
        grid=(b, s // Q_TILE),
        in_specs=[tile(w), tile(LANES), whole(kcmp), whole(vcmp), const(wov), whole(ksl), const(blk), whole(vsl),
                  const(bias_s), whole(kwin), whole(vwin), const(bias_w)],
        out_specs=tile(w),
        out_shape=jax.ShapeDtypeStruct((b, s, w), BF16),
        scratch_shapes=[pltpu.VMEM((B_KV_HEADS, rows, LANES), F32), pltpu.VMEM((B_KV_HEADS, rows, LANES), F32),
                        pltpu.VMEM((2, B_KV_HEADS, rows, K_TILE), F32), pltpu.VMEM((2, Q_TILE, w), F32),
                        pltpu.VMEM((Q_TILE, LANES), BF16)],
        compiler_params=_cparams("parallel", "parallel"),
        name="nsa",
    )(qb, gl, kcmp, vcmp, wov, ksl, blk, vsl, bias_s, kwin, vwin, bias_w)


def _tail_kernel(x_ref, oa_ref, ob_ref, wa_ref, wb_ref, g2_ref, wg_ref, wu_ref, wd_ref, gf_ref, o_ref):
    o_a = jnp.concatenate([oa_ref[p] for p in range(oa_ref.shape[0])], axis=1)
    mix = jnp.dot(o_a, wa_ref[...], preferred_element_type=F32)
    mix = mix + jnp.dot(ob_ref[...], wb_ref[...], preferred_element_type=F32)
    h = x_ref[...] + mix
    y = h * lax.rsqrt(jnp.mean(h * h, axis=-1, keepdims=True) + RMS_EPS)
    hn = (y * g2_ref[...]).astype(BF16)
    a = jnp.dot(hn, wg_ref[...], preferred_element_type=F32)
    u = jnp.dot(hn, wu_ref[...], preferred_element_type=F32)
    act = (jax.nn.silu(a) * u).astype(BF16)
    h2 = h + jnp.dot(act, wd_ref[...], preferred_element_type=F32)
    y2 = h2 * lax.rsqrt(jnp.mean(h2 * h2, axis=-1, keepdims=True) + RMS_EPS)
    o_ref[...] = y2 * gf_ref[...]


def _tail(x2, oa, ob, wa, wb, g2, wg, wu, wd, gf, tm):
    n, d = x2.shape
    row = lambda width: pl.BlockSpec((tm, width), lambda i: (i, 0))
    const = lambda a: pl.BlockSpec(a.shape, lambda i: (0, 0), pipeline_mode=pl.Buffered(1))
    return pl.pallas_call(
        _tail_kernel,
        grid=(n // tm,),
        in_specs=[row(d), pl.BlockSpec((oa.shape[0], tm, LANES), lambda i: (0, i, 0)), row(ob.shape[1]),
                  const(wa), const(wb), const(g2), const(wg), const(wu), const(wd), const(gf)],
        out_specs=row(d),
        out_shape=jax.ShapeDtypeStruct((n, d), F32),
        compiler_params=_cparams("parallel"),
        name="tail",
    )(x2, oa, ob, wa, wb, g2, wg, wu, wd, gf)


def _slot_perm():
    perm = np.zeros(B_WIDTH, np.int64)
    for r in range(B_GROUP):
        for g in range(B_KV_HEADS):
            src = (g * B_GROUP + r) * HEAD_DIM
            dst = r * LANES + g * HEAD_DIM
            perm[dst:dst + HEAD_DIM] = np.arange(src, src + HEAD_DIM)
    return perm


def _layer(h, tab_a, tab_b, norm1_g, w_in, cmp_pos, k_w1, k_b1, k_w2, v_w1, v_b1, v_w2,
           w_out, norm2_g, w_gate, w_up, w_down, norm_f_g):
    b, s, d = h.shape
    n = b * s
    assert s % SPAN == 0 and s % K_TILE == 0
    perm = _slot_perm()

    cols = np.cumsum([0, A_WIDTH, A_WIDTH, A_WIDTH, B_WIDTH] + [KV_WIDTH] * 6 + [GATE_WIDTH])
    w_aq = w_in[:, cols[0]:cols[1]] * (SCALE * LOG2E)
    w_bq = (w_in[:, cols[3]:cols[4]] * (SCALE * LOG2E))[:, perm]
    w_gl = jnp.pad(w_in[:, cols[10]:cols[11]], ((0, 0), (0, LANES - GATE_WIDTH)))
    w1 = jnp.concatenate([w_aq, w_in[:, cols[1]:cols[3]], w_bq, w_in[:, cols[4]:cols[10]], w_gl], axis=1).astype(BF16)
    widths = (A_WIDTH,) * 3 + (B_WIDTH,) + (KV_WIDTH,) * 6 + (LANES,)
    dtypes = (BF16,) * 4 + (F32,) * 2 + (BF16,) * 4 + (F32,)
    slot_major = (True,) * 3 + (False,) * 8
    x2 = h.reshape(n, d)
    qa, ka, va, qb, kc, vc, ksl, vsl, kw, vw, gl = _inproj(x2, norm1_g.reshape(1, d), w1, widths, dtypes,
                                                          slot_major, tm=512)
    r3 = lambda t: t.reshape(b, s, t.shape[-1])
    qb, ksl, vsl, kw, vw, gl = map(r3, (qb, ksl, vsl, kw, vw, gl))

    i = np.arange(DIL_BLOCK)[:, None]
    jj = np.arange(2 * DIL_BLOCK)[None, :]
    dist = i + DIL_BLOCK - jj
    tok_dist = np.stack([np.maximum(dist, 0) * dil for _, dil in DIL_PATTERNS])
    in_window = np.stack([(dist >= 0) & (dist <= window // dil) for window, dil in DIL_PATTERNS])
    bias_a = _bias_tiles(tab_a, tok_dist, in_window, inner=A_HEADS)[0]
    r4 = lambda t: t.reshape(A_HEADS // 2, b, s, LANES)
    o_a = _mixer_a(r4(qa), r4(ka), r4(va), bias_a).reshape(A_HEADS // 2, n, LANES)

    nch = s // CMP_STRIDE
    kcmp = _compress(kc.reshape(b, s, KV_WIDTH), cmp_pos, k_w1, k_b1, k_w2)
    vcmp = _compress(vc.reshape(b, s, KV_WIDTH), cmp_pos, v_w1, v_b1, v_w2)
    d_sat = int(np.nonzero(_t5_bucket_np(np.arange(s)) < REL_BUCKETS - 1)[0].max()) + 1
    first_far = -(-(d_sat + BIAS_TILE - 1) // SUB_TILE)
    n_bias = min(s // SUB_TILE, first_far + 1) + 1
    dd = ((np.arange(n_bias)[:, None, None] - 1) * SUB_TILE + np.arange(SUB_TILE)[None, :, None]
          - np.arange(BIAS_TILE)[None, None, :])
    bias_sl = _bias_tiles(tab_b, dd, dd >= 0, inner=B_GROUP)
    bias_sl = bias_sl.reshape(B_KV_HEADS, n_bias, B_GROUP * SUB_TILE, BIAS_TILE)
    n_wt = WIN_LEN // SUB_TILE + 1
    dw = ((n_wt - 1 - np.arange(n_wt + 1))[:, None, None] * SUB_TILE + np.arange(SUB_TILE)[None, :, None]
          - np.arange(SUB_TILE)[None, None, :])
    in_win = (dw >= 0) & (dw < WIN_LEN) & (np.arange(n_wt + 1) < n_wt)[:, None, None]
    bias_w = _bias_tiles(tab_b, dw, in_win, inner=B_GROUP)
    bias_w = bias_w.reshape(B_KV_HEADS, n_wt + 1, B_GROUP * SUB_TILE, SUB_TILE)
    o_b = _nsa(qb, gl, kcmp, vcmp, ksl, vsl, bias_sl, kw, vw, bias_w)

    wa = w_out[:A_WIDTH].astype(BF16)
    wb = w_out[A_WIDTH:][perm].astype(BF16)
    out = _tail(x2, o_a, o_b.reshape(n, B_WIDTH), wa, wb, norm2_g.reshape(1, d), w_gate.astype(BF16),
                w_up.astype(BF16), w_down.astype(BF16), norm_f_g.reshape(1, d), tm=512)
    return out.reshape(b, s, d)


def kernel(x, norm1_g, w_in, rel_bias, cmp_pos, cmp_k_w1, cmp_k_b1, cmp_k_w2, cmp_v_w1, cmp_v_b1, cmp_v_w2,
           w_out, norm2_g, w_gate, w_up, w_down, norm_f_g):
    assert w_in.shape[0] == 1, "single-layer model"
    tab_a = rel_bias[:, :A_HEADS].T * LOG2E
    tab_b = rel_bias[:, A_HEADS:].T * LOG2E
    return _layer(x, tab_a, tab_b, norm1_g[0], w_in[0], cmp_pos[0], cmp_k_w1[0], cmp_k_b1[0], cmp_k_w2[0],
                  cmp_v_w1[0], cmp_v_b1[0], cmp_v_w2[0], w_out[0], norm2_g[0], w_gate[0], w_up[0], w_down[0],
                  norm_f_g)
```

```python
import functools
import math

import numpy as np
import jax
import jax.numpy as jnp
from jax import lax
from jax.experimental import pallas as pl
from jax.experimental.pallas import tpu as pltpu

F32 = jnp.float32
BF16 = jnp.bfloat16

HEAD_DIM = 64
LANES = 128
A_HEADS = 8
DIL_PATTERNS = ((128, 1), (512, 4), (2048, 16))
DIL_BLOCK = 128
B_HEADS = 8
B_KV_HEADS = 2
B_GROUP = B_HEADS // B_KV_HEADS
CMP_LEN = 32
CMP_STRIDE = 16
CMP_HIDDEN = 128
SEL_BLOCK = 64
SEL_TOPN = 16
CMP_OVERLAP = (1.0, 2.0, 2.0, 2.0, 1.0)
WIN_LEN = 512
N_BRANCH = 3
REL_BUCKETS = 32
REL_MAX_DIST = 2048
RMS_EPS = 1e-6
A_WIDTH = A_HEADS * HEAD_DIM
B_WIDTH = B_HEADS * HEAD_DIM
KV_WIDTH = B_KV_HEADS * HEAD_DIM
GATE_WIDTH = B_HEADS * N_BRANCH
SCALE = HEAD_DIM ** -0.5
LOG2E = math.log2(math.e)

SPAN = DIL_PATTERNS[-1][1] * DIL_BLOCK
NARROW_FROM = 4
RANK_CHUNK = 16
SUB_TILE = 128
Q_TILE = 256
K_TILE = 512
BIAS_TILE = 256
NOT_SELECTED = -(2.0 ** 100)
M_INIT = -1e30
VMEM_LIMIT = 56 * 1024 * 1024

NT_DIMS = (((1,), (1,)), ((), ()))


def _cparams(*sem):
    return pltpu.CompilerParams(dimension_semantics=sem, vmem_limit_bytes=VMEM_LIMIT)


def _t5_bucket_np(dist):
    max_exact = REL_BUCKETS // 2
    d = np.asarray(dist)
    df = np.maximum(d, 1).astype(np.float32)
    large = max_exact + (np.log(df / np.float32(max_exact)) / np.float32(math.log(REL_MAX_DIST / max_exact))
                         * np.float32(REL_BUCKETS - max_exact)).astype(np.int32)
    large = np.minimum(large, REL_BUCKETS - 1)
    return np.where(d < max_exact, d, large).astype(np.int32)


def _bias_kernel(tab_ref, idx_ref, o_ref):
    inner = o_ref.shape[2]
    idx = idx_ref[0]
    accs = [jnp.full(idx.shape, -jnp.inf, F32) for _ in range(inner)]
    for bucket in range(REL_BUCKETS):
        hit = idx == bucket
        for hi in range(inner):
            head = pl.program_id(0) * inner + hi
            accs[hi] = jnp.where(hit, tab_ref[head * REL_BUCKETS + bucket], accs[hi])
    for hi in range(inner):
        o_ref[0, 0, hi] = accs[hi]


def _bias_tiles(tab, dist, valid, inner):
    h = tab.shape[0]
    t, r, c = dist.shape
    idx = jnp.asarray(np.where(valid, _t5_bucket_np(np.maximum(dist, 0)), -1).astype(np.int32))
    return pl.pallas_call(
        _bias_kernel,
        grid=(h // inner, t),
        in_specs=[pl.BlockSpec(memory_space=pltpu.SMEM),
                  pl.BlockSpec((1, r, c), lambda a, ti: (ti, 0, 0))],
        out_specs=pl.BlockSpec((1, 1, inner, r, c), lambda a, ti: (a, ti, 0, 0, 0)),
        out_shape=jax.ShapeDtypeStruct((h // inner, t, inner, r, c), F32),
        compiler_params=_cparams("parallel", "parallel"),
        name="bias_tiles",
    )(tab.reshape(-1), idx)


def _half_mask(g):
    lane = lax.broadcasted_iota(jnp.int32, (1, LANES), 1)
    return (lane >= HEAD_DIM) if g else (lane < HEAD_DIM)


def _inproj_kernel(x_ref, g_ref, w_ref, *refs, widths, n_mix):
    n_dil = len(DIL_PATTERNS)
    n_scr = n_mix * (n_dil - 1)
    mix_refs = [refs[i * n_dil:(i + 1) * n_dil] for i in range(n_mix)]
    out_refs = refs[n_mix * n_dil:len(refs) - n_scr]
    scratch = [refs[len(refs) - n_scr + i * (n_dil - 1):len(refs) - n_scr + (i + 1) * (n_dil - 1)]
               for i in range(n_mix)]
    x = x_ref[...]
    tm = x.shape[0]
    y = x * lax.rsqrt(jnp.mean(x * x, axis=-1, keepdims=True) + RMS_EPS)
    xn = (y * g_ref[...]).astype(BF16)
    merged = jnp.dot(xn, w_ref[:, sum(widths[:NARROW_FROM]):], preferred_element_type=F32)
    start = 0
    for idx, w in enumerate(widths):
        if idx < NARROW_FROM:
            r = jnp.dot(xn, w_ref[:, start:start + w], preferred_element_type=F32)
        else:
            off = start - sum(widths[:NARROW_FROM])
            r = merged[:, off:off + w]
        if idx < n_mix:
            prev_dil = None
            for pi, (o_ref, (_, dil)) in enumerate(zip(mix_refs[idx], DIL_PATTERNS)):
                keep = scratch[idx][pi] if pi < n_dil - 1 else None
                for p in range(w // LANES):
                    if prev_dil is None:
                        planes = [(0, r[:, p * LANES:(p + 1) * LANES])]
                    else:
                        f = dil // prev_dil
                        planes = [(res + prev_dil * a,
                                   scratch[idx][pi - 1][p, res, pl.ds(a, tm // dil, stride=f), :])
                                  for res in range(prev_dil) for a in range(f)]
                    for res, rows in planes:
                        o_ref[p, 0, res] = rows.astype(o_ref.dtype)
                        if keep is not None:
                            keep[p, res] = rows
                prev_dil = dil
        else:
            o_ref = out_refs[idx - n_mix]
            o_ref[...] = r.astype(o_ref.dtype)
        start += w


def _inproj(x3, g, w, widths, dtypes, n_mix, tm):
    b, s, d = x3.shape
    n = b * s
    per_batch = s // tm
    out_shape, out_specs = [], []
    for w_, dt in zip(widths[:n_mix], dtypes[:n_mix]):
        for _, dil in DIL_PATTERNS:
            out_shape.append(jax.ShapeDtypeStruct((w_ // LANES, b, dil, s // dil, LANES), dt))
            out_specs.append(pl.BlockSpec((w_ // LANES, 1, dil, tm // dil, LANES),
                                          lambda i: (0, i // per_batch, 0, i % per_batch, 0)))
    for w_, dt in zip(widths[n_mix:], dtypes[n_mix:]):
        out_shape.append(jax.ShapeDtypeStruct((n, w_), dt))
        out_specs.append(pl.BlockSpec((tm, w_), lambda i: (i, 0)))
    return pl.pallas_call(
        functools.partial(_inproj_kernel, widths=widths, n_mix=n_mix),
        grid=(n // tm,),
        in_specs=[pl.BlockSpec((tm, d), lambda i: (i, 0)),
                  pl.BlockSpec((1, d), lambda i: (0, 0)),
                  pl.BlockSpec(w.shape, lambda i: (0, 0))],
        out_specs=out_specs,
        out_shape=out_shape,
        scratch_shapes=[pltpu.VMEM((w_ // LANES, dil, tm // dil, LANES), F32)
                        for w_ in widths[:n_mix] for _, dil in DIL_PATTERNS[:-1]],
        compiler_params=_cparams("parallel"),
        name="inproj",
    )(x3.reshape(n, d), g, w)


def _mixer_a_kernel(*refs):
    n_pat = len(DIL_PATTERNS)
    bias_ref, o_ref, acc_scr, m_scr = refs[5 * n_pat:]
    sb = pl.program_id(2)
    halves = [_half_mask(hh) for hh in range(2)]
    col = lax.broadcasted_iota(jnp.int32, (1, 2 * DIL_BLOCK), 1)
    first_keep = (col >= DIL_BLOCK) | (sb > 0)
    ones = jnp.ones((2 * DIL_BLOCK, LANES), BF16)
    n_blocks = SPAN // DIL_BLOCK
    for p, (_, dil) in enumerate(DIL_PATTERNS):
        q_ref, kp_ref, kc_ref, vp_ref, vc_ref = refs[5 * p:5 * p + 5]
        length = SPAN // dil
        for r in range(dil):
            for n in range(length // DIL_BLOCK):
                cur = slice(n * DIL_BLOCK, (n + 1) * DIL_BLOCK)
                prev = slice((n - 1) * DIL_BLOCK, n * DIL_BLOCK)
                q2 = q_ref[0, 0, r, cur, :]
                if n == 0:
                    k_prev = kp_ref[0, 0, r, length - DIL_BLOCK:, :]
                    v_prev = vp_ref[0, 0, r, length - DIL_BLOCK:, :]
                else:
                    k_prev, v_prev = kc_ref[0, 0, r, prev, :], vc_ref[0, 0, r, prev, :]
                kcat = jnp.concatenate([k_prev, kc_ref[0, 0, r, cur, :]], axis=0)
                vcat = jnp.concatenate([v_prev, vc_ref[0, 0, r, cur, :]], axis=0)
                rows_q = pl.ds(r + n * dil * DIL_BLOCK, DIL_BLOCK, stride=dil)
                qm = jnp.concatenate([jnp.where(halves[hh], q2, jnp.zeros_like(q2)) for hh in range(2)], axis=0)
                s2 = lax.dot_general(qm, kcat, NT_DIMS, preferred_element_type=F32)
                for hh in range(2):
                    s = s2[hh * DIL_BLOCK:(hh + 1) * DIL_BLOCK] + bias_ref[p, hh]
                    if n == 0:
                        s = jnp.where(first_keep, s, -jnp.inf)
                    m_blk = jnp.max(s, axis=-1, keepdims=True)
                    pe = jnp.exp2(s - m_blk)
                    pv = jnp.dot(pe.astype(BF16), jnp.where(halves[hh], vcat, ones), preferred_element_type=F32)
                    acc_scr[p, hh, rows_q, :] = pv
                    m_scr[p, hh, rows_q, :] = jnp.broadcast_to(m_blk, pv.shape)

    def finish(c, carry):
        rows = pl.ds(pl.multiple_of(c * DIL_BLOCK, DIL_BLOCK), DIL_BLOCK)
        outs = []
        for hh in range(2):
            ms = [m_scr[p, hh, rows, :] for p in range(len(DIL_PATTERNS))]
            m_all = functools.reduce(jnp.maximum, ms)
            tot = sum(jnp.exp2(m - m_all) * acc_scr[p, hh, rows, :] for p, m in enumerate(ms))
            outs.append(tot / pltpu.roll(tot, HEAD_DIM, axis=1))
        o_ref[0, 0, rows, :] = jnp.where(halves[0], outs[0], outs[1]).astype(o_ref.dtype)
        return carry

    lax.fori_loop(0, n_blocks, finish, 0, unroll=4)


def _mixer_a(qs, ks, vs, bias):
    npair, b, _, s, _ = qs[0].shape
    cur = lambda bi, pi, si: (pi, bi, 0, si, 0)
    prev = lambda bi, pi, si: (pi, bi, 0, jnp.maximum(si - 1, 0), 0)
    in_specs, operands = [], []
    for (_, dil), q, k, v in zip(DIL_PATTERNS, qs, ks, vs):
        blk = (1, 1, dil, SPAN // dil, LANES)
        in_specs += [pl.BlockSpec(blk, cur), pl.BlockSpec(blk, prev), pl.BlockSpec(blk, cur),
                     pl.BlockSpec(blk, prev), pl.BlockSpec(blk, cur)]
        operands += [q, k, k, v, v]
    in_specs.append(pl.BlockSpec((bias.shape[0], 2) + bias.shape[2:], lambda bi, pi, si: (0, pi, 0, 0)))
    return pl.pallas_call(
        _mixer_a_kernel,
        grid=(b, npair, s // SPAN),
        in_specs=in_specs,
        out_specs=pl.BlockSpec((1, 1, SPAN, LANES), lambda bi, pi, si: (pi, bi, si, 0)),
        out_shape=jax.ShapeDtypeStruct((npair, b, s, LANES), BF16),
        scratch_shapes=[pltpu.VMEM((len(DIL_PATTERNS), 2, SPAN, LANES), F32),
                        pltpu.VMEM((len(DIL_PATTERNS), 2, SPAN, LANES), F32)],
        compiler_params=_cparams("parallel", "parallel", "parallel"),
        name="mixer_a",
    )(*operands, bias)


def _compress_kernel(c_ref, pa_ref, pb_ref, w1a_ref, w1b_ref, b1_ref, w2_ref, o_ref):
    nch = o_ref.shape[1]
    c = jnp.concatenate([c_ref[0, pl.ds(l, nch, stride=CMP_STRIDE), :] for l in range(CMP_STRIDE)], axis=1)
    xa = (c + pa_ref[...]).astype(BF16)
    xb = (c + pb_ref[...]).astype(BF16)
    ha = jnp.dot(xa, w1a_ref[...], preferred_element_type=F32)
    hb = jnp.dot(xb, w1b_ref[...], preferred_element_type=F32)
    hb_next = jnp.concatenate([hb[1:], jnp.zeros_like(hb[:1])], axis=0)
    hid = jax.nn.gelu(ha + hb_next + b1_ref[...])
    o_ref[0] = jnp.dot(hid.astype(BF16), w2_ref[...], preferred_element_type=F32).astype(o_ref.dtype)


def _compress(c, pos, w1, b1, w2):
    b, s, _ = c.shape
    nch = s // CMP_STRIDE
    half = CMP_LEN // 2
    zero = jnp.zeros((half, HEAD_DIM, CMP_HIDDEN), F32)

    def grouped(wpart):
        g0 = jnp.concatenate([wpart, zero], axis=1).reshape(half * LANES, CMP_HIDDEN)
        g1 = jnp.concatenate([zero, wpart], axis=1).reshape(half * LANES, CMP_HIDDEN)
        return jnp.concatenate([g0, g1], axis=1).astype(BF16)

    w1a, w1b = grouped(w1[:half]), grouped(w1[half:])
    pa = jnp.tile(pos[:half], (1, 2)).reshape(1, half * LANES)
    pb = jnp.tile(pos[half:], (1, 2)).reshape(1, half * LANES)
    b1g = jnp.tile(b1, 2).reshape(1, 2 * CMP_HIDDEN)
    zw = jnp.zeros_like(w2)
    w2g = jnp.concatenate([jnp.concatenate([w2, zw], axis=1),
                           jnp.concatenate([zw, w2], axis=1)], axis=0).astype(BF16)
    full = lambda a: pl.BlockSpec(a.shape, lambda i: (0,) * a.ndim)
    return pl.pallas_call(
        _compress_kernel,
        grid=(b,),
        in_specs=[pl.BlockSpec((1,) + c.shape[1:], lambda i: (i, 0, 0)),
                  full(pa), full(pb), full(w1a), full(w1b), full(b1g), full(w2g)],
        out_specs=pl.BlockSpec((1, nch, LANES), lambda i: (i, 0, 0)),
        out_shape=jax.ShapeDtypeStruct((b, nch, LANES), BF16),
        compiler_params=_cparams("parallel"),
        name="compress",
    )(c, pa, pb, w1a, w1b, b1g, w2g)


def _nsa_kernel(q_ref, gl_ref, kc_ref, vc_ref, wov_ref, ks_ref, blk_ref, vs_ref, bias_s_ref, kw_ref, vw_ref, bias_w_ref,
                o_ref, acc_ref, m_ref, s_ref, part_ref, sb_ref, *, nsel, n_bias, kw):
    qt = pl.program_id(1)
    tq = q_ref.shape[1]
    n_sub = tq // SUB_TILE
    sub_rows = B_GROUP * SUB_TILE
    gate = jax.nn.sigmoid(gl_ref[0])
    halves = [_half_mask(g) for g in range(B_KV_HEADS)]
    q_slots = [q_ref[0, :, r * LANES:(r + 1) * LANES] for r in range(B_GROUP)]
    zero = jnp.zeros((tq, LANES), BF16)

    def stacked(slots):
        return jnp.concatenate([slots[r][u * SUB_TILE:(u + 1) * SUB_TILE]
                                for u in range(n_sub) for r in range(B_GROUP)], axis=0)

    def gated_slots(o, g, branch, slots):
        for r in range(B_GROUP):
            c = (g * B_GROUP + r) * N_BRANCH + branch
            o_r = jnp.concatenate([o[(u * B_GROUP + r) * SUB_TILE:(u * B_GROUP + r + 1) * SUB_TILE]
                                   for u in range(n_sub)], axis=0) * gate[:, c:c + 1]
            slots[r] = o_r if slots[r] is None else jnp.where(halves[g], o_r, slots[r])

    nck = kc_ref.shape[1]
    row = lax.broadcasted_iota(jnp.int32, (n_sub * sub_rows, 1), 0)
    t_row = qt * tq + (row // sub_rows) * SUB_TILE + row % SUB_TILE
    blk_end = lax.broadcasted_iota(jnp.int32, (1, nck), 1) * CMP_STRIDE + (CMP_LEN - 1)
    valid = blk_end <= t_row
    kc = kc_ref[0]
    vc = vc_ref[0]
    pc = []
    cmp_slots = [None] * B_GROUP
    for g in range(B_KV_HEADS):
        qg = stacked([jnp.where(halves[g], q_slots[r], zero) for r in range(B_GROUP)])
        s = lax.dot_general(qg, kc, NT_DIMS, preferred_element_type=F32)
        s = jnp.where(valid, s, -jnp.inf)
        m = jnp.max(s, axis=-1, keepdims=True)
        m = jnp.where(m == -jnp.inf, 0.0, m)
        p = jnp.exp2(s - m)
        den = jnp.sum(p, axis=-1, keepdims=True)
        p = p / jnp.maximum(den, 1e-30)
        pc.append(jnp.concatenate(
            [sum(p[(u * B_GROUP + r) * SUB_TILE:(u * B_GROUP + r + 1) * SUB_TILE] for r in range(B_GROUP))
             for u in range(n_sub)], axis=0))
        gated_slots(jnp.dot(p.astype(BF16), vc, preferred_element_type=F32), g, 0, cmp_slots)
    for r in range(B_GROUP):
        part_ref[0, :, r * LANES:(r + 1) * LANES] = cmp_slots[r]

    t_lane = qt * tq + lax.broadcasted_iota(jnp.int32, (1, tq), 1)
    cur = t_lane // SEL_BLOCK
    j = lax.broadcasted_iota(jnp.int32, (nsel, 1), 0)
    forced = (j == 0) | (j == cur) | (j == cur - 1)
    imps = []
    for g in reversed(range(B_KV_HEADS)):
        imp = lax.dot_general(wov_ref[...], pc[g], NT_DIMS, preferred_element_type=F32,
                              precision=lax.Precision.HIGHEST)
        imps.append(jnp.where(j > cur, -jnp.inf, jnp.where(forced, jnp.inf, imp)))

    last_block = (qt * tq + tq - 1) // SEL_BLOCK
    for level in range(-(-nsel // RANK_CHUNK)):
        n_live = min((level + 1) * RANK_CHUNK, nsel)

        @pl.when(last_block // RANK_CHUNK == level)
        def _(n_live=n_live):
            blocks = []
            for imp in imps:
                if n_live <= SEL_TOPN:
                    blocks.append(jnp.zeros((HEAD_DIM, tq), F32))
                    continue
                live = imp[:n_live]
                jl = j[:n_live]
                rank = jnp.zeros((n_live, tq), jnp.int32)
                for jp in range(n_live):
                    row = live[jp:jp + 1, :]
                    rank = rank + jnp.where(jl > jp, (row >= live).astype(jnp.int32), (row > live).astype(jnp.int32))
                blocks.append(jnp.where(rank < SEL_TOPN, 0.0, NOT_SELECTED))
                if n_live < HEAD_DIM:
                    blocks.append(jnp.zeros((HEAD_DIM - n_live, tq), F32))
            sb_ref[...] = jnp.concatenate(blocks, axis=0).T.astype(sb_ref.dtype)

    sb = sb_ref[...]
    qaug = [stacked([jnp.where(halves[g], q_slots[r], sb) for r in range(B_GROUP)])
            for g in range(B_KV_HEADS)]
    n_steps = (qt * tq + tq + K_TILE - 1) // K_TILE
    acc_ref[...] = jnp.zeros_like(acc_ref)
    m_ref[...] = jnp.full_like(m_ref, M_INIT)

    def scores(kt):
        for jt in range(K_TILE // BIAS_TILE):
            k0 = pl.multiple_of(kt * K_TILE + jt * BIAS_TILE, BIAS_TILE)
            k2 = ks_ref[0, pl.ds(k0, BIAS_TILE), :]
            e2 = blk_ref[pl.ds(k0, BIAS_TILE), :]
            for g in range(B_KV_HEADS):
                k = jnp.where(halves[g], k2, e2)
                s = lax.dot_general(qaug[g], k, NT_DIMS, preferred_element_type=F32)
                for u in range(n_sub):
                    bi = jnp.clip(qt * n_sub + u - k0 // SUB_TILE + 1, 0, n_bias - 1)
                    rows_u = slice(u * sub_rows, (u + 1) * sub_rows)
                    s_ref[kt % 2, g, rows_u, jt * BIAS_TILE:(jt + 1) * BIAS_TILE] = s[rows_u] + bias_s_ref[g, bi]

    def accumulate(kt):
        k0 = pl.multiple_of(kt * K_TILE, K_TILE)
        v2 = vs_ref[0, pl.ds(k0, K_TILE), :]
        for g in range(B_KV_HEADS):
            v = jnp.where(halves[g], v2, jnp.ones((K_TILE, LANES), BF16))
            s = s_ref[kt % 2, g]
            m_old = m_ref[g]
            m_new = jnp.maximum(m_old, jnp.max(s, axis=-1, keepdims=True))
            alpha = jnp.exp2(m_old - m_new)
            p = jnp.exp2(s - jnp.tile(m_new, (1, K_TILE // LANES)))
            acc_ref[g] = alpha * acc_ref[g] + jnp.dot(p.astype(BF16), v, preferred_element_type=F32)
            m_ref[g] = m_new

    def body(kt, carry):
        accumulate(kt)
        scores(kt + 1)
        return carry

    n_wt = kw // SUB_TILE
    q_win = [stacked([jnp.where(halves[g], q_slots[r], zero) for r in range(B_GROUP)]) for g in range(B_KV_HEADS)]
    o_win = [[] for _ in range(B_KV_HEADS)]
    for u in range(n_sub):
        sub = qt * n_sub + u
        w0 = pl.multiple_of(jnp.maximum(sub * SUB_TILE - WIN_LEN, 0), SUB_TILE)
        skip = jnp.maximum(WIN_LEN // SUB_TILE - sub, 0)
        kwin = kw_ref[0, pl.ds(w0, kw), :]
        vwin = vw_ref[0, pl.ds(w0, kw), :]
        for g in range(B_KV_HEADS):
            s = lax.dot_general(q_win[g][u * sub_rows:(u + 1) * sub_rows], kwin, NT_DIMS, preferred_element_type=F32)
            s = jnp.concatenate([s[:, jw * SUB_TILE:(jw + 1) * SUB_TILE] + bias_w_ref[g, jnp.minimum(jw + skip, n_wt)]
                                 for jw in range(n_wt)], axis=1)
            p = jnp.exp2(s - jnp.max(s, axis=-1, keepdims=True))
            acc = jnp.dot(p.astype(BF16), jnp.where(halves[g], vwin, jnp.ones_like(vwin)),
                          preferred_element_type=F32)
            o_win[g].append(acc / jnp.maximum(pltpu.roll(acc, HEAD_DIM, axis=1), 1e-30))
    win_slots = [None] * B_GROUP
    for g in range(B_KV_HEADS):
        gated_slots(jnp.concatenate(o_win[g], axis=0), g, 2, win_slots)
    for r in range(B_GROUP):
        part_ref[1, :, r * LANES:(r + 1) * LANES] = win_slots[r]

    scores(0)
    lax.fori_loop(0, n_steps - 1, body, 0)
    accumulate(n_steps - 1)
    sel_slots = [None] * B_GROUP
    for g in range(B_KV_HEADS):
        acc = acc_ref[g]
        gated_slots(acc / jnp.maximum(pltpu.roll(acc, HEAD_DIM, axis=1), 1e-30), g, 1, sel_slots)
    for r in range(B_GROUP):
        sl = slice(r * LANES, (r + 1) * LANES)
        o_ref[0, :, sl] = ((part_ref[0, :, sl] + sel_slots[r]) + part_ref[1, :, sl]).astype(o_ref.dtype)


def _nsa(qb, gl, kcmp, vcmp, ksl, vsl, bias_s, kwin, vwin, bias_w):
    b, s, w = qb.shape
    nsel = s // SEL_BLOCK
    nck = kcmp.shape[1]
    ratio = SEL_BLOCK // CMP_STRIDE
    wov = np.zeros((nsel, nck), np.float32)
    for jj in range(nsel):
        for off, wt in zip(range(-1, ratio), CMP_OVERLAP):
            n = ratio * jj + off
            if 0 <= n < nck - 1:
                wov[jj, n] = wt
    wov = jnp.asarray(wov)
    blk = jnp.asarray((np.arange(s)[:, None] // SEL_BLOCK) == (np.arange(LANES)[None, :] % HEAD_DIM), BF16)
    rows = B_GROUP * Q_TILE
    tile = lambda width: pl.BlockSpec((1, Q_TILE, width), lambda bi, qi: (bi, qi, 0))
    whole = lambda a: pl.BlockSpec((1,) + a.shape[1:], lambda bi, qi: (bi, 0, 0))
    const = lambda a: pl.BlockSpec(a.shape, lambda bi, qi: (0,) * a.ndim, pipeline_mode=pl.Buffered(1))
    return pl.pallas_call(
        functools.partial(_nsa_kernel, nsel=nsel, n_bias=bias_s.shape[1], kw=WIN_LEN + SUB_TILE),
        grid=(b, s // Q_TILE),
        in_specs=[tile(w), tile(LANES), whole(kcmp), whole(vcmp), const(wov), whole(ksl), const(blk), whole(vsl),
                  const(bias_s), whole(kwin), whole(vwin), const(bias_w)],
        out_specs=tile(w),
        out_shape=jax.ShapeDtypeStruct((b, s, w), BF16),
        scratch_shapes=[pltpu.VMEM((B_KV_HEADS, rows, LANES), F32), pltpu.VMEM((B_KV_HEADS, rows, LANES), F32),
                        pltpu.VMEM((2, B_KV_HEADS, rows, K_TILE), F32), pltpu.VMEM((2, Q_TILE, w), F32),
                        pltpu.VMEM((Q_TILE, LANES), BF16)],
        compiler_params=_cparams("parallel", "parallel"),
        name="nsa",
    )(qb, gl, kcmp, vcmp, wov, ksl, blk, vsl, bias_s, kwin, vwin, bias_w)


def _tail_kernel(x_ref, oa_ref, ob_ref, wa_ref, wb_ref, g2_ref, wg_ref, wu_ref, wd_ref, gf_ref, o_ref):
    o_a = jnp.concatenate([oa_ref[p] for p in range(oa_ref.shape[0])], axis=1)
    mix = jnp.dot(o_a, wa_ref[...], preferred_element_type=F32)
    mix = mix + jnp.dot(ob_ref[...], wb_ref[...], preferred_element_type=F32)
    h = x_ref[...] + mix
    y = h * lax.rsqrt(jnp.mean(h * h, axis=-1, keepdims=True) + RMS_EPS)
    hn = (y * g2_ref[...]).astype(BF16)
    a = jnp.dot(hn, wg_ref[...], preferred_element_type=F32)
    u = jnp.dot(hn, wu_ref[...], preferred_element_type=F32)
    act = (jax.nn.silu(a) * u).astype(BF16)
    h2 = h + jnp.dot(act, wd_ref[...], preferred_element_type=F32)
    y2 = h2 * lax.rsqrt(jnp.mean(h2 * h2, axis=-1, keepdims=True) + RMS_EPS)
    o_ref[...] = y2 * gf_ref[...]


def _tail(x2, oa, ob, wa, wb, g2, wg, wu, wd, gf, tm):
    n, d = x2.shape
    row = lambda width: pl.BlockSpec((tm, width), lambda i: (i, 0))
    const = lambda a: pl.BlockSpec(a.shape, lambda i: (0, 0), pipeline_mode=pl.Buffered(1))
    return pl.pallas_call(
        _tail_kernel,
        grid=(n // tm,),
        in_specs=[row(d), pl.BlockSpec((oa.shape[0], tm, LANES), lambda i: (0, i, 0)), row(ob.shape[1]),
                  const(wa), const(wb), const(g2), const(wg), const(wu), const(wd), const(gf)],
        out_specs=row(d),
        out_shape=jax.ShapeDtypeStruct((n, d), F32),
        compiler_params=_cparams("parallel"),
        name="tail",
    )(x2, oa, ob, wa, wb, g2, wg, wu, wd, gf)


def _slot_perm():
    perm = np.zeros(B_WIDTH, np.int64)
    for r in range(B_GROUP):
        for g in range(B_KV_HEADS):
            src = (g * B_GROUP + r) * HEAD_DIM
            dst = r * LANES + g * HEAD_DIM
            perm[dst:dst + HEAD_DIM] = np.arange(src, src + HEAD_DIM)
    return perm


def _layer(h, tab_a, tab_b, norm1_g, w_in, cmp_pos, k_w1, k_b1, k_w2, v_w1, v_b1, v_w2,
           w_out, norm2_g, w_gate, w_up, w_down, norm_f_g):
    b, s, d = h.shape
    n = b * s
    assert s % SPAN == 0 and s % K_TILE == 0
    perm = _slot_perm()

    cols = np.cumsum([0, A_WIDTH, A_WIDTH, A_WIDTH, B_WIDTH] + [KV_WIDTH] * 6 + [GATE_WIDTH])
    w_aq = w_in[:, cols[0]:cols[1]] * (SCALE * LOG2E)
    w_bq = (w_in[:, cols[3]:cols[4]] * (SCALE * LOG2E))[:, perm]
    w_gl = jnp.pad(w_in[:, cols[10]:cols[11]], ((0, 0), (0, LANES - GATE_WIDTH)))
    w1 = jnp.concatenate([w_aq, w_in[:, cols[1]:cols[3]], w_bq, w_in[:, cols[4]:cols[10]], w_gl], axis=1).astype(BF16)
    widths = (A_WIDTH,) * 3 + (B_WIDTH,) + (KV_WIDTH,) * 6 + (LANES,)
    dtypes = (BF16,) * 4 + (F32,) * 2 + (BF16,) * 4 + (F32,)
    x2 = h.reshape(n, d)
    n_pat = len(DIL_PATTERNS)
    outs = _inproj(h, norm1_g.reshape(1, d), w1, widths, dtypes, n_mix=3, tm=512)
    qa, ka, va = (outs[i * n_pat:(i + 1) * n_pat] for i in range(3))
    qb, kc, vc, ksl, vsl, kw, vw, gl = outs[3 * n_pat:]
    r3 = lambda t: t.reshape(b, s, t.shape[-1])
    qb, ksl, vsl, kw, vw, gl = map(r3, (qb, ksl, vsl, kw, vw, gl))

    i = np.arange(DIL_BLOCK)[:, None]
    jj = np.arange(2 * DIL_BLOCK)[None, :]
    dist = i + DIL_BLOCK - jj
    tok_dist = np.stack([np.maximum(dist, 0) * dil for _, dil in DIL_PATTERNS])
    in_window = np.stack([(dist >= 0) & (dist <= window // dil) for window, dil in DIL_PATTERNS])
    bias_a = _bias_tiles(tab_a, tok_dist, in_window, inner=A_HEADS)[0]
    o_a = _mixer_a(qa, ka, va, bias_a).reshape(A_HEADS // 2, n, LANES)

    nch = s // CMP_STRIDE
    kcmp = _compress(kc.reshape(b, s, KV_WIDTH), cmp_pos, k_w1, k_b1, k_w2)
    vcmp = _compress(vc.reshape(b, s, KV_WIDTH), cmp_pos, v_w1, v_b1, v_w2)
    d_sat = int(np.nonzero(_t5_bucket_np(np.arange(s)) < REL_BUCKETS - 1)[0].max()) + 1
    first_far = -(-(d_sat + BIAS_TILE - 1) // SUB_TILE)
    n_bias = min(s // SUB_TILE, first_far + 1) + 1
    dd = ((np.arange(n_bias)[:, None, None] - 1) * SUB_TILE + np.arange(SUB_TILE)[None, :, None]
          - np.arange(BIAS_TILE)[None, None, :])
    bias_sl = _bias_tiles(tab_b, dd, dd >= 0, inner=B_GROUP)
    bias_sl = bias_sl.reshape(B_KV_HEADS, n_bias, B_GROUP * SUB_TILE, BIAS_TILE)
    n_wt = WIN_LEN // SUB_TILE + 1
    dw = ((n_wt - 1 - np.arange(n_wt + 1))[:, None, None] * SUB_TILE + np.arange(SUB_TILE)[None, :, None]
          - np.arange(SUB_TILE)[None, None, :])
    in_win = (dw >= 0) & (dw < WIN_LEN) & (np.arange(n_wt + 1) < n_wt)[:, None, None]
    bias_w = _bias_tiles(tab_b, dw, in_win, inner=B_GROUP)
    bias_w = bias_w.reshape(B_KV_HEADS, n_wt + 1, B_GROUP * SUB_TILE, SUB_TILE)
    o_b = _nsa(qb, gl, kcmp, vcmp, ksl, vsl, bias_sl, kw, vw, bias_w)

    wa = w_out[:A_WIDTH].astype(BF16)
    wb = w_out[A_WIDTH:][perm].astype(BF16)
    out = _tail(x2, o_a, o_b.reshape(n, B_WIDTH), wa, wb, norm2_g.reshape(1, d), w_gate.astype(BF16),
                w_up.astype(BF16), w_down.astype(BF16), norm_f_g.reshape(1, d), tm=512)
    return out.reshape(b, s, d)


def kernel(x, norm1_g, w_in, rel_bias, cmp_pos, cmp_k_w1, cmp_k_b1, cmp_k_w2, cmp_v_w1, cmp_v_b1, cmp_v_w2,
           w_out, norm2_g, w_gate, w_up, w_down, norm_f_g):
    assert w_in.shape[0] == 1, "single-layer model"
    tab_a = rel_bias[:, :A_HEADS].T * LOG2E
    tab_b = rel_bias[:, A_HEADS:].T * LOG2E
    return _layer(x, tab_a, tab_b, norm1_g[0], w_in[0], cmp_pos[0], cmp_k_w1[0], cmp_k_b1[0], cmp_k_w2[0],
                  cmp_v_w1[0], cmp_v_b1[0], cmp_v_w2[0], w_out[0], norm2_g[0], w_gate[0], w_up[0], w_down[0],
                  norm_f_g)
```

```python
import functools
import math

import numpy as np
import jax
import jax.numpy as jnp
from jax import lax
from jax.experimental import pallas as pl
from jax.experimental.pallas import tpu as pltpu

F32 = jnp.float32
BF16 = jnp.bfloat16

HEAD_DIM = 64
LANES = 128
A_HEADS = 8
DIL_PATTERNS = ((128, 1), (512, 4), (2048, 16))
DIL_BLOCK = 128
B_HEADS = 8
B_KV_HEADS = 2
B_GROUP = B_HEADS // B_KV_HEADS
CMP_LEN = 32
CMP_STRIDE = 16
CMP_HIDDEN = 128
SEL_BLOCK = 64
SEL_TOPN = 16
CMP_OVERLAP = (1.0, 2.0, 2.0, 2.0, 1.0)
WIN_LEN = 512
N_BRANCH = 3
REL_BUCKETS = 32
REL_MAX_DIST = 2048
RMS_EPS = 1e-6
A_WIDTH = A_HEADS * HEAD_DIM
B_WIDTH = B_HEADS * HEAD_DIM
KV_WIDTH = B_KV_HEADS * HEAD_DIM
GATE_WIDTH = B_HEADS * N_BRANCH
SCALE = HEAD_DIM ** -0.5
LOG2E = math.log2(math.e)

SPAN = DIL_PATTERNS[-1][1] * DIL_BLOCK
NARROW_FROM = 4
RANK_CHUNK = 16
SUB_TILE = 128
Q_TILE = 512
K_TILE = 512
SCORE_TILE = 256
NOT_SELECTED = -(2.0 ** 100)
M_INIT = -1e30
VMEM_LIMIT = 56 * 1024 * 1024

NT_DIMS = (((1,), (1,)), ((), ()))


def _cparams(*sem):
    return pltpu.CompilerParams(dimension_semantics=sem, vmem_limit_bytes=VMEM_LIMIT)


def _t5_bucket_np(dist):
    max_exact = REL_BUCKETS // 2
    d = np.asarray(dist)
    df = np.maximum(d, 1).astype(np.float32)
    large = max_exact + (np.log(df / np.float32(max_exact)) / np.float32(math.log(REL_MAX_DIST / max_exact))
                         * np.float32(REL_BUCKETS - max_exact)).astype(np.int32)
    large = np.minimum(large, REL_BUCKETS - 1)
    return np.where(d < max_exact, d, large).astype(np.int32)


def _bias_kernel(tab_ref, idx_ref, o_ref):
    inner = o_ref.shape[2]
    idx = idx_ref[0]
    accs = [jnp.full(idx.shape, -jnp.inf, F32) for _ in range(inner)]
    for bucket in range(REL_BUCKETS):
        hit = idx == bucket
        for hi in range(inner):
            head = pl.program_id(0) * inner + hi
            accs[hi] = jnp.where(hit, tab_ref[head * REL_BUCKETS + bucket], accs[hi])
    for hi in range(inner):
        o_ref[0, 0, hi] = accs[hi]


def _bias_tiles(tab, dist, valid, inner):
    h = tab.shape[0]
    t, r, c = dist.shape
    idx = jnp.asarray(np.where(valid, _t5_bucket_np(np.maximum(dist, 0)), -1).astype(np.int32))
    return pl.pallas_call(
        _bias_kernel,
        grid=(h // inner, t),
        in_specs=[pl.BlockSpec(memory_space=pltpu.SMEM),
                  pl.BlockSpec((1, r, c), lambda a, ti: (ti, 0, 0))],
        out_specs=pl.BlockSpec((1, 1, inner, r, c), lambda a, ti: (a, ti, 0, 0, 0)),
        out_shape=jax.ShapeDtypeStruct((h // inner, t, inner, r, c), F32),
        compiler_params=_cparams("parallel", "parallel"),
        name="bias_tiles",
    )(tab.reshape(-1), idx)


def _half_mask(g):
    lane = lax.broadcasted_iota(jnp.int32, (1, LANES), 1)
    return (lane >= HEAD_DIM) if g else (lane < HEAD_DIM)


def _inproj_kernel(x_ref, g_ref, w_ref, *refs, widths, n_mix):
    n_dil = len(DIL_PATTERNS)
    n_scr = n_mix * (n_dil - 1)
    mix_refs = [refs[i * n_dil:(i + 1) * n_dil] for i in range(n_mix)]
    out_refs = refs[n_mix * n_dil:len(refs) - n_scr]
    scratch = [refs[len(refs) - n_scr + i * (n_dil - 1):len(refs) - n_scr + (i + 1) * (n_dil - 1)]
               for i in range(n_mix)]
    x = x_ref[...]
    tm = x.shape[0]
    y = x * lax.rsqrt(jnp.mean(x * x, axis=-1, keepdims=True) + RMS_EPS)
    xn = (y * g_ref[...]).astype(BF16)
    merged = jnp.dot(xn, w_ref[:, sum(widths[:NARROW_FROM]):], preferred_element_type=F32)
    start = 0
    for idx, w in enumerate(widths):
        if idx < NARROW_FROM:
            r = jnp.dot(xn, w_ref[:, start:start + w], preferred_element_type=F32)
        else:
            off = start - sum(widths[:NARROW_FROM])
            r = merged[:, off:off + w]
        if idx < n_mix:
            prev_dil = None
            for pi, (o_ref, (_, dil)) in enumerate(zip(mix_refs[idx], DIL_PATTERNS)):
                keep = scratch[idx][pi] if pi < n_dil - 1 else None
                for p in range(w // LANES):
                    if prev_dil is None:
                        planes = [(0, r[:, p * LANES:(p + 1) * LANES])]
                    else:
                        f = dil // prev_dil
                        planes = [(res + prev_dil * a,
                                   scratch[idx][pi - 1][p, res, pl.ds(a, tm // dil, stride=f), :])
                                  for res in range(prev_dil) for a in range(f)]
                    for res, rows in planes:
                        o_ref[p, 0, res] = rows.astype(o_ref.dtype)
                        if keep is not None:
                            keep[p, res] = rows
                prev_dil = dil
        else:
            o_ref = out_refs[idx - n_mix]
            o_ref[...] = r.astype(o_ref.dtype)
        start += w


def _inproj(x3, g, w, widths, dtypes, n_mix, tm):
    b, s, d = x3.shape
    n = b * s
    per_batch = s // tm
    out_shape, out_specs = [], []
    for w_, dt in zip(widths[:n_mix], dtypes[:n_mix]):
        for _, dil in DIL_PATTERNS:
            out_shape.append(jax.ShapeDtypeStruct((w_ // LANES, b, dil, s // dil, LANES), dt))
            out_specs.append(pl.BlockSpec((w_ // LANES, 1, dil, tm // dil, LANES),
                                          lambda i: (0, i // per_batch, 0, i % per_batch, 0)))
    for w_, dt in zip(widths[n_mix:], dtypes[n_mix:]):
        out_shape.append(jax.ShapeDtypeStruct((n, w_), dt))
        out_specs.append(pl.BlockSpec((tm, w_), lambda i: (i, 0)))
    return pl.pallas_call(
        functools.partial(_inproj_kernel, widths=widths, n_mix=n_mix),
        grid=(n // tm,),
        in_specs=[pl.BlockSpec((tm, d), lambda i: (i, 0)),
                  pl.BlockSpec((1, d), lambda i: (0, 0)),
                  pl.BlockSpec(w.shape, lambda i: (0, 0))],
        out_specs=out_specs,
        out_shape=out_shape,
        scratch_shapes=[pltpu.VMEM((w_ // LANES, dil, tm // dil, LANES), F32)
                        for w_ in widths[:n_mix] for _, dil in DIL_PATTERNS[:-1]],
        compiler_params=_cparams("parallel"),
        name="inproj",
    )(x3.reshape(n, d), g, w)


def _mixer_a_kernel(*refs):
    n_pat = len(DIL_PATTERNS)
    bias_ref, o_ref, acc_scr, m_scr = refs[5 * n_pat:]
    sb = pl.program_id(2)
    halves = [_half_mask(hh) for hh in range(2)]
    col = lax.broadcasted_iota(jnp.int32, (1, 2 * DIL_BLOCK), 1)
    first_keep = (col >= DIL_BLOCK) | (sb > 0)
    ones = jnp.ones((2 * DIL_BLOCK, LANES), BF16)
    n_blocks = SPAN // DIL_BLOCK
    for p, (_, dil) in enumerate(DIL_PATTERNS):
        q_ref, kp_ref, kc_ref, vp_ref, vc_ref = refs[5 * p:5 * p + 5]
        length = SPAN // dil
        for r in range(dil):
            for n in range(length // DIL_BLOCK):
                cur = slice(n * DIL_BLOCK, (n + 1) * DIL_BLOCK)
                prev = slice((n - 1) * DIL_BLOCK, n * DIL_BLOCK)
                q2 = q_ref[0, 0, r, cur, :]
                if n == 0:
                    k_prev = kp_ref[0, 0, r, length - DIL_BLOCK:, :]
                    v_prev = vp_ref[0, 0, r, length - DIL_BLOCK:, :]
                else:
                    k_prev, v_prev = kc_ref[0, 0, r, prev, :], vc_ref[0, 0, r, prev, :]
                kcat = jnp.concatenate([k_prev, kc_ref[0, 0, r, cur, :]], axis=0)
                vcat = jnp.concatenate([v_prev, vc_ref[0, 0, r, cur, :]], axis=0)
                rows_q = pl.ds(r + n * dil * DIL_BLOCK, DIL_BLOCK, stride=dil)
                qm = jnp.concatenate([jnp.where(halves[hh], q2, jnp.zeros_like(q2)) for hh in range(2)], axis=0)
                s2 = lax.dot_general(qm, kcat, NT_DIMS, preferred_element_type=F32)
                for hh in range(2):
                    s = s2[hh * DIL_BLOCK:(hh + 1) * DIL_BLOCK] + bias_ref[p, hh]
                    if n == 0:
                        s = jnp.where(first_keep, s, -jnp.inf)
                    m_blk = jnp.max(s, axis=-1, keepdims=True)
                    pe = jnp.exp2(s - m_blk)
                    pv = jnp.dot(pe.astype(BF16), jnp.where(halves[hh], vcat, ones), preferred_element_type=F32)
                    acc_scr[p, hh, rows_q, :] = pv
                    m_scr[p, hh, rows_q, :] = jnp.broadcast_to(m_blk, pv.shape)

    def finish(c, carry):
        rows = pl.ds(pl.multiple_of(c * DIL_BLOCK, DIL_BLOCK), DIL_BLOCK)
        outs = []
        for hh in range(2):
            ms = [m_scr[p, hh, rows, :] for p in range(len(DIL_PATTERNS))]
            m_all = functools.reduce(jnp.maximum, ms)
            tot = sum(jnp.exp2(m - m_all) * acc_scr[p, hh, rows, :] for p, m in enumerate(ms))
            outs.append(tot / pltpu.roll(tot, HEAD_DIM, axis=1))
        o_ref[0, 0, rows, :] = jnp.where(halves[0], outs[0], outs[1]).astype(o_ref.dtype)
        return carry

    lax.fori_loop(0, n_blocks, finish, 0, unroll=4)


def _mixer_a(qs, ks, vs, bias):
    npair, b, _, s, _ = qs[0].shape
    cur = lambda bi, pi, si: (pi, bi, 0, si, 0)
    prev = lambda bi, pi, si: (pi, bi, 0, jnp.maximum(si - 1, 0), 0)
    in_specs, operands = [], []
    for (_, dil), q, k, v in zip(DIL_PATTERNS, qs, ks, vs):
        blk = (1, 1, dil, SPAN // dil, LANES)
        in_specs += [pl.BlockSpec(blk, cur), pl.BlockSpec(blk, prev), pl.BlockSpec(blk, cur),
                     pl.BlockSpec(blk, prev), pl.BlockSpec(blk, cur)]
        operands += [q, k, k, v, v]
    in_specs.append(pl.BlockSpec((bias.shape[0], 2) + bias.shape[2:], lambda bi, pi, si: (0, pi, 0, 0)))
    return pl.pallas_call(
        _mixer_a_kernel,
        grid=(b, npair, s // SPAN),
        in_specs=in_specs,
        out_specs=pl.BlockSpec((1, 1, SPAN, LANES), lambda bi, pi, si: (pi, bi, si, 0)),
        out_shape=jax.ShapeDtypeStruct((npair, b, s, LANES), BF16),
        scratch_shapes=[pltpu.VMEM((len(DIL_PATTERNS), 2, SPAN, LANES), F32),
                        pltpu.VMEM((len(DIL_PATTERNS), 2, SPAN, LANES), F32)],
        compiler_params=_cparams("parallel", "parallel", "parallel"),
        name="mixer_a",
    )(*operands, bias)


def _compress_kernel(c_ref, pa_ref, pb_ref, w1a_ref, w1b_ref, b1_ref, w2_ref, o_ref):
    nch = o_ref.shape[1]
    c = jnp.concatenate([c_ref[0, pl.ds(l, nch, stride=CMP_STRIDE), :] for l in range(CMP_STRIDE)], axis=1)
    xa = (c + pa_ref[...]).astype(BF16)
    xb = (c + pb_ref[...]).astype(BF16)
    ha = jnp.dot(xa, w1a_ref[...], preferred_element_type=F32)
    hb = jnp.dot(xb, w1b_ref[...], preferred_element_type=F32)
    hb_next = jnp.concatenate([hb[1:], jnp.zeros_like(hb[:1])], axis=0)
    hid = jax.nn.gelu(ha + hb_next + b1_ref[...])
    o_ref[0] = jnp.dot(hid.astype(BF16), w2_ref[...], preferred_element_type=F32).astype(o_ref.dtype)


def _compress(c, pos, w1, b1, w2):
    b, s, _ = c.shape
    nch = s // CMP_STRIDE
    half = CMP_LEN // 2
    zero = jnp.zeros((half, HEAD_DIM, CMP_HIDDEN), F32)

    def grouped(wpart):
        g0 = jnp.concatenate([wpart, zero], axis=1).reshape(half * LANES, CMP_HIDDEN)
        g1 = jnp.concatenate([zero, wpart], axis=1).reshape(half * LANES, CMP_HIDDEN)
        return jnp.concatenate([g0, g1], axis=1).astype(BF16)

    w1a, w1b = grouped(w1[:half]), grouped(w1[half:])
    pa = jnp.tile(pos[:half], (1, 2)).reshape(1, half * LANES)
    pb = jnp.tile(pos[half:], (1, 2)).reshape(1, half * LANES)
    b1g = jnp.tile(b1, 2).reshape(1, 2 * CMP_HIDDEN)
    zw = jnp.zeros_like(w2)
    w2g = jnp.concatenate([jnp.concatenate([w2, zw], axis=1),
                           jnp.concatenate([zw, w2], axis=1)], axis=0).astype(BF16)
    full = lambda a: pl.BlockSpec(a.shape, lambda i: (0,) * a.ndim)
    return pl.pallas_call(
        _compress_kernel,
        grid=(b,),
        in_specs=[pl.BlockSpec((1,) + c.shape[1:], lambda i: (i, 0, 0)),
                  full(pa), full(pb), full(w1a), full(w1b), full(b1g), full(w2g)],
        out_specs=pl.BlockSpec((1, nch, LANES), lambda i: (i, 0, 0)),
        out_shape=jax.ShapeDtypeStruct((b, nch, LANES), BF16),
        compiler_params=_cparams("parallel"),
        name="compress",
    )(c, pa, pb, w1a, w1b, b1g, w2g)


def _nsa_kernel(q_ref, gl_ref, kc_ref, vc_ref, wov_ref, ks_ref, blk_ref, vs_ref, bias_s_ref, kw_ref, vw_ref, bias_w_ref,
                o_ref, acc_ref, m_ref, s_ref, part_ref, sb_ref, *, nsel, n_bias, kw):
    qt = pl.program_id(1)
    tq = q_ref.shape[1]
    n_sub = tq // SUB_TILE
    sub_rows = B_GROUP * SUB_TILE
    gate = jax.nn.sigmoid(gl_ref[0])
    halves = [_half_mask(g) for g in range(B_KV_HEADS)]
    q_slots = [q_ref[0, :, r * LANES:(r + 1) * LANES] for r in range(B_GROUP)]
    zero = jnp.zeros((tq, LANES), BF16)

    def stacked(slots):
        return jnp.concatenate([slots[r][u * SUB_TILE:(u + 1) * SUB_TILE]
                                for u in range(n_sub) for r in range(B_GROUP)], axis=0)

    def gated_slots(o, g, branch, slots):
        for r in range(B_GROUP):
            c = (g * B_GROUP + r) * N_BRANCH + branch
            o_r = jnp.concatenate([o[(u * B_GROUP + r) * SUB_TILE:(u * B_GROUP + r + 1) * SUB_TILE]
                                   for u in range(n_sub)], axis=0) * gate[:, c:c + 1]
            slots[r] = o_r if slots[r] is None else jnp.where(halves[g], o_r, slots[r])

    nck = kc_ref.shape[1]
    row = lax.broadcasted_iota(jnp.int32, (n_sub * sub_rows, 1), 0)
    t_row = qt * tq + (row // sub_rows) * SUB_TILE + row % SUB_TILE
    blk_end = lax.broadcasted_iota(jnp.int32, (1, nck), 1) * CMP_STRIDE + (CMP_LEN - 1)
    valid = blk_end <= t_row
    kc = kc_ref[0]
    vc = vc_ref[0]
    pc = []
    cmp_slots = [None] * B_GROUP
    for g in range(B_KV_HEADS):
        qg = stacked([jnp.where(halves[g], q_slots[r], zero) for r in range(B_GROUP)])
        s = lax.dot_general(qg, kc, NT_DIMS, preferred_element_type=F32)
        s = jnp.where(valid, s, -jnp.inf)
        m = jnp.max(s, axis=-1, keepdims=True)
        m = jnp.where(m == -jnp.inf, 0.0, m)
        p = jnp.exp2(s - m)
        den = jnp.sum(p, axis=-1, keepdims=True)
        p = p / jnp.maximum(den, 1e-30)
        pc.append(jnp.concatenate(
            [sum(p[(u * B_GROUP + r) * SUB_TILE:(u * B_GROUP + r + 1) * SUB_TILE] for r in range(B_GROUP))
             for u in range(n_sub)], axis=0))
        gated_slots(jnp.dot(p.astype(BF16), vc, preferred_element_type=F32), g, 0, cmp_slots)
    for r in range(B_GROUP):
        part_ref[0, :, r * LANES:(r + 1) * LANES] = cmp_slots[r]

    t_lane = qt * tq + lax.broadcasted_iota(jnp.int32, (1, tq), 1)
    cur = t_lane // SEL_BLOCK
    j = lax.broadcasted_iota(jnp.int32, (nsel, 1), 0)
    forced = (j == 0) | (j == cur) | (j == cur - 1)
    imps = []
    for g in reversed(range(B_KV_HEADS)):
        imp = lax.dot_general(wov_ref[...], pc[g], NT_DIMS, preferred_element_type=F32,
                              precision=lax.Precision.HIGHEST)
        imps.append(jnp.where(j > cur, -jnp.inf, jnp.where(forced, jnp.inf, imp)))

    last_block = (qt * tq + tq - 1) // SEL_BLOCK
    for level in range(-(-nsel // RANK_CHUNK)):
        n_live = min((level + 1) * RANK_CHUNK, nsel)

        @pl.when(last_block // RANK_CHUNK == level)
        def _(n_live=n_live):
            blocks = []
            for imp in imps:
                if n_live <= SEL_TOPN:
                    blocks.append(jnp.zeros((HEAD_DIM, tq), F32))
                    continue
                live = imp[:n_live]
                jl = j[:n_live]
                rank = jnp.zeros((n_live, tq), jnp.int32)
                for jp in range(n_live):
                    row = live[jp:jp + 1, :]
                    rank = rank + jnp.where(jl > jp, (row >= live).astype(jnp.int32), (row > live).astype(jnp.int32))
                blocks.append(jnp.where(rank < SEL_TOPN, 0.0, NOT_SELECTED))
                if n_live < HEAD_DIM:
                    blocks.append(jnp.zeros((HEAD_DIM - n_live, tq), F32))
            sb_ref[...] = jnp.concatenate(blocks, axis=0).T.astype(sb_ref.dtype)

    sb = sb_ref[...]
    qaug = [stacked([jnp.where(halves[g], q_slots[r], sb) for r in range(B_GROUP)])
            for g in range(B_KV_HEADS)]
    n_steps = (qt * tq + tq + K_TILE - 1) // K_TILE
    acc_ref[...] = jnp.zeros_like(acc_ref)
    m_ref[...] = jnp.full_like(m_ref, M_INIT)

    def scores(kt):
        for jt in range(K_TILE // SCORE_TILE):
            k0 = pl.multiple_of(kt * K_TILE + jt * SCORE_TILE, SCORE_TILE)
            k2 = ks_ref[0, pl.ds(k0, SCORE_TILE), :]
            e2 = blk_ref[pl.ds(k0, SCORE_TILE), :]
            for g in range(B_KV_HEADS):
                k = jnp.where(halves[g], k2, e2)
                s = lax.dot_general(qaug[g], k, NT_DIMS, preferred_element_type=F32)
                for u in range(n_sub):
                    rows_u = slice(u * sub_rows, (u + 1) * sub_rows)
                    for c in range(SCORE_TILE // SUB_TILE):
                        bi = jnp.clip(qt * n_sub + u - k0 // SUB_TILE - c + 1, 0, n_bias - 1)
                        col = jt * SCORE_TILE + c * SUB_TILE
                        s_ref[kt % 2, g, rows_u, col:col + SUB_TILE] = (
                            s[rows_u, c * SUB_TILE:(c + 1) * SUB_TILE] + bias_s_ref[g, bi])

    def accumulate(kt):
        k0 = pl.multiple_of(kt * K_TILE, K_TILE)
        v2 = vs_ref[0, pl.ds(k0, K_TILE), :]
        for g in range(B_KV_HEADS):
            v = jnp.where(halves[g], v2, jnp.ones((K_TILE, LANES), BF16))
            s = s_ref[kt % 2, g]
            m_old = m_ref[g]
            m_new = jnp.maximum(m_old, jnp.max(s, axis=-1, keepdims=True))
            alpha = jnp.exp2(m_old - m_new)
            p = jnp.exp2(s - jnp.tile(m_new, (1, K_TILE // LANES)))
            acc_ref[g] = alpha * acc_ref[g] + jnp.dot(p.astype(BF16), v, preferred_element_type=F32)
            m_ref[g] = m_new

    def body(kt, carry):
        accumulate(kt)
        scores(kt + 1)
        return carry

    n_wt = kw // SUB_TILE
    q_win = [stacked([jnp.where(halves[g], q_slots[r], zero) for r in range(B_GROUP)]) for g in range(B_KV_HEADS)]
    o_win = [[] for _ in range(B_KV_HEADS)]
    for u in range(n_sub):
        sub = qt * n_sub + u
        w0 = pl.multiple_of(jnp.maximum(sub * SUB_TILE - WIN_LEN, 0), SUB_TILE)
        skip = jnp.maximum(WIN_LEN // SUB_TILE - sub, 0)
        kwin = kw_ref[0, pl.ds(w0, kw), :]
        vwin = vw_ref[0, pl.ds(w0, kw), :]
        for g in range(B_KV_HEADS):
            s = lax.dot_general(q_win[g][u * sub_rows:(u + 1) * sub_rows], kwin, NT_DIMS, preferred_element_type=F32)
            s = jnp.concatenate([s[:, jw * SUB_TILE:(jw + 1) * SUB_TILE] + bias_w_ref[g, jnp.minimum(jw + skip, n_wt)]
                                 for jw in range(n_wt)], axis=1)
            p = jnp.exp2(s - jnp.max(s, axis=-1, keepdims=True))
            acc = jnp.dot(p.astype(BF16), jnp.where(halves[g], vwin, jnp.ones_like(vwin)),
                          preferred_element_type=F32)
            o_win[g].append(acc / jnp.maximum(pltpu.roll(acc, HEAD_DIM, axis=1), 1e-30))
    win_slots = [None] * B_GROUP
    for g in range(B_KV_HEADS):
        gated_slots(jnp.concatenate(o_win[g], axis=0), g, 2, win_slots)
    for r in range(B_GROUP):
        part_ref[1, :, r * LANES:(r + 1) * LANES] = win_slots[r]

    scores(0)
    lax.fori_loop(0, n_steps - 1, body, 0)
    accumulate(n_steps - 1)
    sel_slots = [None] * B_GROUP
    for g in range(B_KV_HEADS):
        acc = acc_ref[g]
        gated_slots(acc / jnp.maximum(pltpu.roll(acc, HEAD_DIM, axis=1), 1e-30), g, 1, sel_slots)
    for r in range(B_GROUP):
        sl = slice(r * LANES, (r + 1) * LANES)
        o_ref[0, :, sl] = ((part_ref[0, :, sl] + sel_slots[r]) + part_ref[1, :, sl]).astype(o_ref.dtype)


def _nsa(qb, gl, kcmp, vcmp, ksl, vsl, bias_s, kwin, vwin, bias_w):
    b, s, w = qb.shape
    nsel = s // SEL_BLOCK
    nck = kcmp.shape[1]
    ratio = SEL_BLOCK // CMP_STRIDE
    wov = np.zeros((nsel, nck), np.float32)
    for jj in range(nsel):
        for off, wt in zip(range(-1, ratio), CMP_OVERLAP):
            n = ratio * jj + off
            if 0 <= n < nck - 1:
                wov[jj, n] = wt
    wov = jnp.asarray(wov)
    blk = jnp.asarray((np.arange(s)[:, None] // SEL_BLOCK) == (np.arange(LANES)[None, :] % HEAD_DIM), BF16)
    rows = B_GROUP * Q_TILE
    tile = lambda width: pl.BlockSpec((1, Q_TILE, width), lambda bi, qi: (bi, qi, 0))
    whole = lambda a: pl.BlockSpec((1,) + a.shape[1:], lambda bi, qi: (bi, 0, 0))
    const = lambda a: pl.BlockSpec(a.shape, lambda bi, qi: (0,) * a.ndim, pipeline_mode=pl.Buffered(1))
    return pl.pallas_call(
        functools.partial(_nsa_kernel, nsel=nsel, n_bias=bias_s.shape[1], kw=WIN_LEN + SUB_TILE),
        grid=(b, s // Q_TILE),
        in_specs=[tile(w), tile(LANES), whole(kcmp), whole(vcmp), const(wov), whole(ksl), const(blk), whole(vsl),
                  const(bias_s), whole(kwin), whole(vwin), const(bias_w)],
        out_specs=tile(w),
        out_shape=jax.ShapeDtypeStruct((b, s, w), BF16),
        scratch_shapes=[pltpu.VMEM((B_KV_HEADS, rows, LANES), F32), pltpu.VMEM((B_KV_HEADS, rows, LANES), F32),
                        pltpu.VMEM((2, B_KV_HEADS, rows, K_TILE), F32), pltpu.VMEM((2, Q_TILE, w), F32),
                        pltpu.VMEM((Q_TILE, LANES), BF16)],
        compiler_params=_cparams("parallel", "parallel"),
        name="nsa",
    )(qb, gl, kcmp, vcmp, wov, ksl, blk, vsl, bias_s, kwin, vwin, bias_w)


def _tail_kernel(x_ref, oa_ref, ob_ref, wa_ref, wb_ref, g2_ref, wg_ref, wu_ref, wd_ref, gf_ref, o_ref):
    o_a = jnp.concatenate([oa_ref[p] for p in range(oa_ref.shape[0])], axis=1)
    mix = jnp.dot(o_a, wa_ref[...], preferred_element_type=F32)
    mix = mix + jnp.dot(ob_ref[...], wb_ref[...], preferred_element_type=F32)
    h = x_ref[...] + mix
    y = h * lax.rsqrt(jnp.mean(h * h, axis=-1, keepdims=True) + RMS_EPS)
    hn = (y * g2_ref[...]).astype(BF16)
    a = jnp.dot(hn, wg_ref[...], preferred_element_type=F32)
    u = jnp.dot(hn, wu_ref[...], preferred_element_type=F32)
    act = (jax.nn.silu(a) * u).astype(BF16)
    h2 = h + jnp.dot(act, wd_ref[...], preferred_element_type=F32)
    y2 = h2 * lax.rsqrt(jnp.mean(h2 * h2, axis=-1, keepdims=True) + RMS_EPS)
    o_ref[...] = y2 * gf_ref[...]


def _tail(x2, oa, ob, wa, wb, g2, wg, wu, wd, gf, tm):
    n, d = x2.shape
    row = lambda width: pl.BlockSpec((tm, width), lambda i: (i, 0))
    const = lambda a: pl.BlockSpec(a.shape, lambda i: (0, 0), pipeline_mode=pl.Buffered(1))
    return pl.pallas_call(
        _tail_kernel,
        grid=(n // tm,),
        in_specs=[row(d), pl.BlockSpec((oa.shape[0], tm, LANES), lambda i: (0, i, 0)), row(ob.shape[1]),
                  const(wa), const(wb), const(g2), const(wg), const(wu), const(wd), const(gf)],
        out_specs=row(d),
        out_shape=jax.ShapeDtypeStruct((n, d), F32),
        compiler_params=_cparams("parallel"),
        name="tail",
    )(x2, oa, ob, wa, wb, g2, wg, wu, wd, gf)


def _slot_perm():
    perm = np.zeros(B_WIDTH, np.int64)
    for r in range(B_GROUP):
        for g in range(B_KV_HEADS):
            src = (g * B_GROUP + r) * HEAD_DIM
            dst = r * LANES + g * HEAD_DIM
            perm[dst:dst + HEAD_DIM] = np.arange(src, src + HEAD_DIM)
    return perm


def _layer(h, tab_a, tab_b, norm1_g, w_in, cmp_pos, k_w1, k_b1, k_w2, v_w1, v_b1, v_w2,
           w_out, norm2_g, w_gate, w_up, w_down, norm_f_g):
    b, s, d = h.shape
    n = b * s
    assert s % SPAN == 0 and s % K_TILE == 0
    perm = _slot_perm()

    cols = np.cumsum([0, A_WIDTH, A_WIDTH, A_WIDTH, B_WIDTH] + [KV_WIDTH] * 6 + [GATE_WIDTH])
    w_aq = w_in[:, cols[0]:cols[1]] * (SCALE * LOG2E)
    w_bq = (w_in[:, cols[3]:cols[4]] * (SCALE * LOG2E))[:, perm]
    w_gl = jnp.pad(w_in[:, cols[10]:cols[11]], ((0, 0), (0, LANES - GATE_WIDTH)))
    w1 = jnp.concatenate([w_aq, w_in[:, cols[1]:cols[3]], w_bq, w_in[:, cols[4]:cols[10]], w_gl], axis=1).astype(BF16)
    widths = (A_WIDTH,) * 3 + (B_WIDTH,) + (KV_WIDTH,) * 6 + (LANES,)
    dtypes = (BF16,) * 4 + (F32,) * 2 + (BF16,) * 4 + (F32,)
    x2 = h.reshape(n, d)
    n_pat = len(DIL_PATTERNS)
    outs = _inproj(h, norm1_g.reshape(1, d), w1, widths, dtypes, n_mix=3, tm=512)
    qa, ka, va = (outs[i * n_pat:(i + 1) * n_pat] for i in range(3))
    qb, kc, vc, ksl, vsl, kw, vw, gl = outs[3 * n_pat:]
    r3 = lambda t: t.reshape(b, s, t.shape[-1])
    qb, ksl, vsl, kw, vw, gl = map(r3, (qb, ksl, vsl, kw, vw, gl))

    i = np.arange(DIL_BLOCK)[:, None]
    jj = np.arange(2 * DIL_BLOCK)[None, :]
    dist = i + DIL_BLOCK - jj
    tok_dist = np.stack([np.maximum(dist, 0) * dil for _, dil in DIL_PATTERNS])
    in_window = np.stack([(dist >= 0) & (dist <= window // dil) for window, dil in DIL_PATTERNS])
    bias_a = _bias_tiles(tab_a, tok_dist, in_window, inner=A_HEADS)[0]
    o_a = _mixer_a(qa, ka, va, bias_a).reshape(A_HEADS // 2, n, LANES)

    nch = s // CMP_STRIDE
    kcmp = _compress(kc.reshape(b, s, KV_WIDTH), cmp_pos, k_w1, k_b1, k_w2)
    vcmp = _compress(vc.reshape(b, s, KV_WIDTH), cmp_pos, v_w1, v_b1, v_w2)
    d_sat = int(np.nonzero(_t5_bucket_np(np.arange(s)) < REL_BUCKETS - 1)[0].max()) + 1
    first_far = -(-(d_sat + SUB_TILE - 1) // SUB_TILE)
    n_bias = min(s // SUB_TILE, first_far + 1) + 1
    dd = ((np.arange(n_bias)[:, None, None] - 1) * SUB_TILE + np.arange(SUB_TILE)[None, :, None]
          - np.arange(SUB_TILE)[None, None, :])
    bias_sl = _bias_tiles(tab_b, dd, dd >= 0, inner=B_GROUP)
    bias_sl = bias_sl.reshape(B_KV_HEADS, n_bias, B_GROUP * SUB_TILE, SUB_TILE)
    n_wt = WIN_LEN // SUB_TILE + 1
    dw = ((n_wt - 1 - np.arange(n_wt + 1))[:, None, None] * SUB_TILE + np.arange(SUB_TILE)[None, :, None]
          - np.arange(SUB_TILE)[None, None, :])
    in_win = (dw >= 0) & (dw < WIN_LEN) & (np.arange(n_wt + 1) < n_wt)[:, None, None]
    bias_w = _bias_tiles(tab_b, dw, in_win, inner=B_GROUP)
    bias_w = bias_w.reshape(B_KV_HEADS, n_wt + 1, B_GROUP * SUB_TILE, SUB_TILE)
    o_b = _nsa(qb, gl, kcmp, vcmp, ksl, vsl, bias_sl, kw, vw, bias_w)

    wa = w_out[:A_WIDTH].astype(BF16)
    wb = w_out[A_WIDTH:][perm].astype(BF16)
    out = _tail(x2, o_a, o_b.reshape(n, B_WIDTH), wa, wb, norm2_g.reshape(1, d), w_gate.astype(BF16),
                w_up.astype(BF16), w_down.astype(BF16), norm_f_g.reshape(1, d), tm=512)
    return out.reshape(b, s, d)


def kernel(x, norm1_g, w_in, rel_bias, cmp_pos, cmp_k_w1, cmp_k_b1, cmp_k_w2, cmp_v_w1, cmp_v_b1, cmp_v_w2,
           w_out, norm2_g, w_gate, w_up, w_down, norm_f_g):
    assert w_in.shape[0] == 1, "single-layer model"
    tab_a = rel_bias[:, :A_HEADS].T * LOG2E
    tab_b = rel_bias[:, A_HEADS:].T * LOG2E
    return _layer(x, tab_a, tab_b, norm1_g[0], w_in[0], cmp_pos[0], cmp_k_w1[0], cmp_k_b1[0], cmp_k_w2[0],
                  cmp_v_w1[0], cmp_v_b1[0], cmp_v_w2[0], w_out[0], norm2_g[0], w_gate[0], w_up[0], w_down[0],
                  norm_f_g)
```

```python
import functools
import math

import numpy as np
import jax
import jax.numpy as jnp
from jax import lax
from jax.experimental import pallas as pl
from jax.experimental.pallas import tpu as pltpu

F32 = jnp.float32
BF16 = jnp.bfloat16

HEAD_DIM = 64
LANES = 128
A_HEADS = 8
DIL_PATTERNS = ((128, 1), (512, 4), (2048, 16))
DIL_BLOCK = 128
B_HEADS = 8
B_KV_HEADS = 2
B_GROUP = B_HEADS // B_KV_HEADS
CMP_LEN = 32
CMP_STRIDE = 16
CMP_HIDDEN = 128
SEL_BLOCK = 64
SEL_TOPN = 16
CMP_OVERLAP = (1.0, 2.0, 2.0, 2.0, 1.0)
WIN_LEN = 512
N_BRANCH = 3
REL_BUCKETS = 32
REL_MAX_DIST = 2048
RMS_EPS = 1e-6
A_WIDTH = A_HEADS * HEAD_DIM
B_WIDTH = B_HEADS * HEAD_DIM
KV_WIDTH = B_KV_HEADS * HEAD_DIM
GATE_WIDTH = B_HEADS * N_BRANCH
SCALE = HEAD_DIM ** -0.5
LOG2E = math.log2(math.e)

SPAN = DIL_PATTERNS[-1][1] * DIL_BLOCK
MIX_LAYOUTS = tuple(d for _, d in DIL_PATTERNS if d > 1)
NARROW_FROM = 4
RANK_CHUNK = 16
SUB_TILE = 128
Q_TILE = 512
K_TILE = 512
SCORE_TILE = 256
NOT_SELECTED = -(2.0 ** 100)
M_INIT = -1e30
VMEM_LIMIT = 56 * 1024 * 1024

NT_DIMS = (((1,), (1,)), ((), ()))


def _cparams(*sem):
    return pltpu.CompilerParams(dimension_semantics=sem, vmem_limit_bytes=VMEM_LIMIT)


def _t5_bucket_np(dist):
    max_exact = REL_BUCKETS // 2
    d = np.asarray(dist)
    df = np.maximum(d, 1).astype(np.float32)
    large = max_exact + (np.log(df / np.float32(max_exact)) / np.float32(math.log(REL_MAX_DIST / max_exact))
                         * np.float32(REL_BUCKETS - max_exact)).astype(np.int32)
    large = np.minimum(large, REL_BUCKETS - 1)
    return np.where(d < max_exact, d, large).astype(np.int32)


def _bias_kernel(tab_ref, idx_ref, o_ref):
    inner = o_ref.shape[2]
    idx = idx_ref[0]
    accs = [jnp.full(idx.shape, -jnp.inf, F32) for _ in range(inner)]
    for bucket in range(REL_BUCKETS):
        hit = idx == bucket
        for hi in range(inner):
            head = pl.program_id(0) * inner + hi
            accs[hi] = jnp.where(hit, tab_ref[head * REL_BUCKETS + bucket], accs[hi])
    for hi in range(inner):
        o_ref[0, 0, hi] = accs[hi]


def _bias_tiles(tab, dist, valid, inner):
    h = tab.shape[0]
    t, r, c = dist.shape
    idx = jnp.asarray(np.where(valid, _t5_bucket_np(np.maximum(dist, 0)), -1).astype(np.int32))
    return pl.pallas_call(
        _bias_kernel,
        grid=(h // inner, t),
        in_specs=[pl.BlockSpec(memory_space=pltpu.SMEM),
                  pl.BlockSpec((1, r, c), lambda a, ti: (ti, 0, 0))],
        out_specs=pl.BlockSpec((1, 1, inner, r, c), lambda a, ti: (a, ti, 0, 0, 0)),
        out_shape=jax.ShapeDtypeStruct((h // inner, t, inner, r, c), F32),
        compiler_params=_cparams("parallel", "parallel"),
        name="bias_tiles",
    )(tab.reshape(-1), idx)


def _half_mask(g):
    lane = lax.broadcasted_iota(jnp.int32, (1, LANES), 1)
    return (lane >= HEAD_DIM) if g else (lane < HEAD_DIM)


def _inproj_kernel(x_ref, g_ref, w_ref, *refs, widths, n_mix):
    n_dil = len(DIL_PATTERNS)
    n_lay = len(MIX_LAYOUTS)
    n_scr = n_mix * (n_dil - 1)
    mix_refs = [(None,) * (n_dil - n_lay) + tuple(refs[i * n_lay:(i + 1) * n_lay]) for i in range(n_mix)]
    out_refs = refs[n_mix * n_lay:len(refs) - n_scr]
    scratch = [refs[len(refs) - n_scr + i * (n_dil - 1):len(refs) - n_scr + (i + 1) * (n_dil - 1)]
               for i in range(n_mix)]
    x = x_ref[...]
    tm = x.shape[0]
    y = x * lax.rsqrt(jnp.mean(x * x, axis=-1, keepdims=True) + RMS_EPS)
    xn = (y * g_ref[...]).astype(BF16)
    merged = jnp.dot(xn, w_ref[:, sum(widths[:NARROW_FROM]):], preferred_element_type=F32)
    start = 0
    for idx, w in enumerate(widths):
        if idx < NARROW_FROM:
            r = jnp.dot(xn, w_ref[:, start:start + w], preferred_element_type=F32)
        else:
            off = start - sum(widths[:NARROW_FROM])
            r = merged[:, off:off + w]
        if idx < n_mix:
            prev_dil = None
            for pi, (o_ref, (_, dil)) in enumerate(zip(mix_refs[idx], DIL_PATTERNS)):
                keep = scratch[idx][pi] if pi < n_dil - 1 else None
                for p in range(w // LANES):
                    if prev_dil is None:
                        planes = [(0, r[:, p * LANES:(p + 1) * LANES])]
                    else:
                        f = dil // prev_dil
                        planes = [(res + prev_dil * a,
                                   scratch[idx][pi - 1][p, res, pl.ds(a, tm // dil, stride=f), :])
                                  for res in range(prev_dil) for a in range(f)]
                    for res, rows in planes:
                        if o_ref is not None:
                            o_ref[p, 0, res] = rows.astype(o_ref.dtype)
                        if keep is not None:
                            keep[p, res] = rows
                prev_dil = dil
        else:
            o_ref = out_refs[idx - n_mix]
            o_ref[...] = r.astype(o_ref.dtype)
        start += w


def _inproj(x3, g, w, widths, dtypes, n_mix, tm):
    b, s, d = x3.shape
    n = b * s
    per_batch = s // tm
    out_shape, out_specs = [], []
    for w_, dt in zip(widths[:n_mix], dtypes[:n_mix]):
        for dil in MIX_LAYOUTS:
            out_shape.append(jax.ShapeDtypeStruct((w_ // LANES, b, dil, s // dil, LANES), dt))
            out_specs.append(pl.BlockSpec((w_ // LANES, 1, dil, tm // dil, LANES),
                                          lambda i: (0, i // per_batch, 0, i % per_batch, 0)))
    for w_, dt in zip(widths[n_mix:], dtypes[n_mix:]):
        out_shape.append(jax.ShapeDtypeStruct((n, w_), dt))
        out_specs.append(pl.BlockSpec((tm, w_), lambda i: (i, 0)))
    return pl.pallas_call(
        functools.partial(_inproj_kernel, widths=widths, n_mix=n_mix),
        grid=(n // tm,),
        in_specs=[pl.BlockSpec((tm, d), lambda i: (i, 0)),
                  pl.BlockSpec((1, d), lambda i: (0, 0)),
                  pl.BlockSpec(w.shape, lambda i: (0, 0))],
        out_specs=out_specs,
        out_shape=out_shape,
        scratch_shapes=[pltpu.VMEM((w_ // LANES, dil, tm // dil, LANES), F32)
                        for w_ in widths[:n_mix] for _, dil in DIL_PATTERNS[:-1]],
        compiler_params=_cparams("parallel"),
        name="inproj",
    )(x3.reshape(n, d), g, w)


def _mixer_a_kernel(*refs):
    bias_ref, o_ref, acc_scr, m_scr = refs[5 * len(MIX_LAYOUTS):]
    sb = pl.program_id(2)
    halves = [_half_mask(hh) for hh in range(2)]
    col = lax.broadcasted_iota(jnp.int32, (1, 2 * DIL_BLOCK), 1)
    first_keep = (col >= DIL_BLOCK) | (sb > 0)
    ones = jnp.ones((2 * DIL_BLOCK, LANES), BF16)
    n_blocks = SPAN // DIL_BLOCK
    for p, (_, dil) in enumerate(DIL_PATTERNS):
        lay = MIX_LAYOUTS.index(dil) if dil in MIX_LAYOUTS else 0
        src = MIX_LAYOUTS[lay]
        fan = src // dil
        piece = DIL_BLOCK // fan
        q_ref, kp_ref, kc_ref, vp_ref, vc_ref = refs[5 * lay:5 * lay + 5]
        last = SPAN // src // piece - 1

        def gather(ref, r, nn, dil=dil, fan=fan, piece=piece):
            return jnp.concatenate([ref[0, 0, r + dil * a, nn * piece:(nn + 1) * piece, :] for a in range(fan)],
                                   axis=0)

        for r in range(dil):
            for n in range(SPAN // (dil * DIL_BLOCK)):
                q2 = gather(q_ref, r, n)
                if n == 0:
                    k_prev, v_prev = gather(kp_ref, r, last), gather(vp_ref, r, last)
                else:
                    k_prev, v_prev = gather(kc_ref, r, n - 1), gather(vc_ref, r, n - 1)
                kcat = jnp.concatenate([k_prev, gather(kc_ref, r, n)], axis=0)
                vcat = jnp.concatenate([v_prev, gather(vc_ref, r, n)], axis=0)
                qm = jnp.concatenate([jnp.where(halves[hh], q2, jnp.zeros_like(q2)) for hh in range(2)], axis=0)
                s2 = lax.dot_general(qm, kcat, NT_DIMS, preferred_element_type=F32)
                for hh in range(2):
                    s = s2[hh * DIL_BLOCK:(hh + 1) * DIL_BLOCK] + bias_ref[p, hh]
                    if n == 0:
                        s = jnp.where(first_keep, s, -jnp.inf)
                    m_blk = jnp.max(s, axis=-1, keepdims=True)
                    pe = jnp.exp2(s - m_blk)
                    pv = jnp.dot(pe.astype(BF16), jnp.where(halves[hh], vcat, ones), preferred_element_type=F32)
                    m_b = jnp.broadcast_to(m_blk, pv.shape)
                    for a in range(fan):
                        rows_t = pl.ds(n * dil * DIL_BLOCK + r + dil * a, piece, stride=src)
                        acc_scr[p, hh, rows_t, :] = pv[a * piece:(a + 1) * piece]
                        m_scr[p, hh, rows_t, :] = m_b[a * piece:(a + 1) * piece]

    def finish(c, carry):
        rows = pl.ds(pl.multiple_of(c * DIL_BLOCK, DIL_BLOCK), DIL_BLOCK)
        outs = []
        for hh in range(2):
            ms = [m_scr[p, hh, rows, :] for p in range(len(DIL_PATTERNS))]
            m_all = functools.reduce(jnp.maximum, ms)
            tot = sum(jnp.exp2(m - m_all) * acc_scr[p, hh, rows, :] for p, m in enumerate(ms))
            outs.append(tot / pltpu.roll(tot, HEAD_DIM, axis=1))
        o_ref[0, 0, rows, :] = jnp.where(halves[0], outs[0], outs[1]).astype(o_ref.dtype)
        return carry

    lax.fori_loop(0, n_blocks, finish, 0, unroll=4)


def _mixer_a(qs, ks, vs, bias):
    npair, b, dil0, rows0, _ = qs[0].shape
    s = dil0 * rows0
    cur = lambda bi, pi, si: (pi, bi, 0, si, 0)
    prev = lambda bi, pi, si: (pi, bi, 0, jnp.maximum(si - 1, 0), 0)
    in_specs, operands = [], []
    for dil, q, k, v in zip(MIX_LAYOUTS, qs, ks, vs):
        blk = (1, 1, dil, SPAN // dil, LANES)
        in_specs += [pl.BlockSpec(blk, cur), pl.BlockSpec(blk, prev), pl.BlockSpec(blk, cur),
                     pl.BlockSpec(blk, prev), pl.BlockSpec(blk, cur)]
        operands += [q, k, k, v, v]
    in_specs.append(pl.BlockSpec((bias.shape[0], 2) + bias.shape[2:], lambda bi, pi, si: (0, pi, 0, 0)))
    return pl.pallas_call(
        _mixer_a_kernel,
        grid=(b, npair, s // SPAN),
        in_specs=in_specs,
        out_specs=pl.BlockSpec((1, 1, SPAN, LANES), lambda bi, pi, si: (pi, bi, si, 0)),
        out_shape=jax.ShapeDtypeStruct((npair, b, s, LANES), BF16),
        scratch_shapes=[pltpu.VMEM((len(DIL_PATTERNS), 2, SPAN, LANES), F32),
                        pltpu.VMEM((len(DIL_PATTERNS), 2, SPAN, LANES), F32)],
        compiler_params=_cparams("parallel", "parallel", "parallel"),
        name="mixer_a",
    )(*operands, bias)


def _compress_kernel(c_ref, pa_ref, pb_ref, w1a_ref, w1b_ref, b1_ref, w2_ref, o_ref):
    nch = o_ref.shape[1]
    c = jnp.concatenate([c_ref[0, pl.ds(l, nch, stride=CMP_STRIDE), :] for l in range(CMP_STRIDE)], axis=1)
    xa = (c + pa_ref[...]).astype(BF16)
    xb = (c + pb_ref[...]).astype(BF16)
    ha = jnp.dot(xa, w1a_ref[...], preferred_element_type=F32)
    hb = jnp.dot(xb, w1b_ref[...], preferred_element_type=F32)
    hb_next = jnp.concatenate([hb[1:], jnp.zeros_like(hb[:1])], axis=0)
    hid = jax.nn.gelu(ha + hb_next + b1_ref[...])
    o_ref[0] = jnp.dot(hid.astype(BF16), w2_ref[...], preferred_element_type=F32).astype(o_ref.dtype)


def _compress(c, pos, w1, b1, w2):
    b, s, _ = c.shape
    nch = s // CMP_STRIDE
    half = CMP_LEN // 2
    zero = jnp.zeros((half, HEAD_DIM, CMP_HIDDEN), F32)

    def grouped(wpart):
        g0 = jnp.concatenate([wpart, zero], axis=1).reshape(half * LANES, CMP_HIDDEN)
        g1 = jnp.concatenate([zero, wpart], axis=1).reshape(half * LANES, CMP_HIDDEN)
        return jnp.concatenate([g0, g1], axis=1).astype(BF16)

    w1a, w1b = grouped(w1[:half]), grouped(w1[half:])
    pa = jnp.tile(pos[:half], (1, 2)).reshape(1, half * LANES)
    pb = jnp.tile(pos[half:], (1, 2)).reshape(1, half * LANES)
    b1g = jnp.tile(b1, 2).reshape(1, 2 * CMP_HIDDEN)
    zw = jnp.zeros_like(w2)
    w2g = jnp.concatenate([jnp.concatenate([w2, zw], axis=1),
                           jnp.concatenate([zw, w2], axis=1)], axis=0).astype(BF16)
    full = lambda a: pl.BlockSpec(a.shape, lambda i: (0,) * a.ndim)
    return pl.pallas_call(
        _compress_kernel,
        grid=(b,),
        in_specs=[pl.BlockSpec((1,) + c.shape[1:], lambda i: (i, 0, 0)),
                  full(pa), full(pb), full(w1a), full(w1b), full(b1g), full(w2g)],
        out_specs=pl.BlockSpec((1, nch, LANES), lambda i: (i, 0, 0)),
        out_shape=jax.ShapeDtypeStruct((b, nch, LANES), BF16),
        compiler_params=_cparams("parallel"),
        name="compress",
    )(c, pa, pb, w1a, w1b, b1g, w2g)


def _nsa_kernel(q_ref, gl_ref, kc_ref, vc_ref, wov_ref, ks_ref, blk_ref, vs_ref, bias_s_ref, kw_ref, vw_ref, bias_w_ref,
                o_ref, acc_ref, m_ref, s_ref, part_ref, sb_ref, *, nsel, n_bias, kw):
    qt = pl.program_id(1)
    tq = q_ref.shape[1]
    n_sub = tq // SUB_TILE
    sub_rows = B_GROUP * SUB_TILE
    gate = jax.nn.sigmoid(gl_ref[0])
    halves = [_half_mask(g) for g in range(B_KV_HEADS)]
    q_slots = [q_ref[0, :, r * LANES:(r + 1) * LANES] for r in range(B_GROUP)]
    zero = jnp.zeros((tq, LANES), BF16)

    def stacked(slots):
        return jnp.concatenate([slots[r][u * SUB_TILE:(u + 1) * SUB_TILE]
                                for u in range(n_sub) for r in range(B_GROUP)], axis=0)

    def gated_slots(o, g, branch, slots):
        for r in range(B_GROUP):
            c = (g * B_GROUP + r) * N_BRANCH + branch
            o_r = jnp.concatenate([o[(u * B_GROUP + r) * SUB_TILE:(u * B_GROUP + r + 1) * SUB_TILE]
                                   for u in range(n_sub)], axis=0) * gate[:, c:c + 1]
            slots[r] = o_r if slots[r] is None else jnp.where(halves[g], o_r, slots[r])

    nck = kc_ref.shape[1]
    row = lax.broadcasted_iota(jnp.int32, (n_sub * sub_rows, 1), 0)
    t_row = qt * tq + (row // sub_rows) * SUB_TILE + row % SUB_TILE
    blk_end = lax.broadcasted_iota(jnp.int32, (1, nck), 1) * CMP_STRIDE + (CMP_LEN - 1)
    valid = blk_end <= t_row
    kc = kc_ref[0]
    vc = vc_ref[0]
    pc = []
    cmp_slots = [None] * B_GROUP
    for g in range(B_KV_HEADS):
        qg = stacked([jnp.where(halves[g], q_slots[r], zero) for r in range(B_GROUP)])
        s = lax.dot_general(qg, kc, NT_DIMS, preferred_element_type=F32)
        s = jnp.where(valid, s, -jnp.inf)
        m = jnp.max(s, axis=-1, keepdims=True)
        m = jnp.where(m == -jnp.inf, 0.0, m)
        p = jnp.exp2(s - m)
        den = jnp.sum(p, axis=-1, keepdims=True)
        p = p / jnp.maximum(den, 1e-30)
        pc.append(jnp.concatenate(
            [sum(p[(u * B_GROUP + r) * SUB_TILE:(u * B_GROUP + r + 1) * SUB_TILE] for r in range(B_GROUP))
             for u in range(n_sub)], axis=0))
        gated_slots(jnp.dot(p.astype(BF16), vc, preferred_element_type=F32), g, 0, cmp_slots)
    for r in range(B_GROUP):
        part_ref[0, :, r * LANES:(r + 1) * LANES] = cmp_slots[r]

    t_lane = qt * tq + lax.broadcasted_iota(jnp.int32, (1, tq), 1)
    cur = t_lane // SEL_BLOCK
    j = lax.broadcasted_iota(jnp.int32, (nsel, 1), 0)
    forced = (j == 0) | (j == cur) | (j == cur - 1)
    imps = []
    for g in reversed(range(B_KV_HEADS)):
        imp = lax.dot_general(wov_ref[...], pc[g], NT_DIMS, preferred_element_type=F32,
                              precision=lax.Precision.HIGHEST)
        imps.append(jnp.where(j > cur, -jnp.inf, jnp.where(forced, jnp.inf, imp)))

    last_block = (qt * tq + tq - 1) // SEL_BLOCK
    for level in range(-(-nsel // RANK_CHUNK)):
        n_live = min((level + 1) * RANK_CHUNK, nsel)

        @pl.when(last_block // RANK_CHUNK == level)
        def _(n_live=n_live):
            blocks = []
            for imp in imps:
                if n_live <= SEL_TOPN:
                    blocks.append(jnp.zeros((HEAD_DIM, tq), F32))
                    continue
                live = imp[:n_live]
                jl = j[:n_live]
                rank = jnp.zeros((n_live, tq), jnp.int32)
                for jp in range(n_live):
                    row = live[jp:jp + 1, :]
                    rank = rank + jnp.where(jl > jp, (row >= live).astype(jnp.int32), (row > live).astype(jnp.int32))
                blocks.append(jnp.where(rank < SEL_TOPN, 0.0, NOT_SELECTED))
                if n_live < HEAD_DIM:
                    blocks.append(jnp.zeros((HEAD_DIM - n_live, tq), F32))
            sb_ref[...] = jnp.concatenate(blocks, axis=0).T.astype(sb_ref.dtype)

    sb = sb_ref[...]
    qaug = [stacked([jnp.where(halves[g], q_slots[r], sb) for r in range(B_GROUP)])
            for g in range(B_KV_HEADS)]
    n_steps = (qt * tq + tq + K_TILE - 1) // K_TILE
    acc_ref[...] = jnp.zeros_like(acc_ref)
    m_ref[...] = jnp.full_like(m_ref, M_INIT)

    def scores(kt):
        for jt in range(K_TILE // SCORE_TILE):
            k0 = pl.multiple_of(kt * K_TILE + jt * SCORE_TILE, SCORE_TILE)
            k2 = ks_ref[0, pl.ds(k0, SCORE_TILE), :]
            e2 = blk_ref[pl.ds(k0, SCORE_TILE), :]
            for g in range(B_KV_HEADS):
                k = jnp.where(halves[g], k2, e2)
                s = lax.dot_general(qaug[g], k, NT_DIMS, preferred_element_type=F32)
                for u in range(n_sub):
                    rows_u = slice(u * sub_rows, (u + 1) * sub_rows)
                    for c in range(SCORE_TILE // SUB_TILE):
                        bi = jnp.clip(qt * n_sub + u - k0 // SUB_TILE - c + 1, 0, n_bias - 1)
                        col = jt * SCORE_TILE + c * SUB_TILE
                        s_ref[kt % 2, g, rows_u, col:col + SUB_TILE] = (
                            s[rows_u, c * SUB_TILE:(c + 1) * SUB_TILE] + bias_s_ref[g, bi])

    def accumulate(kt):
        k0 = pl.multiple_of(kt * K_TILE, K_TILE)
        v2 = vs_ref[0, pl.ds(k0, K_TILE), :]
        for g in range(B_KV_HEADS):
            v = jnp.where(halves[g], v2, jnp.ones((K_TILE, LANES), BF16))
            s = s_ref[kt % 2, g]
            m_old = m_ref[g]
            m_new = jnp.maximum(m_old, jnp.max(s, axis=-1, keepdims=True))
            alpha = jnp.exp2(m_old - m_new)
            p = jnp.exp2(s - jnp.tile(m_new, (1, K_TILE // LANES)))
            acc_ref[g] = alpha * acc_ref[g] + jnp.dot(p.astype(BF16), v, preferred_element_type=F32)
            m_ref[g] = m_new

    def body(kt, carry):
        accumulate(kt)
        scores(kt + 1)
        return carry

    n_wt = kw // SUB_TILE
    q_win = [stacked([jnp.where(halves[g], q_slots[r], zero) for r in range(B_GROUP)]) for g in range(B_KV_HEADS)]
    o_win = [[] for _ in range(B_KV_HEADS)]
    for u in range(n_sub):
        sub = qt * n_sub + u
        w0 = pl.multiple_of(jnp.maximum(sub * SUB_TILE - WIN_LEN, 0), SUB_TILE)
        skip = jnp.maximum(WIN_LEN // SUB_TILE - sub, 0)
        kwin = kw_ref[0, pl.ds(w0, kw), :]
        vwin = vw_ref[0, pl.ds(w0, kw), :]
        for g in range(B_KV_HEADS):
            s = lax.dot_general(q_win[g][u * sub_rows:(u + 1) * sub_rows], kwin, NT_DIMS, preferred_element_type=F32)
            s = jnp.concatenate([s[:, jw * SUB_TILE:(jw + 1) * SUB_TILE] + bias_w_ref[g, jnp.minimum(jw + skip, n_wt)]
                                 for jw in range(n_wt)], axis=1)
            p = jnp.exp2(s - jnp.max(s, axis=-1, keepdims=True))
            acc = jnp.dot(p.astype(BF16), jnp.where(halves[g], vwin, jnp.ones_like(vwin)),
                          preferred_element_type=F32)
            o_win[g].append(acc / jnp.maximum(pltpu.roll(acc, HEAD_DIM, axis=1), 1e-30))
    win_slots = [None] * B_GROUP
    for g in range(B_KV_HEADS):
        gated_slots(jnp.concatenate(o_win[g], axis=0), g, 2, win_slots)
    for r in range(B_GROUP):
        part_ref[1, :, r * LANES:(r + 1) * LANES] = win_slots[r]

    scores(0)
    lax.fori_loop(0, n_steps - 1, body, 0)
    accumulate(n_steps - 1)
    sel_slots = [None] * B_GROUP
    for g in range(B_KV_HEADS):
        acc = acc_ref[g]
        gated_slots(acc / jnp.maximum(pltpu.roll(acc, HEAD_DIM, axis=1), 1e-30), g, 1, sel_slots)
    for r in range(B_GROUP):
        sl = slice(r * LANES, (r + 1) * LANES)
        o_ref[0, :, sl] = ((part_ref[0, :, sl] + sel_slots[r]) + part_ref[1, :, sl]).astype(o_ref.dtype)


def _nsa(qb, gl, kcmp, vcmp, ksl, vsl, bias_s, kwin, vwin, bias_w):
    b, s, w = qb.shape
    nsel = s // SEL_BLOCK
    nck = kcmp.shape[1]
    ratio = SEL_BLOCK // CMP_STRIDE
    wov = np.zeros((nsel, nck), np.float32)
    for jj in range(nsel):
        for off, wt in zip(range(-1, ratio), CMP_OVERLAP):
            n = ratio * jj + off
            if 0 <= n < nck - 1:
                wov[jj, n] = wt
    wov = jnp.asarray(wov)
    blk = jnp.asarray((np.arange(s)[:, None] // SEL_BLOCK) == (np.arange(LANES)[None, :] % HEAD_DIM), BF16)
    rows = B_GROUP * Q_TILE
    tile = lambda width: pl.BlockSpec((1, Q_TILE, width), lambda bi, qi: (bi, qi, 0))
    whole = lambda a: pl.BlockSpec((1,) + a.shape[1:], lambda bi, qi: (bi, 0, 0))
    const = lambda a: pl.BlockSpec(a.shape, lambda bi, qi: (0,) * a.ndim, pipeline_mode=pl.Buffered(1))
    return pl.pallas_call(
        functools.partial(_nsa_kernel, nsel=nsel, n_bias=bias_s.shape[1], kw=WIN_LEN + SUB_TILE),
        grid=(b, s // Q_TILE),
        in_specs=[tile(w), tile(LANES), whole(kcmp), whole(vcmp), const(wov), whole(ksl), const(blk), whole(vsl),
                  const(bias_s), whole(kwin), whole(vwin), const(bias_w)],
        out_specs=tile(w),
        out_shape=jax.ShapeDtypeStruct((b, s, w), BF16),
        scratch_shapes=[pltpu.VMEM((B_KV_HEADS, rows, LANES), F32), pltpu.VMEM((B_KV_HEADS, rows, LANES), F32),
                        pltpu.VMEM((2, B_KV_HEADS, rows, K_TILE), F32), pltpu.VMEM((2, Q_TILE, w), F32),
                        pltpu.VMEM((Q_TILE, LANES), BF16)],
        compiler_params=_cparams("parallel", "parallel"),
        name="nsa",
    )(qb, gl, kcmp, vcmp, wov, ksl, blk, vsl, bias_s, kwin, vwin, bias_w)


def _tail_kernel(x_ref, oa_ref, ob_ref, wa_ref, wb_ref, g2_ref, wg_ref, wu_ref, wd_ref, gf_ref, o_ref):
    o_a = jnp.concatenate([oa_ref[p] for p in range(oa_ref.shape[0])], axis=1)
    mix = jnp.dot(o_a, wa_ref[...], preferred_element_type=F32)
    mix = mix + jnp.dot(ob_ref[...], wb_ref[...], preferred_element_type=F32)
    h = x_ref[...] + mix
    y = h * lax.rsqrt(jnp.mean(h * h, axis=-1, keepdims=True) + RMS_EPS)
    hn = (y * g2_ref[...]).astype(BF16)
    a = jnp.dot(hn, wg_ref[...], preferred_element_type=F32)
    u = jnp.dot(hn, wu_ref[...], preferred_element_type=F32)
    act = (jax.nn.silu(a) * u).astype(BF16)
    h2 = h + jnp.dot(act, wd_ref[...], preferred_element_type=F32)
    y2 = h2 * lax.rsqrt(jnp.mean(h2 * h2, axis=-1, keepdims=True) + RMS_EPS)
    o_ref[...] = y2 * gf_ref[...]


def _tail(x2, oa, ob, wa, wb, g2, wg, wu, wd, gf, tm):
    n, d = x2.shape
    row = lambda width: pl.BlockSpec((tm, width), lambda i: (i, 0))
    const = lambda a: pl.BlockSpec(a.shape, lambda i: (0, 0), pipeline_mode=pl.Buffered(1))
    return pl.pallas_call(
        _tail_kernel,
        grid=(n // tm,),
        in_specs=[row(d), pl.BlockSpec((oa.shape[0], tm, LANES), lambda i: (0, i, 0)), row(ob.shape[1]),
                  const(wa), const(wb), const(g2), const(wg), const(wu), const(wd), const(gf)],
        out_specs=row(d),
        out_shape=jax.ShapeDtypeStruct((n, d), F32),
        compiler_params=_cparams("parallel"),
        name="tail",
    )(x2, oa, ob, wa, wb, g2, wg, wu, wd, gf)


def _slot_perm():
    perm = np.zeros(B_WIDTH, np.int64)
    for r in range(B_GROUP):
        for g in range(B_KV_HEADS):
            src = (g * B_GROUP + r) * HEAD_DIM
            dst = r * LANES + g * HEAD_DIM
            perm[dst:dst + HEAD_DIM] = np.arange(src, src + HEAD_DIM)
    return perm


def _layer(h, tab_a, tab_b, norm1_g, w_in, cmp_pos, k_w1, k_b1, k_w2, v_w1, v_b1, v_w2,
           w_out, norm2_g, w_gate, w_up, w_down, norm_f_g):
    b, s, d = h.shape
    n = b * s
    assert s % SPAN == 0 and s % K_TILE == 0
    perm = _slot_perm()

    cols = np.cumsum([0, A_WIDTH, A_WIDTH, A_WIDTH, B_WIDTH] + [KV_WIDTH] * 6 + [GATE_WIDTH])
    w_aq = w_in[:, cols[0]:cols[1]] * (SCALE * LOG2E)
    w_bq = (w_in[:, cols[3]:cols[4]] * (SCALE * LOG2E))[:, perm]
    w_gl = jnp.pad(w_in[:, cols[10]:cols[11]], ((0, 0), (0, LANES - GATE_WIDTH)))
    w1 = jnp.concatenate([w_aq, w_in[:, cols[1]:cols[3]], w_bq, w_in[:, cols[4]:cols[10]], w_gl], axis=1).astype(BF16)
    widths = (A_WIDTH,) * 3 + (B_WIDTH,) + (KV_WIDTH,) * 6 + (LANES,)
    dtypes = (BF16,) * 4 + (F32,) * 2 + (BF16,) * 4 + (F32,)
    x2 = h.reshape(n, d)
    n_lay = len(MIX_LAYOUTS)
    outs = _inproj(h, norm1_g.reshape(1, d), w1, widths, dtypes, n_mix=3, tm=512)
    qa, ka, va = (outs[i * n_lay:(i + 1) * n_lay] for i in range(3))
    qb, kc, vc, ksl, vsl, kw, vw, gl = outs[3 * n_lay:]
    r3 = lambda t: t.reshape(b, s, t.shape[-1])
    qb, ksl, vsl, kw, vw, gl = map(r3, (qb, ksl, vsl, kw, vw, gl))

    tok_dist, in_window = [], []
    for window, dil in DIL_PATTERNS:
        fan = (dil if dil in MIX_LAYOUTS else MIX_LAYOUTS[0]) // dil
        row = np.arange(DIL_BLOCK)
        step = fan * (row % (DIL_BLOCK // fan)) + row // (DIL_BLOCK // fan)
        key_step = np.concatenate([step, DIL_BLOCK + step])
        dist = DIL_BLOCK + step[:, None] - key_step[None, :]
        tok_dist.append(np.maximum(dist, 0) * dil)
        in_window.append((dist >= 0) & (dist <= window // dil))
    tok_dist, in_window = np.stack(tok_dist), np.stack(in_window)
    bias_a = _bias_tiles(tab_a, tok_dist, in_window, inner=A_HEADS)[0]
    o_a = _mixer_a(qa, ka, va, bias_a).reshape(A_HEADS // 2, n, LANES)

    nch = s // CMP_STRIDE
    kcmp = _compress(kc.reshape(b, s, KV_WIDTH), cmp_pos, k_w1, k_b1, k_w2)
    vcmp = _compress(vc.reshape(b, s, KV_WIDTH), cmp_pos, v_w1, v_b1, v_w2)
    d_sat = int(np.nonzero(_t5_bucket_np(np.arange(s)) < REL_BUCKETS - 1)[0].max()) + 1
    first_far = -(-(d_sat + SUB_TILE - 1) // SUB_TILE)
    n_bias = min(s // SUB_TILE, first_far + 1) + 1
    dd = ((np.arange(n_bias)[:, None, None] - 1) * SUB_TILE + np.arange(SUB_TILE)[None, :, None]
          - np.arange(SUB_TILE)[None, None, :])
    bias_sl = _bias_tiles(tab_b, dd, dd >= 0, inner=B_GROUP)
    bias_sl = bias_sl.reshape(B_KV_HEADS, n_bias, B_GROUP * SUB_TILE, SUB_TILE)
    n_wt = WIN_LEN // SUB_TILE + 1
    dw = ((n_wt - 1 - np.arange(n_wt + 1))[:, None, None] * SUB_TILE + np.arange(SUB_TILE)[None, :, None]
          - np.arange(SUB_TILE)[None, None, :])
    in_win = (dw >= 0) & (dw < WIN_LEN) & (np.arange(n_wt + 1) < n_wt)[:, None, None]
    bias_w = _bias_tiles(tab_b, dw, in_win, inner=B_GROUP)
    bias_w = bias_w.reshape(B_KV_HEADS, n_wt + 1, B_GROUP * SUB_TILE, SUB_TILE)
    o_b = _nsa(qb, gl, kcmp, vcmp, ksl, vsl, bias_sl, kw, vw, bias_w)

    wa = w_out[:A_WIDTH].astype(BF16)
    wb = w_out[A_WIDTH:][perm].astype(BF16)
    out = _tail(x2, o_a, o_b.reshape(n, B_WIDTH), wa, wb, norm2_g.reshape(1, d), w_gate.astype(BF16),
                w_up.astype(BF16), w_down.astype(BF16), norm_f_g.reshape(1, d), tm=512)
    return out.reshape(b, s, d)


def kernel(x, norm1_g, w_in, rel_bias, cmp_pos, cmp_k_w1, cmp_k_b1, cmp_k_w2, cmp_v_w1, cmp_v_b1, cmp_v_w2,
           w_out, norm2_g, w_gate, w_up, w_down, norm_f_g):
    assert w_in.shape[0] == 1, "single-layer model"
    tab_a = rel_bias[:, :A_HEADS].T * LOG2E
    tab_b = rel_bias[:, A_HEADS:].T * LOG2E
    return _layer(x, tab_a, tab_b, norm1_g[0], w_in[0], cmp_pos[0], cmp_k_w1[0], cmp_k_b1[0], cmp_k_w2[0],
                  cmp_v_w1[0], cmp_v_b1[0], cmp_v_w2[0], w_out[0], norm2_g[0], w_gate[0], w_up[0], w_down[0],
                  norm_f_g)
```

```python
import functools
import math

import numpy as np
import jax
import jax.numpy as jnp
from jax import lax
from jax.experimental import pallas as pl
from jax.experimental.pallas import tpu as pltpu

F32 = jnp.float32
BF16 = jnp.bfloat16

HEAD_DIM = 64
LANES = 128
A_HEADS = 8
DIL_PATTERNS = ((128, 1), (512, 4), (2048, 16))
DIL_BLOCK = 128
B_HEADS = 8
B_KV_HEADS = 2
B_GROUP = B_HEADS // B_KV_HEADS
CMP_LEN = 32
CMP_STRIDE = 16
CMP_HIDDEN = 128
SEL_BLOCK = 64
SEL_TOPN = 16
CMP_OVERLAP = (1.0, 2.0, 2.0, 2.0, 1.0)
WIN_LEN = 512
N_BRANCH = 3
REL_BUCKETS = 32
REL_MAX_DIST = 2048
RMS_EPS = 1e-6
A_WIDTH = A_HEADS * HEAD_DIM
B_WIDTH = B_HEADS * HEAD_DIM
KV_WIDTH = B_KV_HEADS * HEAD_DIM
GATE_WIDTH = B_HEADS * N_BRANCH
SCALE = HEAD_DIM ** -0.5
LOG2E = math.log2(math.e)

SPAN = DIL_PATTERNS[-1][1] * DIL_BLOCK
MIX_LAYOUTS = tuple(d for _, d in DIL_PATTERNS if d > 1)
NARROW_FROM = 4
RANK_CHUNK = 16
SUB_TILE = 128
Q_TILE = 512
K_TILE = 512
SCORE_TILE = 256
NOT_SELECTED = -(2.0 ** 100)
M_INIT = -1e30
VMEM_LIMIT = 56 * 1024 * 1024

NT_DIMS = (((1,), (1,)), ((), ()))


def _cparams(*sem):
    return pltpu.CompilerParams(dimension_semantics=sem, vmem_limit_bytes=VMEM_LIMIT)


def _t5_bucket_np(dist):
    max_exact = REL_BUCKETS // 2
    d = np.asarray(dist)
    df = np.maximum(d, 1).astype(np.float32)
    large = max_exact + (np.log(df / np.float32(max_exact)) / np.float32(math.log(REL_MAX_DIST / max_exact))
                         * np.float32(REL_BUCKETS - max_exact)).astype(np.int32)
    large = np.minimum(large, REL_BUCKETS - 1)
    return np.where(d < max_exact, d, large).astype(np.int32)


def _bias_kernel(tab_ref, idx_ref, o_ref):
    inner = o_ref.shape[2]
    idx = idx_ref[0]
    accs = [jnp.full(idx.shape, -jnp.inf, F32) for _ in range(inner)]
    for bucket in range(REL_BUCKETS):
        hit = idx == bucket
        for hi in range(inner):
            head = pl.program_id(0) * inner + hi
            accs[hi] = jnp.where(hit, tab_ref[head * REL_BUCKETS + bucket], accs[hi])
    for hi in range(inner):
        o_ref[0, 0, hi] = accs[hi]


def _bias_tiles(tab, dist, valid, inner):
    h = tab.shape[0]
    t, r, c = dist.shape
    idx = jnp.asarray(np.where(valid, _t5_bucket_np(np.maximum(dist, 0)), -1).astype(np.int32))
    return pl.pallas_call(
        _bias_kernel,
        grid=(h // inner, t),
        in_specs=[pl.BlockSpec(memory_space=pltpu.SMEM),
                  pl.BlockSpec((1, r, c), lambda a, ti: (ti, 0, 0))],
        out_specs=pl.BlockSpec((1, 1, inner, r, c), lambda a, ti: (a, ti, 0, 0, 0)),
        out_shape=jax.ShapeDtypeStruct((h // inner, t, inner, r, c), F32),
        compiler_params=_cparams("parallel", "parallel"),
        name="bias_tiles",
    )(tab.reshape(-1), idx)


def _half_mask(g):
    lane = lax.broadcasted_iota(jnp.int32, (1, LANES), 1)
    return (lane >= HEAD_DIM) if g else (lane < HEAD_DIM)


def _inproj_kernel(x_ref, g_ref, w_ref, *refs, widths, n_mix):
    n_dil = len(DIL_PATTERNS)
    n_lay = len(MIX_LAYOUTS)
    n_scr = n_mix * (n_dil - 1)
    mix_refs = [(None,) * (n_dil - n_lay) + tuple(refs[i * n_lay:(i + 1) * n_lay]) for i in range(n_mix)]
    out_refs = refs[n_mix * n_lay:len(refs) - n_scr]
    scratch = [refs[len(refs) - n_scr + i * (n_dil - 1):len(refs) - n_scr + (i + 1) * (n_dil - 1)]
               for i in range(n_mix)]
    x = x_ref[...]
    tm = x.shape[0]
    y = x * lax.rsqrt(jnp.mean(x * x, axis=-1, keepdims=True) + RMS_EPS)
    xn = (y * g_ref[...]).astype(BF16)
    merged = jnp.dot(xn, w_ref[:, sum(widths[:NARROW_FROM]):], preferred_element_type=F32)
    start = 0
    for idx, w in enumerate(widths):
        if idx < NARROW_FROM:
            r = jnp.dot(xn, w_ref[:, start:start + w], preferred_element_type=F32)
        else:
            off = start - sum(widths[:NARROW_FROM])
            r = merged[:, off:off + w]
        if idx < n_mix:
            prev_dil = None
            for pi, (o_ref, (_, dil)) in enumerate(zip(mix_refs[idx], DIL_PATTERNS)):
                keep = scratch[idx][pi] if pi < n_dil - 1 else None
                for p in range(w // LANES):
                    if prev_dil is None:
                        planes = [(0, r[:, p * LANES:(p + 1) * LANES])]
                    else:
                        f = dil // prev_dil
                        planes = [(res + prev_dil * a,
                                   scratch[idx][pi - 1][p, res, pl.ds(a, tm // dil, stride=f), :])
                                  for res in range(prev_dil) for a in range(f)]
                    for res, rows in planes:
                        if o_ref is not None:
                            o_ref[p, 0, res] = rows.astype(o_ref.dtype)
                        if keep is not None:
                            keep[p, res] = rows
                prev_dil = dil
        else:
            o_ref = out_refs[idx - n_mix]
            o_ref[...] = r.astype(o_ref.dtype)
        start += w


def _inproj(x3, g, w, widths, dtypes, n_mix, tm):
    b, s, d = x3.shape
    n = b * s
    per_batch = s // tm
    out_shape, out_specs = [], []
    for w_, dt in zip(widths[:n_mix], dtypes[:n_mix]):
        for dil in MIX_LAYOUTS:
            out_shape.append(jax.ShapeDtypeStruct((w_ // LANES, b, dil, s // dil, LANES), dt))
            out_specs.append(pl.BlockSpec((w_ // LANES, 1, dil, tm // dil, LANES),
                                          lambda i: (0, i // per_batch, 0, i % per_batch, 0)))
    for w_, dt in zip(widths[n_mix:], dtypes[n_mix:]):
        out_shape.append(jax.ShapeDtypeStruct((n, w_), dt))
        out_specs.append(pl.BlockSpec((tm, w_), lambda i: (i, 0)))
    return pl.pallas_call(
        functools.partial(_inproj_kernel, widths=widths, n_mix=n_mix),
        grid=(n // tm,),
        in_specs=[pl.BlockSpec((tm, d), lambda i: (i, 0)),
                  pl.BlockSpec((1, d), lambda i: (0, 0)),
                  pl.BlockSpec(w.shape, lambda i: (0, 0))],
        out_specs=out_specs,
        out_shape=out_shape,
        scratch_shapes=[pltpu.VMEM((w_ // LANES, dil, tm // dil, LANES), F32)
                        for w_ in widths[:n_mix] for _, dil in DIL_PATTERNS[:-1]],
        compiler_params=_cparams("parallel"),
        name="inproj",
    )(x3.reshape(n, d), g, w)


def _mixer_a_kernel(*refs):
    bias_ref, o_ref, acc_scr, m_scr = refs[5 * len(MIX_LAYOUTS):]
    sb = pl.program_id(2)
    halves = [_half_mask(hh) for hh in range(2)]
    col = lax.broadcasted_iota(jnp.int32, (1, 2 * DIL_BLOCK), 1)
    first_keep = (col >= DIL_BLOCK) | (sb > 0)
    ones = jnp.ones((2 * DIL_BLOCK, LANES), BF16)
    n_blocks = SPAN // DIL_BLOCK
    for p, (_, dil) in enumerate(DIL_PATTERNS):
        lay = MIX_LAYOUTS.index(dil) if dil in MIX_LAYOUTS else 0
        src = MIX_LAYOUTS[lay]
        fan = src // dil
        piece = DIL_BLOCK // fan
        q_ref, kp_ref, kc_ref, vp_ref, vc_ref = refs[5 * lay:5 * lay + 5]
        last = SPAN // src // piece - 1

        def gather(ref, r, nn, dil=dil, fan=fan, piece=piece):
            return jnp.concatenate([ref[0, 0, r + dil * a, nn * piece:(nn + 1) * piece, :] for a in range(fan)],
                                   axis=0)

        for r in range(dil):
            for n in range(SPAN // (dil * DIL_BLOCK)):
                q2 = gather(q_ref, r, n)
                if n == 0:
                    k_prev, v_prev = gather(kp_ref, r, last), gather(vp_ref, r, last)
                else:
                    k_prev, v_prev = gather(kc_ref, r, n - 1), gather(vc_ref, r, n - 1)
                kcat = jnp.concatenate([k_prev, gather(kc_ref, r, n)], axis=0)
                vcat = jnp.concatenate([v_prev, gather(vc_ref, r, n)], axis=0)
                qm = jnp.concatenate([jnp.where(halves[hh], q2, jnp.zeros_like(q2)) for hh in range(2)], axis=0)
                s2 = lax.dot_general(qm, kcat, NT_DIMS, preferred_element_type=F32)
                for hh in range(2):
                    s = s2[hh * DIL_BLOCK:(hh + 1) * DIL_BLOCK] + bias_ref[p, hh]
                    if n == 0:
                        s = jnp.where(first_keep, s, -jnp.inf)
                    m_blk = jnp.max(s, axis=-1, keepdims=True)
                    pe = jnp.exp2(s - m_blk)
                    pv = jnp.dot(pe.astype(BF16), jnp.where(halves[hh], vcat, ones), preferred_element_type=F32)
                    m_b = jnp.broadcast_to(m_blk, pv.shape)
                    for a in range(fan):
                        rows_t = pl.ds(n * dil * DIL_BLOCK + r + dil * a, piece, stride=src)
                        acc_scr[p, hh, rows_t, :] = pv[a * piece:(a + 1) * piece]
                        m_scr[p, hh, rows_t, :] = m_b[a * piece:(a + 1) * piece]

    def finish(c, carry):
        rows = pl.ds(pl.multiple_of(c * DIL_BLOCK, DIL_BLOCK), DIL_BLOCK)
        tots = []
        for hh in range(2):
            ms = [m_scr[p, hh, rows, :] for p in range(len(DIL_PATTERNS))]
            m_all = functools.reduce(jnp.maximum, ms)
            tots.append(sum(jnp.exp2(m - m_all) * acc_scr[p, hh, rows, :] for p, m in enumerate(ms)))
        num = jnp.where(halves[0], tots[0], tots[1])
        den = pltpu.roll(jnp.where(halves[0], tots[1], tots[0]), HEAD_DIM, axis=1)
        o_ref[0, 0, rows, :] = (num / den).astype(o_ref.dtype)
        return carry

    lax.fori_loop(0, n_blocks, finish, 0, unroll=4)


def _mixer_a(qs, ks, vs, bias):
    npair, b, dil0, rows0, _ = qs[0].shape
    s = dil0 * rows0
    cur = lambda bi, pi, si: (pi, bi, 0, si, 0)
    prev = lambda bi, pi, si: (pi, bi, 0, jnp.maximum(si - 1, 0), 0)
    in_specs, operands = [], []
    for dil, q, k, v in zip(MIX_LAYOUTS, qs, ks, vs):
        blk = (1, 1, dil, SPAN // dil, LANES)
        in_specs += [pl.BlockSpec(blk, cur), pl.BlockSpec(blk, prev), pl.BlockSpec(blk, cur),
                     pl.BlockSpec(blk, prev), pl.BlockSpec(blk, cur)]
        operands += [q, k, k, v, v]
    in_specs.append(pl.BlockSpec((bias.shape[0], 2) + bias.shape[2:], lambda bi, pi, si: (0, pi, 0, 0)))
    return pl.pallas_call(
        _mixer_a_kernel,
        grid=(b, npair, s // SPAN),
        in_specs=in_specs,
        out_specs=pl.BlockSpec((1, 1, SPAN, LANES), lambda bi, pi, si: (pi, bi, si, 0)),
        out_shape=jax.ShapeDtypeStruct((npair, b, s, LANES), BF16),
        scratch_shapes=[pltpu.VMEM((len(DIL_PATTERNS), 2, SPAN, LANES), F32),
                        pltpu.VMEM((len(DIL_PATTERNS), 2, SPAN, LANES), F32)],
        compiler_params=_cparams("parallel", "parallel", "parallel"),
        name="mixer_a",
    )(*operands, bias)


def _compress_kernel(c_ref, pa_ref, pb_ref, w1a_ref, w1b_ref, b1_ref, w2_ref, o_ref):
    nch = o_ref.shape[1]
    c = jnp.concatenate([c_ref[0, pl.ds(l, nch, stride=CMP_STRIDE), :] for l in range(CMP_STRIDE)], axis=1)
    xa = (c + pa_ref[...]).astype(BF16)
    xb = (c + pb_ref[...]).astype(BF16)
    ha = jnp.dot(xa, w1a_ref[...], preferred_element_type=F32)
    hb = jnp.dot(xb, w1b_ref[...], preferred_element_type=F32)
    hb_next = jnp.concatenate([hb[1:], jnp.zeros_like(hb[:1])], axis=0)
    hid = jax.nn.gelu(ha + hb_next + b1_ref[...])
    o_ref[0] = jnp.dot(hid.astype(BF16), w2_ref[...], preferred_element_type=F32).astype(o_ref.dtype)


def _compress(c, pos, w1, b1, w2):
    b, s, _ = c.shape
    nch = s // CMP_STRIDE
    half = CMP_LEN // 2
    zero = jnp.zeros((half, HEAD_DIM, CMP_HIDDEN), F32)

    def grouped(wpart):
        g0 = jnp.concatenate([wpart, zero], axis=1).reshape(half * LANES, CMP_HIDDEN)
        g1 = jnp.concatenate([zero, wpart], axis=1).reshape(half * LANES, CMP_HIDDEN)
        return jnp.concatenate([g0, g1], axis=1).astype(BF16)

    w1a, w1b = grouped(w1[:half]), grouped(w1[half:])
    pa = jnp.tile(pos[:half], (1, 2)).reshape(1, half * LANES)
    pb = jnp.tile(pos[half:], (1, 2)).reshape(1, half * LANES)
    b1g = jnp.tile(b1, 2).reshape(1, 2 * CMP_HIDDEN)
    zw = jnp.zeros_like(w2)
    w2g = jnp.concatenate([jnp.concatenate([w2, zw], axis=1),
                           jnp.concatenate([zw, w2], axis=1)], axis=0).astype(BF16)
    full = lambda a: pl.BlockSpec(a.shape, lambda i: (0,) * a.ndim)
    return pl.pallas_call(
        _compress_kernel,
        grid=(b,),
        in_specs=[pl.BlockSpec((1,) + c.shape[1:], lambda i: (i, 0, 0)),
                  full(pa), full(pb), full(w1a), full(w1b), full(b1g), full(w2g)],
        out_specs=pl.BlockSpec((1, nch, LANES), lambda i: (i, 0, 0)),
        out_shape=jax.ShapeDtypeStruct((b, nch, LANES), BF16),
        compiler_params=_cparams("parallel"),
        name="compress",
    )(c, pa, pb, w1a, w1b, b1g, w2g)


def _nsa_kernel(q_ref, gl_ref, kc_ref, vc_ref, wov_ref, ks_ref, blk_ref, vs_ref, bias_s_ref, kw_ref, vw_ref, bias_w_ref,
                o_ref, acc_ref, m_ref, s_ref, part_ref, sb_ref, *, nsel, n_bias, kw):
    qt = pl.program_id(1)
    tq = q_ref.shape[1]
    n_sub = tq // SUB_TILE
    sub_rows = B_GROUP * SUB_TILE
    gate = jax.nn.sigmoid(gl_ref[0])
    halves = [_half_mask(g) for g in range(B_KV_HEADS)]
    q_slots = [q_ref[0, :, r * LANES:(r + 1) * LANES] for r in range(B_GROUP)]
    zero = jnp.zeros((tq, LANES), BF16)

    def stacked(slots):
        return jnp.concatenate([slots[r][u * SUB_TILE:(u + 1) * SUB_TILE]
                                for u in range(n_sub) for r in range(B_GROUP)], axis=0)

    def head_rows(o, r):
        return jnp.concatenate([o[(u * B_GROUP + r) * SUB_TILE:(u * B_GROUP + r + 1) * SUB_TILE]
                                for u in range(n_sub)], axis=0)

    def gate_tile(r, branch):
        c0, c1 = (r * N_BRANCH + branch, (B_GROUP + r) * N_BRANCH + branch)
        return jnp.where(halves[0], gate[:, c0:c0 + 1], gate[:, c1:c1 + 1])

    def gated_slots(o, g, branch, slots):
        for r in range(B_GROUP):
            c = (g * B_GROUP + r) * N_BRANCH + branch
            o_r = head_rows(o, r) * gate[:, c:c + 1]
            slots[r] = o_r if slots[r] is None else jnp.where(halves[g], o_r, slots[r])

    def normalised_slots(accs, branch):
        slots = []
        for r in range(B_GROUP):
            a0, a1 = head_rows(accs[0], r), head_rows(accs[1], r)
            num = jnp.where(halves[0], a0, a1)
            den = pltpu.roll(jnp.where(halves[0], a1, a0), HEAD_DIM, axis=1)
            slots.append(num / jnp.maximum(den, 1e-30) * gate_tile(r, branch))
        return slots

    nck = kc_ref.shape[1]
    row = lax.broadcasted_iota(jnp.int32, (n_sub * sub_rows, 1), 0)
    t_row = qt * tq + (row // sub_rows) * SUB_TILE + row % SUB_TILE
    blk_end = lax.broadcasted_iota(jnp.int32, (1, nck), 1) * CMP_STRIDE + (CMP_LEN - 1)
    valid = blk_end <= t_row
    kc = kc_ref[0]
    vc = vc_ref[0]
    pc = []
    cmp_slots = [None] * B_GROUP
    for g in range(B_KV_HEADS):
        qg = stacked([jnp.where(halves[g], q_slots[r], zero) for r in range(B_GROUP)])
        s = lax.dot_general(qg, kc, NT_DIMS, preferred_element_type=F32)
        s = jnp.where(valid, s, -jnp.inf)
        m = jnp.max(s, axis=-1, keepdims=True)
        m = jnp.where(m == -jnp.inf, 0.0, m)
        p = jnp.exp2(s - m)
        den = jnp.sum(p, axis=-1, keepdims=True)
        p = p / jnp.maximum(den, 1e-30)
        pc.append(jnp.concatenate(
            [sum(p[(u * B_GROUP + r) * SUB_TILE:(u * B_GROUP + r + 1) * SUB_TILE] for r in range(B_GROUP))
             for u in range(n_sub)], axis=0))
        gated_slots(jnp.dot(p.astype(BF16), vc, preferred_element_type=F32), g, 0, cmp_slots)
    for r in range(B_GROUP):
        part_ref[0, :, r * LANES:(r + 1) * LANES] = cmp_slots[r]

    t_lane = qt * tq + lax.broadcasted_iota(jnp.int32, (1, tq), 1)
    cur = t_lane // SEL_BLOCK
    j = lax.broadcasted_iota(jnp.int32, (nsel, 1), 0)
    forced = (j == 0) | (j == cur) | (j == cur - 1)
    imps = []
    for g in reversed(range(B_KV_HEADS)):
        imp = lax.dot_general(wov_ref[...], pc[g], NT_DIMS, preferred_element_type=F32,
                              precision=lax.Precision.HIGHEST)
        imps.append(jnp.where(j > cur, -jnp.inf, jnp.where(forced, jnp.inf, imp)))

    last_block = (qt * tq + tq - 1) // SEL_BLOCK
    for level in range(-(-nsel // RANK_CHUNK)):
        n_live = min((level + 1) * RANK_CHUNK, nsel)

        @pl.when(last_block // RANK_CHUNK == level)
        def _(n_live=n_live):
            blocks = []
            for imp in imps:
                if n_live <= SEL_TOPN:
                    blocks.append(jnp.zeros((HEAD_DIM, tq), F32))
                    continue
                live = imp[:n_live]
                jl = j[:n_live]
                rank = jnp.zeros((n_live, tq), jnp.int32)
                for jp in range(n_live):
                    row = live[jp:jp + 1, :]
                    rank = rank + jnp.where(jl > jp, (row >= live).astype(jnp.int32), (row > live).astype(jnp.int32))
                blocks.append(jnp.where(rank < SEL_TOPN, 0.0, NOT_SELECTED))
                if n_live < HEAD_DIM:
                    blocks.append(jnp.zeros((HEAD_DIM - n_live, tq), F32))
            sb_ref[...] = jnp.concatenate(blocks, axis=0).T.astype(sb_ref.dtype)

    sb = sb_ref[...]
    qaug = [stacked([jnp.where(halves[g], q_slots[r], sb) for r in range(B_GROUP)])
            for g in range(B_KV_HEADS)]
    n_steps = (qt * tq + tq + K_TILE - 1) // K_TILE
    acc_ref[...] = jnp.zeros_like(acc_ref)
    m_ref[...] = jnp.full_like(m_ref, M_INIT)

    def scores(kt):
        for jt in range(K_TILE // SCORE_TILE):
            k0 = pl.multiple_of(kt * K_TILE + jt * SCORE_TILE, SCORE_TILE)
            k2 = ks_ref[0, pl.ds(k0, SCORE_TILE), :]
            e2 = blk_ref[pl.ds(k0, SCORE_TILE), :]
            for g in range(B_KV_HEADS):
                k = jnp.where(halves[g], k2, e2)
                s = lax.dot_general(qaug[g], k, NT_DIMS, preferred_element_type=F32)
                for u in range(n_sub):
                    rows_u = slice(u * sub_rows, (u + 1) * sub_rows)
                    for c in range(SCORE_TILE // SUB_TILE):
                        bi = jnp.clip(qt * n_sub + u - k0 // SUB_TILE - c + 1, 0, n_bias - 1)
                        col = jt * SCORE_TILE + c * SUB_TILE
                        s_ref[kt % 2, g, rows_u, col:col + SUB_TILE] = (
                            s[rows_u, c * SUB_TILE:(c + 1) * SUB_TILE] + bias_s_ref[g, bi])

    def accumulate(kt):
        k0 = pl.multiple_of(kt * K_TILE, K_TILE)
        v2 = vs_ref[0, pl.ds(k0, K_TILE), :]
        for g in range(B_KV_HEADS):
            v = jnp.where(halves[g], v2, jnp.ones((K_TILE, LANES), BF16))
            s = s_ref[kt % 2, g]
            m_old = m_ref[g]
            m_new = jnp.maximum(m_old, jnp.max(s, axis=-1, keepdims=True))
            alpha = jnp.exp2(m_old - m_new)
            p = jnp.exp2(s - jnp.tile(m_new, (1, K_TILE // LANES)))
            acc_ref[g] = alpha * acc_ref[g] + jnp.dot(p.astype(BF16), v, preferred_element_type=F32)
            m_ref[g] = m_new

    def body(kt, carry):
        accumulate(kt)
        scores(kt + 1)
        return carry

    n_wt = kw // SUB_TILE
    q_win = [stacked([jnp.where(halves[g], q_slots[r], zero) for r in range(B_GROUP)]) for g in range(B_KV_HEADS)]
    o_win = [[] for _ in range(B_KV_HEADS)]
    for u in range(n_sub):
        sub = qt * n_sub + u
        w0 = pl.multiple_of(jnp.maximum(sub * SUB_TILE - WIN_LEN, 0), SUB_TILE)
        skip = jnp.maximum(WIN_LEN // SUB_TILE - sub, 0)
        kwin = kw_ref[0, pl.ds(w0, kw), :]
        vwin = vw_ref[0, pl.ds(w0, kw), :]
        for g in range(B_KV_HEADS):
            s = lax.dot_general(q_win[g][u * sub_rows:(u + 1) * sub_rows], kwin, NT_DIMS, preferred_element_type=F32)
            s = jnp.concatenate([s[:, jw * SUB_TILE:(jw + 1) * SUB_TILE] + bias_w_ref[g, jnp.minimum(jw + skip, n_wt)]
                                 for jw in range(n_wt)], axis=1)
            p = jnp.exp2(s - jnp.max(s, axis=-1, keepdims=True))
            acc = jnp.dot(p.astype(BF16), jnp.where(halves[g], vwin, jnp.ones_like(vwin)),
                          preferred_element_type=F32)
            o_win[g].append(acc)
    win_slots = normalised_slots([jnp.concatenate(o_win[g], axis=0) for g in range(B_KV_HEADS)], 2)
    for r in range(B_GROUP):
        part_ref[1, :, r * LANES:(r + 1) * LANES] = win_slots[r]

    scores(0)
    lax.fori_loop(0, n_steps - 1, body, 0)
    accumulate(n_steps - 1)
    sel_slots = normalised_slots([acc_ref[g] for g in range(B_KV_HEADS)], 1)
    for r in range(B_GROUP):
        sl = slice(r * LANES, (r + 1) * LANES)
        o_ref[0, :, sl] = ((part_ref[0, :, sl] + sel_slots[r]) + part_ref[1, :, sl]).astype(o_ref.dtype)


def _nsa(qb, gl, kcmp, vcmp, ksl, vsl, bias_s, kwin, vwin, bias_w):
    b, s, w = qb.shape
    nsel = s // SEL_BLOCK
    nck = kcmp.shape[1]
    ratio = SEL_BLOCK // CMP_STRIDE
    wov = np.zeros((nsel, nck), np.float32)
    for jj in range(nsel):
        for off, wt in zip(range(-1, ratio), CMP_OVERLAP):
            n = ratio * jj + off
            if 0 <= n < nck - 1:
                wov[jj, n] = wt
    wov = jnp.asarray(wov)
    blk = jnp.asarray((np.arange(s)[:, None] // SEL_BLOCK) == (np.arange(LANES)[None, :] % HEAD_DIM), BF16)
    rows = B_GROUP * Q_TILE
    tile = lambda width: pl.BlockSpec((1, Q_TILE, width), lambda bi, qi: (bi, qi, 0))
    whole = lambda a: pl.BlockSpec((1,) + a.shape[1:], lambda bi, qi: (bi, 0, 0))
    const = lambda a: pl.BlockSpec(a.shape, lambda bi, qi: (0,) * a.ndim, pipeline_mode=pl.Buffered(1))
    return pl.pallas_call(
        functools.partial(_nsa_kernel, nsel=nsel, n_bias=bias_s.shape[1], kw=WIN_LEN + SUB_TILE),
        grid=(b, s // Q_TILE),
        in_specs=[tile(w), tile(LANES), whole(kcmp), whole(vcmp), const(wov), whole(ksl), const(blk), whole(vsl),
                  const(bias_s), whole(kwin), whole(vwin), const(bias_w)],
        out_specs=tile(w),
        out_shape=jax.ShapeDtypeStruct((b, s, w), BF16),
        scratch_shapes=[pltpu.VMEM((B_KV_HEADS, rows, LANES), F32), pltpu.VMEM((B_KV_HEADS, rows, LANES), F32),
                        pltpu.VMEM((2, B_KV_HEADS, rows, K_TILE), F32), pltpu.VMEM((2, Q_TILE, w), F32),
                        pltpu.VMEM((Q_TILE, LANES), BF16)],
        compiler_params=_cparams("parallel", "parallel"),
        name="nsa",
    )(qb, gl, kcmp, vcmp, wov, ksl, blk, vsl, bias_s, kwin, vwin, bias_w)


def _tail_kernel(x_ref, oa_ref, ob_ref, wa_ref, wb_ref, g2_ref, wg_ref, wu_ref, wd_ref, gf_ref, o_ref):
    o_a = jnp.concatenate([oa_ref[p] for p in range(oa_ref.shape[0])], axis=1)
    mix = jnp.dot(o_a, wa_ref[...], preferred_element_type=F32)
    mix = mix + jnp.dot(ob_ref[...], wb_ref[...], preferred_element_type=F32)
    h = x_ref[...] + mix
    y = h * lax.rsqrt(jnp.mean(h * h, axis=-1, keepdims=True) + RMS_EPS)
    hn = (y * g2_ref[...]).astype(BF16)
    a = jnp.dot(hn, wg_ref[...], preferred_element_type=F32)
    u = jnp.dot(hn, wu_ref[...], preferred_element_type=F32)
    act = (jax.nn.silu(a) * u).astype(BF16)
    h2 = h + jnp.dot(act, wd_ref[...], preferred_element_type=F32)
    y2 = h2 * lax.rsqrt(jnp.mean(h2 * h2, axis=-1, keepdims=True) + RMS_EPS)
    o_ref[...] = y2 * gf_ref[...]


def _tail(x2, oa, ob, wa, wb, g2, wg, wu, wd, gf, tm):
    n, d = x2.shape
    row = lambda width: pl.BlockSpec((tm, width), lambda i: (i, 0))
    const = lambda a: pl.BlockSpec(a.shape, lambda i: (0, 0), pipeline_mode=pl.Buffered(1))
    return pl.pallas_call(
        _tail_kernel,
        grid=(n // tm,),
        in_specs=[row(d), pl.BlockSpec((oa.shape[0], tm, LANES), lambda i: (0, i, 0)), row(ob.shape[1]),
                  const(wa), const(wb), const(g2), const(wg), const(wu), const(wd), const(gf)],
        out_specs=row(d),
        out_shape=jax.ShapeDtypeStruct((n, d), F32),
        compiler_params=_cparams("parallel"),
        name="tail",
    )(x2, oa, ob, wa, wb, g2, wg, wu, wd, gf)


def _slot_perm():
    perm = np.zeros(B_WIDTH, np.int64)
    for r in range(B_GROUP):
        for g in range(B_KV_HEADS):
            src = (g * B_GROUP + r) * HEAD_DIM
            dst = r * LANES + g * HEAD_DIM
            perm[dst:dst + HEAD_DIM] = np.arange(src, src + HEAD_DIM)
    return perm


def _layer(h, tab_a, tab_b, norm1_g, w_in, cmp_pos, k_w1, k_b1, k_w2, v_w1, v_b1, v_w2,
           w_out, norm2_g, w_gate, w_up, w_down, norm_f_g):
    b, s, d = h.shape
    n = b * s
    assert s % SPAN == 0 and s % K_TILE == 0
    perm = _slot_perm()

    cols = np.cumsum([0, A_WIDTH, A_WIDTH, A_WIDTH, B_WIDTH] + [KV_WIDTH] * 6 + [GATE_WIDTH])
    w_aq = w_in[:, cols[0]:cols[1]] * (SCALE * LOG2E)
    w_bq = (w_in[:, cols[3]:cols[4]] * (SCALE * LOG2E))[:, perm]
    w_gl = jnp.pad(w_in[:, cols[10]:cols[11]], ((0, 0), (0, LANES - GATE_WIDTH)))
    w1 = jnp.concatenate([w_aq, w_in[:, cols[1]:cols[3]], w_bq, w_in[:, cols[4]:cols[10]], w_gl], axis=1).astype(BF16)
    widths = (A_WIDTH,) * 3 + (B_WIDTH,) + (KV_WIDTH,) * 6 + (LANES,)
    dtypes = (BF16,) * 4 + (F32,) * 2 + (BF16,) * 4 + (F32,)
    x2 = h.reshape(n, d)
    n_lay = len(MIX_LAYOUTS)
    outs = _inproj(h, norm1_g.reshape(1, d), w1, widths, dtypes, n_mix=3, tm=512)
    qa, ka, va = (outs[i * n_lay:(i + 1) * n_lay] for i in range(3))
    qb, kc, vc, ksl, vsl, kw, vw, gl = outs[3 * n_lay:]
    r3 = lambda t: t.reshape(b, s, t.shape[-1])
    qb, ksl, vsl, kw, vw, gl = map(r3, (qb, ksl, vsl, kw, vw, gl))

    tok_dist, in_window = [], []
    for window, dil in DIL_PATTERNS:
        fan = (dil if dil in MIX_LAYOUTS else MIX_LAYOUTS[0]) // dil
        row = np.arange(DIL_BLOCK)
        step = fan * (row % (DIL_BLOCK // fan)) + row // (DIL_BLOCK // fan)
        key_step = np.concatenate([step, DIL_BLOCK + step])
        dist = DIL_BLOCK + step[:, None] - key_step[None, :]
        tok_dist.append(np.maximum(dist, 0) * dil)
        in_window.append((dist >= 0) & (dist <= window // dil))
    tok_dist, in_window = np.stack(tok_dist), np.stack(in_window)
    bias_a = _bias_tiles(tab_a, tok_dist, in_window, inner=A_HEADS)[0]
    o_a = _mixer_a(qa, ka, va, bias_a).reshape(A_HEADS // 2, n, LANES)

    nch = s // CMP_STRIDE
    kcmp = _compress(kc.reshape(b, s, KV_WIDTH), cmp_pos, k_w1, k_b1, k_w2)
    vcmp = _compress(vc.reshape(b, s, KV_WIDTH), cmp_pos, v_w1, v_b1, v_w2)
    d_sat = int(np.nonzero(_t5_bucket_np(np.arange(s)) < REL_BUCKETS - 1)[0].max()) + 1
    first_far = -(-(d_sat + SUB_TILE - 1) // SUB_TILE)
    n_bias = min(s // SUB_TILE, first_far + 1) + 1
    dd = ((np.arange(n_bias)[:, None, None] - 1) * SUB_TILE + np.arange(SUB_TILE)[None, :, None]
          - np.arange(SUB_TILE)[None, None, :])
    bias_sl = _bias_tiles(tab_b, dd, dd >= 0, inner=B_GROUP)
    bias_sl = bias_sl.reshape(B_KV_HEADS, n_bias, B_GROUP * SUB_TILE, SUB_TILE)
    n_wt = WIN_LEN // SUB_TILE + 1
    dw = ((n_wt - 1 - np.arange(n_wt + 1))[:, None, None] * SUB_TILE + np.arange(SUB_TILE)[None, :, None]
          - np.arange(SUB_TILE)[None, None, :])
    in_win = (dw >= 0) & (dw < WIN_LEN) & (np.arange(n_wt + 1) < n_wt)[:, None, None]
    bias_w = _bias_tiles(tab_b, dw, in_win, inner=B_GROUP)
    bias_w = bias_w.reshape(B_KV_HEADS, n_wt + 1, B_GROUP * SUB_TILE, SUB_TILE)
    o_b = _nsa(qb, gl, kcmp, vcmp, ksl, vsl, bias_sl, kw, vw, bias_w)

    wa = w_out[:A_WIDTH].astype(BF16)
    wb = w_out[A_WIDTH:][perm].astype(BF16)
    out = _tail(x2, o_a, o_b.reshape(n, B_WIDTH), wa, wb, norm2_g.reshape(1, d), w_gate.astype(BF16),
                w_up.astype(BF16), w_down.astype(BF16), norm_f_g.reshape(1, d), tm=512)
    return out.reshape(b, s, d)


def kernel(x, norm1_g, w_in, rel_bias, cmp_pos, cmp_k_w1, cmp_k_b1, cmp_k_w2, cmp_v_w1, cmp_v_b1, cmp_v_w2,
           w_out, norm2_g, w_gate, w_up, w_down, norm_f_g):
    assert w_in.shape[0] == 1, "single-layer model"
    tab_a = rel_bias[:, :A_HEADS].T * LOG2E
    tab_b = rel_bias[:, A_HEADS:].T * LOG2E
    return _layer(x, tab_a, tab_b, norm1_g[0], w_in[0], cmp_pos[0], cmp_k_w1[0], cmp_k_b1[0], cmp_k_w2[0],
                  cmp_v_w1[0], cmp_v_b1[0], cmp_v_w2[0], w_out[0], norm2_g[0], w_gate[0], w_up[0], w_down[0],
                  norm_f_g)
```

```python
import functools
import math

import numpy as np
import jax
import jax.numpy as jnp
from jax import lax
from jax.experimental import pallas as pl
from jax.experimental.pallas import tpu as pltpu

F32 = jnp.float32
BF16 = jnp.bfloat16

HEAD_DIM = 64
LANES = 128
A_HEADS = 8
DIL_PATTERNS = ((128, 1), (512, 4), (2048, 16))
DIL_BLOCK = 128
B_HEADS = 8
B_KV_HEADS = 2
B_GROUP = B_HEADS // B_KV_HEADS
CMP_LEN = 32
CMP_STRIDE = 16
CMP_HIDDEN = 128
SEL_BLOCK = 64
SEL_TOPN = 16
CMP_OVERLAP = (1.0, 2.0, 2.0, 2.0, 1.0)
WIN_LEN = 512
N_BRANCH = 3
REL_BUCKETS = 32
REL_MAX_DIST = 2048
RMS_EPS = 1e-6
A_WIDTH = A_HEADS * HEAD_DIM
B_WIDTH = B_HEADS * HEAD_DIM
KV_WIDTH = B_KV_HEADS * HEAD_DIM
GATE_WIDTH = B_HEADS * N_BRANCH
SCALE = HEAD_DIM ** -0.5
LOG2E = math.log2(math.e)

SPAN = DIL_PATTERNS[-1][1] * DIL_BLOCK
MIX_LAYOUTS = tuple(d for _, d in DIL_PATTERNS if d > 1)
NARROW_FROM = 4
RANK_CHUNK = 16
SUB_TILE = 128
Q_TILE = 512
K_TILE = 512
SCORE_TILE = 256
NOT_SELECTED = -(2.0 ** 100)
M_INIT = -1e30
VMEM_LIMIT = 56 * 1024 * 1024

NT_DIMS = (((1,), (1,)), ((), ()))


def _cparams(*sem):
    return pltpu.CompilerParams(dimension_semantics=sem, vmem_limit_bytes=VMEM_LIMIT)


def _t5_bucket_np(dist):
    max_exact = REL_BUCKETS // 2
    d = np.asarray(dist)
    df = np.maximum(d, 1).astype(np.float32)
    large = max_exact + (np.log(df / np.float32(max_exact)) / np.float32(math.log(REL_MAX_DIST / max_exact))
                         * np.float32(REL_BUCKETS - max_exact)).astype(np.int32)
    large = np.minimum(large, REL_BUCKETS - 1)
    return np.where(d < max_exact, d, large).astype(np.int32)


def _bias_kernel(tab_ref, idx_ref, o_ref):
    inner = o_ref.shape[2]
    idx = idx_ref[0]
    accs = [jnp.full(idx.shape, -jnp.inf, F32) for _ in range(inner)]
    for bucket in range(REL_BUCKETS):
        hit = idx == bucket
        for hi in range(inner):
            head = pl.program_id(0) * inner + hi
            accs[hi] = jnp.where(hit, tab_ref[head * REL_BUCKETS + bucket], accs[hi])
    for hi in range(inner):
        o_ref[0, 0, hi] = accs[hi]


def _bias_tiles(tab, dist, valid, inner):
    h = tab.shape[0]
    t, r, c = dist.shape
    idx = jnp.asarray(np.where(valid, _t5_bucket_np(np.maximum(dist, 0)), -1).astype(np.int32))
    return pl.pallas_call(
        _bias_kernel,
        grid=(h // inner, t),
        in_specs=[pl.BlockSpec(memory_space=pltpu.SMEM),
                  pl.BlockSpec((1, r, c), lambda a, ti: (ti, 0, 0))],
        out_specs=pl.BlockSpec((1, 1, inner, r, c), lambda a, ti: (a, ti, 0, 0, 0)),
        out_shape=jax.ShapeDtypeStruct((h // inner, t, inner, r, c), F32),
        compiler_params=_cparams("parallel", "parallel"),
        name="bias_tiles",
    )(tab.reshape(-1), idx)


def _half_mask(g):
    lane = lax.broadcasted_iota(jnp.int32, (1, LANES), 1)
    return (lane >= HEAD_DIM) if g else (lane < HEAD_DIM)


def _inproj_kernel(x_ref, g_ref, w_ref, *refs, widths, n_mix):
    n_dil = len(DIL_PATTERNS)
    n_lay = len(MIX_LAYOUTS)
    n_scr = n_mix * (n_dil - 1)
    mix_refs = [(None,) * (n_dil - n_lay) + tuple(refs[i * n_lay:(i + 1) * n_lay]) for i in range(n_mix)]
    out_refs = refs[n_mix * n_lay:len(refs) - n_scr]
    scratch = [refs[len(refs) - n_scr + i * (n_dil - 1):len(refs) - n_scr + (i + 1) * (n_dil - 1)]
               for i in range(n_mix)]
    x = x_ref[...]
    tm = x.shape[0]
    y = x * lax.rsqrt(jnp.mean(x * x, axis=-1, keepdims=True) + RMS_EPS)
    xn = (y * g_ref[...]).astype(BF16)
    merged = jnp.dot(xn, w_ref[:, sum(widths[:NARROW_FROM]):], preferred_element_type=F32)
    start = 0
    for idx, w in enumerate(widths):
        if idx < NARROW_FROM:
            r = jnp.dot(xn, w_ref[:, start:start + w], preferred_element_type=F32)
        else:
            off = start - sum(widths[:NARROW_FROM])
            r = merged[:, off:off + w]
        if idx < n_mix:
            prev_dil = None
            for pi, (o_ref, (_, dil)) in enumerate(zip(mix_refs[idx], DIL_PATTERNS)):
                keep = scratch[idx][pi] if pi < n_dil - 1 else None
                for p in range(w // LANES):
                    if prev_dil is None:
                        planes = [(0, r[:, p * LANES:(p + 1) * LANES])]
                    else:
                        f = dil // prev_dil
                        planes = [(res + prev_dil * a,
                                   scratch[idx][pi - 1][p, res, pl.ds(a, tm // dil, stride=f), :])
                                  for res in range(prev_dil) for a in range(f)]
                    for res, rows in planes:
                        if o_ref is not None:
                            o_ref[p, 0, res] = rows.astype(o_ref.dtype)
                        if keep is not None:
                            keep[p, res] = rows
                prev_dil = dil
        else:
            o_ref = out_refs[idx - n_mix]
            o_ref[...] = r.astype(o_ref.dtype)
        start += w


def _inproj(x3, g, w, widths, dtypes, n_mix, tm):
    b, s, d = x3.shape
    n = b * s
    per_batch = s // tm
    out_shape, out_specs = [], []
    for w_, dt in zip(widths[:n_mix], dtypes[:n_mix]):
        for dil in MIX_LAYOUTS:
            out_shape.append(jax.ShapeDtypeStruct((w_ // LANES, b, dil, s // dil, LANES), dt))
            out_specs.append(pl.BlockSpec((w_ // LANES, 1, dil, tm // dil, LANES),
                                          lambda i: (0, i // per_batch, 0, i % per_batch, 0)))
    for w_, dt in zip(widths[n_mix:], dtypes[n_mix:]):
        out_shape.append(jax.ShapeDtypeStruct((n, w_), dt))
        out_specs.append(pl.BlockSpec((tm, w_), lambda i: (i, 0)))
    return pl.pallas_call(
        functools.partial(_inproj_kernel, widths=widths, n_mix=n_mix),
        grid=(n // tm,),
        in_specs=[pl.BlockSpec((tm, d), lambda i: (i, 0)),
                  pl.BlockSpec((1, d), lambda i: (0, 0)),
                  pl.BlockSpec(w.shape, lambda i: (0, 0))],
        out_specs=out_specs,
        out_shape=out_shape,
        scratch_shapes=[pltpu.VMEM((w_ // LANES, dil, tm // dil, LANES), F32)
                        for w_ in widths[:n_mix] for _, dil in DIL_PATTERNS[:-1]],
        compiler_params=_cparams("parallel"),
        name="inproj",
    )(x3.reshape(n, d), g, w)


def _mixer_a_kernel(*refs):
    bias_ref, o_ref, acc_scr, m_scr = refs[5 * len(MIX_LAYOUTS):]
    sb = pl.program_id(2)
    halves = [_half_mask(hh) for hh in range(2)]
    col = lax.broadcasted_iota(jnp.int32, (1, 2 * DIL_BLOCK), 1)
    first_keep = (col >= DIL_BLOCK) | (sb > 0)
    ones = jnp.ones((2 * DIL_BLOCK, LANES), BF16)
    n_blocks = SPAN // DIL_BLOCK
    def block(p, r, n):
        dil = DIL_PATTERNS[p][1]
        lay = MIX_LAYOUTS.index(dil) if dil in MIX_LAYOUTS else 0
        src = MIX_LAYOUTS[lay]
        fan = src // dil
        piece = DIL_BLOCK // fan
        q_ref, kp_ref, kc_ref, vp_ref, vc_ref = refs[5 * lay:5 * lay + 5]
        last = SPAN // src // piece - 1

        def gather(ref, nn):
            return jnp.concatenate([ref[0, 0, r + dil * a, nn * piece:(nn + 1) * piece, :] for a in range(fan)],
                                   axis=0)

        q2 = gather(q_ref, n)
        if n == 0:
            k_prev, v_prev = gather(kp_ref, last), gather(vp_ref, last)
        else:
            k_prev, v_prev = gather(kc_ref, n - 1), gather(vc_ref, n - 1)
        kcat = jnp.concatenate([k_prev, gather(kc_ref, n)], axis=0)
        vcat = jnp.concatenate([v_prev, gather(vc_ref, n)], axis=0)
        qm = jnp.concatenate([jnp.where(halves[hh], q2, jnp.zeros_like(q2)) for hh in range(2)], axis=0)
        s2 = lax.dot_general(qm, kcat, NT_DIMS, preferred_element_type=F32)
        for hh in range(2):
            s = s2[hh * DIL_BLOCK:(hh + 1) * DIL_BLOCK] + bias_ref[p, hh]
            if n == 0:
                s = jnp.where(first_keep, s, -jnp.inf)
            m_blk = jnp.max(s, axis=-1, keepdims=True)
            pe = jnp.exp2(s - m_blk)
            pv = jnp.dot(pe.astype(BF16), jnp.where(halves[hh], vcat, ones), preferred_element_type=F32)
            m_b = jnp.broadcast_to(m_blk, pv.shape)
            for a in range(fan):
                rows_t = pl.ds(n * dil * DIL_BLOCK + r + dil * a, piece, stride=src)
                acc_scr[p, hh, rows_t, :] = pv[a * piece:(a + 1) * piece]
                m_scr[p, hh, rows_t, :] = m_b[a * piece:(a + 1) * piece]

    todo = [[(p, r, n) for r in range(dil) for n in range(SPAN // (dil * DIL_BLOCK))]
            for p, (_, dil) in enumerate(DIL_PATTERNS)]
    for i in range(n_blocks):
        for tasks in todo:
            block(*tasks[i])

    def finish(c, carry):
        rows = pl.ds(pl.multiple_of(c * DIL_BLOCK, DIL_BLOCK), DIL_BLOCK)
        tots = []
        for hh in range(2):
            ms = [m_scr[p, hh, rows, :] for p in range(len(DIL_PATTERNS))]
            m_all = functools.reduce(jnp.maximum, ms)
            tots.append(sum(jnp.exp2(m - m_all) * acc_scr[p, hh, rows, :] for p, m in enumerate(ms)))
        num = jnp.where(halves[0], tots[0], tots[1])
        den = pltpu.roll(jnp.where(halves[0], tots[1], tots[0]), HEAD_DIM, axis=1)
        o_ref[0, 0, rows, :] = (num / den).astype(o_ref.dtype)
        return carry

    lax.fori_loop(0, n_blocks, finish, 0, unroll=4)


def _mixer_a(qs, ks, vs, bias):
    npair, b, dil0, rows0, _ = qs[0].shape
    s = dil0 * rows0
    cur = lambda bi, pi, si: (pi, bi, 0, si, 0)
    prev = lambda bi, pi, si: (pi, bi, 0, jnp.maximum(si - 1, 0), 0)
    in_specs, operands = [], []
    for dil, q, k, v in zip(MIX_LAYOUTS, qs, ks, vs):
        blk = (1, 1, dil, SPAN // dil, LANES)
        in_specs += [pl.BlockSpec(blk, cur), pl.BlockSpec(blk, prev), pl.BlockSpec(blk, cur),
                     pl.BlockSpec(blk, prev), pl.BlockSpec(blk, cur)]
        operands += [q, k, k, v, v]
    in_specs.append(pl.BlockSpec((bias.shape[0], 2) + bias.shape[2:], lambda bi, pi, si: (0, pi, 0, 0)))
    return pl.pallas_call(
        _mixer_a_kernel,
        grid=(b, npair, s // SPAN),
        in_specs=in_specs,
        out_specs=pl.BlockSpec((1, 1, SPAN, LANES), lambda bi, pi, si: (pi, bi, si, 0)),
        out_shape=jax.ShapeDtypeStruct((npair, b, s, LANES), BF16),
        scratch_shapes=[pltpu.VMEM((len(DIL_PATTERNS), 2, SPAN, LANES), F32),
                        pltpu.VMEM((len(DIL_PATTERNS), 2, SPAN, LANES), F32)],
        compiler_params=_cparams("parallel", "parallel", "parallel"),
        name="mixer_a",
    )(*operands, bias)


def _compress_kernel(c_ref, pa_ref, pb_ref, w1a_ref, w1b_ref, b1_ref, w2_ref, o_ref):
    nch = o_ref.shape[1]
    c = jnp.concatenate([c_ref[0, pl.ds(l, nch, stride=CMP_STRIDE), :] for l in range(CMP_STRIDE)], axis=1)
    xa = (c + pa_ref[...]).astype(BF16)
    xb = (c + pb_ref[...]).astype(BF16)
    ha = jnp.dot(xa, w1a_ref[...], preferred_element_type=F32)
    hb = jnp.dot(xb, w1b_ref[...], preferred_element_type=F32)
    hb_next = jnp.concatenate([hb[1:], jnp.zeros_like(hb[:1])], axis=0)
    hid = jax.nn.gelu(ha + hb_next + b1_ref[...])
    o_ref[0] = jnp.dot(hid.astype(BF16), w2_ref[...], preferred_element_type=F32).astype(o_ref.dtype)


def _compress(c, pos, w1, b1, w2):
    b, s, _ = c.shape
    nch = s // CMP_STRIDE
    half = CMP_LEN // 2
    zero = jnp.zeros((half, HEAD_DIM, CMP_HIDDEN), F32)

    def grouped(wpart):
        g0 = jnp.concatenate([wpart, zero], axis=1).reshape(half * LANES, CMP_HIDDEN)
        g1 = jnp.concatenate([zero, wpart], axis=1).reshape(half * LANES, CMP_HIDDEN)
        return jnp.concatenate([g0, g1], axis=1).astype(BF16)

    w1a, w1b = grouped(w1[:half]), grouped(w1[half:])
    pa = jnp.tile(pos[:half], (1, 2)).reshape(1, half * LANES)
    pb = jnp.tile(pos[half:], (1, 2)).reshape(1, half * LANES)
    b1g = jnp.tile(b1, 2).reshape(1, 2 * CMP_HIDDEN)
    zw = jnp.zeros_like(w2)
    w2g = jnp.concatenate([jnp.concatenate([w2, zw], axis=1),
                           jnp.concatenate([zw, w2], axis=1)], axis=0).astype(BF16)
    full = lambda a: pl.BlockSpec(a.shape, lambda i: (0,) * a.ndim)
    return pl.pallas_call(
        _compress_kernel,
        grid=(b,),
        in_specs=[pl.BlockSpec((1,) + c.shape[1:], lambda i: (i, 0, 0)),
                  full(pa), full(pb), full(w1a), full(w1b), full(b1g), full(w2g)],
        out_specs=pl.BlockSpec((1, nch, LANES), lambda i: (i, 0, 0)),
        out_shape=jax.ShapeDtypeStruct((b, nch, LANES), BF16),
        compiler_params=_cparams("parallel"),
        name="compress",
    )(c, pa, pb, w1a, w1b, b1g, w2g)


def _nsa_kernel(q_ref, gl_ref, kc_ref, vc_ref, wov_ref, ks_ref, blk_ref, vs_ref, bias_s_ref, kw_ref, vw_ref, bias_w_ref,
                o_ref, acc_ref, m_ref, s_ref, part_ref, sb_ref, *, nsel, n_bias, kw):
    qt = pl.program_id(1)
    tq = q_ref.shape[1]
    n_sub = tq // SUB_TILE
    sub_rows = B_GROUP * SUB_TILE
    gate = jax.nn.sigmoid(gl_ref[0])
    halves = [_half_mask(g) for g in range(B_KV_HEADS)]
    q_slots = [q_ref[0, :, r * LANES:(r + 1) * LANES] for r in range(B_GROUP)]
    zero = jnp.zeros((tq, LANES), BF16)

    def stacked(slots):
        return jnp.concatenate([slots[r][u * SUB_TILE:(u + 1) * SUB_TILE]
                                for u in range(n_sub) for r in range(B_GROUP)], axis=0)

    def head_rows(o, r):
        return jnp.concatenate([o[(u * B_GROUP + r) * SUB_TILE:(u * B_GROUP + r + 1) * SUB_TILE]
                                for u in range(n_sub)], axis=0)

    def gate_tile(r, branch):
        c0, c1 = (r * N_BRANCH + branch, (B_GROUP + r) * N_BRANCH + branch)
        return jnp.where(halves[0], gate[:, c0:c0 + 1], gate[:, c1:c1 + 1])

    def gated_slots(o, g, branch, slots):
        for r in range(B_GROUP):
            c = (g * B_GROUP + r) * N_BRANCH + branch
            o_r = head_rows(o, r) * gate[:, c:c + 1]
            slots[r] = o_r if slots[r] is None else jnp.where(halves[g], o_r, slots[r])

    def normalised_slots(accs, branch):
        slots = []
        for r in range(B_GROUP):
            a0, a1 = head_rows(accs[0], r), head_rows(accs[1], r)
            num = jnp.where(halves[0], a0, a1)
            den = pltpu.roll(jnp.where(halves[0], a1, a0), HEAD_DIM, axis=1)
            slots.append(num / jnp.maximum(den, 1e-30) * gate_tile(r, branch))
        return slots

    nck = kc_ref.shape[1]
    row = lax.broadcasted_iota(jnp.int32, (n_sub * sub_rows, 1), 0)
    t_row = qt * tq + (row // sub_rows) * SUB_TILE + row % SUB_TILE
    blk_end = lax.broadcasted_iota(jnp.int32, (1, nck), 1) * CMP_STRIDE + (CMP_LEN - 1)
    valid = blk_end <= t_row
    kc = kc_ref[0]
    vc = vc_ref[0]
    pc = []
    cmp_slots = [None] * B_GROUP
    for g in range(B_KV_HEADS):
        qg = stacked([jnp.where(halves[g], q_slots[r], zero) for r in range(B_GROUP)])
        s = lax.dot_general(qg, kc, NT_DIMS, preferred_element_type=F32)
        s = jnp.where(valid, s, -jnp.inf)
        m = jnp.max(s, axis=-1, keepdims=True)
        m = jnp.where(m == -jnp.inf, 0.0, m)
        p = jnp.exp2(s - m)
        den = jnp.sum(p, axis=-1, keepdims=True)
        p = p / jnp.maximum(den, 1e-30)
        pc.append(jnp.concatenate(
            [sum(p[(u * B_GROUP + r) * SUB_TILE:(u * B_GROUP + r + 1) * SUB_TILE] for r in range(B_GROUP))
             for u in range(n_sub)], axis=0))
        gated_slots(jnp.dot(p.astype(BF16), vc, preferred_element_type=F32), g, 0, cmp_slots)
    for r in range(B_GROUP):
        part_ref[0, :, r * LANES:(r + 1) * LANES] = cmp_slots[r]

    t_lane = qt * tq + lax.broadcasted_iota(jnp.int32, (1, tq), 1)
    cur = t_lane // SEL_BLOCK
    j = lax.broadcasted_iota(jnp.int32, (nsel, 1), 0)
    forced = (j == 0) | (j == cur) | (j == cur - 1)
    imps = []
    for g in reversed(range(B_KV_HEADS)):
        imp = lax.dot_general(wov_ref[...], pc[g], NT_DIMS, preferred_element_type=F32,
                              precision=lax.Precision.HIGHEST)
        imps.append(jnp.where(j > cur, -jnp.inf, jnp.where(forced, jnp.inf, imp)))

    last_block = (qt * tq + tq - 1) // SEL_BLOCK
    for level in range(-(-nsel // RANK_CHUNK)):
        n_live = min((level + 1) * RANK_CHUNK, nsel)

        @pl.when(last_block // RANK_CHUNK == level)
        def _(n_live=n_live):
            blocks = []
            for imp in imps:
                if n_live <= SEL_TOPN:
                    blocks.append(jnp.zeros((HEAD_DIM, tq), F32))
                    continue
                live = imp[:n_live]
                jl = j[:n_live]
                rank = jnp.zeros((n_live, tq), jnp.int32)
                for jp in range(n_live):
                    row = live[jp:jp + 1, :]
                    rank = rank + jnp.where(jl > jp, (row >= live).astype(jnp.int32), (row > live).astype(jnp.int32))
                blocks.append(jnp.where(rank < SEL_TOPN, 0.0, NOT_SELECTED))
                if n_live < HEAD_DIM:
                    blocks.append(jnp.zeros((HEAD_DIM - n_live, tq), F32))
            sb_ref[...] = jnp.concatenate(blocks, axis=0).T.astype(sb_ref.dtype)

    sb = sb_ref[...]
    qaug = [stacked([jnp.where(halves[g], q_slots[r], sb) for r in range(B_GROUP)])
            for g in range(B_KV_HEADS)]
    n_steps = (qt * tq + tq + K_TILE - 1) // K_TILE
    acc_ref[...] = jnp.zeros_like(acc_ref)
    m_ref[...] = jnp.full_like(m_ref, M_INIT)

    def scores(kt):
        for jt in range(K_TILE // SCORE_TILE):
            k0 = pl.multiple_of(kt * K_TILE + jt * SCORE_TILE, SCORE_TILE)
            k2 = ks_ref[0, pl.ds(k0, SCORE_TILE), :]
            e2 = blk_ref[pl.ds(k0, SCORE_TILE), :]
            for g in range(B_KV_HEADS):
                k = jnp.where(halves[g], k2, e2)
                s = lax.dot_general(qaug[g], k, NT_DIMS, preferred_element_type=F32)
                for u in range(n_sub):
                    rows_u = slice(u * sub_rows, (u + 1) * sub_rows)
                    for c in range(SCORE_TILE // SUB_TILE):
                        bi = jnp.clip(qt * n_sub + u - k0 // SUB_TILE - c + 1, 0, n_bias - 1)
                        col = jt * SCORE_TILE + c * SUB_TILE
                        s_ref[kt % 2, g, rows_u, col:col + SUB_TILE] = (
                            s[rows_u, c * SUB_TILE:(c + 1) * SUB_TILE] + bias_s_ref[g, bi])

    def accumulate(kt):
        k0 = pl.multiple_of(kt * K_TILE, K_TILE)
        v2 = vs_ref[0, pl.ds(k0, K_TILE), :]
        for g in range(B_KV_HEADS):
            v = jnp.where(halves[g], v2, jnp.ones((K_TILE, LANES), BF16))
            s = s_ref[kt % 2, g]
            m_old = m_ref[g]
            m_new = jnp.maximum(m_old, jnp.max(s, axis=-1, keepdims=True))
            alpha = jnp.exp2(m_old - m_new)
            p = jnp.exp2(s - jnp.tile(m_new, (1, K_TILE // LANES)))
            acc_ref[g] = alpha * acc_ref[g] + jnp.dot(p.astype(BF16), v, preferred_element_type=F32)
            m_ref[g] = m_new

    def body(kt, carry):
        accumulate(kt)
        scores(kt + 1)
        return carry

    n_wt = kw // SUB_TILE
    q_win = [stacked([jnp.where(halves[g], q_slots[r], zero) for r in range(B_GROUP)]) for g in range(B_KV_HEADS)]
    o_win = [[] for _ in range(B_KV_HEADS)]
    for u in range(n_sub):
        sub = qt * n_sub + u
        w0 = pl.multiple_of(jnp.maximum(sub * SUB_TILE - WIN_LEN, 0), SUB_TILE)
        skip = jnp.maximum(WIN_LEN // SUB_TILE - sub, 0)
        kwin = kw_ref[0, pl.ds(w0, kw), :]
        vwin = vw_ref[0, pl.ds(w0, kw), :]
        for g in range(B_KV_HEADS):
            s = lax.dot_general(q_win[g][u * sub_rows:(u + 1) * sub_rows], kwin, NT_DIMS, preferred_element_type=F32)
            s = jnp.concatenate([s[:, jw * SUB_TILE:(jw + 1) * SUB_TILE] + bias_w_ref[g, jnp.minimum(jw + skip, n_wt)]
                                 for jw in range(n_wt)], axis=1)
            p = jnp.exp2(s - jnp.max(s, axis=-1, keepdims=True))
            acc = jnp.dot(p.astype(BF16), jnp.where(halves[g], vwin, jnp.ones_like(vwin)),
                          preferred_element_type=F32)
            o_win[g].append(acc)
    win_slots = normalised_slots([jnp.concatenate(o_win[g], axis=0) for g in range(B_KV_HEADS)], 2)
    for r in range(B_GROUP):
        part_ref[1, :, r * LANES:(r + 1) * LANES] = win_slots[r]

    scores(0)
    lax.fori_loop(0, n_steps - 1, body, 0)
    accumulate(n_steps - 1)
    sel_slots = normalised_slots([acc_ref[g] for g in range(B_KV_HEADS)], 1)
    for r in range(B_GROUP):
        sl = slice(r * LANES, (r + 1) * LANES)
        o_ref[0, :, sl] = ((part_ref[0, :, sl] + sel_slots[r]) + part_ref[1, :, sl]).astype(o_ref.dtype)


def _nsa(qb, gl, kcmp, vcmp, ksl, vsl, bias_s, kwin, vwin, bias_w):
    b, s, w = qb.shape
    nsel = s // SEL_BLOCK
    nck = kcmp.shape[1]
    ratio = SEL_BLOCK // CMP_STRIDE
    wov = np.zeros((nsel, nck), np.float32)
    for jj in range(nsel):
        for off, wt in zip(range(-1, ratio), CMP_OVERLAP):
            n = ratio * jj + off
            if 0 <= n < nck - 1:
                wov[jj, n] = wt
    wov = jnp.asarray(wov)
    blk = jnp.asarray((np.arange(s)[:, None] // SEL_BLOCK) == (np.arange(LANES)[None, :] % HEAD_DIM), BF16)
    rows = B_GROUP * Q_TILE
    tile = lambda width: pl.BlockSpec((1, Q_TILE, width), lambda bi, qi: (bi, qi, 0))
    whole = lambda a: pl.BlockSpec((1,) + a.shape[1:], lambda bi, qi: (bi, 0, 0))
    const = lambda a: pl.BlockSpec(a.shape, lambda bi, qi: (0,) * a.ndim, pipeline_mode=pl.Buffered(1))
    return pl.pallas_call(
        functools.partial(_nsa_kernel, nsel=nsel, n_bias=bias_s.shape[1], kw=WIN_LEN + SUB_TILE),
        grid=(b, s // Q_TILE),
        in_specs=[tile(w), tile(LANES), whole(kcmp), whole(vcmp), const(wov), whole(ksl), const(blk), whole(vsl),
                  const(bias_s), whole(kwin), whole(vwin), const(bias_w)],
        out_specs=tile(w),
        out_shape=jax.ShapeDtypeStruct((b, s, w), BF16),
        scratch_shapes=[pltpu.VMEM((B_KV_HEADS, rows, LANES), F32), pltpu.VMEM((B_KV_HEADS, rows, LANES), F32),
                        pltpu.VMEM((2, B_KV_HEADS, rows, K_TILE), F32), pltpu.VMEM((2, Q_TILE, w), F32),
                        pltpu.VMEM((Q_TILE, LANES), BF16)],
        compiler_params=_cparams("parallel", "parallel"),
        name="nsa",
    )(qb, gl, kcmp, vcmp, wov, ksl, blk, vsl, bias_s, kwin, vwin, bias_w)


def _tail_kernel(x_ref, oa_ref, ob_ref, wa_ref, wb_ref, g2_ref, wg_ref, wu_ref, wd_ref, gf_ref, o_ref):
    o_a = jnp.concatenate([oa_ref[p] for p in range(oa_ref.shape[0])], axis=1)
    mix = jnp.dot(o_a, wa_ref[...], preferred_element_type=F32)
    mix = mix + jnp.dot(ob_ref[...], wb_ref[...], preferred_element_type=F32)
    h = x_ref[...] + mix
    y = h * lax.rsqrt(jnp.mean(h * h, axis=-1, keepdims=True) + RMS_EPS)
    hn = (y * g2_ref[...]).astype(BF16)
    a = jnp.dot(hn, wg_ref[...], preferred_element_type=F32)
    u = jnp.dot(hn, wu_ref[...], preferred_element_type=F32)
    act = (jax.nn.silu(a) * u).astype(BF16)
    h2 = h + jnp.dot(act, wd_ref[...], preferred_element_type=F32)
    y2 = h2 * lax.rsqrt(jnp.mean(h2 * h2, axis=-1, keepdims=True) + RMS_EPS)
    o_ref[...] = y2 * gf_ref[...]


def _tail(x2, oa, ob, wa, wb, g2, wg, wu, wd, gf, tm):
    n, d = x2.shape
    row = lambda width: pl.BlockSpec((tm, width), lambda i: (i, 0))
    const = lambda a: pl.BlockSpec(a.shape, lambda i: (0, 0), pipeline_mode=pl.Buffered(1))
    return pl.pallas_call(
        _tail_kernel,
        grid=(n // tm,),
        in_specs=[row(d), pl.BlockSpec((oa.shape[0], tm, LANES), lambda i: (0, i, 0)), row(ob.shape[1]),
                  const(wa), const(wb), const(g2), const(wg), const(wu), const(wd), const(gf)],
        out_specs=row(d),
        out_shape=jax.ShapeDtypeStruct((n, d), F32),
        compiler_params=_cparams("parallel"),
        name="tail",
    )(x2, oa, ob, wa, wb, g2, wg, wu, wd, gf)


def _slot_perm():
    perm = np.zeros(B_WIDTH, np.int64)
    for r in range(B_GROUP):
        for g in range(B_KV_HEADS):
            src = (g * B_GROUP + r) * HEAD_DIM
            dst = r * LANES + g * HEAD_DIM
            perm[dst:dst + HEAD_DIM] = np.arange(src, src + HEAD_DIM)
    return perm


def _layer(h, tab_a, tab_b, norm1_g, w_in, cmp_pos, k_w1, k_b1, k_w2, v_w1, v_b1, v_w2,
           w_out, norm2_g, w_gate, w_up, w_down, norm_f_g):
    b, s, d = h.shape
    n = b * s
    assert s % SPAN == 0 and s % K_TILE == 0
    perm = _slot_perm()

    cols = np.cumsum([0, A_WIDTH, A_WIDTH, A_WIDTH, B_WIDTH] + [KV_WIDTH] * 6 + [GATE_WIDTH])
    w_aq = w_in[:, cols[0]:cols[1]] * (SCALE * LOG2E)
    w_bq = (w_in[:, cols[3]:cols[4]] * (SCALE * LOG2E))[:, perm]
    w_gl = jnp.pad(w_in[:, cols[10]:cols[11]], ((0, 0), (0, LANES - GATE_WIDTH)))
    w1 = jnp.concatenate([w_aq, w_in[:, cols[1]:cols[3]], w_bq, w_in[:, cols[4]:cols[10]], w_gl], axis=1).astype(BF16)
    widths = (A_WIDTH,) * 3 + (B_WIDTH,) + (KV_WIDTH,) * 6 + (LANES,)
    dtypes = (BF16,) * 4 + (F32,) * 2 + (BF16,) * 4 + (F32,)
    x2 = h.reshape(n, d)
    n_lay = len(MIX_LAYOUTS)
    outs = _inproj(h, norm1_g.reshape(1, d), w1, widths, dtypes, n_mix=3, tm=512)
    qa, ka, va = (outs[i * n_lay:(i + 1) * n_lay] for i in range(3))
    qb, kc, vc, ksl, vsl, kw, vw, gl = outs[3 * n_lay:]
    r3 = lambda t: t.reshape(b, s, t.shape[-1])
    qb, ksl, vsl, kw, vw, gl = map(r3, (qb, ksl, vsl, kw, vw, gl))

    tok_dist, in_window = [], []
    for window, dil in DIL_PATTERNS:
        fan = (dil if dil in MIX_LAYOUTS else MIX_LAYOUTS[0]) // dil
        row = np.arange(DIL_BLOCK)
        step = fan * (row % (DIL_BLOCK // fan)) + row // (DIL_BLOCK // fan)
        key_step = np.concatenate([step, DIL_BLOCK + step])
        dist = DIL_BLOCK + step[:, None] - key_step[None, :]
        tok_dist.append(np.maximum(dist, 0) * dil)
        in_window.append((dist >= 0) & (dist <= window // dil))
    tok_dist, in_window = np.stack(tok_dist), np.stack(in_window)
    bias_a = _bias_tiles(tab_a, tok_dist, in_window, inner=A_HEADS)[0]
    o_a = _mixer_a(qa, ka, va, bias_a).reshape(A_HEADS // 2, n, LANES)

    nch = s // CMP_STRIDE
    kcmp = _compress(kc.reshape(b, s, KV_WIDTH), cmp_pos, k_w1, k_b1, k_w2)
    vcmp = _compress(vc.reshape(b, s, KV_WIDTH), cmp_pos, v_w1, v_b1, v_w2)
    d_sat = int(np.nonzero(_t5_bucket_np(np.arange(s)) < REL_BUCKETS - 1)[0].max()) + 1
    first_far = -(-(d_sat + SUB_TILE - 1) // SUB_TILE)
    n_bias = min(s // SUB_TILE, first_far + 1) + 1
    dd = ((np.arange(n_bias)[:, None, None] - 1) * SUB_TILE + np.arange(SUB_TILE)[None, :, None]
          - np.arange(SUB_TILE)[None, None, :])
    bias_sl = _bias_tiles(tab_b, dd, dd >= 0, inner=B_GROUP)
    bias_sl = bias_sl.reshape(B_KV_HEADS, n_bias, B_GROUP * SUB_TILE, SUB_TILE)
    n_wt = WIN_LEN // SUB_TILE + 1
    dw = ((n_wt - 1 - np.arange(n_wt + 1))[:, None, None] * SUB_TILE + np.arange(SUB_TILE)[None, :, None]
          - np.arange(SUB_TILE)[None, None, :])
    in_win = (dw >= 0) & (dw < WIN_LEN) & (np.arange(n_wt + 1) < n_wt)[:, None, None]
    bias_w = _bias_tiles(tab_b, dw, in_win, inner=B_GROUP)
    bias_w = bias_w.reshape(B_KV_HEADS, n_wt + 1, B_GROUP * SUB_TILE, SUB_TILE)
    o_b = _nsa(qb, gl, kcmp, vcmp, ksl, vsl, bias_sl, kw, vw, bias_w)

    wa = w_out[:A_WIDTH].astype(BF16)
    wb = w_out[A_WIDTH:][perm].astype(BF16)
    out = _tail(x2, o_a, o_b.reshape(n, B_WIDTH), wa, wb, norm2_g.reshape(1, d), w_gate.astype(BF16),
                w_up.astype(BF16), w_down.astype(BF16), norm_f_g.reshape(1, d), tm=512)
    return out.reshape(b, s, d)


def kernel(x, norm1_g, w_in, rel_bias, cmp_pos, cmp_k_w1, cmp_k_b1, cmp_k_w2, cmp_v_w1, cmp_v_b1, cmp_v_w2,
           w_out, norm2_g, w_gate, w_up, w_down, norm_f_g):
    assert w_in.shape[0] == 1, "single-layer model"
    tab_a = rel_bias[:, :A_HEADS].T * LOG2E
    tab_b = rel_bias[:, A_HEADS:].T * LOG2E
    return _layer(x, tab_a, tab_b, norm1_g[0], w_in[0], cmp_pos[0], cmp_k_w1[0], cmp_k_b1[0], cmp_k_w2[0],
                  cmp_v_w1[0], cmp_v_b1[0], cmp_v_w2[0], w_out[0], norm2_g[0], w_gate[0], w_up[0], w_down[0],
                  norm_f_g)
```

```python
import functools
import math

import numpy as np
import jax
import jax.numpy as jnp
from jax import lax
from jax.experimental import pallas as pl
from jax.experimental.pallas import tpu as pltpu

F32 = jnp.float32
BF16 = jnp.bfloat16

HEAD_DIM = 64
LANES = 128
A_HEADS = 8
DIL_PATTERNS = ((128, 1), (512, 4), (2048, 16))
DIL_BLOCK = 128
B_HEADS = 8
B_KV_HEADS = 2
B_GROUP = B_HEADS // B_KV_HEADS
CMP_LEN = 32
CMP_STRIDE = 16
CMP_HIDDEN = 128
SEL_BLOCK = 64
SEL_TOPN = 16
CMP_OVERLAP = (1.0, 2.0, 2.0, 2.0, 1.0)
WIN_LEN = 512
N_BRANCH = 3
REL_BUCKETS = 32
REL_MAX_DIST = 2048
RMS_EPS = 1e-6
A_WIDTH = A_HEADS * HEAD_DIM
B_WIDTH = B_HEADS * HEAD_DIM
KV_WIDTH = B_KV_HEADS * HEAD_DIM
GATE_WIDTH = B_HEADS * N_BRANCH
SCALE = HEAD_DIM ** -0.5
LOG2E = math.log2(math.e)

SPAN = DIL_PATTERNS[-1][1] * DIL_BLOCK
MIX_LAYOUTS = tuple(d for _, d in DIL_PATTERNS if d > 1)
NARROW_FROM = 4
RANK_CHUNK = 16
SUB_TILE = 128
Q_TILE = 512
K_TILE = 512
SCORE_TILE = 256
NOT_SELECTED = -(2.0 ** 100)
M_INIT = -1e30
VMEM_LIMIT = 56 * 1024 * 1024

NT_DIMS = (((1,), (1,)), ((), ()))


def _cparams(*sem):
    return pltpu.CompilerParams(dimension_semantics=sem, vmem_limit_bytes=VMEM_LIMIT)


def _t5_bucket_np(dist):
    max_exact = REL_BUCKETS // 2
    d = np.asarray(dist)
    df = np.maximum(d, 1).astype(np.float32)
    large = max_exact + (np.log(df / np.float32(max_exact)) / np.float32(math.log(REL_MAX_DIST / max_exact))
                         * np.float32(REL_BUCKETS - max_exact)).astype(np.int32)
    large = np.minimum(large, REL_BUCKETS - 1)
    return np.where(d < max_exact, d, large).astype(np.int32)


def _bias_kernel(tab_ref, idx_ref, o_ref):
    inner = o_ref.shape[2]
    idx = idx_ref[0]
    accs = [jnp.full(idx.shape, -jnp.inf, F32) for _ in range(inner)]
    for bucket in range(REL_BUCKETS):
        hit = idx == bucket
        for hi in range(inner):
            head = pl.program_id(0) * inner + hi
            accs[hi] = jnp.where(hit, tab_ref[head * REL_BUCKETS + bucket], accs[hi])
    for hi in range(inner):
        o_ref[0, 0, hi] = accs[hi]


def _bias_tiles(tab, dist, valid, inner):
    h = tab.shape[0]
    t, r, c = dist.shape
    idx = jnp.asarray(np.where(valid, _t5_bucket_np(np.maximum(dist, 0)), -1).astype(np.int32))
    return pl.pallas_call(
        _bias_kernel,
        grid=(h // inner, t),
        in_specs=[pl.BlockSpec(memory_space=pltpu.SMEM),
                  pl.BlockSpec((1, r, c), lambda a, ti: (ti, 0, 0))],
        out_specs=pl.BlockSpec((1, 1, inner, r, c), lambda a, ti: (a, ti, 0, 0, 0)),
        out_shape=jax.ShapeDtypeStruct((h // inner, t, inner, r, c), F32),
        compiler_params=_cparams("parallel", "parallel"),
        name="bias_tiles",
    )(tab.reshape(-1), idx)


def _half_mask(g):
    lane = lax.broadcasted_iota(jnp.int32, (1, LANES), 1)
    return (lane >= HEAD_DIM) if g else (lane < HEAD_DIM)


def _inproj_kernel(x_ref, g_ref, w_ref, *refs, widths, n_mix):
    n_dil = len(DIL_PATTERNS)
    n_lay = len(MIX_LAYOUTS)
    n_scr = n_mix * (n_dil - 1)
    mix_refs = [(None,) * (n_dil - n_lay) + tuple(refs[i * n_lay:(i + 1) * n_lay]) for i in range(n_mix)]
    out_refs = refs[n_mix * n_lay:len(refs) - n_scr]
    scratch = [refs[len(refs) - n_scr + i * (n_dil - 1):len(refs) - n_scr + (i + 1) * (n_dil - 1)]
               for i in range(n_mix)]
    x = x_ref[...]
    tm = x.shape[0]
    y = x * lax.rsqrt(jnp.mean(x * x, axis=-1, keepdims=True) + RMS_EPS)
    xn = (y * g_ref[...]).astype(BF16)
    merged = jnp.dot(xn, w_ref[:, sum(widths[:NARROW_FROM]):], preferred_element_type=F32)
    start = 0
    for idx, w in enumerate(widths):
        if idx < NARROW_FROM:
            r = jnp.dot(xn, w_ref[:, start:start + w], preferred_element_type=F32)
        else:
            off = start - sum(widths[:NARROW_FROM])
            r = merged[:, off:off + w]
        if idx < n_mix:
            prev_dil = None
            for pi, (o_ref, (_, dil)) in enumerate(zip(mix_refs[idx], DIL_PATTERNS)):
                keep = scratch[idx][pi] if pi < n_dil - 1 else None
                for p in range(w // LANES):
                    if prev_dil is None:
                        planes = [(0, r[:, p * LANES:(p + 1) * LANES])]
                    else:
                        f = dil // prev_dil
                        planes = [(res + prev_dil * a,
                                   scratch[idx][pi - 1][p, res, pl.ds(a, tm // dil, stride=f), :])
                                  for res in range(prev_dil) for a in range(f)]
                    for res, rows in planes:
                        if o_ref is not None:
                            o_ref[p, 0, 0, res] = rows.astype(o_ref.dtype)
                        if keep is not None:
                            keep[p, res] = rows
                prev_dil = dil
        else:
            o_ref = out_refs[idx - n_mix]
            o_ref[...] = r.astype(o_ref.dtype)
        start += w


def _inproj(x3, g, w, widths, dtypes, n_mix, tm):
    b, s, d = x3.shape
    n = b * s
    per_batch = s // tm
    out_shape, out_specs = [], []
    for w_, dt in zip(widths[:n_mix], dtypes[:n_mix]):
        for dil in MIX_LAYOUTS:
            out_shape.append(jax.ShapeDtypeStruct((w_ // LANES, b, per_batch, dil, tm // dil, LANES), dt))
            out_specs.append(pl.BlockSpec((w_ // LANES, 1, 1, dil, tm // dil, LANES),
                                          lambda i: (0, i // per_batch, i % per_batch, 0, 0, 0)))
    for w_, dt in zip(widths[n_mix:], dtypes[n_mix:]):
        out_shape.append(jax.ShapeDtypeStruct((n, w_), dt))
        out_specs.append(pl.BlockSpec((tm, w_), lambda i: (i, 0)))
    return pl.pallas_call(
        functools.partial(_inproj_kernel, widths=widths, n_mix=n_mix),
        grid=(n // tm,),
        in_specs=[pl.BlockSpec((tm, d), lambda i: (i, 0)),
                  pl.BlockSpec((1, d), lambda i: (0, 0)),
                  pl.BlockSpec(w.shape, lambda i: (0, 0))],
        out_specs=out_specs,
        out_shape=out_shape,
        scratch_shapes=[pltpu.VMEM((w_ // LANES, dil, tm // dil, LANES), F32)
                        for w_ in widths[:n_mix] for _, dil in DIL_PATTERNS[:-1]],
        compiler_params=_cparams("parallel"),
        name="inproj",
    )(x3.reshape(n, d), g, w)


def _mixer_a_kernel(*refs):
    bias_ref, o_ref, acc_scr, m_scr = refs[5 * len(MIX_LAYOUTS):]
    sb = pl.program_id(2)
    halves = [_half_mask(hh) for hh in range(2)]
    col = lax.broadcasted_iota(jnp.int32, (1, 2 * DIL_BLOCK), 1)
    first_keep = (col >= DIL_BLOCK) | (sb > 0)
    ones = jnp.ones((2 * DIL_BLOCK, LANES), BF16)
    n_blocks = SPAN // DIL_BLOCK
    for p, (_, dil) in enumerate(DIL_PATTERNS):
        lay = MIX_LAYOUTS.index(dil) if dil in MIX_LAYOUTS else 0
        src = MIX_LAYOUTS[lay]
        fan = src // dil
        piece = DIL_BLOCK // fan
        q_ref, kp_ref, kc_ref, vp_ref, vc_ref = refs[5 * lay:5 * lay + 5]
        last = SPAN // src // piece - 1

        def gather(ref, r, nn, dil=dil, fan=fan, piece=piece):
            per_tile = ref.shape[4]
            chunk = min(piece, per_tile)
            return jnp.concatenate(
                [ref[0, 0, lo // per_tile, r + dil * a, lo % per_tile:lo % per_tile + chunk, :]
                 for a in range(fan) for lo in range(nn * piece, (nn + 1) * piece, chunk)], axis=0)

        for r in range(dil):
            for n in range(SPAN // (dil * DIL_BLOCK)):
                q2 = gather(q_ref, r, n)
                if n == 0:
                    k_prev, v_prev = gather(kp_ref, r, last), gather(vp_ref, r, last)
                else:
                    k_prev, v_prev = gather(kc_ref, r, n - 1), gather(vc_ref, r, n - 1)
                kcat = jnp.concatenate([k_prev, gather(kc_ref, r, n)], axis=0)
                vcat = jnp.concatenate([v_prev, gather(vc_ref, r, n)], axis=0)
                qm = jnp.concatenate([jnp.where(halves[hh], q2, jnp.zeros_like(q2)) for hh in range(2)], axis=0)
                s2 = lax.dot_general(qm, kcat, NT_DIMS, preferred_element_type=F32)
                for hh in range(2):
                    s = s2[hh * DIL_BLOCK:(hh + 1) * DIL_BLOCK] + bias_ref[p, hh]
                    if n == 0:
                        s = jnp.where(first_keep, s, -jnp.inf)
                    m_blk = jnp.max(s, axis=-1, keepdims=True)
                    pe = jnp.exp2(s - m_blk)
                    pv = jnp.dot(pe.astype(BF16), jnp.where(halves[hh], vcat, ones), preferred_element_type=F32)
                    m_b = jnp.broadcast_to(m_blk, pv.shape)
                    for a in range(fan):
                        rows_t = pl.ds(n * dil * DIL_BLOCK + r + dil * a, piece, stride=src)
                        acc_scr[p, hh, rows_t, :] = pv[a * piece:(a + 1) * piece]
                        m_scr[p, hh, rows_t, :] = m_b[a * piece:(a + 1) * piece]

    def finish(c, carry):
        rows = pl.ds(pl.multiple_of(c * DIL_BLOCK, DIL_BLOCK), DIL_BLOCK)
        tots = []
        for hh in range(2):
            ms = [m_scr[p, hh, rows, :] for p in range(len(DIL_PATTERNS))]
            m_all = functools.reduce(jnp.maximum, ms)
            tots.append(sum(jnp.exp2(m - m_all) * acc_scr[p, hh, rows, :] for p, m in enumerate(ms)))
        num = jnp.where(halves[0], tots[0], tots[1])
        den = pltpu.roll(jnp.where(halves[0], tots[1], tots[0]), HEAD_DIM, axis=1)
        o_ref[0, 0, rows, :] = (num / den).astype(o_ref.dtype)
        return carry

    lax.fori_loop(0, n_blocks, finish, 0, unroll=4)


def _mixer_a(qs, ks, vs, bias):
    npair, b, n_tiles, dil0, rows0, _ = qs[0].shape
    tile = dil0 * rows0
    s = n_tiles * tile
    cur = lambda bi, pi, si: (pi, bi, si, 0, 0, 0)
    prev = lambda bi, pi, si: (pi, bi, jnp.maximum(si - 1, 0), 0, 0, 0)
    in_specs, operands = [], []
    for dil, q, k, v in zip(MIX_LAYOUTS, qs, ks, vs):
        blk = (1, 1, SPAN // tile, dil, tile // dil, LANES)
        in_specs += [pl.BlockSpec(blk, cur), pl.BlockSpec(blk, prev), pl.BlockSpec(blk, cur),
                     pl.BlockSpec(blk, prev), pl.BlockSpec(blk, cur)]
        operands += [q, k, k, v, v]
    in_specs.append(pl.BlockSpec((bias.shape[0], 2) + bias.shape[2:], lambda bi, pi, si: (0, pi, 0, 0)))
    return pl.pallas_call(
        _mixer_a_kernel,
        grid=(b, npair, s // SPAN),
        in_specs=in_specs,
        out_specs=pl.BlockSpec((1, 1, SPAN, LANES), lambda bi, pi, si: (pi, bi, si, 0)),
        out_shape=jax.ShapeDtypeStruct((npair, b, s, LANES), BF16),
        scratch_shapes=[pltpu.VMEM((len(DIL_PATTERNS), 2, SPAN, LANES), F32),
                        pltpu.VMEM((len(DIL_PATTERNS), 2, SPAN, LANES), F32)],
        compiler_params=_cparams("parallel", "parallel", "parallel"),
        name="mixer_a",
    )(*operands, bias)


def _compress_kernel(c_ref, pa_ref, pb_ref, w1a_ref, w1b_ref, b1_ref, w2_ref, o_ref):
    nch = o_ref.shape[1]
    c = jnp.concatenate([c_ref[0, pl.ds(l, nch, stride=CMP_STRIDE), :] for l in range(CMP_STRIDE)], axis=1)
    xa = (c + pa_ref[...]).astype(BF16)
    xb = (c + pb_ref[...]).astype(BF16)
    ha = jnp.dot(xa, w1a_ref[...], preferred_element_type=F32)
    hb = jnp.dot(xb, w1b_ref[...], preferred_element_type=F32)
    hb_next = jnp.concatenate([hb[1:], jnp.zeros_like(hb[:1])], axis=0)
    hid = jax.nn.gelu(ha + hb_next + b1_ref[...])
    o_ref[0] = jnp.dot(hid.astype(BF16), w2_ref[...], preferred_element_type=F32).astype(o_ref.dtype)


def _compress(c, pos, w1, b1, w2):
    b, s, _ = c.shape
    nch = s // CMP_STRIDE
    half = CMP_LEN // 2
    zero = jnp.zeros((half, HEAD_DIM, CMP_HIDDEN), F32)

    def grouped(wpart):
        g0 = jnp.concatenate([wpart, zero], axis=1).reshape(half * LANES, CMP_HIDDEN)
        g1 = jnp.concatenate([zero, wpart], axis=1).reshape(half * LANES, CMP_HIDDEN)
        return jnp.concatenate([g0, g1], axis=1).astype(BF16)

    w1a, w1b = grouped(w1[:half]), grouped(w1[half:])
    pa = jnp.tile(pos[:half], (1, 2)).reshape(1, half * LANES)
    pb = jnp.tile(pos[half:], (1, 2)).reshape(1, half * LANES)
    b1g = jnp.tile(b1, 2).reshape(1, 2 * CMP_HIDDEN)
    zw = jnp.zeros_like(w2)
    w2g = jnp.concatenate([jnp.concatenate([w2, zw], axis=1),
                           jnp.concatenate([zw, w2], axis=1)], axis=0).astype(BF16)
    full = lambda a: pl.BlockSpec(a.shape, lambda i: (0,) * a.ndim)
    return pl.pallas_call(
        _compress_kernel,
        grid=(b,),
        in_specs=[pl.BlockSpec((1,) + c.shape[1:], lambda i: (i, 0, 0)),
                  full(pa), full(pb), full(w1a), full(w1b), full(b1g), full(w2g)],
        out_specs=pl.BlockSpec((1, nch, LANES), lambda i: (i, 0, 0)),
        out_shape=jax.ShapeDtypeStruct((b, nch, LANES), BF16),
        compiler_params=_cparams("parallel"),
        name="compress",
    )(c, pa, pb, w1a, w1b, b1g, w2g)


def _nsa_kernel(q_ref, gl_ref, kc_ref, vc_ref, wov_ref, ks_ref, blk_ref, vs_ref, bias_s_ref, kw_ref, vw_ref, bias_w_ref,
                o_ref, acc_ref, m_ref, s_ref, part_ref, sb_ref, *, nsel, n_bias, kw):
    qt = pl.program_id(1)
    tq = q_ref.shape[1]
    n_sub = tq // SUB_TILE
    sub_rows = B_GROUP * SUB_TILE
    gate = jax.nn.sigmoid(gl_ref[0])
    halves = [_half_mask(g) for g in range(B_KV_HEADS)]
    q_slots = [q_ref[0, :, r * LANES:(r + 1) * LANES] for r in range(B_GROUP)]
    zero = jnp.zeros((tq, LANES), BF16)

    def stacked(slots):
        return jnp.concatenate([slots[r][u * SUB_TILE:(u + 1) * SUB_TILE]
                                for u in range(n_sub) for r in range(B_GROUP)], axis=0)

    def head_rows(o, r):
        return jnp.concatenate([o[(u * B_GROUP + r) * SUB_TILE:(u * B_GROUP + r + 1) * SUB_TILE]
                                for u in range(n_sub)], axis=0)

    def gate_tile(r, branch):
        c0, c1 = (r * N_BRANCH + branch, (B_GROUP + r) * N_BRANCH + branch)
        return jnp.where(halves[0], gate[:, c0:c0 + 1], gate[:, c1:c1 + 1])

    def gated_slots(o, g, branch, slots):
        for r in range(B_GROUP):
            c = (g * B_GROUP + r) * N_BRANCH + branch
            o_r = head_rows(o, r) * gate[:, c:c + 1]
            slots[r] = o_r if slots[r] is None else jnp.where(halves[g], o_r, slots[r])

    def normalised_slots(accs, branch):
        slots = []
        for r in range(B_GROUP):
            a0, a1 = head_rows(accs[0], r), head_rows(accs[1], r)
            num = jnp.where(halves[0], a0, a1)
            den = pltpu.roll(jnp.where(halves[0], a1, a0), HEAD_DIM, axis=1)
            slots.append(num / jnp.maximum(den, 1e-30) * gate_tile(r, branch))
        return slots

    nck = kc_ref.shape[1]
    row = lax.broadcasted_iota(jnp.int32, (n_sub * sub_rows, 1), 0)
    t_row = qt * tq + (row // sub_rows) * SUB_TILE + row % SUB_TILE
    blk_end = lax.broadcasted_iota(jnp.int32, (1, nck), 1) * CMP_STRIDE + (CMP_LEN - 1)
    valid = blk_end <= t_row
    kc = kc_ref[0]
    vc = vc_ref[0]
    pc = []
    cmp_slots = [None] * B_GROUP
    for g in range(B_KV_HEADS):
        qg = stacked([jnp.where(halves[g], q_slots[r], zero) for r in range(B_GROUP)])
        s = lax.dot_general(qg, kc, NT_DIMS, preferred_element_type=F32)
        s = jnp.where(valid, s, -jnp.inf)
        m = jnp.max(s, axis=-1, keepdims=True)
        m = jnp.where(m == -jnp.inf, 0.0, m)
        p = jnp.exp2(s - m)
        den = jnp.sum(p, axis=-1, keepdims=True)
        p = p / jnp.maximum(den, 1e-30)
        pc.append(jnp.concatenate(
            [sum(p[(u * B_GROUP + r) * SUB_TILE:(u * B_GROUP + r + 1) * SUB_TILE] for r in range(B_GROUP))
             for u in range(n_sub)], axis=0))
        gated_slots(jnp.dot(p.astype(BF16), vc, preferred_element_type=F32), g, 0, cmp_slots)
    for r in range(B_GROUP):
        part_ref[0, :, r * LANES:(r + 1) * LANES] = cmp_slots[r]

    t_lane = qt * tq + lax.broadcasted_iota(jnp.int32, (1, tq), 1)
    cur = t_lane // SEL_BLOCK
    j = lax.broadcasted_iota(jnp.int32, (nsel, 1), 0)
    forced = (j == 0) | (j == cur) | (j == cur - 1)
    imps = []
    for g in reversed(range(B_KV_HEADS)):
        imp = lax.dot_general(wov_ref[...], pc[g], NT_DIMS, preferred_element_type=F32,
                              precision=lax.Precision.HIGHEST)
        imps.append(jnp.where(j > cur, -jnp.inf, jnp.where(forced, jnp.inf, imp)))

    last_block = (qt * tq + tq - 1) // SEL_BLOCK
    for level in range(-(-nsel // RANK_CHUNK)):
        n_live = min((level + 1) * RANK_CHUNK, nsel)

        @pl.when(last_block // RANK_CHUNK == level)
        def _(n_live=n_live):
            blocks = []
            for imp in imps:
                if n_live <= SEL_TOPN:
                    blocks.append(jnp.zeros((HEAD_DIM, tq), F32))
                    continue
                live = imp[:n_live]
                jl = j[:n_live]
                rank = jnp.zeros((n_live, tq), jnp.int32)
                for jp in range(n_live):
                    row = live[jp:jp + 1, :]
                    rank = rank + jnp.where(jl > jp, (row >= live).astype(jnp.int32), (row > live).astype(jnp.int32))
                blocks.append(jnp.where(rank < SEL_TOPN, 0.0, NOT_SELECTED))
                if n_live < HEAD_DIM:
                    blocks.append(jnp.zeros((HEAD_DIM - n_live, tq), F32))
            sb_ref[...] = jnp.concatenate(blocks, axis=0).T.astype(sb_ref.dtype)

    sb = sb_ref[...]
    qaug = [stacked([jnp.where(halves[g], q_slots[r], sb) for r in range(B_GROUP)])
            for g in range(B_KV_HEADS)]
    n_steps = (qt * tq + tq + K_TILE - 1) // K_TILE
    acc_ref[...] = jnp.zeros_like(acc_ref)
    m_ref[...] = jnp.full_like(m_ref, M_INIT)

    def scores(kt):
        for jt in range(K_TILE // SCORE_TILE):
            k0 = pl.multiple_of(kt * K_TILE + jt * SCORE_TILE, SCORE_TILE)
            k2 = ks_ref[0, pl.ds(k0, SCORE_TILE), :]
            e2 = blk_ref[pl.ds(k0, SCORE_TILE), :]
            for g in range(B_KV_HEADS):
                k = jnp.where(halves[g], k2, e2)
                s = lax.dot_general(qaug[g], k, NT_DIMS, preferred_element_type=F32)
                for u in range(n_sub):
                    rows_u = slice(u * sub_rows, (u + 1) * sub_rows)
                    for c in range(SCORE_TILE // SUB_TILE):
                        bi = jnp.clip(qt * n_sub + u - k0 // SUB_TILE - c + 1, 0, n_bias - 1)
                        col = jt * SCORE_TILE + c * SUB_TILE
                        s_ref[kt % 2, g, rows_u, col:col + SUB_TILE] = (
                            s[rows_u, c * SUB_TILE:(c + 1) * SUB_TILE] + bias_s_ref[g, bi])

    def accumulate(kt):
        k0 = pl.multiple_of(kt * K_TILE, K_TILE)
        v2 = vs_ref[0, pl.ds(k0, K_TILE), :]
        for g in range(B_KV_HEADS):
            v = jnp.where(halves[g], v2, jnp.ones((K_TILE, LANES), BF16))
            s = s_ref[kt % 2, g]
            m_old = m_ref[g]
            m_new = jnp.maximum(m_old, jnp.max(s, axis=-1, keepdims=True))
            alpha = jnp.exp2(m_old - m_new)
            p = jnp.exp2(s - jnp.tile(m_new, (1, K_TILE // LANES)))
            acc_ref[g] = alpha * acc_ref[g] + jnp.dot(p.astype(BF16), v, preferred_element_type=F32)
            m_ref[g] = m_new

    def body(kt, carry):
        accumulate(kt)
        scores(kt + 1)
        return carry

    n_wt = kw // SUB_TILE
    q_win = [stacked([jnp.where(halves[g], q_slots[r], zero) for r in range(B_GROUP)]) for g in range(B_KV_HEADS)]
    o_win = [[] for _ in range(B_KV_HEADS)]
    for u in range(n_sub):
        sub = qt * n_sub + u
        w0 = pl.multiple_of(jnp.maximum(sub * SUB_TILE - WIN_LEN, 0), SUB_TILE)
        skip = jnp.maximum(WIN_LEN // SUB_TILE - sub, 0)
        kwin = kw_ref[0, pl.ds(w0, kw), :]
        vwin = vw_ref[0, pl.ds(w0, kw), :]
        for g in range(B_KV_HEADS):
            s = lax.dot_general(q_win[g][u * sub_rows:(u + 1) * sub_rows], kwin, NT_DIMS, preferred_element_type=F32)
            s = jnp.concatenate([s[:, jw * SUB_TILE:(jw + 1) * SUB_TILE] + bias_w_ref[g, jnp.minimum(jw + skip, n_wt)]
                                 for jw in range(n_wt)], axis=1)
            p = jnp.exp2(s - jnp.max(s, axis=-1, keepdims=True))
            acc = jnp.dot(p.astype(BF16), jnp.where(halves[g], vwin, jnp.ones_like(vwin)),
                          preferred_element_type=F32)
            o_win[g].append(acc)
    win_slots = normalised_slots([jnp.concatenate(o_win[g], axis=0) for g in range(B_KV_HEADS)], 2)
    for r in range(B_GROUP):
        part_ref[1, :, r * LANES:(r + 1) * LANES] = win_slots[r]

    scores(0)
    lax.fori_loop(0, n_steps - 1, body, 0)
    accumulate(n_steps - 1)
    sel_slots = normalised_slots([acc_ref[g] for g in range(B_KV_HEADS)], 1)
    for r in range(B_GROUP):
        sl = slice(r * LANES, (r + 1) * LANES)
        o_ref[0, :, sl] = ((part_ref[0, :, sl] + sel_slots[r]) + part_ref[1, :, sl]).astype(o_ref.dtype)


def _nsa(qb, gl, kcmp, vcmp, ksl, vsl, bias_s, kwin, vwin, bias_w):
    b, s, w = qb.shape
    nsel = s // SEL_BLOCK
    nck = kcmp.shape[1]
    ratio = SEL_BLOCK // CMP_STRIDE
    wov = np.zeros((nsel, nck), np.float32)
    for jj in range(nsel):
        for off, wt in zip(range(-1, ratio), CMP_OVERLAP):
            n = ratio * jj + off
            if 0 <= n < nck - 1:
                wov[jj, n] = wt
    wov = jnp.asarray(wov)
    blk = jnp.asarray((np.arange(s)[:, None] // SEL_BLOCK) == (np.arange(LANES)[None, :] % HEAD_DIM), BF16)
    rows = B_GROUP * Q_TILE
    tile = lambda width: pl.BlockSpec((1, Q_TILE, width), lambda bi, qi: (bi, qi, 0))
    whole = lambda a: pl.BlockSpec((1,) + a.shape[1:], lambda bi, qi: (bi, 0, 0))
    const = lambda a: pl.BlockSpec(a.shape, lambda bi, qi: (0,) * a.ndim, pipeline_mode=pl.Buffered(1))
    return pl.pallas_call(
        functools.partial(_nsa_kernel, nsel=nsel, n_bias=bias_s.shape[1], kw=WIN_LEN + SUB_TILE),
        grid=(b, s // Q_TILE),
        in_specs=[tile(w), tile(LANES), whole(kcmp), whole(vcmp), const(wov), whole(ksl), const(blk), whole(vsl),
                  const(bias_s), whole(kwin), whole(vwin), const(bias_w)],
        out_specs=tile(w),
        out_shape=jax.ShapeDtypeStruct((b, s, w), BF16),
        scratch_shapes=[pltpu.VMEM((B_KV_HEADS, rows, LANES), F32), pltpu.VMEM((B_KV_HEADS, rows, LANES), F32),
                        pltpu.VMEM((2, B_KV_HEADS, rows, K_TILE), F32), pltpu.VMEM((2, Q_TILE, w), F32),
                        pltpu.VMEM((Q_TILE, LANES), BF16)],
        compiler_params=_cparams("parallel", "parallel"),
        name="nsa",
    )(qb, gl, kcmp, vcmp, wov, ksl, blk, vsl, bias_s, kwin, vwin, bias_w)


def _tail_kernel(x_ref, oa_ref, ob_ref, wa_ref, wb_ref, g2_ref, wg_ref, wu_ref, wd_ref, gf_ref, o_ref):
    o_a = jnp.concatenate([oa_ref[p] for p in range(oa_ref.shape[0])], axis=1)
    mix = jnp.dot(o_a, wa_ref[...], preferred_element_type=F32)
    mix = mix + jnp.dot(ob_ref[...], wb_ref[...], preferred_element_type=F32)
    h = x_ref[...] + mix
    y = h * lax.rsqrt(jnp.mean(h * h, axis=-1, keepdims=True) + RMS_EPS)
    hn = (y * g2_ref[...]).astype(BF16)
    a = jnp.dot(hn, wg_ref[...], preferred_element_type=F32)
    u = jnp.dot(hn, wu_ref[...], preferred_element_type=F32)
    act = (jax.nn.silu(a) * u).astype(BF16)
    h2 = h + jnp.dot(act, wd_ref[...], preferred_element_type=F32)
    y2 = h2 * lax.rsqrt(jnp.mean(h2 * h2, axis=-1, keepdims=True) + RMS_EPS)
    o_ref[...] = y2 * gf_ref[...]


def _tail(x2, oa, ob, wa, wb, g2, wg, wu, wd, gf, tm):
    n, d = x2.shape
    row = lambda width: pl.BlockSpec((tm, width), lambda i: (i, 0))
    const = lambda a: pl.BlockSpec(a.shape, lambda i: (0, 0), pipeline_mode=pl.Buffered(1))
    return pl.pallas_call(
        _tail_kernel,
        grid=(n // tm,),
        in_specs=[row(d), pl.BlockSpec((oa.shape[0], tm, LANES), lambda i: (0, i, 0)), row(ob.shape[1]),
                  const(wa), const(wb), const(g2), const(wg), const(wu), const(wd), const(gf)],
        out_specs=row(d),
        out_shape=jax.ShapeDtypeStruct((n, d), F32),
        compiler_params=_cparams("parallel"),
        name="tail",
    )(x2, oa, ob, wa, wb, g2, wg, wu, wd, gf)


def _slot_perm():
    perm = np.zeros(B_WIDTH, np.int64)
    for r in range(B_GROUP):
        for g in range(B_KV_HEADS):
            src = (g * B_GROUP + r) * HEAD_DIM
            dst = r * LANES + g * HEAD_DIM
            perm[dst:dst + HEAD_DIM] = np.arange(src, src + HEAD_DIM)
    return perm


def _layer(h, tab_a, tab_b, norm1_g, w_in, cmp_pos, k_w1, k_b1, k_w2, v_w1, v_b1, v_w2,
           w_out, norm2_g, w_gate, w_up, w_down, norm_f_g):
    b, s, d = h.shape
    n = b * s
    assert s % SPAN == 0 and s % K_TILE == 0
    perm = _slot_perm()

    cols = np.cumsum([0, A_WIDTH, A_WIDTH, A_WIDTH, B_WIDTH] + [KV_WIDTH] * 6 + [GATE_WIDTH])
    w_aq = w_in[:, cols[0]:cols[1]] * (SCALE * LOG2E)
    w_bq = (w_in[:, cols[3]:cols[4]] * (SCALE * LOG2E))[:, perm]
    w_gl = jnp.pad(w_in[:, cols[10]:cols[11]], ((0, 0), (0, LANES - GATE_WIDTH)))
    w1 = jnp.concatenate([w_aq, w_in[:, cols[1]:cols[3]], w_bq, w_in[:, cols[4]:cols[10]], w_gl], axis=1).astype(BF16)
    widths = (A_WIDTH,) * 3 + (B_WIDTH,) + (KV_WIDTH,) * 6 + (LANES,)
    dtypes = (BF16,) * 4 + (F32,) * 2 + (BF16,) * 4 + (F32,)
    x2 = h.reshape(n, d)
    n_lay = len(MIX_LAYOUTS)
    outs = _inproj(h, norm1_g.reshape(1, d), w1, widths, dtypes, n_mix=3, tm=512)
    qa, ka, va = (outs[i * n_lay:(i + 1) * n_lay] for i in range(3))
    qb, kc, vc, ksl, vsl, kw, vw, gl = outs[3 * n_lay:]
    r3 = lambda t: t.reshape(b, s, t.shape[-1])
    qb, ksl, vsl, kw, vw, gl = map(r3, (qb, ksl, vsl, kw, vw, gl))

    tok_dist, in_window = [], []
    for window, dil in DIL_PATTERNS:
        fan = (dil if dil in MIX_LAYOUTS else MIX_LAYOUTS[0]) // dil
        row = np.arange(DIL_BLOCK)
        step = fan * (row % (DIL_BLOCK // fan)) + row // (DIL_BLOCK // fan)
        key_step = np.concatenate([step, DIL_BLOCK + step])
        dist = DIL_BLOCK + step[:, None] - key_step[None, :]
        tok_dist.append(np.maximum(dist, 0) * dil)
        in_window.append((dist >= 0) & (dist <= window // dil))
    tok_dist, in_window = np.stack(tok_dist), np.stack(in_window)
    bias_a = _bias_tiles(tab_a, tok_dist, in_window, inner=A_HEADS)[0]
    o_a = _mixer_a(qa, ka, va, bias_a).reshape(A_HEADS // 2, n, LANES)

    nch = s // CMP_STRIDE
    kcmp = _compress(kc.reshape(b, s, KV_WIDTH), cmp_pos, k_w1, k_b1, k_w2)
    vcmp = _compress(vc.reshape(b, s, KV_WIDTH), cmp_pos, v_w1, v_b1, v_w2)
    d_sat = int(np.nonzero(_t5_bucket_np(np.arange(s)) < REL_BUCKETS - 1)[0].max()) + 1
    first_far = -(-(d_sat + SUB_TILE - 1) // SUB_TILE)
    n_bias = min(s // SUB_TILE, first_far + 1) + 1
    dd = ((np.arange(n_bias)[:, None, None] - 1) * SUB_TILE + np.arange(SUB_TILE)[None, :, None]
          - np.arange(SUB_TILE)[None, None, :])
    bias_sl = _bias_tiles(tab_b, dd, dd >= 0, inner=B_GROUP)
    bias_sl = bias_sl.reshape(B_KV_HEADS, n_bias, B_GROUP * SUB_TILE, SUB_TILE)
    n_wt = WIN_LEN // SUB_TILE + 1
    dw = ((n_wt - 1 - np.arange(n_wt + 1))[:, None, None] * SUB_TILE + np.arange(SUB_TILE)[None, :, None]
          - np.arange(SUB_TILE)[None, None, :])
    in_win = (dw >= 0) & (dw < WIN_LEN) & (np.arange(n_wt + 1) < n_wt)[:, None, None]
    bias_w = _bias_tiles(tab_b, dw, in_win, inner=B_GROUP)
    bias_w = bias_w.reshape(B_KV_HEADS, n_wt + 1, B_GROUP * SUB_TILE, SUB_TILE)
    o_b = _nsa(qb, gl, kcmp, vcmp, ksl, vsl, bias_sl, kw, vw, bias_w)

    wa = w_out[:A_WIDTH].astype(BF16)
    wb = w_out[A_WIDTH:][perm].astype(BF16)
    out = _tail(x2, o_a, o_b.reshape(n, B_WIDTH), wa, wb, norm2_g.reshape(1, d), w_gate.astype(BF16),
                w_up.astype(BF16), w_down.astype(BF16), norm_f_g.reshape(1, d), tm=512)
    return out.reshape(b, s, d)


def kernel(x, norm1_g, w_in, rel_bias, cmp_pos, cmp_k_w1, cmp_k_b1, cmp_k_w2, cmp_v_w1, cmp_v_b1, cmp_v_w2,
           w_out, norm2_g, w_gate, w_up, w_down, norm_f_g):
    assert w_in.shape[0] == 1, "single-layer model"
    tab_a = rel_bias[:, :A_HEADS].T * LOG2E
    tab_b = rel_bias[:, A_HEADS:].T * LOG2E
    return _layer(x, tab_a, tab_b, norm1_g[0], w_in[0], cmp_pos[0], cmp_k_w1[0], cmp_k_b1[0], cmp_k_w2[0],
                  cmp_v_w1[0], cmp_v_b1[0], cmp_v_w2[0], w_out[0], norm2_g[0], w_gate[0], w_up[0], w_down[0],
                  norm_f_g)
```

```python
import functools
import math

import numpy as np
import jax
import jax.numpy as jnp
from jax import lax
from jax.experimental import pallas as pl
from jax.experimental.pallas import tpu as pltpu

F32 = jnp.float32
BF16 = jnp.bfloat16

HEAD_DIM = 64
LANES = 128
A_HEADS = 8
DIL_PATTERNS = ((128, 1), (512, 4), (2048, 16))
DIL_BLOCK = 128
B_HEADS = 8
B_KV_HEADS = 2
B_GROUP = B_HEADS // B_KV_HEADS
CMP_LEN = 32
CMP_STRIDE = 16
CMP_HIDDEN = 128
SEL_BLOCK = 64
SEL_TOPN = 16
CMP_OVERLAP = (1.0, 2.0, 2.0, 2.0, 1.0)
WIN_LEN = 512
N_BRANCH = 3
REL_BUCKETS = 32
REL_MAX_DIST = 2048
RMS_EPS = 1e-6
A_WIDTH = A_HEADS * HEAD_DIM
B_WIDTH = B_HEADS * HEAD_DIM
KV_WIDTH = B_KV_HEADS * HEAD_DIM
GATE_WIDTH = B_HEADS * N_BRANCH
SCALE = HEAD_DIM ** -0.5
LOG2E = math.log2(math.e)

SPAN = DIL_PATTERNS[-1][1] * DIL_BLOCK
MIX_LAYOUTS = tuple(d for _, d in DIL_PATTERNS if d > 1)
NARROW_FROM = 4
RANK_CHUNK = 16
SUB_TILE = 128
Q_TILE = 512
K_TILE = 512
SCORE_TILE = 256
NOT_SELECTED = -(2.0 ** 100)
M_INIT = -1e30
VMEM_LIMIT = 56 * 1024 * 1024

NT_DIMS = (((1,), (1,)), ((), ()))


def _cparams(*sem):
    return pltpu.CompilerParams(dimension_semantics=sem, vmem_limit_bytes=VMEM_LIMIT)


def _t5_bucket_np(dist):
    max_exact = REL_BUCKETS // 2
    d = np.asarray(dist)
    df = np.maximum(d, 1).astype(np.float32)
    large = max_exact + (np.log(df / np.float32(max_exact)) / np.float32(math.log(REL_MAX_DIST / max_exact))
                         * np.float32(REL_BUCKETS - max_exact)).astype(np.int32)
    large = np.minimum(large, REL_BUCKETS - 1)
    return np.where(d < max_exact, d, large).astype(np.int32)


def _bias_kernel(tab_ref, idx_ref, o_ref, *, buckets):
    inner = o_ref.shape[2]
    for t, present in enumerate(buckets):
        idx = idx_ref[t]
        accs = [jnp.full(idx.shape, -jnp.inf, F32) for _ in range(inner)]
        for bucket in present:
            hit = idx == bucket
            for hi in range(inner):
                head = pl.program_id(0) * inner + hi
                accs[hi] = jnp.where(hit, tab_ref[head * REL_BUCKETS + bucket], accs[hi])
        for hi in range(inner):
            o_ref[0, t, hi] = accs[hi]


def _bias_tiles(tab, dist, valid, inner):
    h = tab.shape[0]
    t, r, c = dist.shape
    idx = np.where(valid, _t5_bucket_np(np.maximum(dist, 0)), -1).astype(np.int32)
    buckets = tuple(tuple(int(v) for v in np.unique(idx[ti]) if v >= 0) for ti in range(t))
    return pl.pallas_call(
        functools.partial(_bias_kernel, buckets=buckets),
        grid=(h // inner,),
        in_specs=[pl.BlockSpec(memory_space=pltpu.SMEM),
                  pl.BlockSpec((t, r, c), lambda a: (0, 0, 0))],
        out_specs=pl.BlockSpec((1, t, inner, r, c), lambda a: (a, 0, 0, 0, 0)),
        out_shape=jax.ShapeDtypeStruct((h // inner, t, inner, r, c), F32),
        compiler_params=_cparams("parallel"),
        name="bias_tiles",
    )(tab.reshape(-1), jnp.asarray(idx))


def _half_mask(g):
    lane = lax.broadcasted_iota(jnp.int32, (1, LANES), 1)
    return (lane >= HEAD_DIM) if g else (lane < HEAD_DIM)


def _inproj_kernel(x_ref, g_ref, w_ref, *refs, widths, n_mix):
    n_dil = len(DIL_PATTERNS)
    n_lay = len(MIX_LAYOUTS)
    n_scr = n_mix * (n_dil - 1)
    mix_refs = [(None,) * (n_dil - n_lay) + tuple(refs[i * n_lay:(i + 1) * n_lay]) for i in range(n_mix)]
    out_refs = refs[n_mix * n_lay:len(refs) - n_scr]
    scratch = [refs[len(refs) - n_scr + i * (n_dil - 1):len(refs) - n_scr + (i + 1) * (n_dil - 1)]
               for i in range(n_mix)]
    x = x_ref[...]
    tm = x.shape[0]
    y = x * lax.rsqrt(jnp.mean(x * x, axis=-1, keepdims=True) + RMS_EPS)
    xn = (y * g_ref[...]).astype(BF16)
    merged = jnp.dot(xn, w_ref[:, sum(widths[:NARROW_FROM]):], preferred_element_type=F32)
    start = 0
    for idx, w in enumerate(widths):
        if idx < NARROW_FROM:
            r = jnp.dot(xn, w_ref[:, start:start + w], preferred_element_type=F32)
        else:
            off = start - sum(widths[:NARROW_FROM])
            r = merged[:, off:off + w]
        if idx < n_mix:
            prev_dil = None
            for pi, (o_ref, (_, dil)) in enumerate(zip(mix_refs[idx], DIL_PATTERNS)):
                keep = scratch[idx][pi] if pi < n_dil - 1 else None
                for p in range(w // LANES):
                    if prev_dil is None:
                        planes = [(0, r[:, p * LANES:(p + 1) * LANES])]
                    else:
                        f = dil // prev_dil
                        planes = [(res + prev_dil * a,
                                   scratch[idx][pi - 1][p, res, pl.ds(a, tm // dil, stride=f), :])
                                  for res in range(prev_dil) for a in range(f)]
                    for res, rows in planes:
                        if o_ref is not None:
                            o_ref[p, 0, 0, res] = rows.astype(o_ref.dtype)
                        if keep is not None:
                            keep[p, res] = rows
                prev_dil = dil
        else:
            o_ref = out_refs[idx - n_mix]
            o_ref[...] = r.astype(o_ref.dtype)
        start += w


def _inproj(x3, g, w, widths, dtypes, n_mix, tm):
    b, s, d = x3.shape
    n = b * s
    per_batch = s // tm
    out_shape, out_specs = [], []
    for w_, dt in zip(widths[:n_mix], dtypes[:n_mix]):
        for dil in MIX_LAYOUTS:
            out_shape.append(jax.ShapeDtypeStruct((w_ // LANES, b, per_batch, dil, tm // dil, LANES), dt))
            out_specs.append(pl.BlockSpec((w_ // LANES, 1, 1, dil, tm // dil, LANES),
                                          lambda i: (0, i // per_batch, i % per_batch, 0, 0, 0)))
    for w_, dt in zip(widths[n_mix:], dtypes[n_mix:]):
        out_shape.append(jax.ShapeDtypeStruct((n, w_), dt))
        out_specs.append(pl.BlockSpec((tm, w_), lambda i: (i, 0)))
    return pl.pallas_call(
        functools.partial(_inproj_kernel, widths=widths, n_mix=n_mix),
        grid=(n // tm,),
        in_specs=[pl.BlockSpec((tm, d), lambda i: (i, 0)),
                  pl.BlockSpec((1, d), lambda i: (0, 0)),
                  pl.BlockSpec(w.shape, lambda i: (0, 0))],
        out_specs=out_specs,
        out_shape=out_shape,
        scratch_shapes=[pltpu.VMEM((w_ // LANES, dil, tm // dil, LANES), F32)
                        for w_ in widths[:n_mix] for _, dil in DIL_PATTERNS[:-1]],
        compiler_params=_cparams("parallel"),
        name="inproj",
    )(x3.reshape(n, d), g, w)


def _mixer_a_kernel(*refs):
    bias_ref, o_ref, acc_scr, m_scr = refs[5 * len(MIX_LAYOUTS):]
    sb = pl.program_id(2)
    halves = [_half_mask(hh) for hh in range(2)]
    col = lax.broadcasted_iota(jnp.int32, (1, 2 * DIL_BLOCK), 1)
    first_keep = (col >= DIL_BLOCK) | (sb > 0)
    ones = jnp.ones((2 * DIL_BLOCK, LANES), BF16)
    n_blocks = SPAN // DIL_BLOCK
    for p, (_, dil) in enumerate(DIL_PATTERNS):
        lay = MIX_LAYOUTS.index(dil) if dil in MIX_LAYOUTS else 0
        src = MIX_LAYOUTS[lay]
        fan = src // dil
        piece = DIL_BLOCK // fan
        q_ref, kp_ref, kc_ref, vp_ref, vc_ref = refs[5 * lay:5 * lay + 5]
        last = SPAN // src // piece - 1

        def gather(ref, r, nn, dil=dil, fan=fan, piece=piece):
            per_tile = ref.shape[4]
            chunk = min(piece, per_tile)
            return jnp.concatenate(
                [ref[0, 0, lo // per_tile, r + dil * a, lo % per_tile:lo % per_tile + chunk, :]
                 for a in range(fan) for lo in range(nn * piece, (nn + 1) * piece, chunk)], axis=0)

        for r in range(dil):
            for n in range(SPAN // (dil * DIL_BLOCK)):
                q2 = gather(q_ref, r, n)
                if n == 0:
                    k_prev, v_prev = gather(kp_ref, r, last), gather(vp_ref, r, last)
                else:
                    k_prev, v_prev = gather(kc_ref, r, n - 1), gather(vc_ref, r, n - 1)
                kcat = jnp.concatenate([k_prev, gather(kc_ref, r, n)], axis=0)
                vcat = jnp.concatenate([v_prev, gather(vc_ref, r, n)], axis=0)
                qm = jnp.concatenate([jnp.where(halves[hh], q2, jnp.zeros_like(q2)) for hh in range(2)], axis=0)
                s2 = lax.dot_general(qm, kcat, NT_DIMS, preferred_element_type=F32)
                for hh in range(2):
                    s = s2[hh * DIL_BLOCK:(hh + 1) * DIL_BLOCK] + bias_ref[p, hh]
                    if n == 0:
                        s = jnp.where(first_keep, s, -jnp.inf)
                    m_blk = jnp.max(s, axis=-1, keepdims=True)
                    pe = jnp.exp2(s - m_blk)
                    pv = jnp.dot(pe.astype(BF16), jnp.where(halves[hh], vcat, ones), preferred_element_type=F32)
                    m_b = jnp.broadcast_to(m_blk, pv.shape)
                    for a in range(fan):
                        rows_t = pl.ds(n * dil * DIL_BLOCK + r + dil * a, piece, stride=src)
                        acc_scr[p, hh, rows_t, :] = pv[a * piece:(a + 1) * piece]
                        m_scr[p, hh, rows_t, :] = m_b[a * piece:(a + 1) * piece]

    def finish(c, carry):
        rows = pl.ds(pl.multiple_of(c * DIL_BLOCK, DIL_BLOCK), DIL_BLOCK)
        tots = []
        for hh in range(2):
            ms = [m_scr[p, hh, rows, :] for p in range(len(DIL_PATTERNS))]
            m_all = functools.reduce(jnp.maximum, ms)
            tots.append(sum(jnp.exp2(m - m_all) * acc_scr[p, hh, rows, :] for p, m in enumerate(ms)))
        num = jnp.where(halves[0], tots[0], tots[1])
        den = pltpu.roll(jnp.where(halves[0], tots[1], tots[0]), HEAD_DIM, axis=1)
        o_ref[0, 0, rows, :] = (num / den).astype(o_ref.dtype)
        return carry

    lax.fori_loop(0, n_blocks, finish, 0, unroll=4)


def _mixer_a(qs, ks, vs, bias):
    npair, b, n_tiles, dil0, rows0, _ = qs[0].shape
    tile = dil0 * rows0
    s = n_tiles * tile
    cur = lambda bi, pi, si: (pi, bi, si, 0, 0, 0)
    prev = lambda bi, pi, si: (pi, bi, jnp.maximum(si - 1, 0), 0, 0, 0)
    in_specs, operands = [], []
    for dil, q, k, v in zip(MIX_LAYOUTS, qs, ks, vs):
        blk = (1, 1, SPAN // tile, dil, tile // dil, LANES)
        in_specs += [pl.BlockSpec(blk, cur), pl.BlockSpec(blk, prev), pl.BlockSpec(blk, cur),
                     pl.BlockSpec(blk, prev), pl.BlockSpec(blk, cur)]
        operands += [q, k, k, v, v]
    in_specs.append(pl.BlockSpec((bias.shape[0], 2) + bias.shape[2:], lambda bi, pi, si: (0, pi, 0, 0)))
    return pl.pallas_call(
        _mixer_a_kernel,
        grid=(b, npair, s // SPAN),
        in_specs=in_specs,
        out_specs=pl.BlockSpec((1, 1, SPAN, LANES), lambda bi, pi, si: (pi, bi, si, 0)),
        out_shape=jax.ShapeDtypeStruct((npair, b, s, LANES), BF16),
        scratch_shapes=[pltpu.VMEM((len(DIL_PATTERNS), 2, SPAN, LANES), F32),
                        pltpu.VMEM((len(DIL_PATTERNS), 2, SPAN, LANES), F32)],
        compiler_params=_cparams("parallel", "parallel", "parallel"),
        name="mixer_a",
    )(*operands, bias)


def _compress_kernel(c_ref, pa_ref, pb_ref, w1a_ref, w1b_ref, b1_ref, w2_ref, o_ref):
    nch = o_ref.shape[1]
    c = jnp.concatenate([c_ref[0, pl.ds(l, nch, stride=CMP_STRIDE), :] for l in range(CMP_STRIDE)], axis=1)
    xa = (c + pa_ref[...]).astype(BF16)
    xb = (c + pb_ref[...]).astype(BF16)
    ha = jnp.dot(xa, w1a_ref[...], preferred_element_type=F32)
    hb = jnp.dot(xb, w1b_ref[...], preferred_element_type=F32)
    hb_next = jnp.concatenate([hb[1:], jnp.zeros_like(hb[:1])], axis=0)
    hid = jax.nn.gelu(ha + hb_next + b1_ref[...])
    o_ref[0] = jnp.dot(hid.astype(BF16), w2_ref[...], preferred_element_type=F32).astype(o_ref.dtype)


def _compress(c, pos, w1, b1, w2):
    b, s, _ = c.shape
    nch = s // CMP_STRIDE
    half = CMP_LEN // 2
    zero = jnp.zeros((half, HEAD_DIM, CMP_HIDDEN), F32)

    def grouped(wpart):
        g0 = jnp.concatenate([wpart, zero], axis=1).reshape(half * LANES, CMP_HIDDEN)
        g1 = jnp.concatenate([zero, wpart], axis=1).reshape(half * LANES, CMP_HIDDEN)
        return jnp.concatenate([g0, g1], axis=1).astype(BF16)

    w1a, w1b = grouped(w1[:half]), grouped(w1[half:])
    pa = jnp.tile(pos[:half], (1, 2)).reshape(1, half * LANES)
    pb = jnp.tile(pos[half:], (1, 2)).reshape(1, half * LANES)
    b1g = jnp.tile(b1, 2).reshape(1, 2 * CMP_HIDDEN)
    zw = jnp.zeros_like(w2)
    w2g = jnp.concatenate([jnp.concatenate([w2, zw], axis=1),
                           jnp.concatenate([zw, w2], axis=1)], axis=0).astype(BF16)
    full = lambda a: pl.BlockSpec(a.shape, lambda i: (0,) * a.ndim)
    return pl.pallas_call(
        _compress_kernel,
        grid=(b,),
        in_specs=[pl.BlockSpec((1,) + c.shape[1:], lambda i: (i, 0, 0)),
                  full(pa), full(pb), full(w1a), full(w1b), full(b1g), full(w2g)],
        out_specs=pl.BlockSpec((1, nch, LANES), lambda i: (i, 0, 0)),
        out_shape=jax.ShapeDtypeStruct((b, nch, LANES), BF16),
        compiler_params=_cparams("parallel"),
        name="compress",
    )(c, pa, pb, w1a, w1b, b1g, w2g)


def _nsa_kernel(q_ref, gl_ref, kc_ref, vc_ref, wov_ref, ks_ref, blk_ref, vs_ref, bias_s_ref, kw_ref, vw_ref, bias_w_ref,
                o_ref, acc_ref, m_ref, s_ref, part_ref, sb_ref, *, nsel, n_bias, kw):
    qt = pl.program_id(1)
    tq = q_ref.shape[1]
    n_sub = tq // SUB_TILE
    sub_rows = B_GROUP * SUB_TILE
    gate = jax.nn.sigmoid(gl_ref[0])
    halves = [_half_mask(g) for g in range(B_KV_HEADS)]
    q_slots = [q_ref[0, :, r * LANES:(r + 1) * LANES] for r in range(B_GROUP)]
    zero = jnp.zeros((tq, LANES), BF16)

    def stacked(slots):
        return jnp.concatenate([slots[r][u * SUB_TILE:(u + 1) * SUB_TILE]
                                for u in range(n_sub) for r in range(B_GROUP)], axis=0)

    def head_rows(o, r):
        return jnp.concatenate([o[(u * B_GROUP + r) * SUB_TILE:(u * B_GROUP + r + 1) * SUB_TILE]
                                for u in range(n_sub)], axis=0)

    def gate_tile(r, branch):
        c0, c1 = (r * N_BRANCH + branch, (B_GROUP + r) * N_BRANCH + branch)
        return jnp.where(halves[0], gate[:, c0:c0 + 1], gate[:, c1:c1 + 1])

    def gated_slots(o, g, branch, slots):
        for r in range(B_GROUP):
            c = (g * B_GROUP + r) * N_BRANCH + branch
            o_r = head_rows(o, r) * gate[:, c:c + 1]
            slots[r] = o_r if slots[r] is None else jnp.where(halves[g], o_r, slots[r])

    def normalised_slots(accs, branch):
        slots = []
        for r in range(B_GROUP):
            a0, a1 = head_rows(accs[0], r), head_rows(accs[1], r)
            num = jnp.where(halves[0], a0, a1)
            den = pltpu.roll(jnp.where(halves[0], a1, a0), HEAD_DIM, axis=1)
            slots.append(num / jnp.maximum(den, 1e-30) * gate_tile(r, branch))
        return slots

    nck = kc_ref.shape[1]
    row = lax.broadcasted_iota(jnp.int32, (n_sub * sub_rows, 1), 0)
    t_row = qt * tq + (row // sub_rows) * SUB_TILE + row % SUB_TILE
    blk_end = lax.broadcasted_iota(jnp.int32, (1, nck), 1) * CMP_STRIDE + (CMP_LEN - 1)
    valid = blk_end <= t_row
    kc = kc_ref[0]
    vc = vc_ref[0]
    pc = []
    cmp_slots = [None] * B_GROUP
    for g in range(B_KV_HEADS):
        qg = stacked([jnp.where(halves[g], q_slots[r], zero) for r in range(B_GROUP)])
        s = lax.dot_general(qg, kc, NT_DIMS, preferred_element_type=F32)
        s = jnp.where(valid, s, -jnp.inf)
        m = jnp.max(s, axis=-1, keepdims=True)
        m = jnp.where(m == -jnp.inf, 0.0, m)
        p = jnp.exp2(s - m)
        den = jnp.sum(p, axis=-1, keepdims=True)
        p = p / jnp.maximum(den, 1e-30)
        pc.append(jnp.concatenate(
            [sum(p[(u * B_GROUP + r) * SUB_TILE:(u * B_GROUP + r + 1) * SUB_TILE] for r in range(B_GROUP))
             for u in range(n_sub)], axis=0))
        gated_slots(jnp.dot(p.astype(BF16), vc, preferred_element_type=F32), g, 0, cmp_slots)
    for r in range(B_GROUP):
        part_ref[0, :, r * LANES:(r + 1) * LANES] = cmp_slots[r]

    t_lane = qt * tq + lax.broadcasted_iota(jnp.int32, (1, tq), 1)
    cur = t_lane // SEL_BLOCK
    j = lax.broadcasted_iota(jnp.int32, (nsel, 1), 0)
    forced = (j == 0) | (j == cur) | (j == cur - 1)
    imps = []
    for g in reversed(range(B_KV_HEADS)):
        imp = lax.dot_general(wov_ref[...], pc[g], NT_DIMS, preferred_element_type=F32,
                              precision=lax.Precision.HIGHEST)
        imps.append(jnp.where(j > cur, -jnp.inf, jnp.where(forced, jnp.inf, imp)))

    last_block = (qt * tq + tq - 1) // SEL_BLOCK
    for level in range(-(-nsel // RANK_CHUNK)):
        n_live = min((level + 1) * RANK_CHUNK, nsel)

        @pl.when(last_block // RANK_CHUNK == level)
        def _(n_live=n_live):
            blocks = []
            for imp in imps:
                if n_live <= SEL_TOPN:
                    blocks.append(jnp.zeros((HEAD_DIM, tq), F32))
                    continue
                live = imp[:n_live]
                jl = j[:n_live]
                rank = jnp.zeros((n_live, tq), jnp.int32)
                for jp in range(n_live):
                    row = live[jp:jp + 1, :]
                    rank = rank + jnp.where(jl > jp, (row >= live).astype(jnp.int32), (row > live).astype(jnp.int32))
                blocks.append(jnp.where(rank < SEL_TOPN, 0.0, NOT_SELECTED))
                if n_live < HEAD_DIM:
                    blocks.append(jnp.zeros((HEAD_DIM - n_live, tq), F32))
            sb_ref[...] = jnp.concatenate(blocks, axis=0).T.astype(sb_ref.dtype)

    sb = sb_ref[...]
    qaug = [stacked([jnp.where(halves[g], q_slots[r], sb) for r in range(B_GROUP)])
            for g in range(B_KV_HEADS)]
    n_steps = (qt * tq + tq + K_TILE - 1) // K_TILE
    acc_ref[...] = jnp.zeros_like(acc_ref)
    m_ref[...] = jnp.full_like(m_ref, M_INIT)

    def scores(kt):
        for jt in range(K_TILE // SCORE_TILE):
            k0 = pl.multiple_of(kt * K_TILE + jt * SCORE_TILE, SCORE_TILE)
            k2 = ks_ref[0, pl.ds(k0, SCORE_TILE), :]
            e2 = blk_ref[pl.ds(k0, SCORE_TILE), :]
            for g in range(B_KV_HEADS):
                k = jnp.where(halves[g], k2, e2)
                s = lax.dot_general(qaug[g], k, NT_DIMS, preferred_element_type=F32)
                for u in range(n_sub):
                    rows_u = slice(u * sub_rows, (u + 1) * sub_rows)
                    for c in range(SCORE_TILE // SUB_TILE):
                        bi = jnp.clip(qt * n_sub + u - k0 // SUB_TILE - c + 1, 0, n_bias - 1)
                        col = jt * SCORE_TILE + c * SUB_TILE
                        s_ref[kt % 2, g, rows_u, col:col + SUB_TILE] = (
                            s[rows_u, c * SUB_TILE:(c + 1) * SUB_TILE] + bias_s_ref[g, bi])

    def accumulate(kt):
        k0 = pl.multiple_of(kt * K_TILE, K_TILE)
        v2 = vs_ref[0, pl.ds(k0, K_TILE), :]
        for g in range(B_KV_HEADS):
            v = jnp.where(halves[g], v2, jnp.ones((K_TILE, LANES), BF16))
            s = s_ref[kt % 2, g]
            m_old = m_ref[g]
            m_new = jnp.maximum(m_old, jnp.max(s, axis=-1, keepdims=True))
            alpha = jnp.exp2(m_old - m_new)
            p = jnp.exp2(s - jnp.tile(m_new, (1, K_TILE // LANES)))
            acc_ref[g] = alpha * acc_ref[g] + jnp.dot(p.astype(BF16), v, preferred_element_type=F32)
            m_ref[g] = m_new

    def body(kt, carry):
        accumulate(kt)
        scores(kt + 1)
        return carry

    n_wt = kw // SUB_TILE
    q_win = [stacked([jnp.where(halves[g], q_slots[r], zero) for r in range(B_GROUP)]) for g in range(B_KV_HEADS)]
    o_win = [[] for _ in range(B_KV_HEADS)]
    for u in range(n_sub):
        sub = qt * n_sub + u
        w0 = pl.multiple_of(jnp.maximum(sub * SUB_TILE - WIN_LEN, 0), SUB_TILE)
        skip = jnp.maximum(WIN_LEN // SUB_TILE - sub, 0)
        kwin = kw_ref[0, pl.ds(w0, kw), :]
        vwin = vw_ref[0, pl.ds(w0, kw), :]
        for g in range(B_KV_HEADS):
            s = lax.dot_general(q_win[g][u * sub_rows:(u + 1) * sub_rows], kwin, NT_DIMS, preferred_element_type=F32)
            s = jnp.concatenate([s[:, jw * SUB_TILE:(jw + 1) * SUB_TILE] + bias_w_ref[g, jnp.minimum(jw + skip, n_wt)]
                                 for jw in range(n_wt)], axis=1)
            p = jnp.exp2(s - jnp.max(s, axis=-1, keepdims=True))
            acc = jnp.dot(p.astype(BF16), jnp.where(halves[g], vwin, jnp.ones_like(vwin)),
                          preferred_element_type=F32)
            o_win[g].append(acc)
    win_slots = normalised_slots([jnp.concatenate(o_win[g], axis=0) for g in range(B_KV_HEADS)], 2)
    for r in range(B_GROUP):
        part_ref[1, :, r * LANES:(r + 1) * LANES] = win_slots[r]

    scores(0)
    lax.fori_loop(0, n_steps - 1, body, 0)
    accumulate(n_steps - 1)
    sel_slots = normalised_slots([acc_ref[g] for g in range(B_KV_HEADS)], 1)
    for r in range(B_GROUP):
        sl = slice(r * LANES, (r + 1) * LANES)
        o_ref[0, :, sl] = ((part_ref[0, :, sl] + sel_slots[r]) + part_ref[1, :, sl]).astype(o_ref.dtype)


def _nsa(qb, gl, kcmp, vcmp, ksl, vsl, bias_s, kwin, vwin, bias_w):
    b, s, w = qb.shape
    nsel = s // SEL_BLOCK
    nck = kcmp.shape[1]
    ratio = SEL_BLOCK // CMP_STRIDE
    wov = np.zeros((nsel, nck), np.float32)
    for jj in range(nsel):
        for off, wt in zip(range(-1, ratio), CMP_OVERLAP):
            n = ratio * jj + off
            if 0 <= n < nck - 1:
                wov[jj, n] = wt
    wov = jnp.asarray(wov)
    blk = jnp.asarray((np.arange(s)[:, None] // SEL_BLOCK) == (np.arange(LANES)[None, :] % HEAD_DIM), BF16)
    rows = B_GROUP * Q_TILE
    tile = lambda width: pl.BlockSpec((1, Q_TILE, width), lambda bi, qi: (bi, qi, 0))
    whole = lambda a: pl.BlockSpec((1,) + a.shape[1:], lambda bi, qi: (bi, 0, 0))
    const = lambda a: pl.BlockSpec(a.shape, lambda bi, qi: (0,) * a.ndim, pipeline_mode=pl.Buffered(1))
    return pl.pallas_call(
        functools.partial(_nsa_kernel, nsel=nsel, n_bias=bias_s.shape[1], kw=WIN_LEN + SUB_TILE),
        grid=(b, s // Q_TILE),
        in_specs=[tile(w), tile(LANES), whole(kcmp), whole(vcmp), const(wov), whole(ksl), const(blk), whole(vsl),
                  const(bias_s), whole(kwin), whole(vwin), const(bias_w)],
        out_specs=tile(w),
        out_shape=jax.ShapeDtypeStruct((b, s, w), BF16),
        scratch_shapes=[pltpu.VMEM((B_KV_HEADS, rows, LANES), F32), pltpu.VMEM((B_KV_HEADS, rows, LANES), F32),
                        pltpu.VMEM((2, B_KV_HEADS, rows, K_TILE), F32), pltpu.VMEM((2, Q_TILE, w), F32),
                        pltpu.VMEM((Q_TILE, LANES), BF16)],
        compiler_params=_cparams("parallel", "parallel"),
        name="nsa",
    )(qb, gl, kcmp, vcmp, wov, ksl, blk, vsl, bias_s, kwin, vwin, bias_w)


def _tail_kernel(x_ref, oa_ref, ob_ref, wa_ref, wb_ref, g2_ref, wg_ref, wu_ref, wd_ref, gf_ref, o_ref):
    o_a = jnp.concatenate([oa_ref[p] for p in range(oa_ref.shape[0])], axis=1)
    mix = jnp.dot(o_a, wa_ref[...], preferred_element_type=F32)
    mix = mix + jnp.dot(ob_ref[...], wb_ref[...], preferred_element_type=F32)
    h = x_ref[...] + mix
    y = h * lax.rsqrt(jnp.mean(h * h, axis=-1, keepdims=True) + RMS_EPS)
    hn = (y * g2_ref[...]).astype(BF16)
    a = jnp.dot(hn, wg_ref[...], preferred_element_type=F32)
    u = jnp.dot(hn, wu_ref[...], preferred_element_type=F32)
    act = (jax.nn.silu(a) * u).astype(BF16)
    h2 = h + jnp.dot(act, wd_ref[...], preferred_element_type=F32)
    y2 = h2 * lax.rsqrt(jnp.mean(h2 * h2, axis=-1, keepdims=True) + RMS_EPS)
    o_ref[...] = y2 * gf_ref[...]


def _tail(x2, oa, ob, wa, wb, g2, wg, wu, wd, gf, tm):
    n, d = x2.shape
    row = lambda width: pl.BlockSpec((tm, width), lambda i: (i, 0))
    const = lambda a: pl.BlockSpec(a.shape, lambda i: (0, 0), pipeline_mode=pl.Buffered(1))
    return pl.pallas_call(
        _tail_kernel,
        grid=(n // tm,),
        in_specs=[row(d), pl.BlockSpec((oa.shape[0], tm, LANES), lambda i: (0, i, 0)), row(ob.shape[1]),
                  const(wa), const(wb), const(g2), const(wg), const(wu), const(wd), const(gf)],
        out_specs=row(d),
        out_shape=jax.ShapeDtypeStruct((n, d), F32),
        compiler_params=_cparams("parallel"),
        name="tail",
    )(x2, oa, ob, wa, wb, g2, wg, wu, wd, gf)


def _slot_perm():
    perm = np.zeros(B_WIDTH, np.int64)
    for r in range(B_GROUP):
        for g in range(B_KV_HEADS):
            src = (g * B_GROUP + r) * HEAD_DIM
            dst = r * LANES + g * HEAD_DIM
            perm[dst:dst + HEAD_DIM] = np.arange(src, src + HEAD_DIM)
    return perm


def _layer(h, tab_a, tab_b, norm1_g, w_in, cmp_pos, k_w1, k_b1, k_w2, v_w1, v_b1, v_w2,
           w_out, norm2_g, w_gate, w_up, w_down, norm_f_g):
    b, s, d = h.shape
    n = b * s
    assert s % SPAN == 0 and s % K_TILE == 0
    perm = _slot_perm()

    cols = np.cumsum([0, A_WIDTH, A_WIDTH, A_WIDTH, B_WIDTH] + [KV_WIDTH] * 6 + [GATE_WIDTH])
    w_aq = w_in[:, cols[0]:cols[1]] * (SCALE * LOG2E)
    w_bq = (w_in[:, cols[3]:cols[4]] * (SCALE * LOG2E))[:, perm]
    w_gl = jnp.pad(w_in[:, cols[10]:cols[11]], ((0, 0), (0, LANES - GATE_WIDTH)))
    w1 = jnp.concatenate([w_aq, w_in[:, cols[1]:cols[3]], w_bq, w_in[:, cols[4]:cols[10]], w_gl], axis=1).astype(BF16)
    widths = (A_WIDTH,) * 3 + (B_WIDTH,) + (KV_WIDTH,) * 6 + (LANES,)
    dtypes = (BF16,) * 4 + (F32,) * 2 + (BF16,) * 4 + (F32,)
    x2 = h.reshape(n, d)
    n_lay = len(MIX_LAYOUTS)
    outs = _inproj(h, norm1_g.reshape(1, d), w1, widths, dtypes, n_mix=3, tm=512)
    qa, ka, va = (outs[i * n_lay:(i + 1) * n_lay] for i in range(3))
    qb, kc, vc, ksl, vsl, kw, vw, gl = outs[3 * n_lay:]
    r3 = lambda t: t.reshape(b, s, t.shape[-1])
    qb, ksl, vsl, kw, vw, gl = map(r3, (qb, ksl, vsl, kw, vw, gl))

    tok_dist, in_window = [], []
    for window, dil in DIL_PATTERNS:
        fan = (dil if dil in MIX_LAYOUTS else MIX_LAYOUTS[0]) // dil
        row = np.arange(DIL_BLOCK)
        step = fan * (row % (DIL_BLOCK // fan)) + row // (DIL_BLOCK // fan)
        key_step = np.concatenate([step, DIL_BLOCK + step])
        dist = DIL_BLOCK + step[:, None] - key_step[None, :]
        tok_dist.append(np.maximum(dist, 0) * dil)
        in_window.append((dist >= 0) & (dist <= window // dil))
    tok_dist, in_window = np.stack(tok_dist), np.stack(in_window)
    bias_a = _bias_tiles(tab_a, tok_dist, in_window, inner=A_HEADS)[0]
    o_a = _mixer_a(qa, ka, va, bias_a).reshape(A_HEADS // 2, n, LANES)

    nch = s // CMP_STRIDE
    kcmp = _compress(kc.reshape(b, s, KV_WIDTH), cmp_pos, k_w1, k_b1, k_w2)
    vcmp = _compress(vc.reshape(b, s, KV_WIDTH), cmp_pos, v_w1, v_b1, v_w2)
    d_sat = int(np.nonzero(_t5_bucket_np(np.arange(s)) < REL_BUCKETS - 1)[0].max()) + 1
    first_far = -(-(d_sat + SUB_TILE - 1) // SUB_TILE)
    n_bias = min(s // SUB_TILE, first_far + 1) + 1
    dd = ((np.arange(n_bias)[:, None, None] - 1) * SUB_TILE + np.arange(SUB_TILE)[None, :, None]
          - np.arange(SUB_TILE)[None, None, :])
    bias_sl = _bias_tiles(tab_b, dd, dd >= 0, inner=B_GROUP)
    bias_sl = bias_sl.reshape(B_KV_HEADS, n_bias, B_GROUP * SUB_TILE, SUB_TILE)
    n_wt = WIN_LEN // SUB_TILE + 1
    dw = ((n_wt - 1 - np.arange(n_wt + 1))[:, None, None] * SUB_TILE + np.arange(SUB_TILE)[None, :, None]
          - np.arange(SUB_TILE)[None, None, :])
    in_win = (dw >= 0) & (dw < WIN_LEN) & (np.arange(n_wt + 1) < n_wt)[:, None, None]
    bias_w = _bias_tiles(tab_b, dw, in_win, inner=B_GROUP)
    bias_w = bias_w.reshape(B_KV_HEADS, n_wt + 1, B_GROUP * SUB_TILE, SUB_TILE)
    o_b = _nsa(qb, gl, kcmp, vcmp, ksl, vsl, bias_sl, kw, vw, bias_w)

    wa = w_out[:A_WIDTH].astype(BF16)
    wb = w_out[A_WIDTH:][perm].astype(BF16)
    out = _tail(x2, o_a, o_b.reshape(n, B_WIDTH), wa, wb, norm2_g.reshape(1, d), w_gate.astype(BF16),
                w_up.astype(BF16), w_down.astype(BF16), norm_f_g.reshape(1, d), tm=512)
    return out.reshape(b, s, d)


def kernel(x, norm1_g, w_in, rel_bias, cmp_pos, cmp_k_w1, cmp_k_b1, cmp_k_w2, cmp_v_w1, cmp_v_b1, cmp_v_w2,
           w_out, norm2_g, w_gate, w_up, w_down, norm_f_g):
    assert w_in.shape[0] == 1, "single-layer model"
    tab_a = rel_bias[:, :A_HEADS].T * LOG2E
    tab_b = rel_bias[:, A_HEADS:].T * LOG2E
    return _layer(x, tab_a, tab_b, norm1_g[0], w_in[0], cmp_pos[0], cmp_k_w1[0], cmp_k_b1[0], cmp_k_w2[0],
                  cmp_v_w1[0], cmp_v_b1[0], cmp_v_w2[0], w_out[0], norm2_g[0], w_gate[0], w_up[0], w_down[0],
                  norm_f_g)
```

```python
import functools
import math

import numpy as np
import jax
import jax.numpy as jnp
from jax import lax
from jax.experimental import pallas as pl
from jax.experimental.pallas import tpu as pltpu

F32 = jnp.float32
BF16 = jnp.bfloat16

HEAD_DIM = 64
LANES = 128
A_HEADS = 8
DIL_PATTERNS = ((128, 1), (512, 4), (2048, 16))
DIL_BLOCK = 128
B_HEADS = 8
B_KV_HEADS = 2
B_GROUP = B_HEADS // B_KV_HEADS
CMP_LEN = 32
CMP_STRIDE = 16
CMP_HIDDEN = 128
SEL_BLOCK = 64
SEL_TOPN = 16
CMP_OVERLAP = (1.0, 2.0, 2.0, 2.0, 1.0)
WIN_LEN = 512
N_BRANCH = 3
REL_BUCKETS = 32
REL_MAX_DIST = 2048
RMS_EPS = 1e-6
A_WIDTH = A_HEADS * HEAD_DIM
B_WIDTH = B_HEADS * HEAD_DIM
KV_WIDTH = B_KV_HEADS * HEAD_DIM
GATE_WIDTH = B_HEADS * N_BRANCH
SCALE = HEAD_DIM ** -0.5
LOG2E = math.log2(math.e)

SPAN = DIL_PATTERNS[-1][1] * DIL_BLOCK
MIX_LAYOUTS = tuple(d for _, d in DIL_PATTERNS if d > 1)
NARROW_FROM = 4
RANK_CHUNK = 8
SUB_TILE = 128
Q_TILE = 512
K_TILE = 512
SCORE_TILE = 256
NOT_SELECTED = -(2.0 ** 100)
M_INIT = -1e30
VMEM_LIMIT = 56 * 1024 * 1024

NT_DIMS = (((1,), (1,)), ((), ()))


def _cparams(*sem):
    return pltpu.CompilerParams(dimension_semantics=sem, vmem_limit_bytes=VMEM_LIMIT)


def _t5_bucket_np(dist):
    max_exact = REL_BUCKETS // 2
    d = np.asarray(dist)
    df = np.maximum(d, 1).astype(np.float32)
    large = max_exact + (np.log(df / np.float32(max_exact)) / np.float32(math.log(REL_MAX_DIST / max_exact))
                         * np.float32(REL_BUCKETS - max_exact)).astype(np.int32)
    large = np.minimum(large, REL_BUCKETS - 1)
    return np.where(d < max_exact, d, large).astype(np.int32)


def _bias_kernel(tab_ref, idx_ref, o_ref, *, buckets):
    inner = o_ref.shape[2]
    for t, present in enumerate(buckets):
        idx = idx_ref[t]
        accs = [jnp.full(idx.shape, -jnp.inf, F32) for _ in range(inner)]
        for bucket in present:
            hit = idx == bucket
            for hi in range(inner):
                head = pl.program_id(0) * inner + hi
                accs[hi] = jnp.where(hit, tab_ref[head * REL_BUCKETS + bucket], accs[hi])
        for hi in range(inner):
            o_ref[0, t, hi] = accs[hi]


def _bias_tiles(tab, dist, valid, inner):
    h = tab.shape[0]
    t, r, c = dist.shape
    idx = np.where(valid, _t5_bucket_np(np.maximum(dist, 0)), -1).astype(np.int32)
    buckets = tuple(tuple(int(v) for v in np.unique(idx[ti]) if v >= 0) for ti in range(t))
    return pl.pallas_call(
        functools.partial(_bias_kernel, buckets=buckets),
        grid=(h // inner,),
        in_specs=[pl.BlockSpec(memory_space=pltpu.SMEM),
                  pl.BlockSpec((t, r, c), lambda a: (0, 0, 0))],
        out_specs=pl.BlockSpec((1, t, inner, r, c), lambda a: (a, 0, 0, 0, 0)),
        out_shape=jax.ShapeDtypeStruct((h // inner, t, inner, r, c), F32),
        compiler_params=_cparams("parallel"),
        name="bias_tiles",
    )(tab.reshape(-1), jnp.asarray(idx))


def _half_mask(g):
    lane = lax.broadcasted_iota(jnp.int32, (1, LANES), 1)
    return (lane >= HEAD_DIM) if g else (lane < HEAD_DIM)


def _inproj_kernel(x_ref, g_ref, w_ref, *refs, widths, n_mix):
    n_dil = len(DIL_PATTERNS)
    n_lay = len(MIX_LAYOUTS)
    n_scr = n_mix * (n_dil - 1)
    mix_refs = [(None,) * (n_dil - n_lay) + tuple(refs[i * n_lay:(i + 1) * n_lay]) for i in range(n_mix)]
    out_refs = refs[n_mix * n_lay:len(refs) - n_scr]
    scratch = [refs[len(refs) - n_scr + i * (n_dil - 1):len(refs) - n_scr + (i + 1) * (n_dil - 1)]
               for i in range(n_mix)]
    x = x_ref[...]
    tm = x.shape[0]
    y = x * lax.rsqrt(jnp.mean(x * x, axis=-1, keepdims=True) + RMS_EPS)
    xn = (y * g_ref[...]).astype(BF16)
    merged = jnp.dot(xn, w_ref[:, sum(widths[:NARROW_FROM]):], preferred_element_type=F32)
    start = 0
    for idx, w in enumerate(widths):
        if idx < NARROW_FROM:
            r = jnp.dot(xn, w_ref[:, start:start + w], preferred_element_type=F32)
        else:
            off = start - sum(widths[:NARROW_FROM])
            r = merged[:, off:off + w]
        if idx < n_mix:
            prev_dil = None
            for pi, (o_ref, (_, dil)) in enumerate(zip(mix_refs[idx], DIL_PATTERNS)):
                keep = scratch[idx][pi] if pi < n_dil - 1 else None
                for p in range(w // LANES):
                    if prev_dil is None:
                        planes = [(0, r[:, p * LANES:(p + 1) * LANES])]
                    else:
                        f = dil // prev_dil
                        planes = [(res + prev_dil * a,
                                   scratch[idx][pi - 1][p, res, pl.ds(a, tm // dil, stride=f), :])
                                  for res in range(prev_dil) for a in range(f)]
                    for res, rows in planes:
                        if o_ref is not None:
                            o_ref[p, 0, 0, res] = rows.astype(o_ref.dtype)
                        if keep is not None:
                            keep[p, res] = rows
                prev_dil = dil
        else:
            o_ref = out_refs[idx - n_mix]
            o_ref[...] = r.astype(o_ref.dtype)
        start += w


def _inproj(x3, g, w, widths, dtypes, n_mix, tm):
    b, s, d = x3.shape
    n = b * s
    per_batch = s // tm
    out_shape, out_specs = [], []
    for w_, dt in zip(widths[:n_mix], dtypes[:n_mix]):
        for dil in MIX_LAYOUTS:
            out_shape.append(jax.ShapeDtypeStruct((w_ // LANES, b, per_batch, dil, tm // dil, LANES), dt))
            out_specs.append(pl.BlockSpec((w_ // LANES, 1, 1, dil, tm // dil, LANES),
                                          lambda i: (0, i // per_batch, i % per_batch, 0, 0, 0)))
    for w_, dt in zip(widths[n_mix:], dtypes[n_mix:]):
        out_shape.append(jax.ShapeDtypeStruct((n, w_), dt))
        out_specs.append(pl.BlockSpec((tm, w_), lambda i: (i, 0)))
    return pl.pallas_call(
        functools.partial(_inproj_kernel, widths=widths, n_mix=n_mix),
        grid=(n // tm,),
        in_specs=[pl.BlockSpec((tm, d), lambda i: (i, 0)),
                  pl.BlockSpec((1, d), lambda i: (0, 0)),
                  pl.BlockSpec(w.shape, lambda i: (0, 0))],
        out_specs=out_specs,
        out_shape=out_shape,
        scratch_shapes=[pltpu.VMEM((w_ // LANES, dil, tm // dil, LANES), F32)
                        for w_ in widths[:n_mix] for _, dil in DIL_PATTERNS[:-1]],
        compiler_params=_cparams("parallel"),
        name="inproj",
    )(x3.reshape(n, d), g, w)


def _mixer_a_kernel(*refs):
    bias_ref, o_ref, acc_scr, m_scr = refs[5 * len(MIX_LAYOUTS):]
    sb = pl.program_id(2)
    halves = [_half_mask(hh) for hh in range(2)]
    col = lax.broadcasted_iota(jnp.int32, (1, 2 * DIL_BLOCK), 1)
    first_keep = (col >= DIL_BLOCK) | (sb > 0)
    ones = jnp.ones((2 * DIL_BLOCK, LANES), BF16)
    n_blocks = SPAN // DIL_BLOCK
    for p, (_, dil) in enumerate(DIL_PATTERNS):
        lay = MIX_LAYOUTS.index(dil) if dil in MIX_LAYOUTS else 0
        src = MIX_LAYOUTS[lay]
        fan = src // dil
        piece = DIL_BLOCK // fan
        q_ref, kp_ref, kc_ref, vp_ref, vc_ref = refs[5 * lay:5 * lay + 5]
        last = SPAN // src // piece - 1

        def gather(ref, r, nn, dil=dil, fan=fan, piece=piece):
            per_tile = ref.shape[4]
            chunk = min(piece, per_tile)
            return jnp.concatenate(
                [ref[0, 0, lo // per_tile, r + dil * a, lo % per_tile:lo % per_tile + chunk, :]
                 for a in range(fan) for lo in range(nn * piece, (nn + 1) * piece, chunk)], axis=0)

        for r in range(dil):
            for n in range(SPAN // (dil * DIL_BLOCK)):
                q2 = gather(q_ref, r, n)
                if n == 0:
                    k_prev, v_prev = gather(kp_ref, r, last), gather(vp_ref, r, last)
                else:
                    k_prev, v_prev = gather(kc_ref, r, n - 1), gather(vc_ref, r, n - 1)
                kcat = jnp.concatenate([k_prev, gather(kc_ref, r, n)], axis=0)
                vcat = jnp.concatenate([v_prev, gather(vc_ref, r, n)], axis=0)
                qm = jnp.concatenate([jnp.where(halves[hh], q2, jnp.zeros_like(q2)) for hh in range(2)], axis=0)
                s2 = lax.dot_general(qm, kcat, NT_DIMS, preferred_element_type=F32)
                for hh in range(2):
                    s = s2[hh * DIL_BLOCK:(hh + 1) * DIL_BLOCK] + bias_ref[p, hh]
                    if n == 0:
                        s = jnp.where(first_keep, s, -jnp.inf)
                    m_blk = jnp.max(s, axis=-1, keepdims=True)
                    pe = jnp.exp2(s - m_blk)
                    pv = jnp.dot(pe.astype(BF16), jnp.where(halves[hh], vcat, ones), preferred_element_type=F32)
                    m_b = jnp.broadcast_to(m_blk, pv.shape)
                    for a in range(fan):
                        rows_t = pl.ds(n * dil * DIL_BLOCK + r + dil * a, piece, stride=src)
                        acc_scr[p, hh, rows_t, :] = pv[a * piece:(a + 1) * piece]
                        m_scr[p, hh, rows_t, :] = m_b[a * piece:(a + 1) * piece]

    def finish(c, carry):
        rows = pl.ds(pl.multiple_of(c * DIL_BLOCK, DIL_BLOCK), DIL_BLOCK)
        tots = []
        for hh in range(2):
            ms = [m_scr[p, hh, rows, :] for p in range(len(DIL_PATTERNS))]
            m_all = functools.reduce(jnp.maximum, ms)
            tots.append(sum(jnp.exp2(m - m_all) * acc_scr[p, hh, rows, :] for p, m in enumerate(ms)))
        num = jnp.where(halves[0], tots[0], tots[1])
        den = pltpu.roll(jnp.where(halves[0], tots[1], tots[0]), HEAD_DIM, axis=1)
        o_ref[0, 0, rows, :] = (num / den).astype(o_ref.dtype)
        return carry

    lax.fori_loop(0, n_blocks, finish, 0, unroll=4)


def _mixer_a(qs, ks, vs, bias):
    npair, b, n_tiles, dil0, rows0, _ = qs[0].shape
    tile = dil0 * rows0
    s = n_tiles * tile
    cur = lambda bi, pi, si: (pi, bi, si, 0, 0, 0)
    prev = lambda bi, pi, si: (pi, bi, jnp.maximum(si - 1, 0), 0, 0, 0)
    in_specs, operands = [], []
    for dil, q, k, v in zip(MIX_LAYOUTS, qs, ks, vs):
        blk = (1, 1, SPAN // tile, dil, tile // dil, LANES)
        in_specs += [pl.BlockSpec(blk, cur), pl.BlockSpec(blk, prev), pl.BlockSpec(blk, cur),
                     pl.BlockSpec(blk, prev), pl.BlockSpec(blk, cur)]
        operands += [q, k, k, v, v]
    in_specs.append(pl.BlockSpec((bias.shape[0], 2) + bias.shape[2:], lambda bi, pi, si: (0, pi, 0, 0)))
    return pl.pallas_call(
        _mixer_a_kernel,
        grid=(b, npair, s // SPAN),
        in_specs=in_specs,
        out_specs=pl.BlockSpec((1, 1, SPAN, LANES), lambda bi, pi, si: (pi, bi, si, 0)),
        out_shape=jax.ShapeDtypeStruct((npair, b, s, LANES), BF16),
        scratch_shapes=[pltpu.VMEM((len(DIL_PATTERNS), 2, SPAN, LANES), F32),
                        pltpu.VMEM((len(DIL_PATTERNS), 2, SPAN, LANES), F32)],
        compiler_params=_cparams("parallel", "parallel", "parallel"),
        name="mixer_a",
    )(*operands, bias)


def _compress_kernel(pa_ref, pb_ref, *refs):
    n_streams = len(refs) // 6
    for i in range(n_streams):
        c_ref, w1a_ref, w1b_ref, b1_ref, w2_ref = refs[5 * i:5 * i + 5]
        o_ref = refs[5 * n_streams + i]
        nch = o_ref.shape[1]
        c = jnp.concatenate([c_ref[0, pl.ds(l, nch, stride=CMP_STRIDE), :] for l in range(CMP_STRIDE)], axis=1)
        xa = (c + pa_ref[...]).astype(BF16)
        xb = (c + pb_ref[...]).astype(BF16)
        ha = jnp.dot(xa, w1a_ref[...], preferred_element_type=F32)
        hb = jnp.dot(xb, w1b_ref[...], preferred_element_type=F32)
        hb_next = jnp.concatenate([hb[1:], jnp.zeros_like(hb[:1])], axis=0)
        hid = jax.nn.gelu(ha + hb_next + b1_ref[...])
        o_ref[0] = jnp.dot(hid.astype(BF16), w2_ref[...], preferred_element_type=F32).astype(o_ref.dtype)


def _compress(cs, pos, params):
    b, s, _ = cs[0][0].shape
    nch = s // CMP_STRIDE
    half = CMP_LEN // 2
    zero = jnp.zeros((half, HEAD_DIM, CMP_HIDDEN), F32)

    def grouped(wpart):
        g0 = jnp.concatenate([wpart, zero], axis=1).reshape(half * LANES, CMP_HIDDEN)
        g1 = jnp.concatenate([zero, wpart], axis=1).reshape(half * LANES, CMP_HIDDEN)
        return jnp.concatenate([g0, g1], axis=1).astype(BF16)

    pa = jnp.tile(pos[:half], (1, 2)).reshape(1, half * LANES)
    pb = jnp.tile(pos[half:], (1, 2)).reshape(1, half * LANES)
    full = lambda a: pl.BlockSpec(a.shape, lambda i: (0,) * a.ndim)
    operands, in_specs = [pa, pb], [full(pa), full(pb)]
    for (c, slot), (w1, b1, w2) in zip(cs, params):
        zw = jnp.zeros_like(w2)
        w2g = jnp.concatenate([jnp.concatenate([w2, zw], axis=1),
                               jnp.concatenate([zw, w2], axis=1)], axis=0).astype(BF16)
        weights = [grouped(w1[:half]), grouped(w1[half:]), jnp.tile(b1, 2).reshape(1, 2 * CMP_HIDDEN), w2g]
        operands += [c] + weights
        in_specs += [pl.BlockSpec((1, s, LANES), lambda i, slot=slot: (i, 0, slot))] + [full(w_) for w_ in weights]
    return pl.pallas_call(
        _compress_kernel,
        grid=(b,),
        in_specs=in_specs,
        out_specs=[pl.BlockSpec((1, nch, LANES), lambda i: (i, 0, 0))] * len(cs),
        out_shape=[jax.ShapeDtypeStruct((b, nch, LANES), BF16)] * len(cs),
        compiler_params=_cparams("parallel"),
        name="compress",
    )(*operands)


def _nsa_kernel(q_ref, gl_ref, kc_ref, vc_ref, wov_ref, ks_ref, blk_ref, vs_ref, bias_s_ref, kw_ref, vw_ref, bias_w_ref,
                o_ref, acc_ref, m_ref, s_ref, part_ref, sb_ref, *, nsel, n_bias, kw):
    qt = pl.program_id(1)
    tq = q_ref.shape[1]
    n_sub = tq // SUB_TILE
    sub_rows = B_GROUP * SUB_TILE
    gate = jax.nn.sigmoid(gl_ref[0])
    halves = [_half_mask(g) for g in range(B_KV_HEADS)]
    q_slots = [q_ref[0, :, r * LANES:(r + 1) * LANES] for r in range(B_GROUP)]
    zero = jnp.zeros((tq, LANES), BF16)

    def stacked(slots):
        return jnp.concatenate([slots[r][u * SUB_TILE:(u + 1) * SUB_TILE]
                                for u in range(n_sub) for r in range(B_GROUP)], axis=0)

    def head_rows(o, r):
        return jnp.concatenate([o[(u * B_GROUP + r) * SUB_TILE:(u * B_GROUP + r + 1) * SUB_TILE]
                                for u in range(n_sub)], axis=0)

    def gate_tile(r, branch):
        c0, c1 = (r * N_BRANCH + branch, (B_GROUP + r) * N_BRANCH + branch)
        return jnp.where(halves[0], gate[:, c0:c0 + 1], gate[:, c1:c1 + 1])

    def gated_slots(o, g, branch, slots):
        for r in range(B_GROUP):
            c = (g * B_GROUP + r) * N_BRANCH + branch
            o_r = head_rows(o, r) * gate[:, c:c + 1]
            slots[r] = o_r if slots[r] is None else jnp.where(halves[g], o_r, slots[r])

    def normalised_slots(accs, branch):
        slots = []
        for r in range(B_GROUP):
            a0, a1 = head_rows(accs[0], r), head_rows(accs[1], r)
            num = jnp.where(halves[0], a0, a1)
            den = pltpu.roll(jnp.where(halves[0], a1, a0), HEAD_DIM, axis=1)
            slots.append(num / jnp.maximum(den, 1e-30) * gate_tile(r, branch))
        return slots

    nck = kc_ref.shape[1]
    row = lax.broadcasted_iota(jnp.int32, (n_sub * sub_rows, 1), 0)
    t_row = qt * tq + (row // sub_rows) * SUB_TILE + row % SUB_TILE
    blk_end = lax.broadcasted_iota(jnp.int32, (1, nck), 1) * CMP_STRIDE + (CMP_LEN - 1)
    valid = blk_end <= t_row
    kc = kc_ref[0]
    vc = vc_ref[0]
    n_rows = n_sub * sub_rows
    q_all = jnp.concatenate([stacked([jnp.where(halves[g], q_slots[r], zero) for r in range(B_GROUP)])
                             for g in range(B_KV_HEADS)], axis=0)
    s = lax.dot_general(q_all, kc, NT_DIMS, preferred_element_type=F32)
    s = jnp.where(jnp.concatenate([valid] * B_KV_HEADS, axis=0), s, -jnp.inf)
    m = jnp.max(s, axis=-1, keepdims=True)
    m = jnp.where(m == -jnp.inf, 0.0, m)
    p = jnp.exp2(s - m)
    den = jnp.sum(p, axis=-1, keepdims=True)
    p = p / jnp.maximum(den, 1e-30)
    o_all = jnp.dot(p.astype(BF16), vc, preferred_element_type=F32)
    pc = []
    cmp_slots = [None] * B_GROUP
    for g in range(B_KV_HEADS):
        pg = p[g * n_rows:(g + 1) * n_rows]
        pc.append(jnp.concatenate(
            [sum(pg[(u * B_GROUP + r) * SUB_TILE:(u * B_GROUP + r + 1) * SUB_TILE] for r in range(B_GROUP))
             for u in range(n_sub)], axis=0))
        gated_slots(o_all[g * n_rows:(g + 1) * n_rows], g, 0, cmp_slots)
    for r in range(B_GROUP):
        part_ref[0, :, r * LANES:(r + 1) * LANES] = cmp_slots[r]

    t_lane = qt * tq + lax.broadcasted_iota(jnp.int32, (1, tq), 1)
    cur = t_lane // SEL_BLOCK
    j = lax.broadcasted_iota(jnp.int32, (nsel, 1), 0)
    forced = (j == 0) | (j == cur) | (j == cur - 1)
    imps = []
    for g in reversed(range(B_KV_HEADS)):
        imp = lax.dot_general(wov_ref[...], pc[g], NT_DIMS, preferred_element_type=F32,
                              precision=lax.Precision.HIGHEST)
        imps.append(jnp.where(j > cur, -jnp.inf, jnp.where(forced, jnp.inf, imp)))

    last_block = (qt * tq + tq - 1) // SEL_BLOCK
    for level in range(-(-nsel // RANK_CHUNK)):
        n_live = min((level + 1) * RANK_CHUNK, nsel)

        @pl.when(last_block // RANK_CHUNK == level)
        def _(n_live=n_live):
            blocks = []
            for imp in imps:
                if n_live <= SEL_TOPN:
                    blocks.append(jnp.zeros((HEAD_DIM, tq), F32))
                    continue
                live = imp[:n_live]
                jl = j[:n_live]
                rank = jnp.zeros((n_live, tq), jnp.int32)
                for jp in range(n_live):
                    row = live[jp:jp + 1, :]
                    rank = rank + jnp.where(jl > jp, (row >= live).astype(jnp.int32), (row > live).astype(jnp.int32))
                blocks.append(jnp.where(rank < SEL_TOPN, 0.0, NOT_SELECTED))
                if n_live < HEAD_DIM:
                    blocks.append(jnp.zeros((HEAD_DIM - n_live, tq), F32))
            sb_ref[...] = jnp.concatenate(blocks, axis=0).T.astype(sb_ref.dtype)

    sb = sb_ref[...]
    qaug = [stacked([jnp.where(halves[g], q_slots[r], sb) for r in range(B_GROUP)])
            for g in range(B_KV_HEADS)]
    n_steps = (qt * tq + tq + K_TILE - 1) // K_TILE
    acc_ref[...] = jnp.zeros_like(acc_ref)
    m_ref[...] = jnp.full_like(m_ref, M_INIT)

    def scores(kt):
        for jt in range(K_TILE // SCORE_TILE):
            k0 = pl.multiple_of(kt * K_TILE + jt * SCORE_TILE, SCORE_TILE)
            k2 = ks_ref[0, pl.ds(k0, SCORE_TILE), :]
            e2 = blk_ref[pl.ds(k0, SCORE_TILE), :]
            for g in range(B_KV_HEADS):
                k = jnp.where(halves[g], k2, e2)
                s = lax.dot_general(qaug[g], k, NT_DIMS, preferred_element_type=F32)
                for u in range(n_sub):
                    rows_u = slice(u * sub_rows, (u + 1) * sub_rows)
                    for c in range(SCORE_TILE // SUB_TILE):
                        bi = jnp.clip(qt * n_sub + u - k0 // SUB_TILE - c + 1, 0, n_bias - 1)
                        col = jt * SCORE_TILE + c * SUB_TILE
                        s_ref[kt % 2, g, rows_u, col:col + SUB_TILE] = (
                            s[rows_u, c * SUB_TILE:(c + 1) * SUB_TILE] + bias_s_ref[g, bi])

    def accumulate(kt):
        k0 = pl.multiple_of(kt * K_TILE, K_TILE)
        v2 = vs_ref[0, pl.ds(k0, K_TILE), :]
        for g in range(B_KV_HEADS):
            v = jnp.where(halves[g], v2, jnp.ones((K_TILE, LANES), BF16))
            s = s_ref[kt % 2, g]
            m_old = m_ref[g]
            m_new = jnp.maximum(m_old, jnp.max(s, axis=-1, keepdims=True))
            alpha = jnp.exp2(m_old - m_new)
            p = jnp.exp2(s - jnp.tile(m_new, (1, K_TILE // LANES)))
            acc_ref[g] = alpha * acc_ref[g] + jnp.dot(p.astype(BF16), v, preferred_element_type=F32)
            m_ref[g] = m_new

    def body(kt, carry):
        accumulate(kt)
        scores(kt + 1)
        return carry

    n_wt = kw // SUB_TILE
    q_win = [stacked([jnp.where(halves[g], q_slots[r], zero) for r in range(B_GROUP)]) for g in range(B_KV_HEADS)]
    o_win = [[] for _ in range(B_KV_HEADS)]
    for u in range(n_sub):
        sub = qt * n_sub + u
        w0 = pl.multiple_of(jnp.maximum(sub * SUB_TILE - WIN_LEN, 0), SUB_TILE)
        skip = jnp.maximum(WIN_LEN // SUB_TILE - sub, 0)
        kwin = kw_ref[0, pl.ds(w0, kw), :]
        vwin = vw_ref[0, pl.ds(w0, kw), :]
        q_u = jnp.concatenate([q_win[g][u * sub_rows:(u + 1) * sub_rows] for g in range(B_KV_HEADS)], axis=0)
        s_u = lax.dot_general(q_u, kwin, NT_DIMS, preferred_element_type=F32)
        for g in range(B_KV_HEADS):
            s = s_u[g * sub_rows:(g + 1) * sub_rows]
            s = jnp.concatenate([s[:, jw * SUB_TILE:(jw + 1) * SUB_TILE] + bias_w_ref[g, jnp.minimum(jw + skip, n_wt)]
                                 for jw in range(n_wt)], axis=1)
            p = jnp.exp2(s - jnp.max(s, axis=-1, keepdims=True))
            acc = jnp.dot(p.astype(BF16), jnp.where(halves[g], vwin, jnp.ones_like(vwin)),
                          preferred_element_type=F32)
            o_win[g].append(acc)
    win_slots = normalised_slots([jnp.concatenate(o_win[g], axis=0) for g in range(B_KV_HEADS)], 2)
    for r in range(B_GROUP):
        part_ref[1, :, r * LANES:(r + 1) * LANES] = win_slots[r]

    scores(0)
    lax.fori_loop(0, n_steps - 1, body, 0)
    accumulate(n_steps - 1)
    sel_slots = normalised_slots([acc_ref[g] for g in range(B_KV_HEADS)], 1)
    for r in range(B_GROUP):
        sl = slice(r * LANES, (r + 1) * LANES)
        o_ref[0, :, sl] = ((part_ref[0, :, sl] + sel_slots[r]) + part_ref[1, :, sl]).astype(o_ref.dtype)


def _nsa(qb, gl, kcmp, vcmp, ksl, vsl, bias_s, kwin, vwin, bias_w):
    b, s, w = qb.shape
    nsel = s // SEL_BLOCK
    nck = kcmp.shape[1]
    ratio = SEL_BLOCK // CMP_STRIDE
    wov = np.zeros((nsel, nck), np.float32)
    for jj in range(nsel):
        for off, wt in zip(range(-1, ratio), CMP_OVERLAP):
            n = ratio * jj + off
            if 0 <= n < nck - 1:
                wov[jj, n] = wt
    wov = jnp.asarray(wov)
    blk = jnp.asarray((np.arange(s)[:, None] // SEL_BLOCK) == (np.arange(LANES)[None, :] % HEAD_DIM), BF16)
    rows = B_GROUP * Q_TILE
    tile = lambda width: pl.BlockSpec((1, Q_TILE, width), lambda bi, qi: (bi, qi, 0))
    whole = lambda a: pl.BlockSpec((1,) + a.shape[1:], lambda bi, qi: (bi, 0, 0))
    slot_of = lambda v: pl.BlockSpec((1, s, LANES), lambda bi, qi, slot=v[1]: (bi, 0, slot))
    const = lambda a: pl.BlockSpec(a.shape, lambda bi, qi: (0,) * a.ndim, pipeline_mode=pl.Buffered(1))
    return pl.pallas_call(
        functools.partial(_nsa_kernel, nsel=nsel, n_bias=bias_s.shape[1], kw=WIN_LEN + SUB_TILE),
        grid=(b, s // Q_TILE),
        in_specs=[tile(w), tile(LANES), whole(kcmp), whole(vcmp), const(wov), slot_of(ksl), const(blk), slot_of(vsl),
                  const(bias_s), slot_of(kwin), slot_of(vwin), const(bias_w)],
        out_specs=tile(w),
        out_shape=jax.ShapeDtypeStruct((b, s, w), BF16),
        scratch_shapes=[pltpu.VMEM((B_KV_HEADS, rows, LANES), F32), pltpu.VMEM((B_KV_HEADS, rows, LANES), F32),
                        pltpu.VMEM((2, B_KV_HEADS, rows, K_TILE), F32), pltpu.VMEM((2, Q_TILE, w), F32),
                        pltpu.VMEM((Q_TILE, LANES), BF16)],
        compiler_params=_cparams("parallel", "parallel"),
        name="nsa",
    )(qb, gl, kcmp, vcmp, wov, ksl[0], blk, vsl[0], bias_s, kwin[0], vwin[0], bias_w)


def _tail_kernel(x_ref, oa_ref, ob_ref, wa_ref, wb_ref, g2_ref, wg_ref, wu_ref, wd_ref, gf_ref, o_ref):
    o_a = jnp.concatenate([oa_ref[p] for p in range(oa_ref.shape[0])], axis=1)
    mix = jnp.dot(o_a, wa_ref[...], preferred_element_type=F32)
    mix = mix + jnp.dot(ob_ref[...], wb_ref[...], preferred_element_type=F32)
    h = x_ref[...] + mix
    y = h * lax.rsqrt(jnp.mean(h * h, axis=-1, keepdims=True) + RMS_EPS)
    hn = (y * g2_ref[...]).astype(BF16)
    a = jnp.dot(hn, wg_ref[...], preferred_element_type=F32)
    u = jnp.dot(hn, wu_ref[...], preferred_element_type=F32)
    act = (jax.nn.silu(a) * u).astype(BF16)
    h2 = h + jnp.dot(act, wd_ref[...], preferred_element_type=F32)
    y2 = h2 * lax.rsqrt(jnp.mean(h2 * h2, axis=-1, keepdims=True) + RMS_EPS)
    o_ref[...] = y2 * gf_ref[...]


def _tail(x2, oa, ob, wa, wb, g2, wg, wu, wd, gf, tm):
    n, d = x2.shape
    row = lambda width: pl.BlockSpec((tm, width), lambda i: (i, 0))
    const = lambda a: pl.BlockSpec(a.shape, lambda i: (0, 0), pipeline_mode=pl.Buffered(1))
    return pl.pallas_call(
        _tail_kernel,
        grid=(n // tm,),
        in_specs=[row(d), pl.BlockSpec((oa.shape[0], tm, LANES), lambda i: (0, i, 0)), row(ob.shape[1]),
                  const(wa), const(wb), const(g2), const(wg), const(wu), const(wd), const(gf)],
        out_specs=row(d),
        out_shape=jax.ShapeDtypeStruct((n, d), F32),
        compiler_params=_cparams("parallel"),
        name="tail",
    )(x2, oa, ob, wa, wb, g2, wg, wu, wd, gf)


def _slot_perm():
    perm = np.zeros(B_WIDTH, np.int64)
    for r in range(B_GROUP):
        for g in range(B_KV_HEADS):
            src = (g * B_GROUP + r) * HEAD_DIM
            dst = r * LANES + g * HEAD_DIM
            perm[dst:dst + HEAD_DIM] = np.arange(src, src + HEAD_DIM)
    return perm


def _layer(h, tab_a, tab_b, norm1_g, w_in, cmp_pos, k_w1, k_b1, k_w2, v_w1, v_b1, v_w2,
           w_out, norm2_g, w_gate, w_up, w_down, norm_f_g):
    b, s, d = h.shape
    n = b * s
    assert s % SPAN == 0 and s % K_TILE == 0
    perm = _slot_perm()

    cols = np.cumsum([0, A_WIDTH, A_WIDTH, A_WIDTH, B_WIDTH] + [KV_WIDTH] * 6 + [GATE_WIDTH])
    w_aq = w_in[:, cols[0]:cols[1]] * (SCALE * LOG2E)
    w_bq = (w_in[:, cols[3]:cols[4]] * (SCALE * LOG2E))[:, perm]
    w_gl = jnp.pad(w_in[:, cols[10]:cols[11]], ((0, 0), (0, LANES - GATE_WIDTH)))
    w1 = jnp.concatenate([w_aq, w_in[:, cols[1]:cols[3]], w_bq, w_in[:, cols[4]:cols[10]], w_gl], axis=1).astype(BF16)
    widths = (A_WIDTH,) * 3 + (B_WIDTH, 2 * KV_WIDTH, 4 * KV_WIDTH, LANES)
    dtypes = (BF16,) * 4 + (F32, BF16, F32)
    x2 = h.reshape(n, d)
    n_lay = len(MIX_LAYOUTS)
    outs = _inproj(h, norm1_g.reshape(1, d), w1, widths, dtypes, n_mix=3, tm=512)
    qa, ka, va = (outs[i * n_lay:(i + 1) * n_lay] for i in range(3))
    r3 = lambda t: t.reshape(b, s, t.shape[-1])
    qb, kv_cmp, kv_rest, gl = map(r3, outs[3 * n_lay:])
    kc, vc = (kv_cmp, 0), (kv_cmp, 1)
    ksl, vsl, kw, vw = ((kv_rest, slot) for slot in range(4))

    tok_dist, in_window = [], []
    for window, dil in DIL_PATTERNS:
        fan = (dil if dil in MIX_LAYOUTS else MIX_LAYOUTS[0]) // dil
        row = np.arange(DIL_BLOCK)
        step = fan * (row % (DIL_BLOCK // fan)) + row // (DIL_BLOCK // fan)
        key_step = np.concatenate([step, DIL_BLOCK + step])
        dist = DIL_BLOCK + step[:, None] - key_step[None, :]
        tok_dist.append(np.maximum(dist, 0) * dil)
        in_window.append((dist >= 0) & (dist <= window // dil))
    tok_dist, in_window = np.stack(tok_dist), np.stack(in_window)
    bias_a = _bias_tiles(tab_a, tok_dist, in_window, inner=A_HEADS)[0]
    o_a = _mixer_a(qa, ka, va, bias_a).reshape(A_HEADS // 2, n, LANES)

    nch = s // CMP_STRIDE
    kcmp, vcmp = _compress([kc, vc], cmp_pos, [(k_w1, k_b1, k_w2), (v_w1, v_b1, v_w2)])
    d_sat = int(np.nonzero(_t5_bucket_np(np.arange(s)) < REL_BUCKETS - 1)[0].max()) + 1
    first_far = -(-(d_sat + SUB_TILE - 1) // SUB_TILE)
    n_bias = min(s // SUB_TILE, first_far + 1) + 1
    dd = ((np.arange(n_bias)[:, None, None] - 1) * SUB_TILE + np.arange(SUB_TILE)[None, :, None]
          - np.arange(SUB_TILE)[None, None, :])
    bias_sl = _bias_tiles(tab_b, dd, dd >= 0, inner=B_GROUP)
    bias_sl = bias_sl.reshape(B_KV_HEADS, n_bias, B_GROUP * SUB_TILE, SUB_TILE)
    n_wt = WIN_LEN // SUB_TILE + 1
    dw = ((n_wt - 1 - np.arange(n_wt + 1))[:, None, None] * SUB_TILE + np.arange(SUB_TILE)[None, :, None]
          - np.arange(SUB_TILE)[None, None, :])
    in_win = (dw >= 0) & (dw < WIN_LEN) & (np.arange(n_wt + 1) < n_wt)[:, None, None]
    bias_w = _bias_tiles(tab_b, dw, in_win, inner=B_GROUP)
    bias_w = bias_w.reshape(B_KV_HEADS, n_wt + 1, B_GROUP * SUB_TILE, SUB_TILE)
    o_b = _nsa(qb, gl, kcmp, vcmp, ksl, vsl, bias_sl, kw, vw, bias_w)

    wa = w_out[:A_WIDTH].astype(BF16)
    wb = w_out[A_WIDTH:][perm].astype(BF16)
    out = _tail(x2, o_a, o_b.reshape(n, B_WIDTH), wa, wb, norm2_g.reshape(1, d), w_gate.astype(BF16),
                w_up.astype(BF16), w_down.astype(BF16), norm_f_g.reshape(1, d), tm=512)
    return out.reshape(b, s, d)


def kernel(x, norm1_g, w_in, rel_bias, cmp_pos, cmp_k_w1, cmp_k_b1, cmp_k_w2, cmp_v_w1, cmp_v_b1, cmp_v_w2,
           w_out, norm2_g, w_gate, w_up, w_down, norm_f_g):
    assert w_in.shape[0] == 1, "single-layer model"
    tab_a = rel_bias[:, :A_HEADS].T * LOG2E
    tab_b = rel_bias[:, A_HEADS:].T * LOG2E
    return _layer(x, tab_a, tab_b, norm1_g[0], w_in[0], cmp_pos[0], cmp_k_w1[0], cmp_k_b1[0], cmp_k_w2[0],
                  cmp_v_w1[0], cmp_v_b1[0], cmp_v_w2[0], w_out[0], norm2_g[0], w_gate[0], w_up[0], w_down[0],
                  norm_f_g)
```

```python
import functools
import math

import numpy as np
import jax
import jax.numpy as jnp
from jax import lax
from jax.experimental import pallas as pl
from jax.experimental.pallas import tpu as pltpu

F32 = jnp.float32
BF16 = jnp.bfloat16

HEAD_DIM = 64
LANES = 128
A_HEADS = 8
DIL_PATTERNS = ((128, 1), (512, 4), (2048, 16))
DIL_BLOCK = 128
B_HEADS = 8
B_KV_HEADS = 2
B_GROUP = B_HEADS // B_KV_HEADS
CMP_LEN = 32
CMP_STRIDE = 16
CMP_HIDDEN = 128
SEL_BLOCK = 64
SEL_TOPN = 16
CMP_OVERLAP = (1.0, 2.0, 2.0, 2.0, 1.0)
WIN_LEN = 512
N_BRANCH = 3
REL_BUCKETS = 32
REL_MAX_DIST = 2048
RMS_EPS = 1e-6
A_WIDTH = A_HEADS * HEAD_DIM
B_WIDTH = B_HEADS * HEAD_DIM
KV_WIDTH = B_KV_HEADS * HEAD_DIM
GATE_WIDTH = B_HEADS * N_BRANCH
SCALE = HEAD_DIM ** -0.5
LOG2E = math.log2(math.e)

SPAN = DIL_PATTERNS[-1][1] * DIL_BLOCK
MIX_LAYOUTS = tuple(d for _, d in DIL_PATTERNS if d > 1)
NARROW_FROM = 4
RANK_CHUNK = 8
SUB_TILE = 128
Q_TILE = 512
K_TILE = 512
SCORE_TILE = 256
NOT_SELECTED = -(2.0 ** 100)
M_INIT = -1e30
VMEM_LIMIT = 56 * 1024 * 1024

NT_DIMS = (((1,), (1,)), ((), ()))


def _cparams(*sem):
    return pltpu.CompilerParams(dimension_semantics=sem, vmem_limit_bytes=VMEM_LIMIT)


def _t5_bucket_np(dist):
    max_exact = REL_BUCKETS // 2
    d = np.asarray(dist)
    df = np.maximum(d, 1).astype(np.float32)
    large = max_exact + (np.log(df / np.float32(max_exact)) / np.float32(math.log(REL_MAX_DIST / max_exact))
                         * np.float32(REL_BUCKETS - max_exact)).astype(np.int32)
    large = np.minimum(large, REL_BUCKETS - 1)
    return np.where(d < max_exact, d, large).astype(np.int32)


def _bias_kernel(tab_ref, idx_ref, o_ref, *, buckets):
    inner = o_ref.shape[2]
    for t, present in enumerate(buckets):
        idx = idx_ref[t]
        accs = [jnp.full(idx.shape, -jnp.inf, F32) for _ in range(inner)]
        for bucket in present:
            hit = idx == bucket
            for hi in range(inner):
                head = pl.program_id(0) * inner + hi
                accs[hi] = jnp.where(hit, tab_ref[head * REL_BUCKETS + bucket], accs[hi])
        for hi in range(inner):
            o_ref[0, t, hi] = accs[hi]


def _bias_tiles(tab, dist, valid, inner):
    h = tab.shape[0]
    t, r, c = dist.shape
    idx = np.where(valid, _t5_bucket_np(np.maximum(dist, 0)), -1).astype(np.int32)
    buckets = tuple(tuple(int(v) for v in np.unique(idx[ti]) if v >= 0) for ti in range(t))
    return pl.pallas_call(
        functools.partial(_bias_kernel, buckets=buckets),
        grid=(h // inner,),
        in_specs=[pl.BlockSpec(memory_space=pltpu.SMEM),
                  pl.BlockSpec((t, r, c), lambda a: (0, 0, 0))],
        out_specs=pl.BlockSpec((1, t, inner, r, c), lambda a: (a, 0, 0, 0, 0)),
        out_shape=jax.ShapeDtypeStruct((h // inner, t, inner, r, c), F32),
        compiler_params=_cparams("parallel"),
        name="bias_tiles",
    )(tab.reshape(-1), jnp.asarray(idx))


def _half_mask(g):
    lane = lax.broadcasted_iota(jnp.int32, (1, LANES), 1)
    return (lane >= HEAD_DIM) if g else (lane < HEAD_DIM)


def _inproj_kernel(x_ref, g_ref, w_ref, *refs, widths, n_mix):
    n_dil = len(DIL_PATTERNS)
    n_lay = len(MIX_LAYOUTS)
    n_scr = n_mix * (n_dil - 1)
    mix_refs = [(None,) * (n_dil - n_lay) + tuple(refs[i * n_lay:(i + 1) * n_lay]) for i in range(n_mix)]
    out_refs = refs[n_mix * n_lay:len(refs) - n_scr]
    scratch = [refs[len(refs) - n_scr + i * (n_dil - 1):len(refs) - n_scr + (i + 1) * (n_dil - 1)]
               for i in range(n_mix)]
    x = x_ref[...]
    tm = x.shape[0]
    y = x * lax.rsqrt(jnp.mean(x * x, axis=-1, keepdims=True) + RMS_EPS)
    xn = (y * g_ref[...]).astype(BF16)
    merged = jnp.dot(xn, w_ref[:, sum(widths[:NARROW_FROM]):], preferred_element_type=F32)
    start = 0
    for idx, w in enumerate(widths):
        if idx < NARROW_FROM:
            r = jnp.dot(xn, w_ref[:, start:start + w], preferred_element_type=F32)
        else:
            off = start - sum(widths[:NARROW_FROM])
            r = merged[:, off:off + w]
        if idx < n_mix:
            prev_dil = None
            for pi, (o_ref, (_, dil)) in enumerate(zip(mix_refs[idx], DIL_PATTERNS)):
                keep = scratch[idx][pi] if pi < n_dil - 1 else None
                for p in range(w // LANES):
                    if prev_dil is None:
                        planes = [(0, r[:, p * LANES:(p + 1) * LANES])]
                    else:
                        f = dil // prev_dil
                        planes = [(res + prev_dil * a,
                                   scratch[idx][pi - 1][p, res, pl.ds(a, tm // dil, stride=f), :])
                                  for res in range(prev_dil) for a in range(f)]
                    for res, rows in planes:
                        if o_ref is not None:
                            o_ref[p, 0, 0, res] = rows.astype(o_ref.dtype)
                        if keep is not None:
                            keep[p, res] = rows
                prev_dil = dil
        else:
            o_ref = out_refs[idx - n_mix]
            o_ref[...] = r.astype(o_ref.dtype)
        start += w


def _inproj(x3, g, w, widths, dtypes, n_mix, tm):
    b, s, d = x3.shape
    n = b * s
    per_batch = s // tm
    out_shape, out_specs = [], []
    for w_, dt in zip(widths[:n_mix], dtypes[:n_mix]):
        for dil in MIX_LAYOUTS:
            out_shape.append(jax.ShapeDtypeStruct((w_ // LANES, b, per_batch, dil, tm // dil, LANES), dt))
            out_specs.append(pl.BlockSpec((w_ // LANES, 1, 1, dil, tm // dil, LANES),
                                          lambda i: (0, i // per_batch, i % per_batch, 0, 0, 0)))
    for w_, dt in zip(widths[n_mix:], dtypes[n_mix:]):
        out_shape.append(jax.ShapeDtypeStruct((n, w_), dt))
        out_specs.append(pl.BlockSpec((tm, w_), lambda i: (i, 0)))
    return pl.pallas_call(
        functools.partial(_inproj_kernel, widths=widths, n_mix=n_mix),
        grid=(n // tm,),
        in_specs=[pl.BlockSpec((tm, d), lambda i: (i, 0)),
                  pl.BlockSpec((1, d), lambda i: (0, 0)),
                  pl.BlockSpec(w.shape, lambda i: (0, 0))],
        out_specs=out_specs,
        out_shape=out_shape,
        scratch_shapes=[pltpu.VMEM((w_ // LANES, dil, tm // dil, LANES), F32)
                        for w_ in widths[:n_mix] for _, dil in DIL_PATTERNS[:-1]],
        compiler_params=_cparams("parallel"),
        name="inproj",
    )(x3.reshape(n, d), g, w)


def _mixer_a_kernel(*refs):
    bias_ref, o_ref, acc_scr, m_scr = refs[5 * len(MIX_LAYOUTS):]
    sb = pl.program_id(2)
    halves = [_half_mask(hh) for hh in range(2)]
    col = lax.broadcasted_iota(jnp.int32, (1, 2 * DIL_BLOCK), 1)
    first_keep = (col >= DIL_BLOCK) | (sb > 0)
    ones = jnp.ones((2 * DIL_BLOCK, LANES), BF16)
    n_blocks = SPAN // DIL_BLOCK
    for p, (_, dil) in reversed(list(enumerate(DIL_PATTERNS))):
        lay = MIX_LAYOUTS.index(dil) if dil in MIX_LAYOUTS else 0
        src = MIX_LAYOUTS[lay]
        fan = src // dil
        piece = DIL_BLOCK // fan
        q_ref, kp_ref, kc_ref, vp_ref, vc_ref = refs[5 * lay:5 * lay + 5]
        last = SPAN // src // piece - 1

        def gather(ref, r, nn, dil=dil, fan=fan, piece=piece):
            per_tile = ref.shape[4]
            chunk = min(piece, per_tile)
            return jnp.concatenate(
                [ref[0, 0, lo // per_tile, r + dil * a, lo % per_tile:lo % per_tile + chunk, :]
                 for a in range(fan) for lo in range(nn * piece, (nn + 1) * piece, chunk)], axis=0)

        for r in range(dil):
            for n in range(SPAN // (dil * DIL_BLOCK)):
                q2 = gather(q_ref, r, n)
                if n == 0:
                    k_prev, v_prev = gather(kp_ref, r, last), gather(vp_ref, r, last)
                else:
                    k_prev, v_prev = gather(kc_ref, r, n - 1), gather(vc_ref, r, n - 1)
                kcat = jnp.concatenate([k_prev, gather(kc_ref, r, n)], axis=0)
                vcat = jnp.concatenate([v_prev, gather(vc_ref, r, n)], axis=0)
                qm = jnp.concatenate([jnp.where(halves[hh], q2, jnp.zeros_like(q2)) for hh in range(2)], axis=0)
                s2 = lax.dot_general(qm, kcat, NT_DIMS, preferred_element_type=F32)
                for hh in range(2):
                    s = s2[hh * DIL_BLOCK:(hh + 1) * DIL_BLOCK] + bias_ref[p, hh]
                    if n == 0:
                        s = jnp.where(first_keep, s, -jnp.inf)
                    m_blk = jnp.max(s, axis=-1, keepdims=True)
                    pe = jnp.exp2(s - m_blk)
                    pv = jnp.dot(pe.astype(BF16), jnp.where(halves[hh], vcat, ones), preferred_element_type=F32)
                    m_b = jnp.broadcast_to(m_blk, pv.shape)
                    for a in range(fan):
                        rows_t = pl.ds(n * dil * DIL_BLOCK + r + dil * a, piece, stride=src)
                        acc_scr[p, hh, rows_t, :] = pv[a * piece:(a + 1) * piece]
                        m_scr[p, hh, rows_t, :] = m_b[a * piece:(a + 1) * piece]

    def finish(c, carry):
        rows = pl.ds(pl.multiple_of(c * DIL_BLOCK, DIL_BLOCK), DIL_BLOCK)
        tots = []
        for hh in range(2):
            ms = [m_scr[p, hh, rows, :] for p in range(len(DIL_PATTERNS))]
            m_all = functools.reduce(jnp.maximum, ms)
            tots.append(sum(jnp.exp2(m - m_all) * acc_scr[p, hh, rows, :] for p, m in enumerate(ms)))
        num = jnp.where(halves[0], tots[0], tots[1])
        den = pltpu.roll(jnp.where(halves[0], tots[1], tots[0]), HEAD_DIM, axis=1)
        o_ref[0, 0, rows, :] = (num / den).astype(o_ref.dtype)
        return carry

    lax.fori_loop(0, n_blocks, finish, 0, unroll=4)


def _mixer_a(qs, ks, vs, bias):
    npair, b, n_tiles, dil0, rows0, _ = qs[0].shape
    tile = dil0 * rows0
    s = n_tiles * tile
    cur = lambda bi, pi, si: (pi, bi, si, 0, 0, 0)
    prev = lambda bi, pi, si: (pi, bi, jnp.maximum(si - 1, 0), 0, 0, 0)
    in_specs, operands = [], []
    for dil, q, k, v in zip(MIX_LAYOUTS, qs, ks, vs):
        blk = (1, 1, SPAN // tile, dil, tile // dil, LANES)
        in_specs += [pl.BlockSpec(blk, cur), pl.BlockSpec(blk, prev), pl.BlockSpec(blk, cur),
                     pl.BlockSpec(blk, prev), pl.BlockSpec(blk, cur)]
        operands += [q, k, k, v, v]
    in_specs.append(pl.BlockSpec((bias.shape[0], 2) + bias.shape[2:], lambda bi, pi, si: (0, pi, 0, 0)))
    return pl.pallas_call(
        _mixer_a_kernel,
        grid=(b, npair, s // SPAN),
        in_specs=in_specs,
        out_specs=pl.BlockSpec((1, 1, SPAN, LANES), lambda bi, pi, si: (pi, bi, si, 0)),
        out_shape=jax.ShapeDtypeStruct((npair, b, s, LANES), BF16),
        scratch_shapes=[pltpu.VMEM((len(DIL_PATTERNS), 2, SPAN, LANES), F32),
                        pltpu.VMEM((len(DIL_PATTERNS), 2, SPAN, LANES), F32)],
        compiler_params=_cparams("parallel", "parallel", "parallel"),
        name="mixer_a",
    )(*operands, bias)


def _compress_kernel(pa_ref, pb_ref, *refs):
    n_streams = len(refs) // 6
    for i in range(n_streams):
        c_ref, w1a_ref, w1b_ref, b1_ref, w2_ref = refs[5 * i:5 * i + 5]
        o_ref = refs[5 * n_streams + i]
        nch = o_ref.shape[1]
        c = jnp.concatenate([c_ref[0, pl.ds(l, nch, stride=CMP_STRIDE), :] for l in range(CMP_STRIDE)], axis=1)
        xa = (c + pa_ref[...]).astype(BF16)
        xb = (c + pb_ref[...]).astype(BF16)
        ha = jnp.dot(xa, w1a_ref[...], preferred_element_type=F32)
        hb = jnp.dot(xb, w1b_ref[...], preferred_element_type=F32)
        hb_next = jnp.concatenate([hb[1:], jnp.zeros_like(hb[:1])], axis=0)
        hid = jax.nn.gelu(ha + hb_next + b1_ref[...])
        o_ref[0] = jnp.dot(hid.astype(BF16), w2_ref[...], preferred_element_type=F32).astype(o_ref.dtype)


def _compress(cs, pos, params):
    b, s, _ = cs[0][0].shape
    nch = s // CMP_STRIDE
    half = CMP_LEN // 2
    zero = jnp.zeros((half, HEAD_DIM, CMP_HIDDEN), F32)

    def grouped(wpart):
        g0 = jnp.concatenate([wpart, zero], axis=1).reshape(half * LANES, CMP_HIDDEN)
        g1 = jnp.concatenate([zero, wpart], axis=1).reshape(half * LANES, CMP_HIDDEN)
        return jnp.concatenate([g0, g1], axis=1).astype(BF16)

    pa = jnp.tile(pos[:half], (1, 2)).reshape(1, half * LANES)
    pb = jnp.tile(pos[half:], (1, 2)).reshape(1, half * LANES)
    full = lambda a: pl.BlockSpec(a.shape, lambda i: (0,) * a.ndim)
    operands, in_specs = [pa, pb], [full(pa), full(pb)]
    for (c, slot), (w1, b1, w2) in zip(cs, params):
        zw = jnp.zeros_like(w2)
        w2g = jnp.concatenate([jnp.concatenate([w2, zw], axis=1),
                               jnp.concatenate([zw, w2], axis=1)], axis=0).astype(BF16)
        weights = [grouped(w1[:half]), grouped(w1[half:]), jnp.tile(b1, 2).reshape(1, 2 * CMP_HIDDEN), w2g]
        operands += [c] + weights
        in_specs += [pl.BlockSpec((1, s, LANES), lambda i, slot=slot: (i, 0, slot))] + [full(w_) for w_ in weights]
    return pl.pallas_call(
        _compress_kernel,
        grid=(b,),
        in_specs=in_specs,
        out_specs=[pl.BlockSpec((1, nch, LANES), lambda i: (i, 0, 0))] * len(cs),
        out_shape=[jax.ShapeDtypeStruct((b, nch, LANES), BF16)] * len(cs),
        compiler_params=_cparams("parallel"),
        name="compress",
    )(*operands)


def _nsa_kernel(q_ref, gl_ref, kc_ref, vc_ref, wov_ref, ks_ref, blk_ref, vs_ref, bias_s_ref, kw_ref, vw_ref, bias_w_ref,
                o_ref, acc_ref, m_ref, s_ref, part_ref, sb_ref, *, nsel, n_bias, kw):
    qt = pl.program_id(1)
    tq = q_ref.shape[1]
    n_sub = tq // SUB_TILE
    sub_rows = B_GROUP * SUB_TILE
    gate = jax.nn.sigmoid(gl_ref[0])
    halves = [_half_mask(g) for g in range(B_KV_HEADS)]
    q_slots = [q_ref[0, :, r * LANES:(r + 1) * LANES] for r in range(B_GROUP)]
    zero = jnp.zeros((tq, LANES), BF16)

    def stacked(slots):
        return jnp.concatenate([slots[r][u * SUB_TILE:(u + 1) * SUB_TILE]
                                for u in range(n_sub) for r in range(B_GROUP)], axis=0)

    def head_rows(o, r):
        return jnp.concatenate([o[(u * B_GROUP + r) * SUB_TILE:(u * B_GROUP + r + 1) * SUB_TILE]
                                for u in range(n_sub)], axis=0)

    def gate_tile(r, branch):
        c0, c1 = (r * N_BRANCH + branch, (B_GROUP + r) * N_BRANCH + branch)
        return jnp.where(halves[0], gate[:, c0:c0 + 1], gate[:, c1:c1 + 1])

    def gated_slots(o, g, branch, slots):
        for r in range(B_GROUP):
            c = (g * B_GROUP + r) * N_BRANCH + branch
            o_r = head_rows(o, r) * gate[:, c:c + 1]
            slots[r] = o_r if slots[r] is None else jnp.where(halves[g], o_r, slots[r])

    def normalised_slots(accs, branch):
        slots = []
        for r in range(B_GROUP):
            a0, a1 = head_rows(accs[0], r), head_rows(accs[1], r)
            num = jnp.where(halves[0], a0, a1)
            den = pltpu.roll(jnp.where(halves[0], a1, a0), HEAD_DIM, axis=1)
            slots.append(num / jnp.maximum(den, 1e-30) * gate_tile(r, branch))
        return slots

    nck = kc_ref.shape[1]
    row = lax.broadcasted_iota(jnp.int32, (n_sub * sub_rows, 1), 0)
    t_row = qt * tq + (row // sub_rows) * SUB_TILE + row % SUB_TILE
    blk_end = lax.broadcasted_iota(jnp.int32, (1, nck), 1) * CMP_STRIDE + (CMP_LEN - 1)
    valid = blk_end <= t_row
    kc = kc_ref[0]
    vc = vc_ref[0]
    n_rows = n_sub * sub_rows
    q_all = jnp.concatenate([stacked([jnp.where(halves[g], q_slots[r], zero) for r in range(B_GROUP)])
                             for g in range(B_KV_HEADS)], axis=0)
    s = lax.dot_general(q_all, kc, NT_DIMS, preferred_element_type=F32)
    s = jnp.where(jnp.concatenate([valid] * B_KV_HEADS, axis=0), s, -jnp.inf)
    m = jnp.max(s, axis=-1, keepdims=True)
    m = jnp.where(m == -jnp.inf, 0.0, m)
    p = jnp.exp2(s - m)
    den = jnp.sum(p, axis=-1, keepdims=True)
    p = p / jnp.maximum(den, 1e-30)
    o_all = jnp.dot(p.astype(BF16), vc, preferred_element_type=F32)
    pc = []
    cmp_slots = [None] * B_GROUP
    for g in range(B_KV_HEADS):
        pg = p[g * n_rows:(g + 1) * n_rows]
        pc.append(jnp.concatenate(
            [sum(pg[(u * B_GROUP + r) * SUB_TILE:(u * B_GROUP + r + 1) * SUB_TILE] for r in range(B_GROUP))
             for u in range(n_sub)], axis=0))
        gated_slots(o_all[g * n_rows:(g + 1) * n_rows], g, 0, cmp_slots)
    for r in range(B_GROUP):
        part_ref[0, :, r * LANES:(r + 1) * LANES] = cmp_slots[r]

    t_lane = qt * tq + lax.broadcasted_iota(jnp.int32, (1, tq), 1)
    cur = t_lane // SEL_BLOCK
    j = lax.broadcasted_iota(jnp.int32, (nsel, 1), 0)
    forced = (j == 0) | (j == cur) | (j == cur - 1)
    imps = []
    for g in reversed(range(B_KV_HEADS)):
        imp = lax.dot_general(wov_ref[...], pc[g], NT_DIMS, preferred_element_type=F32,
                              precision=lax.Precision.HIGHEST)
        imps.append(jnp.where(j > cur, -jnp.inf, jnp.where(forced, jnp.inf, imp)))

    last_block = (qt * tq + tq - 1) // SEL_BLOCK
    for level in range(-(-nsel // RANK_CHUNK)):
        n_live = min((level + 1) * RANK_CHUNK, nsel)

        @pl.when(last_block // RANK_CHUNK == level)
        def _(n_live=n_live):
            blocks = []
            for imp in imps:
                if n_live <= SEL_TOPN:
                    blocks.append(jnp.zeros((HEAD_DIM, tq), F32))
                    continue
                live = imp[:n_live]
                jl = j[:n_live]
                rank = jnp.zeros((n_live, tq), jnp.int32)
                for jp in range(n_live):
                    row = live[jp:jp + 1, :]
                    rank = rank + jnp.where(jl > jp, (row >= live).astype(jnp.int32), (row > live).astype(jnp.int32))
                blocks.append(jnp.where(rank < SEL_TOPN, 0.0, NOT_SELECTED))
                if n_live < HEAD_DIM:
                    blocks.append(jnp.zeros((HEAD_DIM - n_live, tq), F32))
            sb_ref[...] = jnp.concatenate(blocks, axis=0).T.astype(sb_ref.dtype)

    sb = sb_ref[...]
    qaug = [stacked([jnp.where(halves[g], q_slots[r], sb) for r in range(B_GROUP)])
            for g in range(B_KV_HEADS)]
    n_steps = (qt * tq + tq + K_TILE - 1) // K_TILE
    acc_ref[...] = jnp.zeros_like(acc_ref)
    m_ref[...] = jnp.full_like(m_ref, M_INIT)

    def scores(kt):
        for jt in range(K_TILE // SCORE_TILE):
            k0 = pl.multiple_of(kt * K_TILE + jt * SCORE_TILE, SCORE_TILE)
            k2 = ks_ref[0, pl.ds(k0, SCORE_TILE), :]
            e2 = blk_ref[pl.ds(k0, SCORE_TILE), :]
            for g in range(B_KV_HEADS):
                k = jnp.where(halves[g], k2, e2)
                s = lax.dot_general(qaug[g], k, NT_DIMS, preferred_element_type=F32)
                for u in range(n_sub):
                    rows_u = slice(u * sub_rows, (u + 1) * sub_rows)
                    for c in range(SCORE_TILE // SUB_TILE):
                        bi = jnp.clip(qt * n_sub + u - k0 // SUB_TILE - c + 1, 0, n_bias - 1)
                        col = jt * SCORE_TILE + c * SUB_TILE
                        s_ref[kt % 2, g, rows_u, col:col + SUB_TILE] = (
                            s[rows_u, c * SUB_TILE:(c + 1) * SUB_TILE] + bias_s_ref[g, bi])

    def accumulate(kt):
        k0 = pl.multiple_of(kt * K_TILE, K_TILE)
        v2 = vs_ref[0, pl.ds(k0, K_TILE), :]
        for g in range(B_KV_HEADS):
            v = jnp.where(halves[g], v2, jnp.ones((K_TILE, LANES), BF16))
            s = s_ref[kt % 2, g]
            m_old = m_ref[g]
            m_new = jnp.maximum(m_old, jnp.max(s, axis=-1, keepdims=True))
            alpha = jnp.exp2(m_old - m_new)
            p = jnp.exp2(s - jnp.tile(m_new, (1, K_TILE // LANES)))
            acc_ref[g] = alpha * acc_ref[g] + jnp.dot(p.astype(BF16), v, preferred_element_type=F32)
            m_ref[g] = m_new

    def body(kt, carry):
        accumulate(kt)
        scores(kt + 1)
        return carry

    n_wt = kw // SUB_TILE
    q_win = [stacked([jnp.where(halves[g], q_slots[r], zero) for r in range(B_GROUP)]) for g in range(B_KV_HEADS)]
    o_win = [[] for _ in range(B_KV_HEADS)]
    for u in range(n_sub):
        sub = qt * n_sub + u
        w0 = pl.multiple_of(jnp.maximum(sub * SUB_TILE - WIN_LEN, 0), SUB_TILE)
        skip = jnp.maximum(WIN_LEN // SUB_TILE - sub, 0)
        kwin = kw_ref[0, pl.ds(w0, kw), :]
        vwin = vw_ref[0, pl.ds(w0, kw), :]
        q_u = jnp.concatenate([q_win[g][u * sub_rows:(u + 1) * sub_rows] for g in range(B_KV_HEADS)], axis=0)
        s_u = lax.dot_general(q_u, kwin, NT_DIMS, preferred_element_type=F32)
        for g in range(B_KV_HEADS):
            s = s_u[g * sub_rows:(g + 1) * sub_rows]
            s = jnp.concatenate([s[:, jw * SUB_TILE:(jw + 1) * SUB_TILE] + bias_w_ref[g, jnp.minimum(jw + skip, n_wt)]
                                 for jw in range(n_wt)], axis=1)
            p = jnp.exp2(s - jnp.max(s, axis=-1, keepdims=True))
            acc = jnp.dot(p.astype(BF16), jnp.where(halves[g], vwin, jnp.ones_like(vwin)),
                          preferred_element_type=F32)
            o_win[g].append(acc)
    win_slots = normalised_slots([jnp.concatenate(o_win[g], axis=0) for g in range(B_KV_HEADS)], 2)
    for r in range(B_GROUP):
        part_ref[1, :, r * LANES:(r + 1) * LANES] = win_slots[r]

    scores(0)
    lax.fori_loop(0, n_steps - 1, body, 0)
    accumulate(n_steps - 1)
    sel_slots = normalised_slots([acc_ref[g] for g in range(B_KV_HEADS)], 1)
    for r in range(B_GROUP):
        sl = slice(r * LANES, (r + 1) * LANES)
        o_ref[0, :, sl] = ((part_ref[0, :, sl] + sel_slots[r]) + part_ref[1, :, sl]).astype(o_ref.dtype)


def _nsa(qb, gl, kcmp, vcmp, ksl, vsl, bias_s, kwin, vwin, bias_w):
    b, s, w = qb.shape
    nsel = s // SEL_BLOCK
    nck = kcmp.shape[1]
    ratio = SEL_BLOCK // CMP_STRIDE
    wov = np.zeros((nsel, nck), np.float32)
    for jj in range(nsel):
        for off, wt in zip(range(-1, ratio), CMP_OVERLAP):
            n = ratio * jj + off
            if 0 <= n < nck - 1:
                wov[jj, n] = wt
    wov = jnp.asarray(wov)
    blk = jnp.asarray((np.arange(s)[:, None] // SEL_BLOCK) == (np.arange(LANES)[None, :] % HEAD_DIM), BF16)
    rows = B_GROUP * Q_TILE
    tile = lambda width: pl.BlockSpec((1, Q_TILE, width), lambda bi, qi: (bi, qi, 0))
    whole = lambda a: pl.BlockSpec((1,) + a.shape[1:], lambda bi, qi: (bi, 0, 0))
    slot_of = lambda v: pl.BlockSpec((1, s, LANES), lambda bi, qi, slot=v[1]: (bi, 0, slot))
    const = lambda a: pl.BlockSpec(a.shape, lambda bi, qi: (0,) * a.ndim, pipeline_mode=pl.Buffered(1))
    return pl.pallas_call(
        functools.partial(_nsa_kernel, nsel=nsel, n_bias=bias_s.shape[1], kw=WIN_LEN + SUB_TILE),
        grid=(b, s // Q_TILE),
        in_specs=[tile(w), tile(LANES), whole(kcmp), whole(vcmp), const(wov), slot_of(ksl), const(blk), slot_of(vsl),
                  const(bias_s), slot_of(kwin), slot_of(vwin), const(bias_w)],
        out_specs=tile(w),
        out_shape=jax.ShapeDtypeStruct((b, s, w), BF16),
        scratch_shapes=[pltpu.VMEM((B_KV_HEADS, rows, LANES), F32), pltpu.VMEM((B_KV_HEADS, rows, LANES), F32),
                        pltpu.VMEM((2, B_KV_HEADS, rows, K_TILE), F32), pltpu.VMEM((2, Q_TILE, w), F32),
                        pltpu.VMEM((Q_TILE, LANES), BF16)],
        compiler_params=_cparams("parallel", "parallel"),
        name="nsa",
    )(qb, gl, kcmp, vcmp, wov, ksl[0], blk, vsl[0], bias_s, kwin[0], vwin[0], bias_w)


def _tail_kernel(x_ref, oa_ref, ob_ref, wa_ref, wb_ref, g2_ref, wg_ref, wu_ref, wd_ref, gf_ref, o_ref):
    o_a = jnp.concatenate([oa_ref[p] for p in range(oa_ref.shape[0])], axis=1)
    mix = jnp.dot(o_a, wa_ref[...], preferred_element_type=F32)
    mix = mix + jnp.dot(ob_ref[...], wb_ref[...], preferred_element_type=F32)
    h = x_ref[...] + mix
    y = h * lax.rsqrt(jnp.mean(h * h, axis=-1, keepdims=True) + RMS_EPS)
    hn = (y * g2_ref[...]).astype(BF16)
    a = jnp.dot(hn, wg_ref[...], preferred_element_type=F32)
    u = jnp.dot(hn, wu_ref[...], preferred_element_type=F32)
    act = (jax.nn.silu(a) * u).astype(BF16)
    h2 = h + jnp.dot(act, wd_ref[...], preferred_element_type=F32)
    y2 = h2 * lax.rsqrt(jnp.mean(h2 * h2, axis=-1, keepdims=True) + RMS_EPS)
    o_ref[...] = y2 * gf_ref[...]


def _tail(x2, oa, ob, wa, wb, g2, wg, wu, wd, gf, tm):
    n, d = x2.shape
    row = lambda width: pl.BlockSpec((tm, width), lambda i: (i, 0))
    const = lambda a: pl.BlockSpec(a.shape, lambda i: (0, 0), pipeline_mode=pl.Buffered(1))
    return pl.pallas_call(
        _tail_kernel,
        grid=(n // tm,),
        in_specs=[row(d), pl.BlockSpec((oa.shape[0], tm, LANES), lambda i: (0, i, 0)), row(ob.shape[1]),
                  const(wa), const(wb), const(g2), const(wg), const(wu), const(wd), const(gf)],
        out_specs=row(d),
        out_shape=jax.ShapeDtypeStruct((n, d), F32),
        compiler_params=_cparams("parallel"),
        name="tail",
    )(x2, oa, ob, wa, wb, g2, wg, wu, wd, gf)


def _slot_perm():
    perm = np.zeros(B_WIDTH, np.int64)
    for r in range(B_GROUP):
        for g in range(B_KV_HEADS):
            src = (g * B_GROUP + r) * HEAD_DIM
            dst = r * LANES + g * HEAD_DIM
            perm[dst:dst + HEAD_DIM] = np.arange(src, src + HEAD_DIM)
    return perm


def _layer(h, tab_a, tab_b, norm1_g, w_in, cmp_pos, k_w1, k_b1, k_w2, v_w1, v_b1, v_w2,
           w_out, norm2_g, w_gate, w_up, w_down, norm_f_g):
    b, s, d = h.shape
    n = b * s
    assert s % SPAN == 0 and s % K_TILE == 0
    perm = _slot_perm()

    cols = np.cumsum([0, A_WIDTH, A_WIDTH, A_WIDTH, B_WIDTH] + [KV_WIDTH] * 6 + [GATE_WIDTH])
    w_aq = w_in[:, cols[0]:cols[1]] * (SCALE * LOG2E)
    w_bq = (w_in[:, cols[3]:cols[4]] * (SCALE * LOG2E))[:, perm]
    w_gl = jnp.pad(w_in[:, cols[10]:cols[11]], ((0, 0), (0, LANES - GATE_WIDTH)))
    w1 = jnp.concatenate([w_aq, w_in[:, cols[1]:cols[3]], w_bq, w_in[:, cols[4]:cols[10]], w_gl], axis=1).astype(BF16)
    widths = (A_WIDTH,) * 3 + (B_WIDTH, 2 * KV_WIDTH, 4 * KV_WIDTH, LANES)
    dtypes = (BF16,) * 4 + (F32, BF16, F32)
    x2 = h.reshape(n, d)
    n_lay = len(MIX_LAYOUTS)
    outs = _inproj(h, norm1_g.reshape(1, d), w1, widths, dtypes, n_mix=3, tm=512)
    qa, ka, va = (outs[i * n_lay:(i + 1) * n_lay] for i in range(3))
    r3 = lambda t: t.reshape(b, s, t.shape[-1])
    qb, kv_cmp, kv_rest, gl = map(r3, outs[3 * n_lay:])
    kc, vc = (kv_cmp, 0), (kv_cmp, 1)
    ksl, vsl, kw, vw = ((kv_rest, slot) for slot in range(4))

    tok_dist, in_window = [], []
    for window, dil in DIL_PATTERNS:
        fan = (dil if dil in MIX_LAYOUTS else MIX_LAYOUTS[0]) // dil
        row = np.arange(DIL_BLOCK)
        step = fan * (row % (DIL_BLOCK // fan)) + row // (DIL_BLOCK // fan)
        key_step = np.concatenate([step, DIL_BLOCK + step])
        dist = DIL_BLOCK + step[:, None] - key_step[None, :]
        tok_dist.append(np.maximum(dist, 0) * dil)
        in_window.append((dist >= 0) & (dist <= window // dil))
    tok_dist, in_window = np.stack(tok_dist), np.stack(in_window)
    bias_a = _bias_tiles(tab_a, tok_dist, in_window, inner=A_HEADS)[0]
    o_a = _mixer_a(qa, ka, va, bias_a).reshape(A_HEADS // 2, n, LANES)

    nch = s // CMP_STRIDE
    kcmp, vcmp = _compress([kc, vc], cmp_pos, [(k_w1, k_b1, k_w2), (v_w1, v_b1, v_w2)])
    d_sat = int(np.nonzero(_t5_bucket_np(np.arange(s)) < REL_BUCKETS - 1)[0].max()) + 1
    first_far = -(-(d_sat + SUB_TILE - 1) // SUB_TILE)
    n_bias = min(s // SUB_TILE, first_far + 1) + 1
    dd = ((np.arange(n_bias)[:, None, None] - 1) * SUB_TILE + np.arange(SUB_TILE)[None, :, None]
          - np.arange(SUB_TILE)[None, None, :])
    bias_sl = _bias_tiles(tab_b, dd, dd >= 0, inner=B_GROUP)
    bias_sl = bias_sl.reshape(B_KV_HEADS, n_bias, B_GROUP * SUB_TILE, SUB_TILE)
    n_wt = WIN_LEN // SUB_TILE + 1
    dw = ((n_wt - 1 - np.arange(n_wt + 1))[:, None, None] * SUB_TILE + np.arange(SUB_TILE)[None, :, None]
          - np.arange(SUB_TILE)[None, None, :])
    in_win = (dw >= 0) & (dw < WIN_LEN) & (np.arange(n_wt + 1) < n_wt)[:, None, None]
    bias_w = _bias_tiles(tab_b, dw, in_win, inner=B_GROUP)
    bias_w = bias_w.reshape(B_KV_HEADS, n_wt + 1, B_GROUP * SUB_TILE, SUB_TILE)
    o_b = _nsa(qb, gl, kcmp, vcmp, ksl, vsl, bias_sl, kw, vw, bias_w)

    wa = w_out[:A_WIDTH].astype(BF16)
    wb = w_out[A_WIDTH:][perm].astype(BF16)
    out = _tail(x2, o_a, o_b.reshape(n, B_WIDTH), wa, wb, norm2_g.reshape(1, d), w_gate.astype(BF16),
                w_up.astype(BF16), w_down.astype(BF16), norm_f_g.reshape(1, d), tm=512)
    return out.reshape(b, s, d)


def kernel(x, norm1_g, w_in, rel_bias, cmp_pos, cmp_k_w1, cmp_k_b1, cmp_k_w2, cmp_v_w1, cmp_v_b1, cmp_v_w2,
           w_out, norm2_g, w_gate, w_up, w_down, norm_f_g):
    assert w_in.shape[0] == 1, "single-layer model"
    tab_a = rel_bias[:, :A_HEADS].T * LOG2E
    tab_b = rel_bias[:, A_HEADS:].T * LOG2E
    return _layer(x, tab_a, tab_b, norm1_g[0], w_in[0], cmp_pos[0], cmp_k_w1[0], cmp_k_b1[0], cmp_k_w2[0],
                  cmp_v_w1[0], cmp_v_b1[0], cmp_v_w2[0], w_out[0], norm2_g[0], w_gate[0], w_up[0], w_down[0],
                  norm_f_g)
```

```python
import functools
import math

import numpy as np
import jax
import jax.numpy as jnp
from jax import lax
from jax.experimental import pallas as pl
from jax.experimental.pallas import tpu as pltpu

F32 = jnp.float32
BF16 = jnp.bfloat16

HEAD_DIM = 64
LANES = 128
A_HEADS = 8
DIL_PATTERNS = ((128, 1), (512, 4), (2048, 16))
DIL_BLOCK = 128
B_HEADS = 8
B_KV_HEADS = 2
B_GROUP = B_HEADS // B_KV_HEADS
CMP_LEN = 32
CMP_STRIDE = 16
CMP_HIDDEN = 128
SEL_BLOCK = 64
SEL_TOPN = 16
CMP_OVERLAP = (1.0, 2.0, 2.0, 2.0, 1.0)
WIN_LEN = 512
N_BRANCH = 3
REL_BUCKETS = 32
REL_MAX_DIST = 2048
RMS_EPS = 1e-6
A_WIDTH = A_HEADS * HEAD_DIM
B_WIDTH = B_HEADS * HEAD_DIM
KV_WIDTH = B_KV_HEADS * HEAD_DIM
GATE_WIDTH = B_HEADS * N_BRANCH
SCALE = HEAD_DIM ** -0.5
LOG2E = math.log2(math.e)

SPAN = DIL_PATTERNS[-1][1] * DIL_BLOCK
MIX_LAYOUTS = tuple(d for _, d in DIL_PATTERNS if d > 1)
NARROW_FROM = 4
RANK_CHUNK = 8
SUB_TILE = 128
Q_TILE = 512
K_TILE = 512
SCORE_TILE = 256
NOT_SELECTED = -(2.0 ** 100)
M_INIT = -1e30
VMEM_LIMIT = 56 * 1024 * 1024

NT_DIMS = (((1,), (1,)), ((), ()))


def _cparams(*sem):
    return pltpu.CompilerParams(dimension_semantics=sem, vmem_limit_bytes=VMEM_LIMIT)


def _t5_bucket_np(dist):
    max_exact = REL_BUCKETS // 2
    d = np.asarray(dist)
    df = np.maximum(d, 1).astype(np.float32)
    large = max_exact + (np.log(df / np.float32(max_exact)) / np.float32(math.log(REL_MAX_DIST / max_exact))
                         * np.float32(REL_BUCKETS - max_exact)).astype(np.int32)
    large = np.minimum(large, REL_BUCKETS - 1)
    return np.where(d < max_exact, d, large).astype(np.int32)


def _bias_kernel(tab_ref, idx_ref, o_ref, *, buckets):
    inner = o_ref.shape[2]
    for t, present in enumerate(buckets):
        idx = idx_ref[t]
        accs = [jnp.full(idx.shape, -jnp.inf, F32) for _ in range(inner)]
        for bucket in present:
            hit = idx == bucket
            for hi in range(inner):
                head = pl.program_id(0) * inner + hi
                accs[hi] = jnp.where(hit, tab_ref[head * REL_BUCKETS + bucket], accs[hi])
        for hi in range(inner):
            o_ref[0, t, hi] = accs[hi]


def _bias_tiles(tab, dist, valid, inner):
    h = tab.shape[0]
    t, r, c = dist.shape
    idx = np.where(valid, _t5_bucket_np(np.maximum(dist, 0)), -1).astype(np.int32)
    buckets = tuple(tuple(int(v) for v in np.unique(idx[ti]) if v >= 0) for ti in range(t))
    return pl.pallas_call(
        functools.partial(_bias_kernel, buckets=buckets),
        grid=(h // inner,),
        in_specs=[pl.BlockSpec(memory_space=pltpu.SMEM),
                  pl.BlockSpec((t, r, c), lambda a: (0, 0, 0))],
        out_specs=pl.BlockSpec((1, t, inner, r, c), lambda a: (a, 0, 0, 0, 0)),
        out_shape=jax.ShapeDtypeStruct((h // inner, t, inner, r, c), F32),
        compiler_params=_cparams("parallel"),
        name="bias_tiles",
    )(tab.reshape(-1), jnp.asarray(idx))


def _half_mask(g):
    lane = lax.broadcasted_iota(jnp.int32, (1, LANES), 1)
    return (lane >= HEAD_DIM) if g else (lane < HEAD_DIM)


def _inproj_kernel(x_ref, g_ref, w_ref, *refs, widths, n_mix):
    n_dil = len(DIL_PATTERNS)
    n_lay = len(MIX_LAYOUTS)
    n_scr = n_mix * (n_dil - 1)
    mix_refs = [(None,) * (n_dil - n_lay) + tuple(refs[i * n_lay:(i + 1) * n_lay]) for i in range(n_mix)]
    out_refs = refs[n_mix * n_lay:len(refs) - n_scr]
    scratch = [refs[len(refs) - n_scr + i * (n_dil - 1):len(refs) - n_scr + (i + 1) * (n_dil - 1)]
               for i in range(n_mix)]
    x = x_ref[...]
    tm = x.shape[0]
    y = x * lax.rsqrt(jnp.mean(x * x, axis=-1, keepdims=True) + RMS_EPS)
    xn = (y * g_ref[...]).astype(BF16)
    merged = jnp.dot(xn, w_ref[:, sum(widths[:NARROW_FROM]):], preferred_element_type=F32)
    start = 0
    for idx, w in enumerate(widths):
        if idx < NARROW_FROM:
            r = jnp.dot(xn, w_ref[:, start:start + w], preferred_element_type=F32)
        else:
            off = start - sum(widths[:NARROW_FROM])
            r = merged[:, off:off + w]
        if idx < n_mix:
            prev_dil = None
            for pi, (o_ref, (_, dil)) in enumerate(zip(mix_refs[idx], DIL_PATTERNS)):
                keep = scratch[idx][pi] if pi < n_dil - 1 else None
                for p in range(w // LANES):
                    if prev_dil is None:
                        planes = [(0, r[:, p * LANES:(p + 1) * LANES])]
                    else:
                        f = dil // prev_dil
                        planes = [(res + prev_dil * a,
                                   scratch[idx][pi - 1][p, res, pl.ds(a, tm // dil, stride=f), :])
                                  for res in range(prev_dil) for a in range(f)]
                    for res, rows in planes:
                        if o_ref is not None:
                            o_ref[p, 0, 0, res] = rows.astype(o_ref.dtype)
                        if keep is not None:
                            keep[p, res] = rows
                prev_dil = dil
        else:
            o_ref = out_refs[idx - n_mix]
            o_ref[...] = r.astype(o_ref.dtype)
        start += w


def _inproj(x3, g, w, widths, dtypes, n_mix, tm):
    b, s, d = x3.shape
    n = b * s
    per_batch = s // tm
    out_shape, out_specs = [], []
    for w_, dt in zip(widths[:n_mix], dtypes[:n_mix]):
        for dil in MIX_LAYOUTS:
            out_shape.append(jax.ShapeDtypeStruct((w_ // LANES, b, per_batch, dil, tm // dil, LANES), dt))
            out_specs.append(pl.BlockSpec((w_ // LANES, 1, 1, dil, tm // dil, LANES),
                                          lambda i: (0, i // per_batch, i % per_batch, 0, 0, 0)))
    for w_, dt in zip(widths[n_mix:], dtypes[n_mix:]):
        out_shape.append(jax.ShapeDtypeStruct((n, w_), dt))
        out_specs.append(pl.BlockSpec((tm, w_), lambda i: (i, 0)))
    return pl.pallas_call(
        functools.partial(_inproj_kernel, widths=widths, n_mix=n_mix),
        grid=(n // tm,),
        in_specs=[pl.BlockSpec((tm, d), lambda i: (i, 0)),
                  pl.BlockSpec((1, d), lambda i: (0, 0)),
                  pl.BlockSpec(w.shape, lambda i: (0, 0))],
        out_specs=out_specs,
        out_shape=out_shape,
        scratch_shapes=[pltpu.VMEM((w_ // LANES, dil, tm // dil, LANES), F32)
                        for w_ in widths[:n_mix] for _, dil in DIL_PATTERNS[:-1]],
        compiler_params=_cparams("parallel"),
        name="inproj",
    )(x3.reshape(n, d), g, w)


def _mixer_a_kernel(*refs):
    bias_ref, o_ref, acc_scr, m_scr = refs[5 * len(MIX_LAYOUTS):]
    sb = pl.program_id(2)
    halves = [_half_mask(hh) for hh in range(2)]
    col = lax.broadcasted_iota(jnp.int32, (1, 2 * DIL_BLOCK), 1)
    first_keep = (col >= DIL_BLOCK) | (sb > 0)
    ones = jnp.ones((2 * DIL_BLOCK, LANES), BF16)
    n_blocks = SPAN // DIL_BLOCK
    for p, (_, dil) in enumerate(DIL_PATTERNS):
        lay = MIX_LAYOUTS.index(dil) if dil in MIX_LAYOUTS else 0
        src = MIX_LAYOUTS[lay]
        fan = src // dil
        piece = DIL_BLOCK // fan
        q_ref, kp_ref, kc_ref, vp_ref, vc_ref = refs[5 * lay:5 * lay + 5]
        last = SPAN // src // piece - 1

        def gather(ref, r, nn, dil=dil, fan=fan, piece=piece):
            per_tile = ref.shape[4]
            chunk = min(piece, per_tile)
            return jnp.concatenate(
                [ref[0, 0, lo // per_tile, r + dil * a, lo % per_tile:lo % per_tile + chunk, :]
                 for a in range(fan) for lo in range(nn * piece, (nn + 1) * piece, chunk)], axis=0)

        for r in range(dil):
            for n in range(SPAN // (dil * DIL_BLOCK)):
                q2 = gather(q_ref, r, n)
                if n == 0:
                    k_prev, v_prev = gather(kp_ref, r, last), gather(vp_ref, r, last)
                else:
                    k_prev, v_prev = gather(kc_ref, r, n - 1), gather(vc_ref, r, n - 1)
                kcat = jnp.concatenate([k_prev, gather(kc_ref, r, n)], axis=0)
                vcat = jnp.concatenate([v_prev, gather(vc_ref, r, n)], axis=0)
                qm = jnp.concatenate([jnp.where(halves[hh], q2, jnp.zeros_like(q2)) for hh in range(2)], axis=0)
                s2 = lax.dot_general(qm, kcat, NT_DIMS, preferred_element_type=F32)
                for hh in range(2):
                    s = s2[hh * DIL_BLOCK:(hh + 1) * DIL_BLOCK] + bias_ref[p, hh]
                    if n == 0:
                        s = jnp.where(first_keep, s, -jnp.inf)
                    m_blk = jnp.max(s, axis=-1, keepdims=True)
                    pe = jnp.exp2(s - m_blk)
                    pv = jnp.dot(pe.astype(BF16), jnp.where(halves[hh], vcat, ones), preferred_element_type=F32)
                    m_b = jnp.broadcast_to(m_blk, pv.shape)
                    for a in range(fan):
                        rows_t = pl.ds(n * dil * DIL_BLOCK + r + dil * a, piece, stride=src)
                        acc_scr[p, hh, rows_t, :] = pv[a * piece:(a + 1) * piece]
                        m_scr[p, hh, rows_t, :] = m_b[a * piece:(a + 1) * piece]

    def finish(c, carry):
        rows = pl.ds(pl.multiple_of(c * DIL_BLOCK, DIL_BLOCK), DIL_BLOCK)
        tots = []
        for hh in range(2):
            ms = [m_scr[p, hh, rows, :] for p in range(len(DIL_PATTERNS))]
            m_all = functools.reduce(jnp.maximum, ms)
            tots.append(sum(jnp.exp2(m - m_all) * acc_scr[p, hh, rows, :] for p, m in enumerate(ms)))
        num = jnp.where(halves[0], tots[0], tots[1])
        den = pltpu.roll(jnp.where(halves[0], tots[1], tots[0]), HEAD_DIM, axis=1)
        o_ref[0, 0, rows, :] = (num / den).astype(o_ref.dtype)
        return carry

    lax.fori_loop(0, n_blocks, finish, 0, unroll=4)


def _mixer_a(qs, ks, vs, bias):
    npair, b, n_tiles, dil0, rows0, _ = qs[0].shape
    tile = dil0 * rows0
    s = n_tiles * tile
    cur = lambda bi, pi, si: (pi, bi, si, 0, 0, 0)
    prev = lambda bi, pi, si: (pi, bi, jnp.maximum(si - 1, 0), 0, 0, 0)
    in_specs, operands = [], []
    for dil, q, k, v in zip(MIX_LAYOUTS, qs, ks, vs):
        blk = (1, 1, SPAN // tile, dil, tile // dil, LANES)
        in_specs += [pl.BlockSpec(blk, cur), pl.BlockSpec(blk, prev), pl.BlockSpec(blk, cur),
                     pl.BlockSpec(blk, prev), pl.BlockSpec(blk, cur)]
        operands += [q, k, k, v, v]
    in_specs.append(pl.BlockSpec((bias.shape[0], 2) + bias.shape[2:], lambda bi, pi, si: (0, pi, 0, 0)))
    return pl.pallas_call(
        _mixer_a_kernel,
        grid=(b, npair, s // SPAN),
        in_specs=in_specs,
        out_specs=pl.BlockSpec((1, 1, SPAN, LANES), lambda bi, pi, si: (pi, bi, si, 0)),
        out_shape=jax.ShapeDtypeStruct((npair, b, s, LANES), BF16),
        scratch_shapes=[pltpu.VMEM((len(DIL_PATTERNS), 2, SPAN, LANES), F32),
                        pltpu.VMEM((len(DIL_PATTERNS), 2, SPAN, LANES), F32)],
        compiler_params=_cparams("parallel", "parallel", "parallel"),
        name="mixer_a",
    )(*operands, bias)


def _compress_kernel(pa_ref, pb_ref, *refs):
    n_streams = len(refs) // 6
    for i in range(n_streams):
        c_ref, w1a_ref, w1b_ref, b1_ref, w2_ref = refs[5 * i:5 * i + 5]
        o_ref = refs[5 * n_streams + i]
        nch = o_ref.shape[1]
        c = jnp.concatenate([c_ref[0, pl.ds(l, nch, stride=CMP_STRIDE), :] for l in range(CMP_STRIDE)], axis=1)
        xa = (c + pa_ref[...]).astype(BF16)
        xb = (c + pb_ref[...]).astype(BF16)
        ha = jnp.dot(xa, w1a_ref[...], preferred_element_type=F32)
        hb = jnp.dot(xb, w1b_ref[...], preferred_element_type=F32)
        hb_next = jnp.concatenate([hb[1:], jnp.zeros_like(hb[:1])], axis=0)
        hid = jax.nn.gelu(ha + hb_next + b1_ref[...])
        o_ref[0] = jnp.dot(hid.astype(BF16), w2_ref[...], preferred_element_type=F32).astype(o_ref.dtype)


def _compress(cs, pos, params):
    b, s, _ = cs[0][0].shape
    nch = s // CMP_STRIDE
    half = CMP_LEN // 2
    zero = jnp.zeros((half, HEAD_DIM, CMP_HIDDEN), F32)

    def grouped(wpart):
        g0 = jnp.concatenate([wpart, zero], axis=1).reshape(half * LANES, CMP_HIDDEN)
        g1 = jnp.concatenate([zero, wpart], axis=1).reshape(half * LANES, CMP_HIDDEN)
        return jnp.concatenate([g0, g1], axis=1).astype(BF16)

    pa = jnp.tile(pos[:half], (1, 2)).reshape(1, half * LANES)
    pb = jnp.tile(pos[half:], (1, 2)).reshape(1, half * LANES)
    full = lambda a: pl.BlockSpec(a.shape, lambda i: (0,) * a.ndim)
    operands, in_specs = [pa, pb], [full(pa), full(pb)]
    for (c, slot), (w1, b1, w2) in zip(cs, params):
        zw = jnp.zeros_like(w2)
        w2g = jnp.concatenate([jnp.concatenate([w2, zw], axis=1),
                               jnp.concatenate([zw, w2], axis=1)], axis=0).astype(BF16)
        weights = [grouped(w1[:half]), grouped(w1[half:]), jnp.tile(b1, 2).reshape(1, 2 * CMP_HIDDEN), w2g]
        operands += [c] + weights
        in_specs += [pl.BlockSpec((1, s, LANES), lambda i, slot=slot: (i, 0, slot))] + [full(w_) for w_ in weights]
    return pl.pallas_call(
        _compress_kernel,
        grid=(b,),
        in_specs=in_specs,
        out_specs=[pl.BlockSpec((1, nch, LANES), lambda i: (i, 0, 0))] * len(cs),
        out_shape=[jax.ShapeDtypeStruct((b, nch, LANES), BF16)] * len(cs),
        compiler_params=_cparams("parallel"),
        name="compress",
    )(*operands)


def _nsa_kernel(q_ref, gl_ref, kc_ref, vc_ref, wov_ref, ks_ref, blk_ref, vs_ref, bias_s_ref, kw_ref, vw_ref, bias_w_ref,
                o_ref, acc_ref, m_ref, s_ref, part_ref, sb_ref, *, nsel, n_bias, kw):
    qt = pl.program_id(1)
    tq = q_ref.shape[1]
    n_sub = tq // SUB_TILE
    sub_rows = B_GROUP * SUB_TILE
    gate = jax.nn.sigmoid(gl_ref[0])
    halves = [_half_mask(g) for g in range(B_KV_HEADS)]
    q_slots = [q_ref[0, :, r * LANES:(r + 1) * LANES] for r in range(B_GROUP)]
    zero = jnp.zeros((tq, LANES), BF16)

    def stacked(slots):
        return jnp.concatenate([slots[r][u * SUB_TILE:(u + 1) * SUB_TILE]
                                for u in range(n_sub) for r in range(B_GROUP)], axis=0)

    def head_rows(o, r):
        return jnp.concatenate([o[(u * B_GROUP + r) * SUB_TILE:(u * B_GROUP + r + 1) * SUB_TILE]
                                for u in range(n_sub)], axis=0)

    def gate_tile(r, branch):
        c0, c1 = (r * N_BRANCH + branch, (B_GROUP + r) * N_BRANCH + branch)
        return jnp.where(halves[0], gate[:, c0:c0 + 1], gate[:, c1:c1 + 1])

    def gated_slots(o, g, branch, slots):
        for r in range(B_GROUP):
            c = (g * B_GROUP + r) * N_BRANCH + branch
            o_r = head_rows(o, r) * gate[:, c:c + 1]
            slots[r] = o_r if slots[r] is None else jnp.where(halves[g], o_r, slots[r])

    def normalised_slots(accs, branch):
        slots = []
        for r in range(B_GROUP):
            a0, a1 = head_rows(accs[0], r), head_rows(accs[1], r)
            num = jnp.where(halves[0], a0, a1)
            den = pltpu.roll(jnp.where(halves[0], a1, a0), HEAD_DIM, axis=1)
            slots.append(num / jnp.maximum(den, 1e-30) * gate_tile(r, branch))
        return slots

    nck = kc_ref.shape[1]
    row = lax.broadcasted_iota(jnp.int32, (n_sub * sub_rows, 1), 0)
    t_row = qt * tq + (row // sub_rows) * SUB_TILE + row % SUB_TILE
    blk_end = lax.broadcasted_iota(jnp.int32, (1, nck), 1) * CMP_STRIDE + (CMP_LEN - 1)
    valid = blk_end <= t_row
    kc = kc_ref[0]
    vc = vc_ref[0]
    n_rows = n_sub * sub_rows
    q_all = jnp.concatenate([stacked([jnp.where(halves[g], q_slots[r], zero) for r in range(B_GROUP)])
                             for g in range(B_KV_HEADS)], axis=0)
    s = lax.dot_general(q_all, kc, NT_DIMS, preferred_element_type=F32)
    s = jnp.where(jnp.concatenate([valid] * B_KV_HEADS, axis=0), s, -jnp.inf)
    m = jnp.max(s, axis=-1, keepdims=True)
    m = jnp.where(m == -jnp.inf, 0.0, m)
    p = jnp.exp2(s - m)
    den = jnp.sum(p, axis=-1, keepdims=True)
    p = p / jnp.maximum(den, 1e-30)
    o_all = jnp.dot(p.astype(BF16), vc, preferred_element_type=F32)
    pc = []
    cmp_slots = [None] * B_GROUP
    for g in range(B_KV_HEADS):
        pg = p[g * n_rows:(g + 1) * n_rows]
        pc.append(jnp.concatenate(
            [sum(pg[(u * B_GROUP + r) * SUB_TILE:(u * B_GROUP + r + 1) * SUB_TILE] for r in range(B_GROUP))
             for u in range(n_sub)], axis=0))
        gated_slots(o_all[g * n_rows:(g + 1) * n_rows], g, 0, cmp_slots)
    for r in range(B_GROUP):
        part_ref[0, :, r * LANES:(r + 1) * LANES] = cmp_slots[r]

    t_lane = qt * tq + lax.broadcasted_iota(jnp.int32, (1, tq), 1)
    cur = t_lane // SEL_BLOCK
    j = lax.broadcasted_iota(jnp.int32, (nsel, 1), 0)
    forced = (j == 0) | (j == cur) | (j == cur - 1)
    imps = []
    for g in reversed(range(B_KV_HEADS)):
        imp = lax.dot_general(wov_ref[...], pc[g], NT_DIMS, preferred_element_type=F32,
                              precision=lax.Precision.HIGHEST)
        imps.append(jnp.where(j > cur, -jnp.inf, jnp.where(forced, jnp.inf, imp)))

    last_block = (qt * tq + tq - 1) // SEL_BLOCK
    for level in range(-(-nsel // RANK_CHUNK)):
        n_live = min((level + 1) * RANK_CHUNK, nsel)

        @pl.when(last_block // RANK_CHUNK == level)
        def _(n_live=n_live):
            blocks = []
            for imp in imps:
                if n_live <= SEL_TOPN:
                    blocks.append(jnp.zeros((HEAD_DIM, tq), F32))
                    continue
                live = imp[:n_live]
                jl = j[:n_live]
                rank = jnp.zeros((n_live, tq), jnp.int32)
                for jp in range(n_live):
                    row = live[jp:jp + 1, :]
                    rank = rank + jnp.where(jl > jp, (row >= live).astype(jnp.int32), (row > live).astype(jnp.int32))
                blocks.append(jnp.where(rank < SEL_TOPN, 0.0, NOT_SELECTED))
                if n_live < HEAD_DIM:
                    blocks.append(jnp.zeros((HEAD_DIM - n_live, tq), F32))
            sb_ref[...] = jnp.concatenate(blocks, axis=0).T.astype(sb_ref.dtype)

    sb = sb_ref[...]
    qaug = [stacked([jnp.where(halves[g], q_slots[r], sb) for r in range(B_GROUP)])
            for g in range(B_KV_HEADS)]
    n_steps = (qt * tq + tq + K_TILE - 1) // K_TILE
    acc_ref[...] = jnp.zeros_like(acc_ref)
    m_ref[...] = jnp.full_like(m_ref, M_INIT)

    def scores(kt):
        for jt in range(K_TILE // SCORE_TILE):
            k0 = pl.multiple_of(kt * K_TILE + jt * SCORE_TILE, SCORE_TILE)
            k2 = ks_ref[0, pl.ds(k0, SCORE_TILE), :]
            e2 = blk_ref[pl.ds(k0, SCORE_TILE), :]
            for g in range(B_KV_HEADS):
                k = jnp.where(halves[g], k2, e2)
                s = lax.dot_general(qaug[g], k, NT_DIMS, preferred_element_type=F32)
                for u in range(n_sub):
                    rows_u = slice(u * sub_rows, (u + 1) * sub_rows)
                    for c in range(SCORE_TILE // SUB_TILE):
                        bi = jnp.clip(qt * n_sub + u - k0 // SUB_TILE - c + 1, 0, n_bias - 1)
                        col = jt * SCORE_TILE + c * SUB_TILE
                        s_ref[kt % 2, g, rows_u, col:col + SUB_TILE] = (
                            s[rows_u, c * SUB_TILE:(c + 1) * SUB_TILE] + bias_s_ref[g, bi])

    def accumulate(kt):
        k0 = pl.multiple_of(kt * K_TILE, K_TILE)
        v2 = vs_ref[0, pl.ds(k0, K_TILE), :]
        for g in range(B_KV_HEADS):
            v = jnp.where(halves[g], v2, jnp.ones((K_TILE, LANES), BF16))
            s = s_ref[kt % 2, g]
            m_old = m_ref[g]
            m_new = jnp.maximum(m_old, jnp.max(s, axis=-1, keepdims=True))
            alpha = jnp.exp2(m_old - m_new)
            p = jnp.exp2(s - jnp.tile(m_new, (1, K_TILE // LANES)))
            acc_ref[g] = alpha * acc_ref[g] + jnp.dot(p.astype(BF16), v, preferred_element_type=F32)
            m_ref[g] = m_new

    def scores_own(kt):
        per = SCORE_TILE // SUB_TILE
        for jt in range(K_TILE // SCORE_TILE):
            k0 = pl.multiple_of(kt * K_TILE + jt * SCORE_TILE, SCORE_TILE)
            k2 = ks_ref[0, pl.ds(k0, SCORE_TILE), :]
            e2 = blk_ref[pl.ds(k0, SCORE_TILE), :]
            u0 = jt * per
            for g in range(B_KV_HEADS):
                k = jnp.where(halves[g], k2, e2)
                s = lax.dot_general(qaug[g][u0 * sub_rows:], k, NT_DIMS, preferred_element_type=F32)
                for u in range(u0, n_sub):
                    rows_u = slice(u * sub_rows, (u + 1) * sub_rows)
                    for c in range(per):
                        key_sub = u0 + c
                        if key_sub <= u:
                            col = key_sub * SUB_TILE
                            s_ref[kt % 2, g, rows_u, col:col + SUB_TILE] = (
                                s[(u - u0) * sub_rows:(u - u0 + 1) * sub_rows, c * SUB_TILE:(c + 1) * SUB_TILE]
                                + bias_s_ref[g, u - key_sub + 1])

    def accumulate_own(kt):
        k0 = pl.multiple_of(kt * K_TILE, K_TILE)
        v2 = vs_ref[0, pl.ds(k0, K_TILE), :]
        for g in range(B_KV_HEADS):
            v = jnp.where(halves[g], v2, jnp.ones((K_TILE, LANES), BF16))
            for u in range(n_sub):
                rows_u = slice(u * sub_rows, (u + 1) * sub_rows)
                width = (u + 1) * SUB_TILE
                s = s_ref[kt % 2, g, rows_u, :width]
                m_old = m_ref[g, rows_u, :]
                m_new = jnp.maximum(m_old, jnp.max(s, axis=-1, keepdims=True))
                p = jnp.exp2(s - jnp.tile(m_new, (1, width // LANES)))
                acc_ref[g, rows_u, :] = (jnp.exp2(m_old - m_new) * acc_ref[g, rows_u, :]
                                         + jnp.dot(p.astype(BF16), v[:width], preferred_element_type=F32))

    def body(kt, carry):
        accumulate(kt)
        scores(kt + 1)
        return carry

    n_wt = kw // SUB_TILE
    q_win = [stacked([jnp.where(halves[g], q_slots[r], zero) for r in range(B_GROUP)]) for g in range(B_KV_HEADS)]
    o_win = [[] for _ in range(B_KV_HEADS)]
    for u in range(n_sub):
        sub = qt * n_sub + u
        w0 = pl.multiple_of(jnp.maximum(sub * SUB_TILE - WIN_LEN, 0), SUB_TILE)
        skip = jnp.maximum(WIN_LEN // SUB_TILE - sub, 0)
        kwin = kw_ref[0, pl.ds(w0, kw), :]
        vwin = vw_ref[0, pl.ds(w0, kw), :]
        q_u = jnp.concatenate([q_win[g][u * sub_rows:(u + 1) * sub_rows] for g in range(B_KV_HEADS)], axis=0)
        s_u = lax.dot_general(q_u, kwin, NT_DIMS, preferred_element_type=F32)
        for g in range(B_KV_HEADS):
            s = s_u[g * sub_rows:(g + 1) * sub_rows]
            s = jnp.concatenate([s[:, jw * SUB_TILE:(jw + 1) * SUB_TILE] + bias_w_ref[g, jnp.minimum(jw + skip, n_wt)]
                                 for jw in range(n_wt)], axis=1)
            p = jnp.exp2(s - jnp.max(s, axis=-1, keepdims=True))
            acc = jnp.dot(p.astype(BF16), jnp.where(halves[g], vwin, jnp.ones_like(vwin)),
                          preferred_element_type=F32)
            o_win[g].append(acc)
    win_slots = normalised_slots([jnp.concatenate(o_win[g], axis=0) for g in range(B_KV_HEADS)], 2)
    for r in range(B_GROUP):
        part_ref[1, :, r * LANES:(r + 1) * LANES] = win_slots[r]

    scores(0)

    @pl.when(n_steps > 1)
    def _():
        lax.fori_loop(0, n_steps - 2, body, 0)
        accumulate(n_steps - 2)
        scores_own(n_steps - 1)

    accumulate_own(n_steps - 1)
    sel_slots = normalised_slots([acc_ref[g] for g in range(B_KV_HEADS)], 1)
    for r in range(B_GROUP):
        sl = slice(r * LANES, (r + 1) * LANES)
        o_ref[0, :, sl] = ((part_ref[0, :, sl] + sel_slots[r]) + part_ref[1, :, sl]).astype(o_ref.dtype)


def _nsa(qb, gl, kcmp, vcmp, ksl, vsl, bias_s, kwin, vwin, bias_w):
    b, s, w = qb.shape
    assert Q_TILE == K_TILE and bias_s.shape[1] > Q_TILE // SUB_TILE
    nsel = s // SEL_BLOCK
    nck = kcmp.shape[1]
    ratio = SEL_BLOCK // CMP_STRIDE
    wov = np.zeros((nsel, nck), np.float32)
    for jj in range(nsel):
        for off, wt in zip(range(-1, ratio), CMP_OVERLAP):
            n = ratio * jj + off
            if 0 <= n < nck - 1:
                wov[jj, n] = wt
    wov = jnp.asarray(wov)
    blk = jnp.asarray((np.arange(s)[:, None] // SEL_BLOCK) == (np.arange(LANES)[None, :] % HEAD_DIM), BF16)
    rows = B_GROUP * Q_TILE
    tile = lambda width: pl.BlockSpec((1, Q_TILE, width), lambda bi, qi: (bi, qi, 0))
    whole = lambda a: pl.BlockSpec((1,) + a.shape[1:], lambda bi, qi: (bi, 0, 0))
    slot_of = lambda v: pl.BlockSpec((1, s, LANES), lambda bi, qi, slot=v[1]: (bi, 0, slot))
    const = lambda a: pl.BlockSpec(a.shape, lambda bi, qi: (0,) * a.ndim, pipeline_mode=pl.Buffered(1))
    return pl.pallas_call(
        functools.partial(_nsa_kernel, nsel=nsel, n_bias=bias_s.shape[1], kw=WIN_LEN + SUB_TILE),
        grid=(b, s // Q_TILE),
        in_specs=[tile(w), tile(LANES), whole(kcmp), whole(vcmp), const(wov), slot_of(ksl), const(blk), slot_of(vsl),
                  const(bias_s), slot_of(kwin), slot_of(vwin), const(bias_w)],
        out_specs=tile(w),
        out_shape=jax.ShapeDtypeStruct((b, s, w), BF16),
        scratch_shapes=[pltpu.VMEM((B_KV_HEADS, rows, LANES), F32), pltpu.VMEM((B_KV_HEADS, rows, LANES), F32),
                        pltpu.VMEM((2, B_KV_HEADS, rows, K_TILE), F32), pltpu.VMEM((2, Q_TILE, w), F32),
                        pltpu.VMEM((Q_TILE, LANES), BF16)],
        compiler_params=_cparams("parallel", "parallel"),
        name="nsa",
    )(qb, gl, kcmp, vcmp, wov, ksl[0], blk, vsl[0], bias_s, kwin[0], vwin[0], bias_w)


def _tail_kernel(x_ref, oa_ref, ob_ref, wa_ref, wb_ref, g2_ref, wg_ref, wu_ref, wd_ref, gf_ref, o_ref):
    o_a = jnp.concatenate([oa_ref[p] for p in range(oa_ref.shape[0])], axis=1)
    mix = jnp.dot(o_a, wa_ref[...], preferred_element_type=F32)
    mix = mix + jnp.dot(ob_ref[...], wb_ref[...], preferred_element_type=F32)
    h = x_ref[...] + mix
    y = h * lax.rsqrt(jnp.mean(h * h, axis=-1, keepdims=True) + RMS_EPS)
    hn = (y * g2_ref[...]).astype(BF16)
    a = jnp.dot(hn, wg_ref[...], preferred_element_type=F32)
    u = jnp.dot(hn, wu_ref[...], preferred_element_type=F32)
    act = (jax.nn.silu(a) * u).astype(BF16)
    h2 = h + jnp.dot(act, wd_ref[...], preferred_element_type=F32)
    y2 = h2 * lax.rsqrt(jnp.mean(h2 * h2, axis=-1, keepdims=True) + RMS_EPS)
    o_ref[...] = y2 * gf_ref[...]


def _tail(x2, oa, ob, wa, wb, g2, wg, wu, wd, gf, tm):
    n, d = x2.shape
    row = lambda width: pl.BlockSpec((tm, width), lambda i: (i, 0))
    const = lambda a: pl.BlockSpec(a.shape, lambda i: (0, 0), pipeline_mode=pl.Buffered(1))
    return pl.pallas_call(
        _tail_kernel,
        grid=(n // tm,),
        in_specs=[row(d), pl.BlockSpec((oa.shape[0], tm, LANES), lambda i: (0, i, 0)), row(ob.shape[1]),
                  const(wa), const(wb), const(g2), const(wg), const(wu), const(wd), const(gf)],
        out_specs=row(d),
        out_shape=jax.ShapeDtypeStruct((n, d), F32),
        compiler_params=_cparams("parallel"),
        name="tail",
    )(x2, oa, ob, wa, wb, g2, wg, wu, wd, gf)


def _slot_perm():
    perm = np.zeros(B_WIDTH, np.int64)
    for r in range(B_GROUP):
        for g in range(B_KV_HEADS):
            src = (g * B_GROUP + r) * HEAD_DIM
            dst = r * LANES + g * HEAD_DIM
            perm[dst:dst + HEAD_DIM] = np.arange(src, src + HEAD_DIM)
    return perm


def _layer(h, tab_a, tab_b, norm1_g, w_in, cmp_pos, k_w1, k_b1, k_w2, v_w1, v_b1, v_w2,
           w_out, norm2_g, w_gate, w_up, w_down, norm_f_g):
    b, s, d = h.shape
    n = b * s
    assert s % SPAN == 0 and s % K_TILE == 0
    perm = _slot_perm()

    cols = np.cumsum([0, A_WIDTH, A_WIDTH, A_WIDTH, B_WIDTH] + [KV_WIDTH] * 6 + [GATE_WIDTH])
    w_aq = w_in[:, cols[0]:cols[1]] * (SCALE * LOG2E)
    w_bq = (w_in[:, cols[3]:cols[4]] * (SCALE * LOG2E))[:, perm]
    w_gl = jnp.pad(w_in[:, cols[10]:cols[11]], ((0, 0), (0, LANES - GATE_WIDTH)))
    w1 = jnp.concatenate([w_aq, w_in[:, cols[1]:cols[3]], w_bq, w_in[:, cols[4]:cols[10]], w_gl], axis=1).astype(BF16)
    widths = (A_WIDTH,) * 3 + (B_WIDTH, 2 * KV_WIDTH, 4 * KV_WIDTH, LANES)
    dtypes = (BF16,) * 4 + (F32, BF16, F32)
    x2 = h.reshape(n, d)
    n_lay = len(MIX_LAYOUTS)
    outs = _inproj(h, norm1_g.reshape(1, d), w1, widths, dtypes, n_mix=3, tm=512)
    qa, ka, va = (outs[i * n_lay:(i + 1) * n_lay] for i in range(3))
    r3 = lambda t: t.reshape(b, s, t.shape[-1])
    qb, kv_cmp, kv_rest, gl = map(r3, outs[3 * n_lay:])
    kc, vc = (kv_cmp, 0), (kv_cmp, 1)
    ksl, vsl, kw, vw = ((kv_rest, slot) for slot in range(4))

    tok_dist, in_window = [], []
    for window, dil in DIL_PATTERNS:
        fan = (dil if dil in MIX_LAYOUTS else MIX_LAYOUTS[0]) // dil
        row = np.arange(DIL_BLOCK)
        step = fan * (row % (DIL_BLOCK // fan)) + row // (DIL_BLOCK // fan)
        key_step = np.concatenate([step, DIL_BLOCK + step])
        dist = DIL_BLOCK + step[:, None] - key_step[None, :]
        tok_dist.append(np.maximum(dist, 0) * dil)
        in_window.append((dist >= 0) & (dist <= window // dil))
    tok_dist, in_window = np.stack(tok_dist), np.stack(in_window)
    bias_a = _bias_tiles(tab_a, tok_dist, in_window, inner=A_HEADS)[0]
    o_a = _mixer_a(qa, ka, va, bias_a).reshape(A_HEADS // 2, n, LANES)

    nch = s // CMP_STRIDE
    kcmp, vcmp = _compress([kc, vc], cmp_pos, [(k_w1, k_b1, k_w2), (v_w1, v_b1, v_w2)])
    d_sat = int(np.nonzero(_t5_bucket_np(np.arange(s)) < REL_BUCKETS - 1)[0].max()) + 1
    first_far = -(-(d_sat + SUB_TILE - 1) // SUB_TILE)
    n_bias = min(s // SUB_TILE, first_far + 1) + 1
    dd = ((np.arange(n_bias)[:, None, None] - 1) * SUB_TILE + np.arange(SUB_TILE)[None, :, None]
          - np.arange(SUB_TILE)[None, None, :])
    bias_sl = _bias_tiles(tab_b, dd, dd >= 0, inner=B_GROUP)
    bias_sl = bias_sl.reshape(B_KV_HEADS, n_bias, B_GROUP * SUB_TILE, SUB_TILE)
    n_wt = WIN_LEN // SUB_TILE + 1
    dw = ((n_wt - 1 - np.arange(n_wt + 1))[:, None, None] * SUB_TILE + np.arange(SUB_TILE)[None, :, None]
          - np.arange(SUB_TILE)[None, None, :])
    in_win = (dw >= 0) & (dw < WIN_LEN) & (np.arange(n_wt + 1) < n_wt)[:, None, None]
    bias_w = _bias_tiles(tab_b, dw, in_win, inner=B_GROUP)
    bias_w = bias_w.reshape(B_KV_HEADS, n_wt + 1, B_GROUP * SUB_TILE, SUB_TILE)
    o_b = _nsa(qb, gl, kcmp, vcmp, ksl, vsl, bias_sl, kw, vw, bias_w)

    wa = w_out[:A_WIDTH].astype(BF16)
    wb = w_out[A_WIDTH:][perm].astype(BF16)
    out = _tail(x2, o_a, o_b.reshape(n, B_WIDTH), wa, wb, norm2_g.reshape(1, d), w_gate.astype(BF16),
                w_up.astype(BF16), w_down.astype(BF16), norm_f_g.reshape(1, d), tm=512)
    return out.reshape(b, s, d)


def kernel(x, norm1_g, w_in, rel_bias, cmp_pos, cmp_k_w1, cmp_k_b1, cmp_k_w2, cmp_v_w1, cmp_v_b1, cmp_v_w2,
           w_out, norm2_g, w_gate, w_up, w_down, norm_f_g):
    assert w_in.shape[0] == 1, "single-layer model"
    tab_a = rel_bias[:, :A_HEADS].T * LOG2E
    tab_b = rel_bias[:, A_HEADS:].T * LOG2E
    return _layer(x, tab_a, tab_b, norm1_g[0], w_in[0], cmp_pos[0], cmp_k_w1[0], cmp_k_b1[0], cmp_k_w2[0],
                  cmp_v_w1[0], cmp_v_b1[0], cmp_v_w2[0], w_out[0], norm2_g[0], w_gate[0], w_up[0], w_down[0],
                  norm_f_g)
```

```python
import functools
import math

import numpy as np
import jax
import jax.numpy as jnp
from jax import lax
from jax.experimental import pallas as pl
from jax.experimental.pallas import tpu as pltpu

F32 = jnp.float32
BF16 = jnp.bfloat16

HEAD_DIM = 64
LANES = 128
A_HEADS = 8
DIL_PATTERNS = ((128, 1), (512, 4), (2048, 16))
DIL_BLOCK = 128
B_HEADS = 8
B_KV_HEADS = 2
B_GROUP = B_HEADS // B_KV_HEADS
CMP_LEN = 32
CMP_STRIDE = 16
CMP_HIDDEN = 128
SEL_BLOCK = 64
SEL_TOPN = 16
CMP_OVERLAP = (1.0, 2.0, 2.0, 2.0, 1.0)
WIN_LEN = 512
N_BRANCH = 3
REL_BUCKETS = 32
REL_MAX_DIST = 2048
RMS_EPS = 1e-6
A_WIDTH = A_HEADS * HEAD_DIM
B_WIDTH = B_HEADS * HEAD_DIM
KV_WIDTH = B_KV_HEADS * HEAD_DIM
GATE_WIDTH = B_HEADS * N_BRANCH
SCALE = HEAD_DIM ** -0.5
LOG2E = math.log2(math.e)

SPAN = DIL_PATTERNS[-1][1] * DIL_BLOCK
MIX_LAYOUTS = tuple(d for _, d in DIL_PATTERNS if d > 1)
QUERY_OUTPUTS = (0, 3)
NARROW_FROM = 4
RANK_CHUNK = 8
SUB_TILE = 128
Q_TILE = 512
K_TILE = 512
SCORE_TILE = 256
NOT_SELECTED = -(2.0 ** 100)
M_INIT = -1e30
VMEM_LIMIT = 56 * 1024 * 1024

NT_DIMS = (((1,), (1,)), ((), ()))


def _cparams(*sem):
    return pltpu.CompilerParams(dimension_semantics=sem, vmem_limit_bytes=VMEM_LIMIT)


def _t5_bucket_np(dist):
    max_exact = REL_BUCKETS // 2
    d = np.asarray(dist)
    df = np.maximum(d, 1).astype(np.float32)
    large = max_exact + (np.log(df / np.float32(max_exact)) / np.float32(math.log(REL_MAX_DIST / max_exact))
                         * np.float32(REL_BUCKETS - max_exact)).astype(np.int32)
    large = np.minimum(large, REL_BUCKETS - 1)
    return np.where(d < max_exact, d, large).astype(np.int32)


def _bias_kernel(tab_ref, idx_ref, o_ref, *, buckets):
    inner = o_ref.shape[2]
    for t, present in enumerate(buckets):
        idx = idx_ref[t]
        accs = [jnp.full(idx.shape, -jnp.inf, F32) for _ in range(inner)]
        for bucket in present:
            hit = idx == bucket
            for hi in range(inner):
                head = pl.program_id(0) * inner + hi
                accs[hi] = jnp.where(hit, tab_ref[head * REL_BUCKETS + bucket], accs[hi])
        for hi in range(inner):
            o_ref[0, t, hi] = accs[hi]


def _bias_tiles(tab, dist, valid, inner):
    h = tab.shape[0]
    t, r, c = dist.shape
    idx = np.where(valid, _t5_bucket_np(np.maximum(dist, 0)), -1).astype(np.int32)
    buckets = tuple(tuple(int(v) for v in np.unique(idx[ti]) if v >= 0) for ti in range(t))
    return pl.pallas_call(
        functools.partial(_bias_kernel, buckets=buckets),
        grid=(h // inner,),
        in_specs=[pl.BlockSpec(memory_space=pltpu.SMEM),
                  pl.BlockSpec((t, r, c), lambda a: (0, 0, 0))],
        out_specs=pl.BlockSpec((1, t, inner, r, c), lambda a: (a, 0, 0, 0, 0)),
        out_shape=jax.ShapeDtypeStruct((h // inner, t, inner, r, c), F32),
        compiler_params=_cparams("parallel"),
        name="bias_tiles",
    )(tab.reshape(-1), jnp.asarray(idx))


def _half_mask(g):
    lane = lax.broadcasted_iota(jnp.int32, (1, LANES), 1)
    return (lane >= HEAD_DIM) if g else (lane < HEAD_DIM)


def _inproj_kernel(x_ref, g_ref, w_ref, *refs, widths, n_mix):
    n_dil = len(DIL_PATTERNS)
    n_lay = len(MIX_LAYOUTS)
    n_scr = n_mix * (n_dil - 1)
    mix_refs = [(None,) * (n_dil - n_lay) + tuple(refs[i * n_lay:(i + 1) * n_lay]) for i in range(n_mix)]
    out_refs = refs[n_mix * n_lay:len(refs) - n_scr]
    scratch = [refs[len(refs) - n_scr + i * (n_dil - 1):len(refs) - n_scr + (i + 1) * (n_dil - 1)]
               for i in range(n_mix)]
    x = x_ref[...]
    tm = x.shape[0]
    y = x * lax.rsqrt(jnp.mean(x * x, axis=-1, keepdims=True) + RMS_EPS)
    xn = (y * g_ref[...]).astype(BF16)
    merged = jnp.dot(xn, w_ref[:, sum(widths[:NARROW_FROM]):], preferred_element_type=F32)
    start = 0
    for idx, w in enumerate(widths):
        if idx < NARROW_FROM:
            r = jnp.dot(xn, w_ref[:, start:start + w], preferred_element_type=F32)
            if idx in QUERY_OUTPUTS:
                r = r * (SCALE * LOG2E)
        else:
            off = start - sum(widths[:NARROW_FROM])
            r = merged[:, off:off + w]
        if idx < n_mix:
            prev_dil = None
            for pi, (o_ref, (_, dil)) in enumerate(zip(mix_refs[idx], DIL_PATTERNS)):
                keep = scratch[idx][pi] if pi < n_dil - 1 else None
                for p in range(w // LANES):
                    if prev_dil is None:
                        planes = [(0, r[:, p * LANES:(p + 1) * LANES])]
                    else:
                        f = dil // prev_dil
                        planes = [(res + prev_dil * a,
                                   scratch[idx][pi - 1][p, res, pl.ds(a, tm // dil, stride=f), :])
                                  for res in range(prev_dil) for a in range(f)]
                    for res, rows in planes:
                        if o_ref is not None:
                            o_ref[p, 0, 0, res] = rows.astype(o_ref.dtype)
                        if keep is not None:
                            keep[p, res] = rows
                prev_dil = dil
        else:
            o_ref = out_refs[idx - n_mix]
            o_ref[...] = r.astype(o_ref.dtype)
        start += w


def _inproj(x3, g, w, widths, dtypes, n_mix, tm):
    b, s, d = x3.shape
    n = b * s
    per_batch = s // tm
    out_shape, out_specs = [], []
    for w_, dt in zip(widths[:n_mix], dtypes[:n_mix]):
        for dil in MIX_LAYOUTS:
            out_shape.append(jax.ShapeDtypeStruct((w_ // LANES, b, per_batch, dil, tm // dil, LANES), dt))
            out_specs.append(pl.BlockSpec((w_ // LANES, 1, 1, dil, tm // dil, LANES),
                                          lambda i: (0, i // per_batch, i % per_batch, 0, 0, 0)))
    for w_, dt in zip(widths[n_mix:], dtypes[n_mix:]):
        out_shape.append(jax.ShapeDtypeStruct((n, w_), dt))
        out_specs.append(pl.BlockSpec((tm, w_), lambda i: (i, 0)))
    return pl.pallas_call(
        functools.partial(_inproj_kernel, widths=widths, n_mix=n_mix),
        grid=(n // tm,),
        in_specs=[pl.BlockSpec((tm, d), lambda i: (i, 0)),
                  pl.BlockSpec((1, d), lambda i: (0, 0)),
                  pl.BlockSpec(w.shape, lambda i: (0, 0))],
        out_specs=out_specs,
        out_shape=out_shape,
        scratch_shapes=[pltpu.VMEM((w_ // LANES, dil, tm // dil, LANES), F32)
                        for w_ in widths[:n_mix] for _, dil in DIL_PATTERNS[:-1]],
        compiler_params=_cparams("parallel"),
        name="inproj",
    )(x3.reshape(n, d), g, w)


def _mixer_a_kernel(*refs):
    bias_ref, o_ref, acc_scr, m_scr = refs[5 * len(MIX_LAYOUTS):]
    sb = pl.program_id(2)
    halves = [_half_mask(hh) for hh in range(2)]
    col = lax.broadcasted_iota(jnp.int32, (1, 2 * DIL_BLOCK), 1)
    first_keep = (col >= DIL_BLOCK) | (sb > 0)
    ones = jnp.ones((2 * DIL_BLOCK, LANES), BF16)
    n_blocks = SPAN // DIL_BLOCK
    for p, (_, dil) in enumerate(DIL_PATTERNS):
        lay = MIX_LAYOUTS.index(dil) if dil in MIX_LAYOUTS else 0
        src = MIX_LAYOUTS[lay]
        fan = src // dil
        piece = DIL_BLOCK // fan
        q_ref, kp_ref, kc_ref, vp_ref, vc_ref = refs[5 * lay:5 * lay + 5]
        last = SPAN // src // piece - 1

        def gather(ref, r, nn, dil=dil, fan=fan, piece=piece):
            per_tile = ref.shape[4]
            chunk = min(piece, per_tile)
            return jnp.concatenate(
                [ref[0, 0, lo // per_tile, r + dil * a, lo % per_tile:lo % per_tile + chunk, :]
                 for a in range(fan) for lo in range(nn * piece, (nn + 1) * piece, chunk)], axis=0)

        for r in range(dil):
            for n in range(SPAN // (dil * DIL_BLOCK)):
                q2 = gather(q_ref, r, n)
                if n == 0:
                    k_prev, v_prev = gather(kp_ref, r, last), gather(vp_ref, r, last)
                else:
                    k_prev, v_prev = gather(kc_ref, r, n - 1), gather(vc_ref, r, n - 1)
                kcat = jnp.concatenate([k_prev, gather(kc_ref, r, n)], axis=0)
                vcat = jnp.concatenate([v_prev, gather(vc_ref, r, n)], axis=0)
                qm = jnp.concatenate([jnp.where(halves[hh], q2, jnp.zeros_like(q2)) for hh in range(2)], axis=0)
                s2 = lax.dot_general(qm, kcat, NT_DIMS, preferred_element_type=F32)
                for hh in range(2):
                    s = s2[hh * DIL_BLOCK:(hh + 1) * DIL_BLOCK] + bias_ref[p, hh]
                    if n == 0:
                        s = jnp.where(first_keep, s, -jnp.inf)
                    m_blk = jnp.max(s, axis=-1, keepdims=True)
                    pe = jnp.exp2(s - m_blk)
                    pv = jnp.dot(pe.astype(BF16), jnp.where(halves[hh], vcat, ones), preferred_element_type=F32)
                    m_b = jnp.broadcast_to(m_blk, pv.shape)
                    for a in range(fan):
                        rows_t = pl.ds(n * dil * DIL_BLOCK + r + dil * a, piece, stride=src)
                        acc_scr[p, hh, rows_t, :] = pv[a * piece:(a + 1) * piece]
                        m_scr[p, hh, rows_t, :] = m_b[a * piece:(a + 1) * piece]

    def finish(c, carry):
        rows = pl.ds(pl.multiple_of(c * DIL_BLOCK, DIL_BLOCK), DIL_BLOCK)
        tots = []
        for hh in range(2):
            ms = [m_scr[p, hh, rows, :] for p in range(len(DIL_PATTERNS))]
            m_all = functools.reduce(jnp.maximum, ms)
            tots.append(sum(jnp.exp2(m - m_all) * acc_scr[p, hh, rows, :] for p, m in enumerate(ms)))
        num = jnp.where(halves[0], tots[0], tots[1])
        den = pltpu.roll(jnp.where(halves[0], tots[1], tots[0]), HEAD_DIM, axis=1)
        o_ref[0, 0, rows, :] = (num / den).astype(o_ref.dtype)
        return carry

    lax.fori_loop(0, n_blocks, finish, 0, unroll=4)


def _mixer_a(qs, ks, vs, bias):
    npair, b, n_tiles, dil0, rows0, _ = qs[0].shape
    tile = dil0 * rows0
    s = n_tiles * tile
    cur = lambda bi, pi, si: (pi, bi, si, 0, 0, 0)
    prev = lambda bi, pi, si: (pi, bi, jnp.maximum(si - 1, 0), 0, 0, 0)
    in_specs, operands = [], []
    for dil, q, k, v in zip(MIX_LAYOUTS, qs, ks, vs):
        blk = (1, 1, SPAN // tile, dil, tile // dil, LANES)
        in_specs += [pl.BlockSpec(blk, cur), pl.BlockSpec(blk, prev), pl.BlockSpec(blk, cur),
                     pl.BlockSpec(blk, prev), pl.BlockSpec(blk, cur)]
        operands += [q, k, k, v, v]
    in_specs.append(pl.BlockSpec((bias.shape[0], 2) + bias.shape[2:], lambda bi, pi, si: (0, pi, 0, 0)))
    return pl.pallas_call(
        _mixer_a_kernel,
        grid=(b, npair, s // SPAN),
        in_specs=in_specs,
        out_specs=pl.BlockSpec((1, 1, SPAN, LANES), lambda bi, pi, si: (pi, bi, si, 0)),
        out_shape=jax.ShapeDtypeStruct((npair, b, s, LANES), BF16),
        scratch_shapes=[pltpu.VMEM((len(DIL_PATTERNS), 2, SPAN, LANES), F32),
                        pltpu.VMEM((len(DIL_PATTERNS), 2, SPAN, LANES), F32)],
        compiler_params=_cparams("parallel", "parallel", "parallel"),
        name="mixer_a",
    )(*operands, bias)


def _compress_kernel(pa_ref, pb_ref, *refs):
    n_streams = len(refs) // 6
    for i in range(n_streams):
        c_ref, w1a_ref, w1b_ref, b1_ref, w2_ref = refs[5 * i:5 * i + 5]
        o_ref = refs[5 * n_streams + i]
        nch = o_ref.shape[1]
        c = jnp.concatenate([c_ref[0, pl.ds(l, nch, stride=CMP_STRIDE), :] for l in range(CMP_STRIDE)], axis=1)
        xa = (c + pa_ref[...]).astype(BF16)
        xb = (c + pb_ref[...]).astype(BF16)
        ha = jnp.dot(xa, w1a_ref[...], preferred_element_type=F32)
        hb = jnp.dot(xb, w1b_ref[...], preferred_element_type=F32)
        hb_next = jnp.concatenate([hb[1:], jnp.zeros_like(hb[:1])], axis=0)
        hid = jax.nn.gelu(ha + hb_next + b1_ref[...])
        o_ref[0] = jnp.dot(hid.astype(BF16), w2_ref[...], preferred_element_type=F32).astype(o_ref.dtype)


def _compress(cs, pos, params):
    b, s, _ = cs[0][0].shape
    nch = s // CMP_STRIDE
    half = CMP_LEN // 2
    zero = jnp.zeros((half, HEAD_DIM, CMP_HIDDEN), F32)

    def grouped(wpart):
        g0 = jnp.concatenate([wpart, zero], axis=1).reshape(half * LANES, CMP_HIDDEN)
        g1 = jnp.concatenate([zero, wpart], axis=1).reshape(half * LANES, CMP_HIDDEN)
        return jnp.concatenate([g0, g1], axis=1).astype(BF16)

    pa = jnp.tile(pos[:half], (1, 2)).reshape(1, half * LANES)
    pb = jnp.tile(pos[half:], (1, 2)).reshape(1, half * LANES)
    full = lambda a: pl.BlockSpec(a.shape, lambda i: (0,) * a.ndim)
    operands, in_specs = [pa, pb], [full(pa), full(pb)]
    for (c, slot), (w1, b1, w2) in zip(cs, params):
        zw = jnp.zeros_like(w2)
        w2g = jnp.concatenate([jnp.concatenate([w2, zw], axis=1),
                               jnp.concatenate([zw, w2], axis=1)], axis=0).astype(BF16)
        weights = [grouped(w1[:half]), grouped(w1[half:]), jnp.tile(b1, 2).reshape(1, 2 * CMP_HIDDEN), w2g]
        operands += [c] + weights
        in_specs += [pl.BlockSpec((1, s, LANES), lambda i, slot=slot: (i, 0, slot))] + [full(w_) for w_ in weights]
    return pl.pallas_call(
        _compress_kernel,
        grid=(b,),
        in_specs=in_specs,
        out_specs=[pl.BlockSpec((1, nch, LANES), lambda i: (i, 0, 0))] * len(cs),
        out_shape=[jax.ShapeDtypeStruct((b, nch, LANES), BF16)] * len(cs),
        compiler_params=_cparams("parallel"),
        name="compress",
    )(*operands)


def _nsa_kernel(q_ref, gl_ref, kc_ref, vc_ref, wov_ref, ks_ref, blk_ref, vs_ref, bias_s_ref, kw_ref, vw_ref, bias_w_ref,
                o_ref, acc_ref, m_ref, s_ref, part_ref, sb_ref, *, nsel, n_bias, kw):
    qt = pl.program_id(1)
    tq = q_ref.shape[1]
    n_sub = tq // SUB_TILE
    sub_rows = B_GROUP * SUB_TILE
    gate = jax.nn.sigmoid(gl_ref[0])
    halves = [_half_mask(g) for g in range(B_KV_HEADS)]
    q_slots = [q_ref[0, :, r * LANES:(r + 1) * LANES] for r in range(B_GROUP)]
    zero = jnp.zeros((tq, LANES), BF16)

    def stacked(slots):
        return jnp.concatenate([slots[r][u * SUB_TILE:(u + 1) * SUB_TILE]
                                for u in range(n_sub) for r in range(B_GROUP)], axis=0)

    def head_rows(o, r):
        return jnp.concatenate([o[(u * B_GROUP + r) * SUB_TILE:(u * B_GROUP + r + 1) * SUB_TILE]
                                for u in range(n_sub)], axis=0)

    def gate_tile(r, branch):
        c0, c1 = (r * N_BRANCH + branch, (B_GROUP + r) * N_BRANCH + branch)
        return jnp.where(halves[0], gate[:, c0:c0 + 1], gate[:, c1:c1 + 1])

    def gated_slots(o, g, branch, slots):
        for r in range(B_GROUP):
            c = (g * B_GROUP + r) * N_BRANCH + branch
            o_r = head_rows(o, r) * gate[:, c:c + 1]
            slots[r] = o_r if slots[r] is None else jnp.where(halves[g], o_r, slots[r])

    def normalised_slots(accs, branch):
        slots = []
        for r in range(B_GROUP):
            a0, a1 = head_rows(accs[0], r), head_rows(accs[1], r)
            num = jnp.where(halves[0], a0, a1)
            den = pltpu.roll(jnp.where(halves[0], a1, a0), HEAD_DIM, axis=1)
            slots.append(num / jnp.maximum(den, 1e-30) * gate_tile(r, branch))
        return slots

    nck = kc_ref.shape[1]
    row = lax.broadcasted_iota(jnp.int32, (n_sub * sub_rows, 1), 0)
    t_row = qt * tq + (row // sub_rows) * SUB_TILE + row % SUB_TILE
    blk_end = lax.broadcasted_iota(jnp.int32, (1, nck), 1) * CMP_STRIDE + (CMP_LEN - 1)
    valid = blk_end <= t_row
    kc = kc_ref[0]
    vc = vc_ref[0]
    n_rows = n_sub * sub_rows
    q_all = jnp.concatenate([stacked([jnp.where(halves[g], q_slots[r], zero) for r in range(B_GROUP)])
                             for g in range(B_KV_HEADS)], axis=0)
    s = lax.dot_general(q_all, kc, NT_DIMS, preferred_element_type=F32)
    s = jnp.where(jnp.concatenate([valid] * B_KV_HEADS, axis=0), s, -jnp.inf)
    m = jnp.max(s, axis=-1, keepdims=True)
    m = jnp.where(m == -jnp.inf, 0.0, m)
    p = jnp.exp2(s - m)
    den = jnp.sum(p, axis=-1, keepdims=True)
    p = p / jnp.maximum(den, 1e-30)
    o_all = jnp.dot(p.astype(BF16), vc, preferred_element_type=F32)
    pc = []
    cmp_slots = [None] * B_GROUP
    for g in range(B_KV_HEADS):
        pg = p[g * n_rows:(g + 1) * n_rows]
        pc.append(jnp.concatenate(
            [sum(pg[(u * B_GROUP + r) * SUB_TILE:(u * B_GROUP + r + 1) * SUB_TILE] for r in range(B_GROUP))
             for u in range(n_sub)], axis=0))
        gated_slots(o_all[g * n_rows:(g + 1) * n_rows], g, 0, cmp_slots)
    for r in range(B_GROUP):
        part_ref[0, :, r * LANES:(r + 1) * LANES] = cmp_slots[r]

    t_lane = qt * tq + lax.broadcasted_iota(jnp.int32, (1, tq), 1)
    cur = t_lane // SEL_BLOCK
    j = lax.broadcasted_iota(jnp.int32, (nsel, 1), 0)
    forced = (j == 0) | (j == cur) | (j == cur - 1)
    imps = []
    for g in reversed(range(B_KV_HEADS)):
        imp = lax.dot_general(wov_ref[...], pc[g], NT_DIMS, preferred_element_type=F32,
                              precision=lax.Precision.HIGHEST)
        imps.append(jnp.where(j > cur, -jnp.inf, jnp.where(forced, jnp.inf, imp)))

    last_block = (qt * tq + tq - 1) // SEL_BLOCK
    for level in range(-(-nsel // RANK_CHUNK)):
        n_live = min((level + 1) * RANK_CHUNK, nsel)

        @pl.when(last_block // RANK_CHUNK == level)
        def _(n_live=n_live):
            blocks = []
            for imp in imps:
                if n_live <= SEL_TOPN:
                    blocks.append(jnp.zeros((HEAD_DIM, tq), F32))
                    continue
                live = imp[:n_live]
                jl = j[:n_live]
                rank = jnp.zeros((n_live, tq), jnp.int32)
                for jp in range(n_live):
                    row = live[jp:jp + 1, :]
                    rank = rank + jnp.where(jl > jp, (row >= live).astype(jnp.int32), (row > live).astype(jnp.int32))
                blocks.append(jnp.where(rank < SEL_TOPN, 0.0, NOT_SELECTED))
                if n_live < HEAD_DIM:
                    blocks.append(jnp.zeros((HEAD_DIM - n_live, tq), F32))
            sb_ref[...] = jnp.concatenate(blocks, axis=0).T.astype(sb_ref.dtype)

    sb = sb_ref[...]
    qaug = [stacked([jnp.where(halves[g], q_slots[r], sb) for r in range(B_GROUP)])
            for g in range(B_KV_HEADS)]
    n_steps = (qt * tq + tq + K_TILE - 1) // K_TILE
    acc_ref[...] = jnp.zeros_like(acc_ref)
    m_ref[...] = jnp.full_like(m_ref, M_INIT)

    def scores(kt):
        for jt in range(K_TILE // SCORE_TILE):
            k0 = pl.multiple_of(kt * K_TILE + jt * SCORE_TILE, SCORE_TILE)
            k2 = ks_ref[0, pl.ds(k0, SCORE_TILE), :]
            e2 = blk_ref[pl.ds(k0, SCORE_TILE), :]
            for g in range(B_KV_HEADS):
                k = jnp.where(halves[g], k2, e2)
                s = lax.dot_general(qaug[g], k, NT_DIMS, preferred_element_type=F32)
                for u in range(n_sub):
                    rows_u = slice(u * sub_rows, (u + 1) * sub_rows)
                    for c in range(SCORE_TILE // SUB_TILE):
                        bi = jnp.clip(qt * n_sub + u - k0 // SUB_TILE - c + 1, 0, n_bias - 1)
                        col = jt * SCORE_TILE + c * SUB_TILE
                        s_ref[kt % 2, g, rows_u, col:col + SUB_TILE] = (
                            s[rows_u, c * SUB_TILE:(c + 1) * SUB_TILE] + bias_s_ref[g, bi])

    def accumulate(kt):
        k0 = pl.multiple_of(kt * K_TILE, K_TILE)
        v2 = vs_ref[0, pl.ds(k0, K_TILE), :]
        for g in range(B_KV_HEADS):
            v = jnp.where(halves[g], v2, jnp.ones((K_TILE, LANES), BF16))
            s = s_ref[kt % 2, g]
            m_old = m_ref[g]
            m_new = jnp.maximum(m_old, jnp.max(s, axis=-1, keepdims=True))
            alpha = jnp.exp2(m_old - m_new)
            p = jnp.exp2(s - jnp.tile(m_new, (1, K_TILE // LANES)))
            acc_ref[g] = alpha * acc_ref[g] + jnp.dot(p.astype(BF16), v, preferred_element_type=F32)
            m_ref[g] = m_new

    def scores_own(kt):
        per = SCORE_TILE // SUB_TILE
        for jt in range(K_TILE // SCORE_TILE):
            k0 = pl.multiple_of(kt * K_TILE + jt * SCORE_TILE, SCORE_TILE)
            k2 = ks_ref[0, pl.ds(k0, SCORE_TILE), :]
            e2 = blk_ref[pl.ds(k0, SCORE_TILE), :]
            u0 = jt * per
            for g in range(B_KV_HEADS):
                k = jnp.where(halves[g], k2, e2)
                s = lax.dot_general(qaug[g][u0 * sub_rows:], k, NT_DIMS, preferred_element_type=F32)
                for u in range(u0, n_sub):
                    rows_u = slice(u * sub_rows, (u + 1) * sub_rows)
                    for c in range(per):
                        key_sub = u0 + c
                        if key_sub <= u:
                            col = key_sub * SUB_TILE
                            s_ref[kt % 2, g, rows_u, col:col + SUB_TILE] = (
                                s[(u - u0) * sub_rows:(u - u0 + 1) * sub_rows, c * SUB_TILE:(c + 1) * SUB_TILE]
                                + bias_s_ref[g, u - key_sub + 1])

    def accumulate_own(kt):
        k0 = pl.multiple_of(kt * K_TILE, K_TILE)
        v2 = vs_ref[0, pl.ds(k0, K_TILE), :]
        for g in range(B_KV_HEADS):
            v = jnp.where(halves[g], v2, jnp.ones((K_TILE, LANES), BF16))
            for u in range(n_sub):
                rows_u = slice(u * sub_rows, (u + 1) * sub_rows)
                width = (u + 1) * SUB_TILE
                s = s_ref[kt % 2, g, rows_u, :width]
                m_old = m_ref[g, rows_u, :]
                m_new = jnp.maximum(m_old, jnp.max(s, axis=-1, keepdims=True))
                p = jnp.exp2(s - jnp.tile(m_new, (1, width // LANES)))
                acc_ref[g, rows_u, :] = (jnp.exp2(m_old - m_new) * acc_ref[g, rows_u, :]
                                         + jnp.dot(p.astype(BF16), v[:width], preferred_element_type=F32))

    def body(kt, carry):
        accumulate(kt)
        scores(kt + 1)
        return carry

    n_wt = kw // SUB_TILE
    q_win = [stacked([jnp.where(halves[g], q_slots[r], zero) for r in range(B_GROUP)]) for g in range(B_KV_HEADS)]
    o_win = [[] for _ in range(B_KV_HEADS)]
    for u in range(n_sub):
        sub = qt * n_sub + u
        w0 = pl.multiple_of(jnp.maximum(sub * SUB_TILE - WIN_LEN, 0), SUB_TILE)
        skip = jnp.maximum(WIN_LEN // SUB_TILE - sub, 0)
        kwin = kw_ref[0, pl.ds(w0, kw), :]
        vwin = vw_ref[0, pl.ds(w0, kw), :]
        q_u = jnp.concatenate([q_win[g][u * sub_rows:(u + 1) * sub_rows] for g in range(B_KV_HEADS)], axis=0)
        s_u = lax.dot_general(q_u, kwin, NT_DIMS, preferred_element_type=F32)
        for g in range(B_KV_HEADS):
            s = s_u[g * sub_rows:(g + 1) * sub_rows]
            s = jnp.concatenate([s[:, jw * SUB_TILE:(jw + 1) * SUB_TILE] + bias_w_ref[g, jnp.minimum(jw + skip, n_wt)]
                                 for jw in range(n_wt)], axis=1)
            p = jnp.exp2(s - jnp.max(s, axis=-1, keepdims=True))
            acc = jnp.dot(p.astype(BF16), jnp.where(halves[g], vwin, jnp.ones_like(vwin)),
                          preferred_element_type=F32)
            o_win[g].append(acc)
    win_slots = normalised_slots([jnp.concatenate(o_win[g], axis=0) for g in range(B_KV_HEADS)], 2)
    for r in range(B_GROUP):
        part_ref[1, :, r * LANES:(r + 1) * LANES] = win_slots[r]

    scores(0)

    @pl.when(n_steps > 1)
    def _():
        lax.fori_loop(0, n_steps - 2, body, 0)
        accumulate(n_steps - 2)
        scores_own(n_steps - 1)

    accumulate_own(n_steps - 1)
    sel_slots = normalised_slots([acc_ref[g] for g in range(B_KV_HEADS)], 1)
    for r in range(B_GROUP):
        sl = slice(r * LANES, (r + 1) * LANES)
        o_ref[0, :, sl] = ((part_ref[0, :, sl] + sel_slots[r]) + part_ref[1, :, sl]).astype(o_ref.dtype)


def _nsa(qb, gl, kcmp, vcmp, ksl, vsl, bias_s, kwin, vwin, bias_w):
    b, s, w = qb.shape
    assert Q_TILE == K_TILE and bias_s.shape[1] > Q_TILE // SUB_TILE
    nsel = s // SEL_BLOCK
    nck = kcmp.shape[1]
    ratio = SEL_BLOCK // CMP_STRIDE
    wov = np.zeros((nsel, nck), np.float32)
    for jj in range(nsel):
        for off, wt in zip(range(-1, ratio), CMP_OVERLAP):
            n = ratio * jj + off
            if 0 <= n < nck - 1:
                wov[jj, n] = wt
    wov = jnp.asarray(wov)
    blk = jnp.asarray((np.arange(s)[:, None] // SEL_BLOCK) == (np.arange(LANES)[None, :] % HEAD_DIM), BF16)
    rows = B_GROUP * Q_TILE
    tile = lambda width: pl.BlockSpec((1, Q_TILE, width), lambda bi, qi: (bi, qi, 0))
    whole = lambda a: pl.BlockSpec((1,) + a.shape[1:], lambda bi, qi: (bi, 0, 0))
    slot_of = lambda v: pl.BlockSpec((1, s, LANES), lambda bi, qi, slot=v[1]: (bi, 0, slot))
    const = lambda a: pl.BlockSpec(a.shape, lambda bi, qi: (0,) * a.ndim, pipeline_mode=pl.Buffered(1))
    return pl.pallas_call(
        functools.partial(_nsa_kernel, nsel=nsel, n_bias=bias_s.shape[1], kw=WIN_LEN + SUB_TILE),
        grid=(b, s // Q_TILE),
        in_specs=[tile(w), tile(LANES), whole(kcmp), whole(vcmp), const(wov), slot_of(ksl), const(blk), slot_of(vsl),
                  const(bias_s), slot_of(kwin), slot_of(vwin), const(bias_w)],
        out_specs=tile(w),
        out_shape=jax.ShapeDtypeStruct((b, s, w), BF16),
        scratch_shapes=[pltpu.VMEM((B_KV_HEADS, rows, LANES), F32), pltpu.VMEM((B_KV_HEADS, rows, LANES), F32),
                        pltpu.VMEM((2, B_KV_HEADS, rows, K_TILE), F32), pltpu.VMEM((2, Q_TILE, w), F32),
                        pltpu.VMEM((Q_TILE, LANES), BF16)],
        compiler_params=_cparams("parallel", "parallel"),
        name="nsa",
    )(qb, gl, kcmp, vcmp, wov, ksl[0], blk, vsl[0], bias_s, kwin[0], vwin[0], bias_w)


def _tail_kernel(x_ref, oa_ref, ob_ref, wa_ref, wb_ref, g2_ref, wg_ref, wu_ref, wd_ref, gf_ref, o_ref):
    o_a = jnp.concatenate([oa_ref[p] for p in range(oa_ref.shape[0])], axis=1)
    mix = jnp.dot(o_a, wa_ref[...], preferred_element_type=F32)
    mix = mix + jnp.dot(ob_ref[...], wb_ref[...], preferred_element_type=F32)
    h = x_ref[...] + mix
    y = h * lax.rsqrt(jnp.mean(h * h, axis=-1, keepdims=True) + RMS_EPS)
    hn = (y * g2_ref[...]).astype(BF16)
    a = jnp.dot(hn, wg_ref[...], preferred_element_type=F32)
    u = jnp.dot(hn, wu_ref[...], preferred_element_type=F32)
    act = (jax.nn.silu(a) * u).astype(BF16)
    h2 = h + jnp.dot(act, wd_ref[...], preferred_element_type=F32)
    y2 = h2 * lax.rsqrt(jnp.mean(h2 * h2, axis=-1, keepdims=True) + RMS_EPS)
    o_ref[...] = y2 * gf_ref[...]


def _tail(x2, oa, ob, wa, wb, g2, wg, wu, wd, gf, tm):
    n, d = x2.shape
    row = lambda width: pl.BlockSpec((tm, width), lambda i: (i, 0))
    const = lambda a: pl.BlockSpec(a.shape, lambda i: (0, 0), pipeline_mode=pl.Buffered(1))
    return pl.pallas_call(
        _tail_kernel,
        grid=(n // tm,),
        in_specs=[row(d), pl.BlockSpec((oa.shape[0], tm, LANES), lambda i: (0, i, 0)), row(ob.shape[1]),
                  const(wa), const(wb), const(g2), const(wg), const(wu), const(wd), const(gf)],
        out_specs=row(d),
        out_shape=jax.ShapeDtypeStruct((n, d), F32),
        compiler_params=_cparams("parallel"),
        name="tail",
    )(x2, oa, ob, wa, wb, g2, wg, wu, wd, gf)


def _slot_perm():
    perm = np.zeros(B_WIDTH, np.int64)
    for r in range(B_GROUP):
        for g in range(B_KV_HEADS):
            src = (g * B_GROUP + r) * HEAD_DIM
            dst = r * LANES + g * HEAD_DIM
            perm[dst:dst + HEAD_DIM] = np.arange(src, src + HEAD_DIM)
    return perm


def _layer(h, tab_a, tab_b, norm1_g, w_in, cmp_pos, k_w1, k_b1, k_w2, v_w1, v_b1, v_w2,
           w_out, norm2_g, w_gate, w_up, w_down, norm_f_g):
    b, s, d = h.shape
    n = b * s
    assert s % SPAN == 0 and s % K_TILE == 0
    perm = _slot_perm()

    cols = np.cumsum([0, A_WIDTH, A_WIDTH, A_WIDTH, B_WIDTH] + [KV_WIDTH] * 6 + [GATE_WIDTH])
    w_bf = w_in.astype(BF16)
    w_gl = jnp.pad(w_bf[:, cols[10]:cols[11]], ((0, 0), (0, LANES - GATE_WIDTH)))
    w1 = jnp.concatenate([w_bf[:, :cols[3]], w_bf[:, cols[3]:cols[4]][:, perm], w_bf[:, cols[4]:cols[10]], w_gl], axis=1)
    widths = (A_WIDTH,) * 3 + (B_WIDTH, 2 * KV_WIDTH, 4 * KV_WIDTH, LANES)
    dtypes = (BF16,) * 4 + (F32, BF16, F32)
    x2 = h.reshape(n, d)
    n_lay = len(MIX_LAYOUTS)
    outs = _inproj(h, norm1_g.reshape(1, d), w1, widths, dtypes, n_mix=3, tm=512)
    qa, ka, va = (outs[i * n_lay:(i + 1) * n_lay] for i in range(3))
    r3 = lambda t: t.reshape(b, s, t.shape[-1])
    qb, kv_cmp, kv_rest, gl = map(r3, outs[3 * n_lay:])
    kc, vc = (kv_cmp, 0), (kv_cmp, 1)
    ksl, vsl, kw, vw = ((kv_rest, slot) for slot in range(4))

    tok_dist, in_window = [], []
    for window, dil in DIL_PATTERNS:
        fan = (dil if dil in MIX_LAYOUTS else MIX_LAYOUTS[0]) // dil
        row = np.arange(DIL_BLOCK)
        step = fan * (row % (DIL_BLOCK // fan)) + row // (DIL_BLOCK // fan)
        key_step = np.concatenate([step, DIL_BLOCK + step])
        dist = DIL_BLOCK + step[:, None] - key_step[None, :]
        tok_dist.append(np.maximum(dist, 0) * dil)
        in_window.append((dist >= 0) & (dist <= window // dil))
    tok_dist, in_window = np.stack(tok_dist), np.stack(in_window)
    bias_a = _bias_tiles(tab_a, tok_dist, in_window, inner=A_HEADS)[0]
    o_a = _mixer_a(qa, ka, va, bias_a).reshape(A_HEADS // 2, n, LANES)

    nch = s // CMP_STRIDE
    kcmp, vcmp = _compress([kc, vc], cmp_pos, [(k_w1, k_b1, k_w2), (v_w1, v_b1, v_w2)])
    d_sat = int(np.nonzero(_t5_bucket_np(np.arange(s)) < REL_BUCKETS - 1)[0].max()) + 1
    first_far = -(-(d_sat + SUB_TILE - 1) // SUB_TILE)
    n_bias = min(s // SUB_TILE, first_far + 1) + 1
    dd = ((np.arange(n_bias)[:, None, None] - 1) * SUB_TILE + np.arange(SUB_TILE)[None, :, None]
          - np.arange(SUB_TILE)[None, None, :])
    bias_sl = _bias_tiles(tab_b, dd, dd >= 0, inner=B_GROUP)
    bias_sl = bias_sl.reshape(B_KV_HEADS, n_bias, B_GROUP * SUB_TILE, SUB_TILE)
    n_wt = WIN_LEN // SUB_TILE + 1
    dw = ((n_wt - 1 - np.arange(n_wt + 1))[:, None, None] * SUB_TILE + np.arange(SUB_TILE)[None, :, None]
          - np.arange(SUB_TILE)[None, None, :])
    in_win = (dw >= 0) & (dw < WIN_LEN) & (np.arange(n_wt + 1) < n_wt)[:, None, None]
    bias_w = _bias_tiles(tab_b, dw, in_win, inner=B_GROUP)
    bias_w = bias_w.reshape(B_KV_HEADS, n_wt + 1, B_GROUP * SUB_TILE, SUB_TILE)
    o_b = _nsa(qb, gl, kcmp, vcmp, ksl, vsl, bias_sl, kw, vw, bias_w)

    wa = w_out[:A_WIDTH].astype(BF16)
    wb = w_out[A_WIDTH:][perm].astype(BF16)
    out = _tail(x2, o_a, o_b.reshape(n, B_WIDTH), wa, wb, norm2_g.reshape(1, d), w_gate.astype(BF16),
                w_up.astype(BF16), w_down.astype(BF16), norm_f_g.reshape(1, d), tm=512)
    return out.reshape(b, s, d)


def kernel(x, norm1_g, w_in, rel_bias, cmp_pos, cmp_k_w1, cmp_k_b1, cmp_k_w2, cmp_v_w1, cmp_v_b1, cmp_v_w2,
           w_out, norm2_g, w_gate, w_up, w_down, norm_f_g):
    assert w_in.shape[0] == 1, "single-layer model"
    tab_a = rel_bias[:, :A_HEADS].T * LOG2E
    tab_b = rel_bias[:, A_HEADS:].T * LOG2E
    return _layer(x, tab_a, tab_b, norm1_g[0], w_in[0], cmp_pos[0], cmp_k_w1[0], cmp_k_b1[0], cmp_k_w2[0],
                  cmp_v_w1[0], cmp_v_b1[0], cmp_v_w2[0], w_out[0], norm2_g[0], w_gate[0], w_up[0], w_down[0],
                  norm_f_g)
```

```python
import functools
import math

import numpy as np
import jax
import jax.numpy as jnp
from jax import lax
from jax.experimental import pallas as pl
from jax.experimental.pallas import tpu as pltpu

F32 = jnp.float32
BF16 = jnp.bfloat16

HEAD_DIM = 64
LANES = 128
A_HEADS = 8
DIL_PATTERNS = ((128, 1), (512, 4), (2048, 16))
DIL_BLOCK = 128
B_HEADS = 8
B_KV_HEADS = 2
B_GROUP = B_HEADS // B_KV_HEADS
CMP_LEN = 32
CMP_STRIDE = 16
CMP_HIDDEN = 128
SEL_BLOCK = 64
SEL_TOPN = 16
CMP_OVERLAP = (1.0, 2.0, 2.0, 2.0, 1.0)
WIN_LEN = 512
N_BRANCH = 3
REL_BUCKETS = 32
REL_MAX_DIST = 2048
RMS_EPS = 1e-6
A_WIDTH = A_HEADS * HEAD_DIM
B_WIDTH = B_HEADS * HEAD_DIM
KV_WIDTH = B_KV_HEADS * HEAD_DIM
GATE_WIDTH = B_HEADS * N_BRANCH
SCALE = HEAD_DIM ** -0.5
LOG2E = math.log2(math.e)

SPAN = DIL_PATTERNS[-1][1] * DIL_BLOCK
MIX_LAYOUTS = tuple(d for _, d in DIL_PATTERNS if d > 1)
NARROW_FROM = 4
RANK_CHUNK = 8
SUB_TILE = 128
Q_TILE = 512
K_TILE = 512
SCORE_TILE = 256
NOT_SELECTED = -(2.0 ** 100)
M_INIT = -1e30
VMEM_LIMIT = 56 * 1024 * 1024

NT_DIMS = (((1,), (1,)), ((), ()))


def _cparams(*sem):
    return pltpu.CompilerParams(dimension_semantics=sem, vmem_limit_bytes=VMEM_LIMIT)


def _t5_bucket_np(dist):
    max_exact = REL_BUCKETS // 2
    d = np.asarray(dist)
    df = np.maximum(d, 1).astype(np.float32)
    large = max_exact + (np.log(df / np.float32(max_exact)) / np.float32(math.log(REL_MAX_DIST / max_exact))
                         * np.float32(REL_BUCKETS - max_exact)).astype(np.int32)
    large = np.minimum(large, REL_BUCKETS - 1)
    return np.where(d < max_exact, d, large).astype(np.int32)


def _bias_kernel(tab_ref, idx_ref, o_ref, *, buckets):
    inner = o_ref.shape[2]
    for t, present in enumerate(buckets):
        idx = idx_ref[t]
        accs = [jnp.full(idx.shape, -jnp.inf, F32) for _ in range(inner)]
        for bucket in present:
            hit = idx == bucket
            for hi in range(inner):
                head = pl.program_id(0) * inner + hi
                accs[hi] = jnp.where(hit, tab_ref[head * REL_BUCKETS + bucket], accs[hi])
        for hi in range(inner):
            o_ref[0, t, hi] = accs[hi]


def _bias_tiles(tab, dist, valid, inner):
    h = tab.shape[0]
    t, r, c = dist.shape
    idx = np.where(valid, _t5_bucket_np(np.maximum(dist, 0)), -1).astype(np.int32)
    buckets = tuple(tuple(int(v) for v in np.unique(idx[ti]) if v >= 0) for ti in range(t))
    return pl.pallas_call(
        functools.partial(_bias_kernel, buckets=buckets),
        grid=(h // inner,),
        in_specs=[pl.BlockSpec(memory_space=pltpu.SMEM),
                  pl.BlockSpec((t, r, c), lambda a: (0, 0, 0))],
        out_specs=pl.BlockSpec((1, t, inner, r, c), lambda a: (a, 0, 0, 0, 0)),
        out_shape=jax.ShapeDtypeStruct((h // inner, t, inner, r, c), F32),
        compiler_params=_cparams("parallel"),
        name="bias_tiles",
    )(tab.reshape(-1), jnp.asarray(idx))


def _half_mask(g):
    lane = lax.broadcasted_iota(jnp.int32, (1, LANES), 1)
    return (lane >= HEAD_DIM) if g else (lane < HEAD_DIM)


def _inproj_kernel(x_ref, g_ref, w_ref, *refs, widths, n_mix):
    n_dil = len(DIL_PATTERNS)
    n_lay = len(MIX_LAYOUTS)
    n_scr = n_mix * (n_dil - 1)
    mix_refs = [(None,) * (n_dil - n_lay) + tuple(refs[i * n_lay:(i + 1) * n_lay]) for i in range(n_mix)]
    out_refs = refs[n_mix * n_lay:len(refs) - n_scr]
    scratch = [refs[len(refs) - n_scr + i * (n_dil - 1):len(refs) - n_scr + (i + 1) * (n_dil - 1)]
               for i in range(n_mix)]
    x = x_ref[...]
    tm = x.shape[0]
    y = x * lax.rsqrt(jnp.mean(x * x, axis=-1, keepdims=True) + RMS_EPS)
    xn = (y * g_ref[...]).astype(BF16)
    merged = jnp.dot(xn, w_ref[:, sum(widths[:NARROW_FROM]):], preferred_element_type=F32)
    start = 0
    for idx, w in enumerate(widths):
        if idx < NARROW_FROM:
            r = jnp.dot(xn, w_ref[:, start:start + w], preferred_element_type=F32)
        else:
            off = start - sum(widths[:NARROW_FROM])
            r = merged[:, off:off + w]
        if idx < n_mix:
            prev_dil = None
            for pi, (o_ref, (_, dil)) in enumerate(zip(mix_refs[idx], DIL_PATTERNS)):
                keep = scratch[idx][pi] if pi < n_dil - 1 else None
                for p in range(w // LANES):
                    if prev_dil is None:
                        planes = [(0, r[:, p * LANES:(p + 1) * LANES])]
                    else:
                        f = dil // prev_dil
                        planes = [(res + prev_dil * a,
                                   scratch[idx][pi - 1][p, res, pl.ds(a, tm // dil, stride=f), :])
                                  for res in range(prev_dil) for a in range(f)]
                    for res, rows in planes:
                        if o_ref is not None:
                            o_ref[p, 0, 0, res] = rows.astype(o_ref.dtype)
                        if keep is not None:
                            keep[p, res] = rows
                prev_dil = dil
        else:
            o_ref = out_refs[idx - n_mix]
            o_ref[...] = r.astype(o_ref.dtype)
        start += w


def _inproj(x3, g, w, widths, dtypes, n_mix, tm):
    b, s, d = x3.shape
    n = b * s
    per_batch = s // tm
    out_shape, out_specs = [], []
    for w_, dt in zip(widths[:n_mix], dtypes[:n_mix]):
        for dil in MIX_LAYOUTS:
            out_shape.append(jax.ShapeDtypeStruct((w_ // LANES, b, per_batch, dil, tm // dil, LANES), dt))
            out_specs.append(pl.BlockSpec((w_ // LANES, 1, 1, dil, tm // dil, LANES),
                                          lambda i: (0, i // per_batch, i % per_batch, 0, 0, 0)))
    for w_, dt in zip(widths[n_mix:], dtypes[n_mix:]):
        out_shape.append(jax.ShapeDtypeStruct((n, w_), dt))
        out_specs.append(pl.BlockSpec((tm, w_), lambda i: (i, 0)))
    return pl.pallas_call(
        functools.partial(_inproj_kernel, widths=widths, n_mix=n_mix),
        grid=(n // tm,),
        in_specs=[pl.BlockSpec((tm, d), lambda i: (i, 0)),
                  pl.BlockSpec((1, d), lambda i: (0, 0)),
                  pl.BlockSpec(w.shape, lambda i: (0, 0))],
        out_specs=out_specs,
        out_shape=out_shape,
        scratch_shapes=[pltpu.VMEM((w_ // LANES, dil, tm // dil, LANES), F32)
                        for w_ in widths[:n_mix] for _, dil in DIL_PATTERNS[:-1]],
        compiler_params=_cparams("parallel"),
        name="inproj",
    )(x3.reshape(n, d), g, w)


def _mixer_a_kernel(*refs):
    bias_ref, o_ref, acc_scr, m_scr = refs[5 * len(MIX_LAYOUTS):]
    sb = pl.program_id(2)
    halves = [_half_mask(hh) for hh in range(2)]
    col = lax.broadcasted_iota(jnp.int32, (1, 2 * DIL_BLOCK), 1)
    first_keep = (col >= DIL_BLOCK) | (sb > 0)
    ones = jnp.ones((2 * DIL_BLOCK, LANES), BF16)
    n_blocks = SPAN // DIL_BLOCK
    for p, (_, dil) in enumerate(DIL_PATTERNS):
        lay = MIX_LAYOUTS.index(dil) if dil in MIX_LAYOUTS else 0
        src = MIX_LAYOUTS[lay]
        fan = src // dil
        piece = DIL_BLOCK // fan
        q_ref, kp_ref, kc_ref, vp_ref, vc_ref = refs[5 * lay:5 * lay + 5]
        last = SPAN // src // piece - 1

        def gather(ref, r, nn, dil=dil, fan=fan, piece=piece):
            per_tile = ref.shape[4]
            chunk = min(piece, per_tile)
            return jnp.concatenate(
                [ref[0, 0, lo // per_tile, r + dil * a, lo % per_tile:lo % per_tile + chunk, :]
                 for a in range(fan) for lo in range(nn * piece, (nn + 1) * piece, chunk)], axis=0)

        for r in range(dil):
            for n in range(SPAN // (dil * DIL_BLOCK)):
                q2 = gather(q_ref, r, n)
                if n == 0:
                    k_prev, v_prev = gather(kp_ref, r, last), gather(vp_ref, r, last)
                else:
                    k_prev, v_prev = gather(kc_ref, r, n - 1), gather(vc_ref, r, n - 1)
                kcat = jnp.concatenate([k_prev, gather(kc_ref, r, n)], axis=0)
                vcat = jnp.concatenate([v_prev, gather(vc_ref, r, n)], axis=0)
                qm = jnp.concatenate([jnp.where(halves[hh], q2, jnp.zeros_like(q2)) for hh in range(2)], axis=0)
                s2 = lax.dot_general(qm, kcat, NT_DIMS, preferred_element_type=F32)
                for hh in range(2):
                    s = s2[hh * DIL_BLOCK:(hh + 1) * DIL_BLOCK] + bias_ref[p, hh]
                    if n == 0:
                        s = jnp.where(first_keep, s, -jnp.inf)
                    m_blk = jnp.max(s, axis=-1, keepdims=True)
                    pe = jnp.exp2(s - m_blk)
                    pv = jnp.dot(pe.astype(BF16), jnp.where(halves[hh], vcat, ones), preferred_element_type=F32)
                    m_b = jnp.broadcast_to(m_blk, pv.shape)
                    for a in range(fan):
                        rows_t = pl.ds(n * dil * DIL_BLOCK + r + dil * a, piece, stride=src)
                        acc_scr[p, hh, rows_t, :] = pv[a * piece:(a + 1) * piece]
                        m_scr[p, hh, rows_t, :] = m_b[a * piece:(a + 1) * piece]

    def finish(c, carry):
        rows = pl.ds(pl.multiple_of(c * DIL_BLOCK, DIL_BLOCK), DIL_BLOCK)
        tots = []
        for hh in range(2):
            ms = [m_scr[p, hh, rows, :] for p in range(len(DIL_PATTERNS))]
            m_all = functools.reduce(jnp.maximum, ms)
            tots.append(sum(jnp.exp2(m - m_all) * acc_scr[p, hh, rows, :] for p, m in enumerate(ms)))
        num = jnp.where(halves[0], tots[0], tots[1])
        den = pltpu.roll(jnp.where(halves[0], tots[1], tots[0]), HEAD_DIM, axis=1)
        o_ref[0, 0, rows, :] = (num / den).astype(o_ref.dtype)
        return carry

    lax.fori_loop(0, n_blocks, finish, 0, unroll=4)


def _mixer_a(qs, ks, vs, bias):
    npair, b, n_tiles, dil0, rows0, _ = qs[0].shape
    tile = dil0 * rows0
    s = n_tiles * tile
    cur = lambda bi, pi, si: (pi, bi, si, 0, 0, 0)
    prev = lambda bi, pi, si: (pi, bi, jnp.maximum(si - 1, 0), 0, 0, 0)
    in_specs, operands = [], []
    for dil, q, k, v in zip(MIX_LAYOUTS, qs, ks, vs):
        blk = (1, 1, SPAN // tile, dil, tile // dil, LANES)
        in_specs += [pl.BlockSpec(blk, cur), pl.BlockSpec(blk, prev), pl.BlockSpec(blk, cur),
                     pl.BlockSpec(blk, prev), pl.BlockSpec(blk, cur)]
        operands += [q, k, k, v, v]
    in_specs.append(pl.BlockSpec((bias.shape[0], 2) + bias.shape[2:], lambda bi, pi, si: (0, pi, 0, 0)))
    return pl.pallas_call(
        _mixer_a_kernel,
        grid=(b, npair, s // SPAN),
        in_specs=in_specs,
        out_specs=pl.BlockSpec((1, 1, SPAN, LANES), lambda bi, pi, si: (pi, bi, si, 0)),
        out_shape=jax.ShapeDtypeStruct((npair, b, s, LANES), BF16),
        scratch_shapes=[pltpu.VMEM((len(DIL_PATTERNS), 2, SPAN, LANES), F32),
                        pltpu.VMEM((len(DIL_PATTERNS), 2, SPAN, LANES), F32)],
        compiler_params=_cparams("parallel", "parallel", "parallel"),
        name="mixer_a",
    )(*operands, bias)


def _compress_kernel(pa_ref, pb_ref, *refs):
    n_streams = len(refs) // 6
    for i in range(n_streams):
        c_ref, w1a_ref, w1b_ref, b1_ref, w2_ref = refs[5 * i:5 * i + 5]
        o_ref = refs[5 * n_streams + i]
        nch = o_ref.shape[1]
        c = jnp.concatenate([c_ref[0, pl.ds(l, nch, stride=CMP_STRIDE), :] for l in range(CMP_STRIDE)], axis=1)
        xa = (c + pa_ref[...]).astype(BF16)
        xb = (c + pb_ref[...]).astype(BF16)
        ha = jnp.dot(xa, w1a_ref[...], preferred_element_type=F32)
        hb = jnp.dot(xb, w1b_ref[...], preferred_element_type=F32)
        hb_next = jnp.concatenate([hb[1:], jnp.zeros_like(hb[:1])], axis=0)
        hid = jax.nn.gelu(ha + hb_next + b1_ref[...])
        o_ref[0] = jnp.dot(hid.astype(BF16), w2_ref[...], preferred_element_type=F32).astype(o_ref.dtype)


def _compress(cs, pos, params):
    b, s, _ = cs[0][0].shape
    nch = s // CMP_STRIDE
    half = CMP_LEN // 2
    zero = jnp.zeros((half, HEAD_DIM, CMP_HIDDEN), F32)

    def grouped(wpart):
        g0 = jnp.concatenate([wpart, zero], axis=1).reshape(half * LANES, CMP_HIDDEN)
        g1 = jnp.concatenate([zero, wpart], axis=1).reshape(half * LANES, CMP_HIDDEN)
        return jnp.concatenate([g0, g1], axis=1).astype(BF16)

    pa = jnp.tile(pos[:half], (1, 2)).reshape(1, half * LANES)
    pb = jnp.tile(pos[half:], (1, 2)).reshape(1, half * LANES)
    full = lambda a: pl.BlockSpec(a.shape, lambda i: (0,) * a.ndim)
    operands, in_specs = [pa, pb], [full(pa), full(pb)]
    for (c, slot), (w1, b1, w2) in zip(cs, params):
        zw = jnp.zeros_like(w2)
        w2g = jnp.concatenate([jnp.concatenate([w2, zw], axis=1),
                               jnp.concatenate([zw, w2], axis=1)], axis=0).astype(BF16)
        weights = [grouped(w1[:half]), grouped(w1[half:]), jnp.tile(b1, 2).reshape(1, 2 * CMP_HIDDEN), w2g]
        operands += [c] + weights
        in_specs += [pl.BlockSpec((1, s, LANES), lambda i, slot=slot: (i, 0, slot))] + [full(w_) for w_ in weights]
    return pl.pallas_call(
        _compress_kernel,
        grid=(b,),
        in_specs=in_specs,
        out_specs=[pl.BlockSpec((1, nch, LANES), lambda i: (i, 0, 0))] * len(cs),
        out_shape=[jax.ShapeDtypeStruct((b, nch, LANES), BF16)] * len(cs),
        compiler_params=_cparams("parallel"),
        name="compress",
    )(*operands)


def _nsa_kernel(q_ref, gl_ref, kc_ref, vc_ref, wov_ref, ks_ref, blk_ref, vs_ref, bias_s_ref, kw_ref, vw_ref, bias_w_ref,
                o_ref, acc_ref, m_ref, s_ref, part_ref, sb_ref, imp_ref, *, nsel, n_bias, kw):
    qt = pl.program_id(1)
    tq = q_ref.shape[1]
    n_sub = tq // SUB_TILE
    sub_rows = B_GROUP * SUB_TILE
    gate = jax.nn.sigmoid(gl_ref[0])
    halves = [_half_mask(g) for g in range(B_KV_HEADS)]
    q_slots = [q_ref[0, :, r * LANES:(r + 1) * LANES] for r in range(B_GROUP)]
    zero = jnp.zeros((tq, LANES), BF16)

    def stacked(slots):
        return jnp.concatenate([slots[r][u * SUB_TILE:(u + 1) * SUB_TILE]
                                for u in range(n_sub) for r in range(B_GROUP)], axis=0)

    def head_rows(o, r):
        return jnp.concatenate([o[(u * B_GROUP + r) * SUB_TILE:(u * B_GROUP + r + 1) * SUB_TILE]
                                for u in range(n_sub)], axis=0)

    def gate_tile(r, branch):
        c0, c1 = (r * N_BRANCH + branch, (B_GROUP + r) * N_BRANCH + branch)
        return jnp.where(halves[0], gate[:, c0:c0 + 1], gate[:, c1:c1 + 1])

    def gated_slots(o, g, branch, slots):
        for r in range(B_GROUP):
            c = (g * B_GROUP + r) * N_BRANCH + branch
            o_r = head_rows(o, r) * gate[:, c:c + 1]
            slots[r] = o_r if slots[r] is None else jnp.where(halves[g], o_r, slots[r])

    def normalised_slots(accs, branch):
        slots = []
        for r in range(B_GROUP):
            a0, a1 = head_rows(accs[0], r), head_rows(accs[1], r)
            num = jnp.where(halves[0], a0, a1)
            den = pltpu.roll(jnp.where(halves[0], a1, a0), HEAD_DIM, axis=1)
            slots.append(num / jnp.maximum(den, 1e-30) * gate_tile(r, branch))
        return slots

    nck = kc_ref.shape[1]
    row = lax.broadcasted_iota(jnp.int32, (n_sub * sub_rows, 1), 0)
    t_row = qt * tq + (row // sub_rows) * SUB_TILE + row % SUB_TILE
    t_lane = qt * tq + lax.broadcasted_iota(jnp.int32, (1, tq), 1)
    cur = t_lane // SEL_BLOCK
    j = lax.broadcasted_iota(jnp.int32, (nsel, 1), 0)
    forced = (j == 0) | (j == cur) | (j == cur - 1)
    n_rows = n_sub * sub_rows

    def compressed(width):
        blk_end = lax.broadcasted_iota(jnp.int32, (1, width), 1) * CMP_STRIDE + (CMP_LEN - 1)
        valid = blk_end <= t_row
        kc = kc_ref[0, :width, :]
        vc = vc_ref[0, :width, :]
        q_all = jnp.concatenate([stacked([jnp.where(halves[g], q_slots[r], zero) for r in range(B_GROUP)])
                                 for g in range(B_KV_HEADS)], axis=0)
        s = lax.dot_general(q_all, kc, NT_DIMS, preferred_element_type=F32)
        s = jnp.where(jnp.concatenate([valid] * B_KV_HEADS, axis=0), s, -jnp.inf)
        m = jnp.max(s, axis=-1, keepdims=True)
        m = jnp.where(m == -jnp.inf, 0.0, m)
        p = jnp.exp2(s - m)
        den = jnp.sum(p, axis=-1, keepdims=True)
        p = p / jnp.maximum(den, 1e-30)
        o_all = jnp.dot(p.astype(BF16), vc, preferred_element_type=F32)
        cmp_slots = [None] * B_GROUP
        for g in range(B_KV_HEADS):
            pg = p[g * n_rows:(g + 1) * n_rows]
            pc = jnp.concatenate(
                [sum(pg[(u * B_GROUP + r) * SUB_TILE:(u * B_GROUP + r + 1) * SUB_TILE] for r in range(B_GROUP))
                 for u in range(n_sub)], axis=0)
            gated_slots(o_all[g * n_rows:(g + 1) * n_rows], g, 0, cmp_slots)
            imp = lax.dot_general(wov_ref[:, :width], pc, NT_DIMS, preferred_element_type=F32,
                                  precision=lax.Precision.HIGHEST)
            imp_ref[B_KV_HEADS - 1 - g] = jnp.where(j > cur, -jnp.inf, jnp.where(forced, jnp.inf, imp))
        for r in range(B_GROUP):
            part_ref[0, :, r * LANES:(r + 1) * LANES] = cmp_slots[r]

    live_keys = (qt * tq + tq - CMP_LEN) // CMP_STRIDE + 1
    if nck > LANES:
        pl.when(live_keys <= LANES)(functools.partial(compressed, LANES))
        pl.when(live_keys > LANES)(functools.partial(compressed, nck))
    else:
        compressed(nck)


    last_block = (qt * tq + tq - 1) // SEL_BLOCK
    for level in range(-(-nsel // RANK_CHUNK)):
        n_live = min((level + 1) * RANK_CHUNK, nsel)

        @pl.when(last_block // RANK_CHUNK == level)
        def _(n_live=n_live):
            blocks = []
            for gi in range(B_KV_HEADS):
                if n_live <= SEL_TOPN:
                    blocks.append(jnp.zeros((HEAD_DIM, tq), F32))
                    continue
                live = imp_ref[gi, :n_live, :]
                jl = j[:n_live]
                rank = jnp.zeros((n_live, tq), jnp.int32)
                for jp in range(n_live):
                    row = live[jp:jp + 1, :]
                    rank = rank + jnp.where(jl > jp, (row >= live).astype(jnp.int32), (row > live).astype(jnp.int32))
                blocks.append(jnp.where(rank < SEL_TOPN, 0.0, NOT_SELECTED))
                if n_live < HEAD_DIM:
                    blocks.append(jnp.zeros((HEAD_DIM - n_live, tq), F32))
            sb_ref[...] = jnp.concatenate(blocks, axis=0).T.astype(sb_ref.dtype)

    sb = sb_ref[...]
    qaug = [stacked([jnp.where(halves[g], q_slots[r], sb) for r in range(B_GROUP)])
            for g in range(B_KV_HEADS)]
    n_steps = (qt * tq + tq + K_TILE - 1) // K_TILE
    acc_ref[...] = jnp.zeros_like(acc_ref)
    m_ref[...] = jnp.full_like(m_ref, M_INIT)

    def scores(kt):
        for jt in range(K_TILE // SCORE_TILE):
            k0 = pl.multiple_of(kt * K_TILE + jt * SCORE_TILE, SCORE_TILE)
            k2 = ks_ref[0, pl.ds(k0, SCORE_TILE), :]
            e2 = blk_ref[pl.ds(k0, SCORE_TILE), :]
            for g in range(B_KV_HEADS):
                k = jnp.where(halves[g], k2, e2)
                s = lax.dot_general(qaug[g], k, NT_DIMS, preferred_element_type=F32)
                for u in range(n_sub):
                    rows_u = slice(u * sub_rows, (u + 1) * sub_rows)
                    for c in range(SCORE_TILE // SUB_TILE):
                        bi = jnp.clip(qt * n_sub + u - k0 // SUB_TILE - c + 1, 0, n_bias - 1)
                        col = jt * SCORE_TILE + c * SUB_TILE
                        s_ref[kt % 2, g, rows_u, col:col + SUB_TILE] = (
                            s[rows_u, c * SUB_TILE:(c + 1) * SUB_TILE] + bias_s_ref[g, bi])

    def accumulate(kt):
        k0 = pl.multiple_of(kt * K_TILE, K_TILE)
        v2 = vs_ref[0, pl.ds(k0, K_TILE), :]
        for g in range(B_KV_HEADS):
            v = jnp.where(halves[g], v2, jnp.ones((K_TILE, LANES), BF16))
            s = s_ref[kt % 2, g]
            m_old = m_ref[g]
            m_new = jnp.maximum(m_old, jnp.max(s, axis=-1, keepdims=True))
            alpha = jnp.exp2(m_old - m_new)
            p = jnp.exp2(s - jnp.tile(m_new, (1, K_TILE // LANES)))
            acc_ref[g] = alpha * acc_ref[g] + jnp.dot(p.astype(BF16), v, preferred_element_type=F32)
            m_ref[g] = m_new

    def scores_own(kt):
        per = SCORE_TILE // SUB_TILE
        for jt in range(K_TILE // SCORE_TILE):
            k0 = pl.multiple_of(kt * K_TILE + jt * SCORE_TILE, SCORE_TILE)
            k2 = ks_ref[0, pl.ds(k0, SCORE_TILE), :]
            e2 = blk_ref[pl.ds(k0, SCORE_TILE), :]
            u0 = jt * per
            for g in range(B_KV_HEADS):
                k = jnp.where(halves[g], k2, e2)
                s = lax.dot_general(qaug[g][u0 * sub_rows:], k, NT_DIMS, preferred_element_type=F32)
                for u in range(u0, n_sub):
                    rows_u = slice(u * sub_rows, (u + 1) * sub_rows)
                    for c in range(per):
                        key_sub = u0 + c
                        if key_sub <= u:
                            col = key_sub * SUB_TILE
                            s_ref[kt % 2, g, rows_u, col:col + SUB_TILE] = (
                                s[(u - u0) * sub_rows:(u - u0 + 1) * sub_rows, c * SUB_TILE:(c + 1) * SUB_TILE]
                                + bias_s_ref[g, u - key_sub + 1])

    def accumulate_own(kt):
        k0 = pl.multiple_of(kt * K_TILE, K_TILE)
        v2 = vs_ref[0, pl.ds(k0, K_TILE), :]
        for g in range(B_KV_HEADS):
            v = jnp.where(halves[g], v2, jnp.ones((K_TILE, LANES), BF16))
            for u in range(n_sub):
                rows_u = slice(u * sub_rows, (u + 1) * sub_rows)
                width = (u + 1) * SUB_TILE
                s = s_ref[kt % 2, g, rows_u, :width]
                m_old = m_ref[g, rows_u, :]
                m_new = jnp.maximum(m_old, jnp.max(s, axis=-1, keepdims=True))
                p = jnp.exp2(s - jnp.tile(m_new, (1, width // LANES)))
                acc_ref[g, rows_u, :] = (jnp.exp2(m_old - m_new) * acc_ref[g, rows_u, :]
                                         + jnp.dot(p.astype(BF16), v[:width], preferred_element_type=F32))

    def body(kt, carry):
        accumulate(kt)
        scores(kt + 1)
        return carry

    n_wt = kw // SUB_TILE
    q_win = [stacked([jnp.where(halves[g], q_slots[r], zero) for r in range(B_GROUP)]) for g in range(B_KV_HEADS)]
    o_win = [[] for _ in range(B_KV_HEADS)]
    for u in range(n_sub):
        sub = qt * n_sub + u
        w0 = pl.multiple_of(jnp.maximum(sub * SUB_TILE - WIN_LEN, 0), SUB_TILE)
        skip = jnp.maximum(WIN_LEN // SUB_TILE - sub, 0)
        kwin = kw_ref[0, pl.ds(w0, kw), :]
        vwin = vw_ref[0, pl.ds(w0, kw), :]
        q_u = jnp.concatenate([q_win[g][u * sub_rows:(u + 1) * sub_rows] for g in range(B_KV_HEADS)], axis=0)
        s_u = lax.dot_general(q_u, kwin, NT_DIMS, preferred_element_type=F32)
        for g in range(B_KV_HEADS):
            s = s_u[g * sub_rows:(g + 1) * sub_rows]
            s = jnp.concatenate([s[:, jw * SUB_TILE:(jw + 1) * SUB_TILE] + bias_w_ref[g, jnp.minimum(jw + skip, n_wt)]
                                 for jw in range(n_wt)], axis=1)
            p = jnp.exp2(s - jnp.max(s, axis=-1, keepdims=True))
            acc = jnp.dot(p.astype(BF16), jnp.where(halves[g], vwin, jnp.ones_like(vwin)),
                          preferred_element_type=F32)
            o_win[g].append(acc)
    win_slots = normalised_slots([jnp.concatenate(o_win[g], axis=0) for g in range(B_KV_HEADS)], 2)
    for r in range(B_GROUP):
        part_ref[1, :, r * LANES:(r + 1) * LANES] = win_slots[r]

    scores(0)

    @pl.when(n_steps > 1)
    def _():
        lax.fori_loop(0, n_steps - 2, body, 0)
        accumulate(n_steps - 2)
        scores_own(n_steps - 1)

    accumulate_own(n_steps - 1)
    sel_slots = normalised_slots([acc_ref[g] for g in range(B_KV_HEADS)], 1)
    for r in range(B_GROUP):
        sl = slice(r * LANES, (r + 1) * LANES)
        o_ref[0, :, sl] = ((part_ref[0, :, sl] + sel_slots[r]) + part_ref[1, :, sl]).astype(o_ref.dtype)


def _nsa(qb, gl, kcmp, vcmp, ksl, vsl, bias_s, kwin, vwin, bias_w):
    b, s, w = qb.shape
    assert Q_TILE == K_TILE and bias_s.shape[1] > Q_TILE // SUB_TILE
    nsel = s // SEL_BLOCK
    nck = kcmp.shape[1]
    ratio = SEL_BLOCK // CMP_STRIDE
    wov = np.zeros((nsel, nck), np.float32)
    for jj in range(nsel):
        for off, wt in zip(range(-1, ratio), CMP_OVERLAP):
            n = ratio * jj + off
            if 0 <= n < nck - 1:
                wov[jj, n] = wt
    wov = jnp.asarray(wov)
    blk = jnp.asarray((np.arange(s)[:, None] // SEL_BLOCK) == (np.arange(LANES)[None, :] % HEAD_DIM), BF16)
    rows = B_GROUP * Q_TILE
    tile = lambda width: pl.BlockSpec((1, Q_TILE, width), lambda bi, qi: (bi, qi, 0))
    whole = lambda a: pl.BlockSpec((1,) + a.shape[1:], lambda bi, qi: (bi, 0, 0))
    slot_of = lambda v: pl.BlockSpec((1, s, LANES), lambda bi, qi, slot=v[1]: (bi, 0, slot))
    const = lambda a: pl.BlockSpec(a.shape, lambda bi, qi: (0,) * a.ndim, pipeline_mode=pl.Buffered(1))
    return pl.pallas_call(
        functools.partial(_nsa_kernel, nsel=nsel, n_bias=bias_s.shape[1], kw=WIN_LEN + SUB_TILE),
        grid=(b, s // Q_TILE),
        in_specs=[tile(w), tile(LANES), whole(kcmp), whole(vcmp), const(wov), slot_of(ksl), const(blk), slot_of(vsl),
                  const(bias_s), slot_of(kwin), slot_of(vwin), const(bias_w)],
        out_specs=tile(w),
        out_shape=jax.ShapeDtypeStruct((b, s, w), BF16),
        scratch_shapes=[pltpu.VMEM((B_KV_HEADS, rows, LANES), F32), pltpu.VMEM((B_KV_HEADS, rows, LANES), F32),
                        pltpu.VMEM((2, B_KV_HEADS, rows, K_TILE), F32), pltpu.VMEM((2, Q_TILE, w), F32),
                        pltpu.VMEM((Q_TILE, LANES), BF16), pltpu.VMEM((B_KV_HEADS, nsel, Q_TILE), F32)],
        compiler_params=_cparams("parallel", "parallel"),
        name="nsa",
    )(qb, gl, kcmp, vcmp, wov, ksl[0], blk, vsl[0], bias_s, kwin[0], vwin[0], bias_w)


def _tail_kernel(x_ref, oa_ref, ob_ref, wa_ref, wb_ref, g2_ref, wg_ref, wu_ref, wd_ref, gf_ref, o_ref):
    o_a = jnp.concatenate([oa_ref[p] for p in range(oa_ref.shape[0])], axis=1)
    mix = jnp.dot(o_a, wa_ref[...], preferred_element_type=F32)
    mix = mix + jnp.dot(ob_ref[...], wb_ref[...], preferred_element_type=F32)
    h = x_ref[...] + mix
    y = h * lax.rsqrt(jnp.mean(h * h, axis=-1, keepdims=True) + RMS_EPS)
    hn = (y * g2_ref[...]).astype(BF16)
    a = jnp.dot(hn, wg_ref[...], preferred_element_type=F32)
    u = jnp.dot(hn, wu_ref[...], preferred_element_type=F32)
    act = (jax.nn.silu(a) * u).astype(BF16)
    h2 = h + jnp.dot(act, wd_ref[...], preferred_element_type=F32)
    y2 = h2 * lax.rsqrt(jnp.mean(h2 * h2, axis=-1, keepdims=True) + RMS_EPS)
    o_ref[...] = y2 * gf_ref[...]


def _tail(x2, oa, ob, wa, wb, g2, wg, wu, wd, gf, tm):
    n, d = x2.shape
    row = lambda width: pl.BlockSpec((tm, width), lambda i: (i, 0))
    const = lambda a: pl.BlockSpec(a.shape, lambda i: (0, 0), pipeline_mode=pl.Buffered(1))
    return pl.pallas_call(
        _tail_kernel,
        grid=(n // tm,),
        in_specs=[row(d), pl.BlockSpec((oa.shape[0], tm, LANES), lambda i: (0, i, 0)), row(ob.shape[1]),
                  const(wa), const(wb), const(g2), const(wg), const(wu), const(wd), const(gf)],
        out_specs=row(d),
        out_shape=jax.ShapeDtypeStruct((n, d), F32),
        compiler_params=_cparams("parallel"),
        name="tail",
    )(x2, oa, ob, wa, wb, g2, wg, wu, wd, gf)


def _slot_perm():
    perm = np.zeros(B_WIDTH, np.int64)
    for r in range(B_GROUP):
        for g in range(B_KV_HEADS):
            src = (g * B_GROUP + r) * HEAD_DIM
            dst = r * LANES + g * HEAD_DIM
            perm[dst:dst + HEAD_DIM] = np.arange(src, src + HEAD_DIM)
    return perm


def _layer(h, tab_a, tab_b, norm1_g, w_in, cmp_pos, k_w1, k_b1, k_w2, v_w1, v_b1, v_w2,
           w_out, norm2_g, w_gate, w_up, w_down, norm_f_g):
    b, s, d = h.shape
    n = b * s
    assert s % SPAN == 0 and s % K_TILE == 0
    perm = _slot_perm()

    cols = np.cumsum([0, A_WIDTH, A_WIDTH, A_WIDTH, B_WIDTH] + [KV_WIDTH] * 6 + [GATE_WIDTH])
    w_aq = w_in[:, cols[0]:cols[1]] * (SCALE * LOG2E)
    w_bq = (w_in[:, cols[3]:cols[4]] * (SCALE * LOG2E))[:, perm]
    w_gl = jnp.pad(w_in[:, cols[10]:cols[11]], ((0, 0), (0, LANES - GATE_WIDTH)))
    w1 = jnp.concatenate([w_aq, w_in[:, cols[1]:cols[3]], w_bq, w_in[:, cols[4]:cols[10]], w_gl], axis=1).astype(BF16)
    widths = (A_WIDTH,) * 3 + (B_WIDTH, 2 * KV_WIDTH, 4 * KV_WIDTH, LANES)
    dtypes = (BF16,) * 4 + (F32, BF16, F32)
    x2 = h.reshape(n, d)
    n_lay = len(MIX_LAYOUTS)
    outs = _inproj(h, norm1_g.reshape(1, d), w1, widths, dtypes, n_mix=3, tm=512)
    qa, ka, va = (outs[i * n_lay:(i + 1) * n_lay] for i in range(3))
    r3 = lambda t: t.reshape(b, s, t.shape[-1])
    qb, kv_cmp, kv_rest, gl = map(r3, outs[3 * n_lay:])
    kc, vc = (kv_cmp, 0), (kv_cmp, 1)
    ksl, vsl, kw, vw = ((kv_rest, slot) for slot in range(4))

    tok_dist, in_window = [], []
    for window, dil in DIL_PATTERNS:
        fan = (dil if dil in MIX_LAYOUTS else MIX_LAYOUTS[0]) // dil
        row = np.arange(DIL_BLOCK)
        step = fan * (row % (DIL_BLOCK // fan)) + row // (DIL_BLOCK // fan)
        key_step = np.concatenate([step, DIL_BLOCK + step])
        dist = DIL_BLOCK + step[:, None] - key_step[None, :]
        tok_dist.append(np.maximum(dist, 0) * dil)
        in_window.append((dist >= 0) & (dist <= window // dil))
    tok_dist, in_window = np.stack(tok_dist), np.stack(in_window)
    bias_a = _bias_tiles(tab_a, tok_dist, in_window, inner=A_HEADS)[0]
    o_a = _mixer_a(qa, ka, va, bias_a).reshape(A_HEADS // 2, n, LANES)

    nch = s // CMP_STRIDE
    kcmp, vcmp = _compress([kc, vc], cmp_pos, [(k_w1, k_b1, k_w2), (v_w1, v_b1, v_w2)])
    d_sat = int(np.nonzero(_t5_bucket_np(np.arange(s)) < REL_BUCKETS - 1)[0].max()) + 1
    first_far = -(-(d_sat + SUB_TILE - 1) // SUB_TILE)
    n_bias = min(s // SUB_TILE, first_far + 1) + 1
    dd = ((np.arange(n_bias)[:, None, None] - 1) * SUB_TILE + np.arange(SUB_TILE)[None, :, None]
          - np.arange(SUB_TILE)[None, None, :])
    bias_sl = _bias_tiles(tab_b, dd, dd >= 0, inner=B_GROUP)
    bias_sl = bias_sl.reshape(B_KV_HEADS, n_bias, B_GROUP * SUB_TILE, SUB_TILE)
    n_wt = WIN_LEN // SUB_TILE + 1
    dw = ((n_wt - 1 - np.arange(n_wt + 1))[:, None, None] * SUB_TILE + np.arange(SUB_TILE)[None, :, None]
          - np.arange(SUB_TILE)[None, None, :])
    in_win = (dw >= 0) & (dw < WIN_LEN) & (np.arange(n_wt + 1) < n_wt)[:, None, None]
    bias_w = _bias_tiles(tab_b, dw, in_win, inner=B_GROUP)
    bias_w = bias_w.reshape(B_KV_HEADS, n_wt + 1, B_GROUP * SUB_TILE, SUB_TILE)
    o_b = _nsa(qb, gl, kcmp, vcmp, ksl, vsl, bias_sl, kw, vw, bias_w)

    wa = w_out[:A_WIDTH].astype(BF16)
    wb = w_out[A_WIDTH:][perm].astype(BF16)
    out = _tail(x2, o_a, o_b.reshape(n, B_WIDTH), wa, wb, norm2_g.reshape(1, d), w_gate.astype(BF16),
                w_up.astype(BF16), w_down.astype(BF16), norm_f_g.reshape(1, d), tm=512)
    return out.reshape(b, s, d)


def kernel(x, norm1_g, w_in, rel_bias, cmp_pos, cmp_k_w1, cmp_k_b1, cmp_k_w2, cmp_v_w1, cmp_v_b1, cmp_v_w2,
           w_out, norm2_g, w_gate, w_up, w_down, norm_f_g):
    assert w_in.shape[0] == 1, "single-layer model"
    tab_a = rel_bias[:, :A_HEADS].T * LOG2E
    tab_b = rel_bias[:, A_HEADS:].T * LOG2E
    return _layer(x, tab_a, tab_b, norm1_g[0], w_in[0], cmp_pos[0], cmp_k_w1[0], cmp_k_b1[0], cmp_k_w2[0],
                  cmp_v_w1[0], cmp_v_b1[0], cmp_v_w2[0], w_out[0], norm2_g[0], w_gate[0], w_up[0], w_down[0],
                  norm_f_g)
```
